```python
import math, functools
import jax, jax.numpy as jnp
from jax import lax
import numpy as np


D_MODEL = 1024
BATCH = 16
SEQ = 2048
DEPTH = 2
DEC_BATCH = 16
DEC_SEQ = 32
PAST_LEN = 1024

CHUNK = 64
Q_BLOCK = 128
HEAD_DIM = 64
ROPE_THETA = 10000.0
NORM_EPS = 1e-6
NEG_INF = -1e30
POOL_WINDOWS = (2, 4, 8, 16)
POOL_GROUPS = len(POOL_WINDOWS)
POOL_DIM = D_MODEL // 2
POOL_GROUP_DIM = POOL_DIM // POOL_GROUPS
POOL_HIST = max(POOL_WINDOWS) - 1
DSA_HEADS = (D_MODEL // 2) // HEAD_DIM
DSA_KV_HEADS = 2
DSA_TOPK = 256
IDX_HEADS = 4
IDX_DIM = 64
SB_HEADS = (D_MODEL // 2) // HEAD_DIM
DIFF_HEADS = (D_MODEL // 2) // (2 * HEAD_DIM)
MOE_GROUPS = 4
MOE_EXPERTS_PER_GROUP = 4
MOE_EXPERTS = MOE_GROUPS * MOE_EXPERTS_PER_GROUP
MOE_TOPK_INNER = 2
MOE_HIDDEN = D_MODEL // 2

N_EVEN = (DEPTH + 1) // 2
N_ODD = DEPTH // 2
EVEN_WIDTHS = (POOL_DIM, DSA_HEADS * HEAD_DIM, DSA_KV_HEADS * HEAD_DIM, DSA_KV_HEADS * HEAD_DIM,
               IDX_HEADS * IDX_DIM, IDX_DIM, IDX_HEADS)
EVEN_IN = sum(EVEN_WIDTHS)
EVEN_MIX = POOL_DIM + DSA_HEADS * HEAD_DIM
ODD_WIDTHS = (SB_HEADS * HEAD_DIM,) * 3 + (DIFF_HEADS * 2 * HEAD_DIM,) * 3
ODD_IN = sum(ODD_WIDTHS)
ODD_MIX = SB_HEADS * HEAD_DIM + DIFF_HEADS * 2 * HEAD_DIM

kernel_name = 'hybrid_stream_pool_dsa_stickbreak_diff_hmoe'


def _split_points(widths):
    return [int(v) for v in np.cumsum(widths)[:-1]]


def rmsnorm(x, g):
    xf = x.astype(jnp.float32)
    y = xf * lax.rsqrt(jnp.mean(xf * xf, axis=-1, keepdims=True) + NORM_EPS)
    return (y * g.astype(jnp.float32)).astype(x.dtype)


def rope(x, pos):
    half = x.shape[-1] // 2
    inv = ROPE_THETA ** (-jnp.arange(half, dtype=jnp.float32) / half)
    ang = pos.astype(jnp.float32)[:, None] * inv[None, :]
    shape = (1, pos.shape[0]) + (1,) * (x.ndim - 3) + (half,)
    cos, sin = jnp.cos(ang).reshape(shape), jnp.sin(ang).reshape(shape)
    xf = x.astype(jnp.float32)
    x1, x2 = xf[..., :half], xf[..., half:]
    return jnp.concatenate([x1 * cos - x2 * sin, x2 * cos + x1 * sin], axis=-1).astype(x.dtype)


def chunk_end(pos):
    return (pos // CHUNK + 1) * CHUNK


def sweep_queries(fn, q_args, q_pos):
    tq = q_pos.shape[0]
    if tq <= Q_BLOCK or tq % Q_BLOCK != 0:
        return fn(*q_args, q_pos)
    nb = tq // Q_BLOCK
    blocks = tuple(jnp.moveaxis(a.reshape((a.shape[0], nb, Q_BLOCK) + a.shape[2:]), 1, 0) for a in q_args)
    out = lax.map(lambda xs: fn(*xs), blocks + (q_pos.reshape(nb, Q_BLOCK),))
    out = jnp.moveaxis(out, 0, 1)
    return out.reshape((out.shape[0], tq) + out.shape[3:])


def pool_mix(u_ext, pos, w_pool, pool_scale):
    t = pos.shape[0]
    cs = jnp.cumsum(u_ext.astype(jnp.float32), axis=1)
    cs = jnp.pad(cs, ((0, 0), (1, 0), (0, 0)))
    u_new = u_ext[:, POOL_HIST:].astype(jnp.float32)
    outs = []
    for g, win in enumerate(POOL_WINDOWS):
        c0, c1 = g * POOL_GROUP_DIM, (g + 1) * POOL_GROUP_DIM
        hi = cs[:, POOL_HIST + 1:, c0:c1]
        lo = cs[:, POOL_HIST + 1 - win:POOL_HIST + 1 - win + t, c0:c1]
        cnt = jnp.minimum(pos + 1, win).astype(jnp.float32)[None, :, None]
        d = ((hi - lo) / cnt - u_new[..., c0:c1]).astype(u_ext.dtype)
        outs.append(d @ w_pool[g])
    return jnp.concatenate(outs, axis=-1) * pool_scale


def dsa_attend(q, qi, wi, q_pos, *, k, v, ki, k_pos, n_sel):
    b, tq, h, dh = q.shape
    g = k.shape[2]
    s_idx = jax.nn.relu(jnp.einsum('bthd,bsd->bths', qi, ki).astype(jnp.float32))
    s_idx = jnp.einsum('bths,bth->bts', s_idx, wi.astype(jnp.float32))
    lim = chunk_end(q_pos)
    adm = k_pos[None, :] < lim[:, None]
    s_idx = jnp.where(adm[None], s_idx, NEG_INF)
    _, sel = lax.top_k(s_idx, n_sel)
    valid = k_pos[sel] < lim[None, :, None]
    ks = jax.vmap(lambda kb, ib: kb[ib])(k, sel)
    vs = jax.vmap(lambda vb, ib: vb[ib])(v, sel)
    qg = q.reshape(b, tq, g, h // g, dh)
    sc = jnp.einsum('btgrd,btkgd->btgrk', qg, ks).astype(jnp.float32) * dh ** -0.5
    sc = jnp.where(valid[:, :, None, None, :], sc, NEG_INF)
    p = jax.nn.softmax(sc, axis=-1)
    o = jnp.einsum('btgrk,btkgd->btgrd', p.astype(vs.dtype), vs)
    return o.reshape(b, tq, h * dh)


def sb_attend(q, q_pos, *, k, v, k_pos):
    b, tq, h, dh = q.shape
    z = jnp.einsum('bthd,bshd->bhts', q, k).astype(jnp.float32) * dh ** -0.5
    causal = (k_pos[None, :] < q_pos[:, None])[None, None]
    lb = jnp.where(causal, jax.nn.log_sigmoid(-z), 0.0)
    suffix = lax.cumsum(lb, axis=3, reverse=True) - lb
    a = jnp.where(causal, jnp.exp(jax.nn.log_sigmoid(z) + suffix), 0.0)
    o = jnp.einsum('bhts,bshd->bthd', a.astype(v.dtype), v)
    return o.reshape(b, tq, h * dh)


def diff_attend(q, q_pos, *, k, v, k_pos, lam, gain, out_scale):
    b, tq, h, _, dh = q.shape
    sc = jnp.einsum('bthcd,bshcd->bhcts', q, k).astype(jnp.float32) * dh ** -0.5
    mask = k_pos[None, :] < chunk_end(q_pos)[:, None]
    p = jax.nn.softmax(jnp.where(mask, sc, NEG_INF), axis=-1)
    a = p[:, :, 0] - lam * p[:, :, 1]
    o = jnp.einsum('bhts,bshe->bthe', a.astype(v.dtype), v).astype(jnp.float32)
    o = o * lax.rsqrt(jnp.mean(o * o, axis=-1, keepdims=True) + NORM_EPS) * gain.astype(jnp.float32) * out_scale
    return o.astype(v.dtype).reshape(b, tq, h * 2 * dh)


def even_mixer(hn, pos, pool_hist, k_hist, v_hist, ki_hist, w_in, w_pool, pool_scale, w_out):
    b, t, _ = hn.shape
    u, q, k, v, qi, ki, wi = jnp.split(hn @ w_in, _split_points(EVEN_WIDTHS), axis=-1)
    q = rope(q.reshape(b, t, DSA_HEADS, HEAD_DIM), pos)
    k = rope(k.reshape(b, t, DSA_KV_HEADS, HEAD_DIM), pos)
    v = v.reshape(b, t, DSA_KV_HEADS, HEAD_DIM)
    qi = rope(qi.reshape(b, t, IDX_HEADS, IDX_DIM), pos)
    ki = rope(ki, pos)
    wi = wi * (IDX_HEADS * IDX_DIM) ** -0.5
    u_ext = jnp.concatenate([pool_hist, u], axis=1)
    a_out = pool_mix(u_ext, pos, w_pool, pool_scale)
    k_all = jnp.concatenate([k_hist, k], axis=1)
    v_all = jnp.concatenate([v_hist, v], axis=1)
    ki_all = jnp.concatenate([ki_hist, ki], axis=1)
    n_keys = k_all.shape[1]
    attend = functools.partial(dsa_attend, k=k_all, v=v_all, ki=ki_all,
                               k_pos=jnp.arange(n_keys, dtype=jnp.int32), n_sel=min(DSA_TOPK, n_keys // 4))
    b_out = sweep_queries(attend, (q, qi, wi), pos)
    y = jnp.concatenate([a_out, b_out], axis=-1) @ w_out
    return y, u_ext[:, -POOL_HIST:], k, v, ki


def odd_mixer(hn, pos, sbk_hist, sbv_hist, dk_hist, dv_hist, w_in, diff_lambda, diff_subln, w_out, lam_init):
    b, t, _ = hn.shape
    sq, sk, sv, dq, dk, dv = jnp.split(hn @ w_in, _split_points(ODD_WIDTHS), axis=-1)
    sq = sq.reshape(b, t, SB_HEADS, HEAD_DIM)
    sk = sk.reshape(b, t, SB_HEADS, HEAD_DIM)
    sv = sv.reshape(b, t, SB_HEADS, HEAD_DIM)
    dq = rope(dq.reshape(b, t, DIFF_HEADS, 2, HEAD_DIM), pos)
    dk = rope(dk.reshape(b, t, DIFF_HEADS, 2, HEAD_DIM), pos)
    dv = dv.reshape(b, t, DIFF_HEADS, 2 * HEAD_DIM)
    sk_all = jnp.concatenate([sbk_hist, sk], axis=1)
    sv_all = jnp.concatenate([sbv_hist, sv], axis=1)
    dk_all = jnp.concatenate([dk_hist, dk], axis=1)
    dv_all = jnp.concatenate([dv_hist, dv], axis=1)
    k_pos = jnp.arange(sk_all.shape[1], dtype=jnp.int32)
    lp = diff_lambda.astype(jnp.float32)
    lam = jnp.exp(jnp.sum(lp[0] * lp[1])) - jnp.exp(jnp.sum(lp[2] * lp[3])) + lam_init
    c_out = sweep_queries(functools.partial(sb_attend, k=sk_all, v=sv_all, k_pos=k_pos), (sq,), pos)
    d_out = sweep_queries(functools.partial(diff_attend, k=dk_all, v=dv_all, k_pos=k_pos, lam=lam,
                                            gain=diff_subln, out_scale=1.0 - lam_init), (dq,), pos)
    y = jnp.concatenate([c_out, d_out], axis=-1) @ w_out
    return y, sk, sv, dk, dv


def hier_moe(x, w_group, b_group, w_expert, b_expert, w1, w3, w2):
    n = x.shape[0]
    g_logits = (x @ w_group).astype(jnp.float32) + b_group.astype(jnp.float32)
    g_sel = jnp.argmax(g_logits, axis=-1)
    g_hot = jax.nn.one_hot(g_sel, MOE_GROUPS, dtype=jnp.float32)
    g_gate = jnp.sum(jax.nn.softmax(g_logits, axis=-1) * g_hot, axis=-1, keepdims=True)
    e_logits = jnp.einsum('nd,gde->nge', x, w_expert).astype(jnp.float32) + b_expert.astype(jnp.float32)
    e_logits = jnp.sum(e_logits * g_hot[:, :, None], axis=1)
    top_v, top_i = lax.top_k(e_logits, MOE_TOPK_INNER)
    top_w = jax.nn.softmax(top_v, axis=-1) * g_gate
    inner = jnp.sum(jax.nn.one_hot(top_i, MOE_EXPERTS_PER_GROUP, dtype=jnp.float32) * top_w[..., None], axis=1)
    gate = (g_hot[:, :, None] * inner[:, None, :]).reshape(n, MOE_EXPERTS).astype(x.dtype)
    y = jnp.zeros_like(x)
    for e in range(MOE_EXPERTS):
        hid = jax.nn.silu(x @ w1[e]) * (x @ w3[e])
        y = y + gate[:, e:e + 1] * (hid @ w2[e])
    return y


def setup_inputs(seed: int = 0) -> dict:
    key = jax.random.key(seed)
    ks = iter(jax.random.split(key, 40))

    def nrm(shape, scale=1.0):
        return jax.random.normal(next(ks), shape, jnp.float32) * scale

    def gain(shape):
        return 1.0 + nrm(shape, 0.1)

    return {
        'x_prompt': nrm((BATCH, SEQ, D_MODEL)),
        'x_sample': nrm((DEC_BATCH, DEC_SEQ, D_MODEL)),
        'cache_pool': nrm((N_EVEN, DEC_BATCH, POOL_HIST, POOL_DIM)),
        'cache_dsa_k': nrm((N_EVEN, DEC_BATCH, PAST_LEN, DSA_KV_HEADS, HEAD_DIM)),
        'cache_dsa_v': nrm((N_EVEN, DEC_BATCH, PAST_LEN, DSA_KV_HEADS, HEAD_DIM)),
        'cache_idx_k': nrm((N_EVEN, DEC_BATCH, PAST_LEN, IDX_DIM)),
        'cache_sb_k': nrm((N_ODD, DEC_BATCH, PAST_LEN, SB_HEADS, HEAD_DIM)),
        'cache_sb_v': nrm((N_ODD, DEC_BATCH, PAST_LEN, SB_HEADS, HEAD_DIM)),
        'cache_diff_k': nrm((N_ODD, DEC_BATCH, PAST_LEN, DIFF_HEADS, 2, HEAD_DIM)),
        'cache_diff_v': nrm((N_ODD, DEC_BATCH, PAST_LEN, DIFF_HEADS, 2 * HEAD_DIM)),
        'norm_mix': gain((DEPTH, D_MODEL)),
        'norm_ffn': gain((DEPTH, D_MODEL)),
        'norm_final': gain((D_MODEL,)),
        'w_in_even': nrm((N_EVEN, D_MODEL, EVEN_IN), D_MODEL ** -0.5),
        'w_pool': nrm((N_EVEN, POOL_GROUPS, POOL_GROUP_DIM, POOL_GROUP_DIM), POOL_GROUP_DIM ** -0.5),
        'pool_scale': gain((N_EVEN, POOL_DIM)),
        'w_out_even': nrm((N_EVEN, EVEN_MIX, D_MODEL), EVEN_MIX ** -0.5),
        'w_in_odd': nrm((N_ODD, D_MODEL, ODD_IN), D_MODEL ** -0.5),
        'diff_lambda': nrm((N_ODD, 4, HEAD_DIM), 0.1),
        'diff_subln': gain((N_ODD, 2 * HEAD_DIM)),
        'w_out_odd': nrm((N_ODD, ODD_MIX, D_MODEL), ODD_MIX ** -0.5),
        'moe_w_group': nrm((DEPTH, D_MODEL, MOE_GROUPS), D_MODEL ** -0.5),
        'moe_b_group': nrm((DEPTH, MOE_GROUPS), 0.01),
        'moe_w_expert': nrm((DEPTH, MOE_GROUPS, D_MODEL, MOE_EXPERTS_PER_GROUP), D_MODEL ** -0.5),
        'moe_b_expert': nrm((DEPTH, MOE_GROUPS, MOE_EXPERTS_PER_GROUP), 0.01),
        'moe_w1': nrm((DEPTH, MOE_EXPERTS, D_MODEL, MOE_HIDDEN), D_MODEL ** -0.5),
        'moe_w3': nrm((DEPTH, MOE_EXPERTS, D_MODEL, MOE_HIDDEN), D_MODEL ** -0.5),
        'moe_w2': nrm((DEPTH, MOE_EXPERTS, MOE_HIDDEN, D_MODEL), MOE_HIDDEN ** -0.5),
    }


def reference(x_prompt, x_sample, cache_pool, cache_dsa_k, cache_dsa_v, cache_idx_k, cache_sb_k, cache_sb_v,
              cache_diff_k, cache_diff_v, norm_mix, norm_ffn, norm_final, w_in_even, w_pool, pool_scale,
              w_out_even, w_in_odd, diff_lambda, diff_subln, w_out_odd, moe_w_group, moe_b_group,
              moe_w_expert, moe_b_expert, moe_w1, moe_w3, moe_w2):
    b, t, d = x_prompt.shape
    bd, td, _ = x_sample.shape
    past = cache_dsa_k.shape[2]
    dt = x_prompt.dtype
    pos_p = jnp.arange(t, dtype=jnp.int32)
    pos_s = past + jnp.arange(td, dtype=jnp.int32)
    h_p, h_s = x_prompt, x_sample
    pool_p, dsak_p, dsav_p, idxk_p, sbk_p, sbv_p, dfk_p, dfv_p = [], [], [], [], [], [], [], []
    pool_s, dsak_s, dsav_s, idxk_s, sbk_s, sbv_s, dfk_s, dfv_s = [], [], [], [], [], [], [], []
    for l in range(DEPTH):
        i = l // 2
        if l % 2 == 0:
            ew = (w_in_even[i], w_pool[i], pool_scale[i], w_out_even[i])
            y, pt, k, v, ki = even_mixer(
                rmsnorm(h_p, norm_mix[l]), pos_p, jnp.zeros((b, POOL_HIST, POOL_DIM), dt),
                jnp.zeros((b, 0, DSA_KV_HEADS, HEAD_DIM), dt), jnp.zeros((b, 0, DSA_KV_HEADS, HEAD_DIM), dt),
                jnp.zeros((b, 0, IDX_DIM), dt), *ew)
            h_p = h_p + y
            pool_p.append(pt); dsak_p.append(k); dsav_p.append(v); idxk_p.append(ki)
            y, pt, k, v, ki = even_mixer(rmsnorm(h_s, norm_mix[l]), pos_s, cache_pool[i], cache_dsa_k[i],
                                         cache_dsa_v[i], cache_idx_k[i], *ew)
            h_s = h_s + y
            pool_s.append(pt); dsak_s.append(k); dsav_s.append(v); idxk_s.append(ki)
        else:
            lam_init = 0.8 - 0.6 * math.exp(-0.3 * l)
            ow = (w_in_odd[i], diff_lambda[i], diff_subln[i], w_out_odd[i], lam_init)
            y, sk, sv, dk, dv = odd_mixer(
                rmsnorm(h_p, norm_mix[l]), pos_p,
                jnp.zeros((b, 0, SB_HEADS, HEAD_DIM), dt), jnp.zeros((b, 0, SB_HEADS, HEAD_DIM), dt),
                jnp.zeros((b, 0, DIFF_HEADS, 2, HEAD_DIM), dt), jnp.zeros((b, 0, DIFF_HEADS, 2 * HEAD_DIM), dt), *ow)
            h_p = h_p + y
            sbk_p.append(sk); sbv_p.append(sv); dfk_p.append(dk); dfv_p.append(dv)
            y, sk, sv, dk, dv = odd_mixer(rmsnorm(h_s, norm_mix[l]), pos_s, cache_sb_k[i], cache_sb_v[i],
                                          cache_diff_k[i], cache_diff_v[i], *ow)
            h_s = h_s + y
            sbk_s.append(sk); sbv_s.append(sv); dfk_s.append(dk); dfv_s.append(dv)
        mw = (moe_w_group[l], moe_b_group[l], moe_w_expert[l], moe_b_expert[l], moe_w1[l], moe_w3[l], moe_w2[l])
        h_p = h_p + hier_moe(rmsnorm(h_p, norm_ffn[l]).reshape(b * t, d), *mw).reshape(b, t, d)
        h_s = h_s + hier_moe(rmsnorm(h_s, norm_ffn[l]).reshape(bd * td, d), *mw).reshape(bd, td, d)
    y_prompt = rmsnorm(h_p, norm_final)
    y_sample = rmsnorm(h_s, norm_final)
    return (y_prompt, y_sample,
            jnp.stack(pool_p), jnp.stack(dsak_p), jnp.stack(dsav_p), jnp.stack(idxk_p),
            jnp.stack(sbk_p), jnp.stack(sbv_p), jnp.stack(dfk_p), jnp.stack(dfv_p),
            jnp.stack(pool_s), jnp.stack(dsak_s), jnp.stack(dsav_s), jnp.stack(idxk_s),
            jnp.stack(sbk_s), jnp.stack(sbv_s), jnp.stack(dfk_s), jnp.stack(dfv_s))
```

```python
import functools
import math

import jax
import jax.numpy as jnp
import numpy as np
from jax import lax
from jax.experimental import pallas as pl
from jax.experimental.pallas import tpu as pltpu

F32 = jnp.float32
BF16 = jnp.bfloat16

LANES = 128
HEAD_DIM = 64
CHUNK = 64
ROPE_THETA = 10000.0
NORM_EPS = 1e-6
NEG_INF = -1e30
PAD_SCORE = -3e38
BIG_POS = 3e38
POOL_WINDOWS = (2, 4, 8, 16)
POOL_HIST = 15
POOL_HIST_PAD = 16
DSA_TOPK = 256
IDX_HEADS = 4
MOE_GROUPS = 4
MOE_EPG = 4
MOE_EXPERTS = 16
GATE_COL0 = MOE_GROUPS
VMEM_LIMIT = 56 * 1024 * 1024
BISECT_STEPS = 8
BISECT_ROUNDS = 48


def _cparams(sem):
    return pltpu.CompilerParams(dimension_semantics=sem, vmem_limit_bytes=VMEM_LIMIT)


def _dot(a, b):
    return jnp.dot(a, b, preferred_element_type=F32)


def _dot_nt(a, b):
    return lax.dot_general(a, b, (((1,), (1,)), ((), ())), preferred_element_type=F32)


def _rms(x, g):
    ms = jnp.mean(x * x, axis=-1, keepdims=True)
    return x * lax.rsqrt(ms + NORM_EPS) * g


def _proj_kernel(x_ref, g_ref, w_ref, tab_ref, *out_refs, segs):
    xn = _rms(x_ref[...], g_ref[...]).astype(BF16)
    for o_ref, (c0, width, mode) in zip(out_refs, segs):
        y = _dot(xn, w_ref[:, c0:c0 + width])
        if mode is None:
            o_ref[...] = y
            continue
        t0 = 0 if mode == "full" else 3 * LANES
        cos = tab_ref[:, t0:t0 + LANES]
        sin_a = tab_ref[:, t0 + LANES:t0 + 2 * LANES]
        sin_b = tab_ref[:, t0 + 2 * LANES:t0 + 3 * LANES]
        for c in range(0, width, LANES):
            yc = y[:, c:c + LANES]
            o_ref[:, c:c + LANES] = (yc * cos + pltpu.roll(yc, LANES - HEAD_DIM // 2, 1) * sin_a
                                     + pltpu.roll(yc, HEAD_DIM // 2, 1) * sin_b)


def _rope_tables(pos):
    half = HEAD_DIM // 2
    inv = ROPE_THETA ** (-jnp.arange(half, dtype=F32) / half)
    ang = pos.astype(F32)[:, None] * inv[None, :]
    cos, sin = jnp.cos(ang), jnp.sin(ang)
    zero, one = jnp.zeros_like(sin), jnp.ones_like(cos)
    cos_h = jnp.concatenate([cos, cos], axis=1)
    sa_h = jnp.concatenate([-sin, zero], axis=1)
    sb_h = jnp.concatenate([zero, sin], axis=1)
    one_h = jnp.concatenate([one, one], axis=1)
    zero_h = jnp.concatenate([zero, zero], axis=1)
    return jnp.concatenate([cos_h, cos_h, sa_h, sa_h, sb_h, sb_h,
                            cos_h, one_h, sa_h, zero_h, sb_h, zero_h], axis=1)


def _project(x, g, w, tab, segs, tm):
    n, d = x.shape
    tt = tab.shape[0]
    nt = tt // tm
    kern = functools.partial(_proj_kernel, segs=segs)
    return pl.pallas_call(
        kern,
        grid=(n // tm,),
        in_specs=[pl.BlockSpec((tm, d), lambda i: (i, 0)),
                  pl.BlockSpec((1, d), lambda i: (0, 0)),
                  pl.BlockSpec(w.shape, lambda i: (0, 0)),
                  pl.BlockSpec((tm, tab.shape[1]), lambda i: (i % nt, 0))],
        out_specs=[pl.BlockSpec((tm, wd), lambda i: (i, 0)) for _, wd, _ in segs],
        out_shape=[jax.ShapeDtypeStruct((n, wd), F32) for _, wd, _ in segs],
        compiler_params=_cparams(("parallel",)),
        name="proj",
    )(x, g.reshape(1, d), w, tab)


def _pool_kernel(u_ref, h_ref, w_ref, s_ref, o_ref, ext_ref, *, t, pos0, rc):
    ext_ref[0:POOL_HIST_PAD, :] = h_ref[0]
    ext_ref[POOL_HIST_PAD:POOL_HIST_PAD + t, :] = u_ref[0]
    for r0 in range(0, t, rc):
        pos = pos0 + r0 + lax.broadcasted_iota(jnp.int32, (rc, 1), 0)
        for g, win in enumerate(POOL_WINDOWS):
            c0 = g * LANES
            u_new = ext_ref[POOL_HIST_PAD + r0:POOL_HIST_PAD + r0 + rc, c0:c0 + LANES]
            s = u_new
            for k in range(1, win):
                s = s + ext_ref[POOL_HIST_PAD + r0 - k:POOL_HIST_PAD + r0 - k + rc, c0:c0 + LANES]
            cnt = jnp.minimum(pos + 1, win).astype(F32)
            dlt = (s / cnt - u_new).astype(BF16)
            o_ref[0, r0:r0 + rc, c0:c0 + LANES] = _dot(dlt, w_ref[g]) * s_ref[:, c0:c0 + LANES]


def _pool_mix(u, hist, w_pool, pool_scale, pos0):
    b, t, c = u.shape
    rc = min(t, 256)
    hist16 = jnp.pad(hist, ((0, 0), (POOL_HIST_PAD - POOL_HIST, 0), (0, 0)))
    kern = functools.partial(_pool_kernel, t=t, pos0=pos0, rc=rc)
    return pl.pallas_call(
        kern,
        grid=(b,),
        in_specs=[pl.BlockSpec((1, t, c), lambda i: (i, 0, 0)),
                  pl.BlockSpec((1, POOL_HIST_PAD, c), lambda i: (i, 0, 0)),
                  pl.BlockSpec(w_pool.shape, lambda i: (0, 0, 0)),
                  pl.BlockSpec((1, c), lambda i: (0, 0))],
        out_specs=pl.BlockSpec((1, t, c), lambda i: (i, 0, 0)),
        out_shape=jax.ShapeDtypeStruct((b, t, c), F32),
        scratch_shapes=[pltpu.VMEM((POOL_HIST_PAD + t, c), F32)],
        compiler_params=_cparams(("parallel",)),
        name="pool_mix",
    )(u, hist16, w_pool, pool_scale.reshape(1, c))


def _dsa_kernel(q_ref, qi_ref, kw_ref, k_ref, v_ref, ki_ref, o_ref, lo_ref, hi_ref,
                *, tq, s_len, n_keys, q_pos0, n_sel):
    i = pl.program_id(1)
    qpos = q_pos0 + i * tq + lax.broadcasted_iota(jnp.int32, (tq, 1), 0)
    lim = (qpos // CHUNK + 1) * CHUNK
    kpos = lax.broadcasted_iota(jnp.int32, (tq, s_len), 1)
    adm = kpos < lim
    padded = s_len > n_keys
    if padded:
        real = kpos < n_keys
        adm = jnp.logical_and(adm, real)

    ki = ki_ref[0].astype(BF16)
    qi = qi_ref[0].astype(BF16)
    wi = kw_ref[0][:, HEAD_DIM:HEAD_DIM + IDX_HEADS] * (IDX_HEADS * HEAD_DIM) ** -0.5
    sidx = jnp.zeros((tq, s_len), F32)
    for h in range(IDX_HEADS):
        sh = _dot_nt(qi[:, h * HEAD_DIM:(h + 1) * HEAD_DIM], ki)
        sidx = sidx + jnp.maximum(sh, 0.0) * wi[:, h:h + 1]
    sm = jnp.where(adm, sidx, NEG_INF)
    if padded:
        sm = jnp.where(real, sm, PAD_SCORE)

    kf = float(n_sel)

    def count_gt(x):
        return jnp.sum(jnp.where(sm > x, 1.0, 0.0), axis=1, keepdims=True)

    row_max = jnp.max(sm, axis=1, keepdims=True)
    row_min = jnp.min(jnp.where(real, sm, BIG_POS) if padded else sm, axis=1, keepdims=True)
    adm_min = jnp.min(jnp.where(adm, sm, BIG_POS), axis=1, keepdims=True)
    few = count_gt(row_min) < kf
    tight = count_gt(adm_min) >= kf
    lo_ref[...] = jnp.where(few, PAD_SCORE, jnp.where(tight, adm_min, row_min))
    hi_ref[...] = jnp.where(few, row_min, jnp.where(tight, row_max, adm_min))

    def bracket(lo, hi):
        above = jnp.min(jnp.where(sm > lo, sm, BIG_POS), axis=1, keepdims=True)
        below = jnp.max(jnp.where(sm <= hi, sm, PAD_SCORE), axis=1, keepdims=True)
        return above, below

    def unresolved(lo, hi):
        above, below = bracket(lo, hi)
        return jnp.sum(jnp.where(above < below, 1, 0))

    def cond(carry):
        rounds, open_rows = carry
        return jnp.logical_and(open_rows > 0, rounds < BISECT_ROUNDS)

    def body(carry):
        rounds, _ = carry
        lo, hi = lo_ref[...], hi_ref[...]
        for _ in range(BISECT_STEPS):
            mid = 0.5 * lo + 0.5 * hi
            under = count_gt(mid) < kf
            hi = jnp.where(under, mid, hi)
            lo = jnp.where(under, lo, mid)
        lo_ref[...] = lo
        hi_ref[...] = hi
        return rounds + 1, unresolved(lo, hi)

    lax.while_loop(cond, body, (jnp.int32(0), unresolved(lo_ref[...], hi_ref[...])))
    _, thr = bracket(lo_ref[...], hi_ref[...])

    gt = sm > thr
    eq = sm == thr
    need = kf - jnp.sum(jnp.where(gt, 1.0, 0.0), axis=1, keepdims=True)
    rr = lax.broadcasted_iota(jnp.int32, (LANES, LANES), 0)
    cc = lax.broadcasted_iota(jnp.int32, (LANES, LANES), 1)
    prefix_ones = jnp.where(rr <= cc, 1.0, 0.0).astype(BF16)
    carry = jnp.zeros((tq, 1), F32)
    parts = []
    for j in range(s_len // LANES):
        sl = slice(j * LANES, (j + 1) * LANES)
        eq_j = eq[:, sl]
        rank = _dot(jnp.where(eq_j, 1.0, 0.0).astype(BF16), prefix_ones) + carry
        parts.append(jnp.logical_or(gt[:, sl], jnp.logical_and(eq_j, rank <= need)))
        carry = rank[:, LANES - 1:LANES]
    mask = jnp.logical_and(jnp.concatenate(parts, axis=1), adm)

    q = q_ref[0]
    kk = k_ref[0].astype(BF16)
    vv = v_ref[0].astype(BF16)
    n_kv = kk.shape[1] // HEAD_DIM
    rep = q.shape[1] // HEAD_DIM // n_kv
    mask_r = jnp.concatenate([mask] * rep, axis=0)
    for g in range(n_kv):
        kg = kk[:, g * HEAD_DIM:(g + 1) * HEAD_DIM]
        vg = vv[:, g * HEAD_DIM:(g + 1) * HEAD_DIM]
        qg = jnp.concatenate([q[:, (g * rep + r) * HEAD_DIM:(g * rep + r + 1) * HEAD_DIM]
                              for r in range(rep)], axis=0).astype(BF16)
        sc = jnp.where(mask_r, _dot_nt(qg, kg) * HEAD_DIM ** -0.5, NEG_INF)
        p = jnp.exp(sc - jnp.max(sc, axis=1, keepdims=True))
        den = jnp.sum(p, axis=1, keepdims=True)
        og = _dot(p.astype(BF16), vg) / den
        for r in range(rep):
            h = g * rep + r
            o_ref[0, :, h * HEAD_DIM:(h + 1) * HEAD_DIM] = og[r * tq:(r + 1) * tq]


def _dsa(q, qi, kiwi, k_all, v_all, ki_all, n_keys, q_pos0):
    b, t, dq = q.shape
    s_len = k_all.shape[1]
    tq = min(t, LANES)
    n_sel = min(DSA_TOPK, n_keys // 4)
    kern = functools.partial(_dsa_kernel, tq=tq, s_len=s_len, n_keys=n_keys, q_pos0=q_pos0, n_sel=n_sel)
    qspec = lambda w: pl.BlockSpec((1, tq, w), lambda bi, i: (bi, i, 0))
    kspec = lambda w: pl.BlockSpec((1, s_len, w), lambda bi, i: (bi, 0, 0))
    return pl.pallas_call(
        kern,
        grid=(b, t // tq),
        in_specs=[qspec(dq), qspec(qi.shape[2]), qspec(kiwi.shape[2]),
                  kspec(k_all.shape[2]), kspec(v_all.shape[2]), kspec(ki_all.shape[2])],
        out_specs=qspec(dq),
        out_shape=jax.ShapeDtypeStruct((b, t, dq), F32),
        scratch_shapes=[pltpu.VMEM((tq, 1), F32), pltpu.VMEM((tq, 1), F32)],
        compiler_params=_cparams(("parallel", "parallel")),
        name="dsa",
    )(q, qi, kiwi, k_all, v_all, ki_all)


def _sb_kernel(q_ref, k_ref, v_ref, o_ref, *, tq, n_keys, q_pos0):
    i = pl.program_id(2)
    q = q_ref[0].astype(BF16)
    qpos = q_pos0 + i * tq + lax.broadcasted_iota(jnp.int32, (tq, 1), 0)
    last_q = q_pos0 + (i + 1) * tq - 1
    n_blocks = (jnp.minimum(last_q, n_keys) + LANES - 1) // LANES
    rr = lax.broadcasted_iota(jnp.int32, (LANES, LANES), 0)
    cc = lax.broadcasted_iota(jnp.int32, (LANES, LANES), 1)
    suffix_ones = jnp.where(rr >= cc, 1.0, 0.0).astype(BF16)
    heads = q.shape[1] // HEAD_DIM

    def body(jj, carry):
        j = n_blocks - 1 - jj
        ks = pl.multiple_of(j * LANES, LANES)
        kb = k_ref[0, pl.ds(ks, LANES), :].astype(BF16)
        vb = v_ref[0, pl.ds(ks, LANES), :].astype(BF16)
        kpos = ks + lax.broadcasted_iota(jnp.int32, (1, LANES), 1)
        causal = jnp.logical_and(kpos < qpos, kpos < n_keys)
        out = []
        for h in range(heads):
            run, acc = carry[2 * h], carry[2 * h + 1]
            hs = slice(h * HEAD_DIM, (h + 1) * HEAD_DIM)
            z = _dot_nt(q[:, hs], kb[:, hs]) * HEAD_DIM ** -0.5
            softplus = jnp.maximum(z, 0.0) + jnp.log1p(jnp.exp(-jnp.abs(z)))
            lb = jnp.where(causal, -softplus, 0.0)
            lb_hi = lb.astype(BF16)
            lb_lo = (lb - lb_hi.astype(F32)).astype(BF16)
            rev = _dot(lb_hi, suffix_ones) + _dot(lb_lo, suffix_ones)
            a = jnp.where(causal, jnp.exp(z + rev + run), 0.0)
            out.append(run + rev[:, 0:1])
            out.append(acc + _dot(a.astype(BF16), vb[:, hs]))
        return tuple(out)

    init = []
    for _ in range(heads):
        init += [jnp.zeros((tq, 1), F32), jnp.zeros((tq, HEAD_DIM), F32)]
    res = lax.fori_loop(0, n_blocks, body, tuple(init))
    for h in range(heads):
        o_ref[0, :, h * HEAD_DIM:(h + 1) * HEAD_DIM] = res[2 * h + 1]


def _sb_attend(q, k_all, v_all, n_keys, q_pos0):
    b, t, d = q.shape
    s_len = k_all.shape[1]
    tq = min(t, LANES)
    kern = functools.partial(_sb_kernel, tq=tq, n_keys=n_keys, q_pos0=q_pos0)
    return pl.pallas_call(
        kern,
        grid=(b, d // LANES, t // tq),
        in_specs=[pl.BlockSpec((1, tq, LANES), lambda bi, h, i: (bi, i, h)),
                  pl.BlockSpec((1, s_len, LANES), lambda bi, h, i: (bi, 0, h)),
                  pl.BlockSpec((1, s_len, LANES), lambda bi, h, i: (bi, 0, h))],
        out_specs=pl.BlockSpec((1, tq, LANES), lambda bi, h, i: (bi, i, h)),
        out_shape=jax.ShapeDtypeStruct((b, t, d), F32),
        compiler_params=_cparams(("parallel", "parallel", "parallel")),
        name="sb_attend",
    )(q, k_all, v_all)


def _diff_kernel(q_ref, k_ref, v_ref, lam_ref, gain_ref, o_ref, *, tq, s_len, n_keys, q_pos0, lam_init):
    i = pl.program_id(2)
    lp = lam_ref[...]
    lam = (jnp.exp(jnp.sum(lp[0:1] * lp[1:2], axis=1, keepdims=True))
           - jnp.exp(jnp.sum(lp[2:3] * lp[3:4], axis=1, keepdims=True)) + lam_init)
    qpos = q_pos0 + i * tq + lax.broadcasted_iota(jnp.int32, (tq, 1), 0)
    lim = (qpos // CHUNK + 1) * CHUNK
    kpos = lax.broadcasted_iota(jnp.int32, (tq, s_len), 1)
    mask = jnp.logical_and(kpos < lim, kpos < n_keys)
    q = q_ref[0].astype(BF16)
    kk = k_ref[0].astype(BF16)
    probs = []
    for c in range(2):
        hs = slice(c * HEAD_DIM, (c + 1) * HEAD_DIM)
        sc = jnp.where(mask, _dot_nt(q[:, hs], kk[:, hs]) * HEAD_DIM ** -0.5, NEG_INF)
        p = jnp.exp(sc - jnp.max(sc, axis=1, keepdims=True))
        probs.append(p / jnp.sum(p, axis=1, keepdims=True))
    a = probs[0] - lam * probs[1]
    o = _dot(a.astype(BF16), v_ref[0].astype(BF16))
    o = o * lax.rsqrt(jnp.mean(o * o, axis=-1, keepdims=True) + NORM_EPS) * gain_ref[...] * (1.0 - lam_init)
    o_ref[0] = o


def _diff_attend(q, k_all, v_all, diff_lambda, gain, n_keys, q_pos0, lam_init):
    b, t, d = q.shape
    s_len = k_all.shape[1]
    tq = min(t, LANES)
    kern = functools.partial(_diff_kernel, tq=tq, s_len=s_len, n_keys=n_keys, q_pos0=q_pos0, lam_init=lam_init)
    return pl.pallas_call(
        kern,
        grid=(b, d // LANES, t // tq),
        in_specs=[pl.BlockSpec((1, tq, LANES), lambda bi, h, i: (bi, i, h)),
                  pl.BlockSpec((1, s_len, LANES), lambda bi, h, i: (bi, 0, h)),
                  pl.BlockSpec((1, s_len, LANES), lambda bi, h, i: (bi, 0, h)),
                  pl.BlockSpec(diff_lambda.shape, lambda bi, h, i: (0, 0)),
                  pl.BlockSpec((1, LANES), lambda bi, h, i: (0, 0))],
        out_specs=pl.BlockSpec((1, tq, LANES), lambda bi, h, i: (bi, i, h)),
        out_shape=jax.ShapeDtypeStruct((b, t, d), F32),
        compiler_params=_cparams(("parallel", "parallel", "parallel")),
        name="diff_attend",
    )(q, k_all, v_all, diff_lambda, gain.reshape(1, LANES))


def _out_kernel(h_ref, a_ref, b_ref, wa_ref, wb_ref, o_ref):
    o_ref[...] = (h_ref[...] + _dot(a_ref[...].astype(BF16), wa_ref[...])
                  + _dot(b_ref[...].astype(BF16), wb_ref[...]))


def _out_proj(h, a, bmix, w_out, tm):
    n, d = h.shape
    ca = a.shape[1]
    wa, wb = w_out[:ca], w_out[ca:]
    row = lambda w: pl.BlockSpec((tm, w), lambda i: (i, 0))
    return pl.pallas_call(
        _out_kernel,
        grid=(n // tm,),
        in_specs=[row(d), row(ca), row(bmix.shape[1]),
                  pl.BlockSpec(wa.shape, lambda i: (0, 0)), pl.BlockSpec(wb.shape, lambda i: (0, 0))],
        out_specs=row(d),
        out_shape=jax.ShapeDtypeStruct((n, d), F32),
        compiler_params=_cparams(("parallel",)),
        name="out_proj",
    )(h, a, bmix, wa, wb)


def _router_kernel(h_ref, g_ref, w_ref, b_ref, xn_ref, gate_ref):
    xn = _rms(h_ref[...], g_ref[...]).astype(BF16)
    xn_ref[...] = xn
    logits = _dot(xn, w_ref[...]) + b_ref[...]
    lane = lax.broadcasted_iota(jnp.int32, logits.shape, 1)
    is_group = lane < MOE_GROUPS
    gl = jnp.where(is_group, logits, NEG_INF)
    g_max = jnp.max(gl, axis=1, keepdims=True)
    g_sel = jnp.min(jnp.where(gl == g_max, lane, LANES), axis=1, keepdims=True)
    g_gate = 1.0 / jnp.sum(jnp.where(is_group, jnp.exp(gl - g_max), 0.0), axis=1, keepdims=True)
    in_group = jnp.logical_and(lane >= GATE_COL0, (lane - GATE_COL0) // MOE_EPG == g_sel)
    in_group = jnp.logical_and(in_group, lane < GATE_COL0 + MOE_EXPERTS)
    el = jnp.where(in_group, logits, NEG_INF)
    top1 = jnp.max(el, axis=1, keepdims=True)
    i1 = jnp.min(jnp.where(jnp.logical_and(in_group, el == top1), lane, LANES), axis=1, keepdims=True)
    rest = jnp.logical_and(in_group, lane != i1)
    el2 = jnp.where(rest, logits, NEG_INF)
    top2 = jnp.max(el2, axis=1, keepdims=True)
    i2 = jnp.min(jnp.where(jnp.logical_and(rest, el2 == top2), lane, LANES), axis=1, keepdims=True)
    e2 = jnp.exp(top2 - top1)
    w1 = g_gate / (1.0 + e2)
    gate_ref[...] = jnp.where(lane == i1, w1, jnp.where(lane == i2, w1 * e2, 0.0))


def _expert_kernel(xn_ref, gate_ref, w13_ref, w2_ref, h_ref, gf_ref, o_ref, acc_ref, *, final_norm):
    e = pl.program_id(1)

    @pl.when(e == 0)
    def _():
        acc_ref[...] = jnp.zeros_like(acc_ref)

    gate = gate_ref[...]
    lane = lax.broadcasted_iota(jnp.int32, gate.shape, 1)
    ge = jnp.sum(jnp.where(lane == GATE_COL0 + e, gate, 0.0), axis=1, keepdims=True)
    up = _dot(xn_ref[...], w13_ref[0])
    hid = w13_ref.shape[2] // 2
    a, b = up[:, :hid], up[:, hid:]
    act = (a * (1.0 / (1.0 + jnp.exp(-a))) * b).astype(BF16)
    acc_ref[...] += ge * _dot(act, w2_ref[0])

    @pl.when(e == pl.num_programs(1) - 1)
    def _():
        y = h_ref[...] + acc_ref[...]
        o_ref[...] = _rms(y, gf_ref[...]) if final_norm else y


def _moe(h, g_ffn, wr, br, w13, w2, g_final, final_norm, tm):
    n, d = h.shape
    row = lambda w: pl.BlockSpec((tm, w), lambda i: (i, 0))
    xn, gate = pl.pallas_call(
        _router_kernel,
        grid=(n // tm,),
        in_specs=[row(d), pl.BlockSpec((1, d), lambda i: (0, 0)),
                  pl.BlockSpec(wr.shape, lambda i: (0, 0)), pl.BlockSpec((1, LANES), lambda i: (0, 0))],
        out_specs=[row(d), row(LANES)],
        out_shape=[jax.ShapeDtypeStruct((n, d), BF16), jax.ShapeDtypeStruct((n, LANES), F32)],
        compiler_params=_cparams(("parallel",)),
        name="moe_router",
    )(h, g_ffn.reshape(1, d), wr, br)
    n_e = w13.shape[0]
    row2 = lambda w: pl.BlockSpec((tm, w), lambda i, e: (i, 0))
    kern = functools.partial(_expert_kernel, final_norm=final_norm)
    return pl.pallas_call(
        kern,
        grid=(n // tm, n_e),
        in_specs=[row2(d), row2(LANES),
                  pl.BlockSpec((1,) + w13.shape[1:], lambda i, e: (e, 0, 0)),
                  pl.BlockSpec((1,) + w2.shape[1:], lambda i, e: (e, 0, 0)),
                  row2(d), pl.BlockSpec((1, d), lambda i, e: (0, 0))],
        out_specs=row2(d),
        out_shape=jax.ShapeDtypeStruct((n, d), F32),
        scratch_shapes=[pltpu.VMEM((tm, d), F32)],
        compiler_params=_cparams(("parallel", "arbitrary")),
        name="moe_experts",
    )(xn, gate, w13, w2, h, g_final.reshape(1, d))


EVEN_SEGS = ((0, 512, None), (512, 512, "full"), (1024, 128, "full"), (1152, 128, None),
             (1280, 256, "full"), (1536, 128, "half"))
ODD_SEGS = ((0, 512, None), (512, 512, None), (1024, 512, None),
            (1536, 512, "full"), (2048, 512, "full"), (2560, 512, None))


def _cat_keys(hist, new):
    allk = jnp.concatenate([hist, new], axis=1) if hist is not None else new
    n_keys = allk.shape[1]
    pad = -n_keys % LANES
    if pad:
        allk = jnp.pad(allk, ((0, 0), (0, pad), (0, 0)))
    return allk, n_keys


def kernel(x_prompt, x_sample, cache_pool, cache_dsa_k, cache_dsa_v, cache_idx_k, cache_sb_k, cache_sb_v,
           cache_diff_k, cache_diff_v, norm_mix, norm_ffn, norm_final, w_in_even, w_pool, pool_scale,
           w_out_even, w_in_odd, diff_lambda, diff_subln, w_out_odd, moe_w_group, moe_b_group,
           moe_w_expert, moe_b_expert, moe_w1, moe_w3, moe_w2):
    b, t, d = x_prompt.shape
    bd, td, _ = x_sample.shape
    past = cache_dsa_k.shape[2]
    depth = norm_mix.shape[0]
    groups = ((b, t, 0, min(512, b * t)), (bd, td, past, bd * td))

    tabs = []
    for (gb, gt, p0, tm) in groups:
        tab = _rope_tables(p0 + jnp.arange(gt, dtype=jnp.int32))
        if tm > gt:
            tab = jnp.tile(tab, (tm // gt, 1))
        tabs.append(tab)

    hs = [x_prompt.reshape(b * t, d), x_sample.reshape(bd * td, d)]
    outs = [dict(), dict()]
    for l in range(depth):
        li = l // 2
        last = l == depth - 1
        if l % 2 == 0:
            n_in = w_in_even.shape[2]
            w_in = jnp.pad(w_in_even[li], ((0, 0), (0, -n_in % LANES))).astype(BF16)
            w_out = w_out_even[li].astype(BF16)
            wp = w_pool[li].astype(BF16)
        else:
            w_in = w_in_odd[li].astype(BF16)
            w_out = w_out_odd[li].astype(BF16)
            lam_init = 0.8 - 0.6 * math.exp(-0.3 * l)
        wr = jnp.concatenate([moe_w_group[l]] + [moe_w_expert[l, g] for g in range(MOE_GROUPS)], axis=1)
        wr = jnp.pad(wr, ((0, 0), (0, LANES - wr.shape[1]))).astype(BF16)
        br = jnp.concatenate([moe_b_group[l], moe_b_expert[l].reshape(-1)])
        br = jnp.pad(br, (0, LANES - br.shape[0])).reshape(1, LANES).astype(F32)
        w13 = jnp.concatenate([moe_w1[l], moe_w3[l]], axis=2).astype(BF16)
        w2 = moe_w2[l].astype(BF16)

        for gi, (gb, gt, p0, tm) in enumerate(groups):
            h = hs[gi]
            o = outs[gi]
            sample = gi == 1
            r3 = lambda x: x.reshape(gb, gt, x.shape[-1])
            if l % 2 == 0:
                u, q, k, v, qi, kiwi = [r3(x) for x in _project(h, norm_mix[l], w_in, tabs[gi], EVEN_SEGS, tm)]
                ki = kiwi[..., :HEAD_DIM]
                hist = cache_pool[li] if sample else jnp.zeros((gb, POOL_HIST, u.shape[2]), F32)
                a_out = _pool_mix(u, hist, wp, pool_scale[li], p0)
                if sample:
                    k_all, n_keys = _cat_keys(cache_dsa_k[li].reshape(gb, past, -1), k)
                    v_all, _ = _cat_keys(cache_dsa_v[li].reshape(gb, past, -1), v)
                    ki_all, _ = _cat_keys(cache_idx_k[li], ki)
                else:
                    (k_all, n_keys), (v_all, _), (ki_all, _) = _cat_keys(None, k), _cat_keys(None, v), _cat_keys(None, ki)
                b_out = _dsa(q, qi, kiwi, k_all, v_all, ki_all, n_keys, p0)
                o.setdefault("pool", []).append(jnp.concatenate([hist, u], axis=1)[:, -POOL_HIST:])
                o.setdefault("dsa_k", []).append(k.reshape(gb, gt, -1, HEAD_DIM))
                o.setdefault("dsa_v", []).append(v.reshape(gb, gt, -1, HEAD_DIM))
                o.setdefault("idx_k", []).append(ki)
                mix_a, mix_b = a_out, b_out
            else:
                sq, sk, sv, dq, dk, dv = [r3(x) for x in _project(h, norm_mix[l], w_in, tabs[gi], ODD_SEGS, tm)]
                if sample:
                    sk_all, n_keys = _cat_keys(cache_sb_k[li].reshape(gb, past, -1), sk)
                    sv_all, _ = _cat_keys(cache_sb_v[li].reshape(gb, past, -1), sv)
                    dk_all, _ = _cat_keys(cache_diff_k[li].reshape(gb, past, -1), dk)
                    dv_all, _ = _cat_keys(cache_diff_v[li].reshape(gb, past, -1), dv)
                else:
                    (sk_all, n_keys), (sv_all, _) = _cat_keys(None, sk), _cat_keys(None, sv)
                    (dk_all, _), (dv_all, _) = _cat_keys(None, dk), _cat_keys(None, dv)
                c_out = _sb_attend(sq, sk_all, sv_all, n_keys, p0)
                d_out = _diff_attend(dq, dk_all, dv_all, diff_lambda[li], diff_subln[li], n_keys, p0, lam_init)
                n_sb = sk.shape[2] // HEAD_DIM
                n_df = dk.shape[2] // (2 * HEAD_DIM)
                o.setdefault("sb_k", []).append(sk.reshape(gb, gt, n_sb, HEAD_DIM))
                o.setdefault("sb_v", []).append(sv.reshape(gb, gt, n_sb, HEAD_DIM))
                o.setdefault("diff_k", []).append(dk.reshape(gb, gt, n_df, 2, HEAD_DIM))
                o.setdefault("diff_v", []).append(dv.reshape(gb, gt, n_df, 2 * HEAD_DIM))
                mix_a, mix_b = c_out, d_out
            h = _out_proj(h, mix_a.reshape(gb * gt, -1), mix_b.reshape(gb * gt, -1), w_out, tm)
            hs[gi] = _moe(h, norm_ffn[l], wr, br, w13, w2, norm_final, last, tm)

    names = ("pool", "dsa_k", "dsa_v", "idx_k", "sb_k", "sb_v", "diff_k", "diff_v")
    res = [hs[0].reshape(b, t, d), hs[1].reshape(bd, td, d)]
    for o in outs:
        res += [jnp.stack(o[nm]) for nm in names]
    return tuple(res)
```

```python
import functools
import math

import jax
import jax.numpy as jnp
import numpy as np
from jax import lax
from jax.experimental import pallas as pl
from jax.experimental.pallas import tpu as pltpu

F32 = jnp.float32
BF16 = jnp.bfloat16

LANES = 128
HEAD_DIM = 64
CHUNK = 64
ROPE_THETA = 10000.0
NORM_EPS = 1e-6
NEG_INF = -1e30
PAD_SCORE = -3e38
BIG_POS = 3e38
POOL_WINDOWS = (2, 4, 8, 16)
POOL_HIST = 15
POOL_HIST_PAD = 16
DSA_TOPK = 256
IDX_HEADS = 4
MOE_GROUPS = 4
MOE_EPG = 4
MOE_EXPERTS = 16
GATE_COL0 = MOE_GROUPS
VMEM_LIMIT = 56 * 1024 * 1024
BISECT_STEPS = 8
BISECT_ROUNDS = 48


def _cparams(sem):
    return pltpu.CompilerParams(dimension_semantics=sem, vmem_limit_bytes=VMEM_LIMIT)


def _dot(a, b):
    return jnp.dot(a, b, preferred_element_type=F32)


def _dot_nt(a, b):
    return lax.dot_general(a, b, (((1,), (1,)), ((), ())), preferred_element_type=F32)


def _rms(x, g):
    ms = jnp.mean(x * x, axis=-1, keepdims=True)
    return x * lax.rsqrt(ms + NORM_EPS) * g


def _proj_kernel(x_ref, g_ref, w_ref, tab_ref, *out_refs, segs):
    xn = _rms(x_ref[...], g_ref[...]).astype(BF16)
    for o_ref, (c0, width, mode) in zip(out_refs, segs):
        y = _dot(xn, w_ref[:, c0:c0 + width])
        if mode is None:
            o_ref[...] = y
            continue
        t0 = 0 if mode == "full" else 3 * LANES
        cos = tab_ref[:, t0:t0 + LANES]
        sin_a = tab_ref[:, t0 + LANES:t0 + 2 * LANES]
        sin_b = tab_ref[:, t0 + 2 * LANES:t0 + 3 * LANES]
        for c in range(0, width, LANES):
            yc = y[:, c:c + LANES]
            o_ref[:, c:c + LANES] = (yc * cos + pltpu.roll(yc, LANES - HEAD_DIM // 2, 1) * sin_a
                                     + pltpu.roll(yc, HEAD_DIM // 2, 1) * sin_b)


def _rope_tables(pos):
    half = HEAD_DIM // 2
    inv = ROPE_THETA ** (-jnp.arange(half, dtype=F32) / half)
    ang = pos.astype(F32)[:, None] * inv[None, :]
    cos, sin = jnp.cos(ang), jnp.sin(ang)
    zero, one = jnp.zeros_like(sin), jnp.ones_like(cos)
    cos_h = jnp.concatenate([cos, cos], axis=1)
    sa_h = jnp.concatenate([-sin, zero], axis=1)
    sb_h = jnp.concatenate([zero, sin], axis=1)
    one_h = jnp.concatenate([one, one], axis=1)
    zero_h = jnp.concatenate([zero, zero], axis=1)
    return jnp.concatenate([cos_h, cos_h, sa_h, sa_h, sb_h, sb_h,
                            cos_h, one_h, sa_h, zero_h, sb_h, zero_h], axis=1)


def _project(x, g, w, tab, segs, tm):
    n, d = x.shape
    tt = tab.shape[0]
    nt = tt // tm
    kern = functools.partial(_proj_kernel, segs=segs)
    return pl.pallas_call(
        kern,
        grid=(n // tm,),
        in_specs=[pl.BlockSpec((tm, d), lambda i: (i, 0)),
                  pl.BlockSpec((1, d), lambda i: (0, 0)),
                  pl.BlockSpec(w.shape, lambda i: (0, 0)),
                  pl.BlockSpec((tm, tab.shape[1]), lambda i: (i % nt, 0))],
        out_specs=[pl.BlockSpec((tm, wd), lambda i: (i, 0)) for _, wd, _ in segs],
        out_shape=[jax.ShapeDtypeStruct((n, wd), F32) for _, wd, _ in segs],
        compiler_params=_cparams(("parallel",)),
        name="proj",
    )(x, g.reshape(1, d), w, tab)


def _pool_kernel(u_ref, h_ref, w_ref, s_ref, o_ref, ext_ref, *, t, pos0, rc):
    ext_ref[0:POOL_HIST_PAD, :] = h_ref[0]
    ext_ref[POOL_HIST_PAD:POOL_HIST_PAD + t, :] = u_ref[0]
    for r0 in range(0, t, rc):
        pos = pos0 + r0 + lax.broadcasted_iota(jnp.int32, (rc, 1), 0)
        for g, win in enumerate(POOL_WINDOWS):
            c0 = g * LANES
            u_new = ext_ref[POOL_HIST_PAD + r0:POOL_HIST_PAD + r0 + rc, c0:c0 + LANES]
            s = u_new
            for k in range(1, win):
                s = s + ext_ref[POOL_HIST_PAD + r0 - k:POOL_HIST_PAD + r0 - k + rc, c0:c0 + LANES]
            cnt = jnp.minimum(pos + 1, win).astype(F32)
            dlt = (s / cnt - u_new).astype(BF16)
            o_ref[0, r0:r0 + rc, c0:c0 + LANES] = _dot(dlt, w_ref[g]) * s_ref[:, c0:c0 + LANES]


def _pool_mix(u, hist, w_pool, pool_scale, pos0):
    b, t, c = u.shape
    rc = min(t, 256)
    hist16 = jnp.pad(hist, ((0, 0), (POOL_HIST_PAD - POOL_HIST, 0), (0, 0)))
    kern = functools.partial(_pool_kernel, t=t, pos0=pos0, rc=rc)
    return pl.pallas_call(
        kern,
        grid=(b,),
        in_specs=[pl.BlockSpec((1, t, c), lambda i: (i, 0, 0)),
                  pl.BlockSpec((1, POOL_HIST_PAD, c), lambda i: (i, 0, 0)),
                  pl.BlockSpec(w_pool.shape, lambda i: (0, 0, 0)),
                  pl.BlockSpec((1, c), lambda i: (0, 0))],
        out_specs=pl.BlockSpec((1, t, c), lambda i: (i, 0, 0)),
        out_shape=jax.ShapeDtypeStruct((b, t, c), F32),
        scratch_shapes=[pltpu.VMEM((POOL_HIST_PAD + t, c), F32)],
        compiler_params=_cparams(("parallel",)),
        name="pool_mix",
    )(u, hist16, w_pool, pool_scale.reshape(1, c))


def _dsa_kernel(q_ref, qi_ref, kw_ref, k_ref, v_ref, ki_ref, o_ref, lo_ref, hi_ref,
                *, tq, extents, n_keys, q_pos0, n_sel):
    i = pl.program_id(1)
    qpos = q_pos0 + i * tq + lax.broadcasted_iota(jnp.int32, (tq, 1), 0)
    lim = jnp.minimum((qpos // CHUNK + 1) * CHUNK, n_keys)
    kf = float(n_sel)
    low = lax.broadcasted_iota(jnp.int32, (tq, LANES), 1) < HEAD_DIM
    q = q_ref[0] * HEAD_DIM ** -0.5
    qi = qi_ref[0].astype(BF16)
    wi = kw_ref[0][:, HEAD_DIM:HEAD_DIM + IDX_HEADS] * (IDX_HEADS * HEAD_DIM) ** -0.5

    def body(ext):
        kpos = lax.broadcasted_iota(jnp.int32, (tq, ext), 1)
        adm = kpos < lim
        padded = ext > n_keys
        virt = float(max(n_keys - ext, 0))

        sidx = jnp.zeros((tq, ext), F32)
        ki = ki_ref[0, :ext, :]
        for h in range(IDX_HEADS):
            sh = _dot_nt(qi[:, h * HEAD_DIM:(h + 1) * HEAD_DIM], ki)
            sidx = sidx + jnp.maximum(sh, 0.0) * wi[:, h:h + 1]
        sm = jnp.where(adm, sidx, NEG_INF)
        if padded:
            real = kpos < n_keys
            sm = jnp.where(real, sm, PAD_SCORE)

        def count_gt(x):
            cnt = jnp.sum(jnp.where(sm > x, 1.0, 0.0), axis=1, keepdims=True)
            return cnt + jnp.where(x < NEG_INF, virt, 0.0) if virt else cnt

        def bracket(lo, hi):
            above = jnp.min(jnp.where(sm > lo, sm, BIG_POS), axis=1, keepdims=True)
            below = jnp.max(jnp.where(sm <= hi, sm, PAD_SCORE), axis=1, keepdims=True)
            if virt:
                above = jnp.minimum(above, jnp.where(lo < NEG_INF, NEG_INF, BIG_POS))
                below = jnp.maximum(below, jnp.where(hi >= NEG_INF, NEG_INF, PAD_SCORE))
            return above, below

        row_max = jnp.max(sm, axis=1, keepdims=True)
        row_min = jnp.min(jnp.where(real, sm, BIG_POS) if padded else sm, axis=1, keepdims=True)
        if virt:
            row_min = jnp.minimum(row_min, NEG_INF)
        adm_min = jnp.min(jnp.where(adm, sm, BIG_POS), axis=1, keepdims=True)
        few = count_gt(row_min) < kf
        tight = count_gt(adm_min) >= kf
        lo_ref[...] = jnp.where(few, PAD_SCORE, jnp.where(tight, adm_min, row_min))
        hi_ref[...] = jnp.where(few, row_min, jnp.where(tight, row_max, adm_min))

        def unresolved(lo, hi):
            above, below = bracket(lo, hi)
            return jnp.sum(jnp.where(above < below, 1, 0))

        def cond(carry):
            rounds, open_rows = carry
            return jnp.logical_and(open_rows > 0, rounds < BISECT_ROUNDS)

        def step(carry):
            rounds, _ = carry
            lo, hi = lo_ref[...], hi_ref[...]
            for _ in range(BISECT_STEPS):
                mid = 0.5 * lo + 0.5 * hi
                under = count_gt(mid) < kf
                hi = jnp.where(under, mid, hi)
                lo = jnp.where(under, lo, mid)
            lo_ref[...] = lo
            hi_ref[...] = hi
            return rounds + 1, unresolved(lo, hi)

        lax.while_loop(cond, step, (jnp.int32(0), unresolved(lo_ref[...], hi_ref[...])))
        _, thr = bracket(lo_ref[...], hi_ref[...])

        gt = sm > thr
        eq = sm == thr
        need = kf - count_gt(thr)
        rr = lax.broadcasted_iota(jnp.int32, (LANES, LANES), 0)
        cc = lax.broadcasted_iota(jnp.int32, (LANES, LANES), 1)
        prefix_ones = jnp.where(rr <= cc, 1.0, 0.0).astype(BF16)
        carry = jnp.zeros((tq, 1), F32)
        parts = []
        for j in range(ext // LANES):
            sl = slice(j * LANES, (j + 1) * LANES)
            eq_j = eq[:, sl]
            rank = _dot(jnp.where(eq_j, 1.0, 0.0).astype(BF16), prefix_ones) + carry
            parts.append(jnp.logical_or(gt[:, sl], jnp.logical_and(eq_j, rank <= need)))
            carry = rank[:, LANES - 1:LANES]
        mask = jnp.logical_and(jnp.concatenate(parts, axis=1), adm)
        mask2 = jnp.concatenate([mask, mask], axis=0)

        kk = k_ref[0, :ext, :]
        vv = v_ref[0, :ext, :]
        low_k = lax.broadcasted_iota(jnp.int32, (ext, LANES), 1) < HEAD_DIM
        vsw = pltpu.roll(vv.astype(F32), HEAD_DIM, 1).astype(BF16)
        one = jnp.ones_like(vv)
        n_kv = LANES // HEAD_DIM
        for g in range(n_kv):
            v_lo = jnp.where(low_k, vv if g == 0 else vsw, one)
            v_hi = jnp.where(low_k, one, vsw if g == 0 else vv)
            stacks = []
            for odd in range(2):
                rows = []
                for m in range(2):
                    c = 2 * g + m
                    qc = q[:, c * LANES:(c + 1) * LANES]
                    if (odd == 1) != (g == 1):
                        qc = pltpu.roll(qc, HEAD_DIM, 1)
                    rows.append(jnp.where(low, qc, 0.0) if g == 0 else jnp.where(low, 0.0, qc))
                qs = jnp.concatenate(rows, axis=0).astype(BF16)
                sc = jnp.where(mask2, _dot_nt(qs, kk), NEG_INF)
                p = jnp.exp(sc - jnp.max(sc, axis=1, keepdims=True))
                og = _dot(p.astype(BF16), v_hi if odd else v_lo)
                stacks.append(og / pltpu.roll(og, HEAD_DIM, 1))
            for m in range(2):
                c = 2 * g + m
                o_ref[0, :, c * LANES:(c + 1) * LANES] = jnp.where(
                    low, stacks[0][m * tq:(m + 1) * tq], stacks[1][m * tq:(m + 1) * tq])

    _for_tile_extent(i, tq, q_pos0, n_keys, extents, body)


def _dsa(q, qi, kiwi, k_all, v_all, ki_all, n_keys, q_pos0):
    b, t, dq = q.shape
    s_len = k_all.shape[1]
    tq = min(t, LANES)
    n_sel = min(DSA_TOPK, n_keys // 4)
    extents = _key_extents(t // tq, tq, q_pos0, n_keys, s_len, 4 * LANES)
    kern = functools.partial(_dsa_kernel, tq=tq, extents=extents, n_keys=n_keys, q_pos0=q_pos0, n_sel=n_sel)
    qspec = lambda w: pl.BlockSpec((1, tq, w), lambda bi, i: (bi, i, 0))
    kspec = lambda w: pl.BlockSpec((1, s_len, w), lambda bi, i: (bi, 0, 0))
    return pl.pallas_call(
        kern,
        grid=(b, t // tq),
        in_specs=[qspec(dq), qspec(qi.shape[2]), qspec(kiwi.shape[2]),
                  kspec(k_all.shape[2]), kspec(v_all.shape[2]), kspec(ki_all.shape[2])],
        out_specs=qspec(dq),
        out_shape=jax.ShapeDtypeStruct((b, t, dq), F32),
        scratch_shapes=[pltpu.VMEM((tq, 1), F32), pltpu.VMEM((tq, 1), F32)],
        compiler_params=_cparams(("parallel", "parallel")),
        name="dsa",
    )(q, qi, kiwi, k_all, v_all, ki_all)


def _sb_kernel(q_ref, k_ref, v_ref, o_ref, acc_ref, run_ref, *, tq, n_keys, q_pos0):
    i = pl.program_id(1)
    pairs = q_ref.shape[2] // LANES
    first_q = q_pos0 + i * tq
    qpos = first_q + lax.broadcasted_iota(jnp.int32, (tq, 1), 0)
    n_blocks = (jnp.minimum(first_q + tq - 1, n_keys) + LANES - 1) // LANES
    n_full = jnp.minimum(first_q, n_keys) // LANES
    low_q = lax.broadcasted_iota(jnp.int32, (tq, LANES), 1) < HEAD_DIM
    low_k = lax.broadcasted_iota(jnp.int32, (LANES, LANES), 1) < HEAD_DIM
    q = q_ref[0] * HEAD_DIM ** -0.5
    qm = []
    for p in range(pairs):
        qp = q[:, p * LANES:(p + 1) * LANES]
        qm.append((jnp.where(low_q, qp, 0.0).astype(BF16), jnp.where(low_q, 0.0, qp).astype(BF16)))
    rr = lax.broadcasted_iota(jnp.int32, (2 * LANES, 2 * LANES), 0)
    cc = lax.broadcasted_iota(jnp.int32, (2 * LANES, 2 * LANES), 1)
    rk = jnp.where(rr >= LANES, rr - LANES, rr)
    cs_rhs = jnp.where(jnp.logical_or(cc >= LANES, rk >= cc), 1.0, 0.0).astype(BF16)
    acc_ref[...] = jnp.zeros_like(acc_ref)
    run_ref[...] = jnp.zeros_like(run_ref)

    def block(j, masked):
        ks = pl.multiple_of(j * LANES, LANES)
        if masked:
            kpos = ks + lax.broadcasted_iota(jnp.int32, (1, LANES), 1)
            causal = jnp.logical_and(kpos < qpos, kpos < n_keys)
        for p in range(pairs):
            ps = slice(p * LANES, (p + 1) * LANES)
            kb = k_ref[0, pl.ds(ks, LANES), ps]
            vb = v_ref[0, pl.ds(ks, LANES), ps]
            zero = jnp.zeros_like(vb)
            v_cat = jnp.concatenate([jnp.where(low_k, vb, zero), jnp.where(low_k, zero, vb)], axis=0)
            a_parts = []
            for c in range(2):
                hsl = slice((2 * p + c) * LANES, (2 * p + c + 1) * LANES)
                z = _dot_nt(qm[p][c], kb)
                sp = jnp.maximum(z, 0.0) + jnp.log(1.0 + jnp.exp(-jnp.abs(z)))
                if masked:
                    sp = jnp.where(causal, sp, 0.0)
                hi = sp.astype(BF16)
                lo = (sp - hi.astype(F32)).astype(BF16)
                cs = _dot(jnp.concatenate([hi, lo], axis=1), cs_rhs)
                run = run_ref[:, hsl]
                a = jnp.exp(z - cs[:, :LANES] - run)
                if masked:
                    a = jnp.where(causal, a, 0.0)
                run_ref[:, hsl] = run + cs[:, LANES:]
                a_parts.append(a.astype(BF16))
            acc_ref[:, ps] += _dot(jnp.concatenate(a_parts, axis=1), v_cat)

    def masked_step(jj, carry):
        block(n_blocks - 1 - jj, True)
        return carry

    def full_step(jj, carry):
        block(n_full - 1 - jj, False)
        return carry

    lax.fori_loop(0, n_blocks - n_full, masked_step, 0)
    lax.fori_loop(0, n_full, full_step, 0)
    o_ref[0] = acc_ref[...]


def _sb_attend(q, k_all, v_all, n_keys, q_pos0):
    b, t, d = q.shape
    s_len = k_all.shape[1]
    tq = min(t, LANES)
    kern = functools.partial(_sb_kernel, tq=tq, n_keys=n_keys, q_pos0=q_pos0)
    return pl.pallas_call(
        kern,
        grid=(b, t // tq),
        in_specs=[pl.BlockSpec((1, tq, d), lambda bi, i: (bi, i, 0)),
                  pl.BlockSpec((1, s_len, d), lambda bi, i: (bi, 0, 0)),
                  pl.BlockSpec((1, s_len, d), lambda bi, i: (bi, 0, 0))],
        out_specs=pl.BlockSpec((1, tq, d), lambda bi, i: (bi, i, 0)),
        out_shape=jax.ShapeDtypeStruct((b, t, d), F32),
        scratch_shapes=[pltpu.VMEM((tq, d), F32), pltpu.VMEM((tq, 2 * d), F32)],
        compiler_params=_cparams(("parallel", "parallel")),
        name="sb_attend",
    )(q, k_all, v_all)


def _key_extents(nq, tq, q_pos0, n_keys, s_len, step):
    need = [min(n_keys, ((q_pos0 + (i + 1) * tq - 1) // CHUNK + 1) * CHUNK) for i in range(nq)]
    return tuple(sorted({min(s_len, -(-n // step) * step) for n in need}))


def _for_tile_extent(i, tq, q_pos0, n_keys, extents, body):
    need = jnp.minimum(n_keys, ((q_pos0 + (i + 1) * tq - 1) // CHUNK + 1) * CHUNK)
    prev = 0
    for ext in extents:
        pl.when(jnp.logical_and(need > prev, need <= ext))(functools.partial(body, ext))
        prev = ext


def _diff_kernel(q_ref, k_ref, v_ref, lam_ref, gain_ref, o_ref, *, tq, extents, n_keys, q_pos0, lam_init):
    i = pl.program_id(2)
    lp = lam_ref[...]
    lam = (jnp.exp(jnp.sum(lp[0:1] * lp[1:2], axis=1, keepdims=True))
           - jnp.exp(jnp.sum(lp[2:3] * lp[3:4], axis=1, keepdims=True)) + lam_init)
    qpos = q_pos0 + i * tq + lax.broadcasted_iota(jnp.int32, (tq, 1), 0)
    lim = jnp.minimum((qpos // CHUNK + 1) * CHUNK, n_keys)
    low = lax.broadcasted_iota(jnp.int32, (tq, LANES), 1) < HEAD_DIM
    q = q_ref[0] * HEAD_DIM ** -0.5
    qm = (jnp.where(low, q, 0.0).astype(BF16), jnp.where(low, 0.0, q).astype(BF16))

    def body(ext):
        mask = lax.broadcasted_iota(jnp.int32, (tq, ext), 1) < lim
        kk = k_ref[0, :ext, :]
        probs = []
        for c in range(2):
            sc = jnp.where(mask, _dot_nt(qm[c], kk), NEG_INF)
            p = jnp.exp(sc - jnp.max(sc, axis=1, keepdims=True))
            probs.append(p * (1.0 / jnp.sum(p, axis=1, keepdims=True)))
        a = probs[0] - lam * probs[1]
        o = _dot(a.astype(BF16), v_ref[0, :ext, :])
        o = o * lax.rsqrt(jnp.mean(o * o, axis=-1, keepdims=True) + NORM_EPS) * gain_ref[...] * (1.0 - lam_init)
        o_ref[0] = o

    _for_tile_extent(i, tq, q_pos0, n_keys, extents, body)


def _diff_attend(q, k_all, v_all, diff_lambda, gain, n_keys, q_pos0, lam_init):
    b, t, d = q.shape
    s_len = k_all.shape[1]
    tq = min(t, LANES)
    extents = _key_extents(t // tq, tq, q_pos0, n_keys, s_len, 2 * LANES)
    kern = functools.partial(_diff_kernel, tq=tq, extents=extents, n_keys=n_keys, q_pos0=q_pos0, lam_init=lam_init)
    return pl.pallas_call(
        kern,
        grid=(b, d // LANES, t // tq),
        in_specs=[pl.BlockSpec((1, tq, LANES), lambda bi, h, i: (bi, i, h)),
                  pl.BlockSpec((1, s_len, LANES), lambda bi, h, i: (bi, 0, h)),
                  pl.BlockSpec((1, s_len, LANES), lambda bi, h, i: (bi, 0, h)),
                  pl.BlockSpec(diff_lambda.shape, lambda bi, h, i: (0, 0)),
                  pl.BlockSpec((1, LANES), lambda bi, h, i: (0, 0))],
        out_specs=pl.BlockSpec((1, tq, LANES), lambda bi, h, i: (bi, i, h)),
        out_shape=jax.ShapeDtypeStruct((b, t, d), F32),
        compiler_params=_cparams(("parallel", "parallel", "parallel")),
        name="diff_attend",
    )(q, k_all, v_all, diff_lambda, gain.reshape(1, LANES))


def _out_kernel(h_ref, a_ref, b_ref, wa_ref, wb_ref, o_ref):
    o_ref[...] = (h_ref[...] + _dot(a_ref[...].astype(BF16), wa_ref[...])
                  + _dot(b_ref[...].astype(BF16), wb_ref[...]))


def _out_proj(h, a, bmix, w_out, tm):
    n, d = h.shape
    ca = a.shape[1]
    wa, wb = w_out[:ca], w_out[ca:]
    row = lambda w: pl.BlockSpec((tm, w), lambda i: (i, 0))
    return pl.pallas_call(
        _out_kernel,
        grid=(n // tm,),
        in_specs=[row(d), row(ca), row(bmix.shape[1]),
                  pl.BlockSpec(wa.shape, lambda i: (0, 0)), pl.BlockSpec(wb.shape, lambda i: (0, 0))],
        out_specs=row(d),
        out_shape=jax.ShapeDtypeStruct((n, d), F32),
        compiler_params=_cparams(("parallel",)),
        name="out_proj",
    )(h, a, bmix, wa, wb)


def _router_kernel(h_ref, g_ref, w_ref, b_ref, xn_ref, gate_ref):
    xn = _rms(h_ref[...], g_ref[...]).astype(BF16)
    xn_ref[...] = xn
    logits = _dot(xn, w_ref[...]) + b_ref[...]
    lane = lax.broadcasted_iota(jnp.int32, logits.shape, 1)
    is_group = lane < MOE_GROUPS
    gl = jnp.where(is_group, logits, NEG_INF)
    g_max = jnp.max(gl, axis=1, keepdims=True)
    g_sel = jnp.min(jnp.where(gl == g_max, lane, LANES), axis=1, keepdims=True)
    g_gate = 1.0 / jnp.sum(jnp.where(is_group, jnp.exp(gl - g_max), 0.0), axis=1, keepdims=True)
    in_group = jnp.logical_and(lane >= GATE_COL0, (lane - GATE_COL0) // MOE_EPG == g_sel)
    in_group = jnp.logical_and(in_group, lane < GATE_COL0 + MOE_EXPERTS)
    el = jnp.where(in_group, logits, NEG_INF)
    top1 = jnp.max(el, axis=1, keepdims=True)
    i1 = jnp.min(jnp.where(jnp.logical_and(in_group, el == top1), lane, LANES), axis=1, keepdims=True)
    rest = jnp.logical_and(in_group, lane != i1)
    el2 = jnp.where(rest, logits, NEG_INF)
    top2 = jnp.max(el2, axis=1, keepdims=True)
    i2 = jnp.min(jnp.where(jnp.logical_and(rest, el2 == top2), lane, LANES), axis=1, keepdims=True)
    e2 = jnp.exp(top2 - top1)
    w1 = g_gate / (1.0 + e2)
    gate_ref[...] = jnp.where(lane == i1, w1, jnp.where(lane == i2, w1 * e2, 0.0))


def _expert_kernel(xn_ref, gate_ref, w13_ref, w2_ref, h_ref, gf_ref, o_ref, acc_ref, *, final_norm):
    e = pl.program_id(1)

    @pl.when(e == 0)
    def _():
        acc_ref[...] = jnp.zeros_like(acc_ref)

    gate = gate_ref[...]
    lane = lax.broadcasted_iota(jnp.int32, gate.shape, 1)
    ge = jnp.sum(jnp.where(lane == GATE_COL0 + e, gate, 0.0), axis=1, keepdims=True)
    up = _dot(xn_ref[...], w13_ref[0])
    hid = w13_ref.shape[2] // 2
    a, b = up[:, :hid], up[:, hid:]
    act = (a * (1.0 / (1.0 + jnp.exp(-a))) * b).astype(BF16)
    acc_ref[...] += ge * _dot(act, w2_ref[0])

    @pl.when(e == pl.num_programs(1) - 1)
    def _():
        y = h_ref[...] + acc_ref[...]
        o_ref[...] = _rms(y, gf_ref[...]) if final_norm else y


def _moe(h, g_ffn, wr, br, w13, w2, g_final, final_norm, tm):
    n, d = h.shape
    row = lambda w: pl.BlockSpec((tm, w), lambda i: (i, 0))
    xn, gate = pl.pallas_call(
        _router_kernel,
        grid=(n // tm,),
        in_specs=[row(d), pl.BlockSpec((1, d), lambda i: (0, 0)),
                  pl.BlockSpec(wr.shape, lambda i: (0, 0)), pl.BlockSpec((1, LANES), lambda i: (0, 0))],
        out_specs=[row(d), row(LANES)],
        out_shape=[jax.ShapeDtypeStruct((n, d), BF16), jax.ShapeDtypeStruct((n, LANES), F32)],
        compiler_params=_cparams(("parallel",)),
        name="moe_router",
    )(h, g_ffn.reshape(1, d), wr, br)
    n_e = w13.shape[0]
    row2 = lambda w: pl.BlockSpec((tm, w), lambda i, e: (i, 0))
    kern = functools.partial(_expert_kernel, final_norm=final_norm)
    return pl.pallas_call(
        kern,
        grid=(n // tm, n_e),
        in_specs=[row2(d), row2(LANES),
                  pl.BlockSpec((1,) + w13.shape[1:], lambda i, e: (e, 0, 0)),
                  pl.BlockSpec((1,) + w2.shape[1:], lambda i, e: (e, 0, 0)),
                  row2(d), pl.BlockSpec((1, d), lambda i, e: (0, 0))],
        out_specs=row2(d),
        out_shape=jax.ShapeDtypeStruct((n, d), F32),
        scratch_shapes=[pltpu.VMEM((tm, d), F32)],
        compiler_params=_cparams(("parallel", "arbitrary")),
        name="moe_experts",
    )(xn, gate, w13, w2, h, g_final.reshape(1, d))


EVEN_SEGS = ((0, 512, None), (512, 512, "full"), (1024, 128, "full"), (1152, 128, None),
             (1280, 256, "full"), (1536, 128, "half"))
ODD_SEGS = ((0, 512, None), (512, 512, None), (1024, 512, None),
            (1536, 512, "full"), (2048, 512, "full"), (2560, 512, None))


def _cat_keys(hist, new):
    allk = jnp.concatenate([hist, new], axis=1) if hist is not None else new
    n_keys = allk.shape[1]
    pad = -n_keys % LANES
    if pad:
        allk = jnp.pad(allk, ((0, 0), (0, pad), (0, 0)))
    return allk.astype(BF16), n_keys


def kernel(x_prompt, x_sample, cache_pool, cache_dsa_k, cache_dsa_v, cache_idx_k, cache_sb_k, cache_sb_v,
           cache_diff_k, cache_diff_v, norm_mix, norm_ffn, norm_final, w_in_even, w_pool, pool_scale,
           w_out_even, w_in_odd, diff_lambda, diff_subln, w_out_odd, moe_w_group, moe_b_group,
           moe_w_expert, moe_b_expert, moe_w1, moe_w3, moe_w2):
    b, t, d = x_prompt.shape
    bd, td, _ = x_sample.shape
    past = cache_dsa_k.shape[2]
    depth = norm_mix.shape[0]
    groups = ((b, t, 0, min(512, b * t)), (bd, td, past, bd * td))

    tabs = []
    for (gb, gt, p0, tm) in groups:
        tab = _rope_tables(p0 + jnp.arange(gt, dtype=jnp.int32))
        if tm > gt:
            tab = jnp.tile(tab, (tm // gt, 1))
        tabs.append(tab)

    hs = [x_prompt.reshape(b * t, d), x_sample.reshape(bd * td, d)]
    outs = [dict(), dict()]
    for l in range(depth):
        li = l // 2
        last = l == depth - 1
        if l % 2 == 0:
            n_in = w_in_even.shape[2]
            w_in = jnp.pad(w_in_even[li], ((0, 0), (0, -n_in % LANES))).astype(BF16)
            w_out = w_out_even[li].astype(BF16)
            wp = w_pool[li].astype(BF16)
        else:
            w_in = w_in_odd[li].astype(BF16)
            w_out = w_out_odd[li].astype(BF16)
            lam_init = 0.8 - 0.6 * math.exp(-0.3 * l)
        wr = jnp.concatenate([moe_w_group[l]] + [moe_w_expert[l, g] for g in range(MOE_GROUPS)], axis=1)
        wr = jnp.pad(wr, ((0, 0), (0, LANES - wr.shape[1]))).astype(BF16)
        br = jnp.concatenate([moe_b_group[l], moe_b_expert[l].reshape(-1)])
        br = jnp.pad(br, (0, LANES - br.shape[0])).reshape(1, LANES).astype(F32)
        w13 = jnp.concatenate([moe_w1[l], moe_w3[l]], axis=2).astype(BF16)
        w2 = moe_w2[l].astype(BF16)

        for gi, (gb, gt, p0, tm) in enumerate(groups):
            h = hs[gi]
            o = outs[gi]
            sample = gi == 1
            r3 = lambda x: x.reshape(gb, gt, x.shape[-1])
            if l % 2 == 0:
                u, q, k, v, qi, kiwi = [r3(x) for x in _project(h, norm_mix[l], w_in, tabs[gi], EVEN_SEGS, tm)]
                ki = kiwi[..., :HEAD_DIM]
                hist = cache_pool[li] if sample else jnp.zeros((gb, POOL_HIST, u.shape[2]), F32)
                a_out = _pool_mix(u, hist, wp, pool_scale[li], p0)
                if sample:
                    k_all, n_keys = _cat_keys(cache_dsa_k[li].reshape(gb, past, -1), k)
                    v_all, _ = _cat_keys(cache_dsa_v[li].reshape(gb, past, -1), v)
                    ki_all, _ = _cat_keys(cache_idx_k[li], ki)
                else:
                    (k_all, n_keys), (v_all, _), (ki_all, _) = _cat_keys(None, k), _cat_keys(None, v), _cat_keys(None, ki)
                b_out = _dsa(q, qi, kiwi, k_all, v_all, ki_all, n_keys, p0)
                o.setdefault("pool", []).append(jnp.concatenate([hist, u], axis=1)[:, -POOL_HIST:])
                o.setdefault("dsa_k", []).append(k.reshape(gb, gt, -1, HEAD_DIM))
                o.setdefault("dsa_v", []).append(v.reshape(gb, gt, -1, HEAD_DIM))
                o.setdefault("idx_k", []).append(ki)
                mix_a, mix_b = a_out, b_out
            else:
                sq, sk, sv, dq, dk, dv = [r3(x) for x in _project(h, norm_mix[l], w_in, tabs[gi], ODD_SEGS, tm)]
                if sample:
                    sk_all, n_keys = _cat_keys(cache_sb_k[li].reshape(gb, past, -1), sk)
                    sv_all, _ = _cat_keys(cache_sb_v[li].reshape(gb, past, -1), sv)
                    dk_all, _ = _cat_keys(cache_diff_k[li].reshape(gb, past, -1), dk)
                    dv_all, _ = _cat_keys(cache_diff_v[li].reshape(gb, past, -1), dv)
                else:
                    (sk_all, n_keys), (sv_all, _) = _cat_keys(None, sk), _cat_keys(None, sv)
                    (dk_all, _), (dv_all, _) = _cat_keys(None, dk), _cat_keys(None, dv)
                c_out = _sb_attend(sq, sk_all, sv_all, n_keys, p0)
                d_out = _diff_attend(dq, dk_all, dv_all, diff_lambda[li], diff_subln[li], n_keys, p0, lam_init)
                n_sb = sk.shape[2] // HEAD_DIM
                n_df = dk.shape[2] // (2 * HEAD_DIM)
                o.setdefault("sb_k", []).append(sk.reshape(gb, gt, n_sb, HEAD_DIM))
                o.setdefault("sb_v", []).append(sv.reshape(gb, gt, n_sb, HEAD_DIM))
                o.setdefault("diff_k", []).append(dk.reshape(gb, gt, n_df, 2, HEAD_DIM))
                o.setdefault("diff_v", []).append(dv.reshape(gb, gt, n_df, 2 * HEAD_DIM))
                mix_a, mix_b = c_out, d_out
            h = _out_proj(h, mix_a.reshape(gb * gt, -1), mix_b.reshape(gb * gt, -1), w_out, tm)
            hs[gi] = _moe(h, norm_ffn[l], wr, br, w13, w2, norm_final, last, tm)

    names = ("pool", "dsa_k", "dsa_v", "idx_k", "sb_k", "sb_v", "diff_k", "diff_v")
    res = [hs[0].reshape(b, t, d), hs[1].reshape(bd, td, d)]
    for o in outs:
        res += [jnp.stack(o[nm]) for nm in names]
    return tuple(res)
```

```python
import functools
import math

import jax
import jax.numpy as jnp
import numpy as np
from jax import lax
from jax.experimental import pallas as pl
from jax.experimental.pallas import tpu as pltpu

F32 = jnp.float32
BF16 = jnp.bfloat16

LANES = 128
HEAD_DIM = 64
CHUNK = 64
ROPE_THETA = 10000.0
NORM_EPS = 1e-6
NEG_INF = -1e30
PAD_SCORE = -3e38
BIG_POS = 3e38
POOL_WINDOWS = (2, 4, 8, 16)
POOL_HIST = 15
POOL_HIST_PAD = 16
DSA_TOPK = 256
IDX_HEADS = 4
MOE_GROUPS = 4
MOE_EPG = 4
MOE_EXPERTS = 16
GATE_COL0 = MOE_GROUPS
VMEM_LIMIT = 56 * 1024 * 1024
BISECT_STEPS = 8
BISECT_ROUNDS = 48
ATTN_TQ = 256
MOE_TM = 1024


def _cparams(sem):
    return pltpu.CompilerParams(dimension_semantics=sem, vmem_limit_bytes=VMEM_LIMIT)


def _dot(a, b):
    return jnp.dot(a, b, preferred_element_type=F32)


def _dot_nt(a, b):
    return lax.dot_general(a, b, (((1,), (1,)), ((), ())), preferred_element_type=F32)


def _rms(x, g):
    ms = jnp.mean(x * x, axis=-1, keepdims=True)
    return x * lax.rsqrt(ms + NORM_EPS) * g


def _proj_kernel(x_ref, g_ref, w_ref, tab_ref, *out_refs, segs):
    xn = _rms(x_ref[...], g_ref[...]).astype(BF16)
    for o_ref, (c0, width, mode) in zip(out_refs, segs):
        y = _dot(xn, w_ref[:, c0:c0 + width])
        if mode is None:
            o_ref[...] = y
            continue
        t0 = 0 if mode == "full" else 3 * LANES
        cos = tab_ref[:, t0:t0 + LANES]
        sin_a = tab_ref[:, t0 + LANES:t0 + 2 * LANES]
        sin_b = tab_ref[:, t0 + 2 * LANES:t0 + 3 * LANES]
        for c in range(0, width, LANES):
            yc = y[:, c:c + LANES]
            o_ref[:, c:c + LANES] = (yc * cos + pltpu.roll(yc, LANES - HEAD_DIM // 2, 1) * sin_a
                                     + pltpu.roll(yc, HEAD_DIM // 2, 1) * sin_b)


def _rope_tables(pos):
    half = HEAD_DIM // 2
    inv = ROPE_THETA ** (-jnp.arange(half, dtype=F32) / half)
    ang = pos.astype(F32)[:, None] * inv[None, :]
    cos, sin = jnp.cos(ang), jnp.sin(ang)
    zero, one = jnp.zeros_like(sin), jnp.ones_like(cos)
    cos_h = jnp.concatenate([cos, cos], axis=1)
    sa_h = jnp.concatenate([-sin, zero], axis=1)
    sb_h = jnp.concatenate([zero, sin], axis=1)
    one_h = jnp.concatenate([one, one], axis=1)
    zero_h = jnp.concatenate([zero, zero], axis=1)
    return jnp.concatenate([cos_h, cos_h, sa_h, sa_h, sb_h, sb_h,
                            cos_h, one_h, sa_h, zero_h, sb_h, zero_h], axis=1)


def _project(x, g, w, tab, segs, tm):
    n, d = x.shape
    tt = tab.shape[0]
    nt = tt // tm
    kern = functools.partial(_proj_kernel, segs=segs)
    return pl.pallas_call(
        kern,
        grid=(n // tm,),
        in_specs=[pl.BlockSpec((tm, d), lambda i: (i, 0)),
                  pl.BlockSpec((1, d), lambda i: (0, 0)),
                  pl.BlockSpec(w.shape, lambda i: (0, 0)),
                  pl.BlockSpec((tm, tab.shape[1]), lambda i: (i % nt, 0))],
        out_specs=[pl.BlockSpec((tm, wd), lambda i: (i, 0)) for _, wd, _ in segs],
        out_shape=[jax.ShapeDtypeStruct((n, wd), F32) for _, wd, _ in segs],
        compiler_params=_cparams(("parallel",)),
        name="proj",
    )(x, g.reshape(1, d), w, tab)


def _pool_kernel(u_ref, h_ref, w_ref, s_ref, o_ref, ext_ref, *, t, pos0, rc):
    ext_ref[0:POOL_HIST_PAD, :] = h_ref[0]
    ext_ref[POOL_HIST_PAD:POOL_HIST_PAD + t, :] = u_ref[0]
    for r0 in range(0, t, rc):
        pos = pos0 + r0 + lax.broadcasted_iota(jnp.int32, (rc, 1), 0)
        for g, win in enumerate(POOL_WINDOWS):
            c0 = g * LANES
            u_new = ext_ref[POOL_HIST_PAD + r0:POOL_HIST_PAD + r0 + rc, c0:c0 + LANES]
            s = u_new
            for k in range(1, win):
                s = s + ext_ref[POOL_HIST_PAD + r0 - k:POOL_HIST_PAD + r0 - k + rc, c0:c0 + LANES]
            cnt = jnp.minimum(pos + 1, win).astype(F32)
            dlt = (s / cnt - u_new).astype(BF16)
            o_ref[0, r0:r0 + rc, c0:c0 + LANES] = _dot(dlt, w_ref[g]) * s_ref[:, c0:c0 + LANES]


def _pool_mix(u, hist, w_pool, pool_scale, pos0):
    b, t, c = u.shape
    rc = min(t, 256)
    hist16 = jnp.pad(hist, ((0, 0), (POOL_HIST_PAD - POOL_HIST, 0), (0, 0)))
    kern = functools.partial(_pool_kernel, t=t, pos0=pos0, rc=rc)
    return pl.pallas_call(
        kern,
        grid=(b,),
        in_specs=[pl.BlockSpec((1, t, c), lambda i: (i, 0, 0)),
                  pl.BlockSpec((1, POOL_HIST_PAD, c), lambda i: (i, 0, 0)),
                  pl.BlockSpec(w_pool.shape, lambda i: (0, 0, 0)),
                  pl.BlockSpec((1, c), lambda i: (0, 0))],
        out_specs=pl.BlockSpec((1, t, c), lambda i: (i, 0, 0)),
        out_shape=jax.ShapeDtypeStruct((b, t, c), F32),
        scratch_shapes=[pltpu.VMEM((POOL_HIST_PAD + t, c), F32)],
        compiler_params=_cparams(("parallel",)),
        name="pool_mix",
    )(u, hist16, w_pool, pool_scale.reshape(1, c))


def _dsa_kernel(q_ref, qi_ref, kw_ref, k_ref, v_ref, ki_ref, o_ref, lo_ref, hi_ref,
                *, tq, extents, n_keys, q_pos0, n_sel):
    i = pl.program_id(1)
    qpos = q_pos0 + i * tq + lax.broadcasted_iota(jnp.int32, (tq, 1), 0)
    lim = jnp.minimum((qpos // CHUNK + 1) * CHUNK, n_keys)
    kf = float(n_sel)
    low = lax.broadcasted_iota(jnp.int32, (tq, LANES), 1) < HEAD_DIM
    q = q_ref[0] * HEAD_DIM ** -0.5
    qi = qi_ref[0].astype(BF16)
    wi = kw_ref[0][:, HEAD_DIM:HEAD_DIM + IDX_HEADS] * (IDX_HEADS * HEAD_DIM) ** -0.5

    def body(ext):
        kpos = lax.broadcasted_iota(jnp.int32, (tq, ext), 1)
        adm = kpos < lim
        padded = ext > n_keys
        virt = float(max(n_keys - ext, 0))

        sidx = jnp.zeros((tq, ext), F32)
        ki = ki_ref[0, :ext, :]
        for h in range(IDX_HEADS):
            sh = _dot_nt(qi[:, h * HEAD_DIM:(h + 1) * HEAD_DIM], ki)
            sidx = sidx + jnp.maximum(sh, 0.0) * wi[:, h:h + 1]
        sm = jnp.where(adm, sidx, NEG_INF)
        if padded:
            real = kpos < n_keys
            sm = jnp.where(real, sm, PAD_SCORE)

        def count_gt(x):
            cnt = jnp.sum(jnp.where(sm > x, 1.0, 0.0), axis=1, keepdims=True)
            return cnt + jnp.where(x < NEG_INF, virt, 0.0) if virt else cnt

        def bracket(lo, hi):
            above = jnp.min(jnp.where(sm > lo, sm, BIG_POS), axis=1, keepdims=True)
            below = jnp.max(jnp.where(sm <= hi, sm, PAD_SCORE), axis=1, keepdims=True)
            if virt:
                above = jnp.minimum(above, jnp.where(lo < NEG_INF, NEG_INF, BIG_POS))
                below = jnp.maximum(below, jnp.where(hi >= NEG_INF, NEG_INF, PAD_SCORE))
            return above, below

        row_max = jnp.max(sm, axis=1, keepdims=True)
        row_min = jnp.min(jnp.where(real, sm, BIG_POS) if padded else sm, axis=1, keepdims=True)
        if virt:
            row_min = jnp.minimum(row_min, NEG_INF)
        adm_min = jnp.min(jnp.where(adm, sm, BIG_POS), axis=1, keepdims=True)
        few = count_gt(row_min) < kf
        tight = count_gt(adm_min) >= kf
        lo_ref[...] = jnp.where(few, PAD_SCORE, jnp.where(tight, adm_min, row_min))
        hi_ref[...] = jnp.where(few, row_min, jnp.where(tight, row_max, adm_min))

        def unresolved(lo, hi):
            above, below = bracket(lo, hi)
            return jnp.sum(jnp.where(above < below, 1, 0))

        def cond(carry):
            rounds, open_rows = carry
            return jnp.logical_and(open_rows > 0, rounds < BISECT_ROUNDS)

        def step(carry):
            rounds, _ = carry
            lo, hi = lo_ref[...], hi_ref[...]
            for _ in range(BISECT_STEPS):
                mid = 0.5 * lo + 0.5 * hi
                under = count_gt(mid) < kf
                hi = jnp.where(under, mid, hi)
                lo = jnp.where(under, lo, mid)
            lo_ref[...] = lo
            hi_ref[...] = hi
            return rounds + 1, unresolved(lo, hi)

        lax.while_loop(cond, step, (jnp.int32(0), unresolved(lo_ref[...], hi_ref[...])))
        _, thr = bracket(lo_ref[...], hi_ref[...])

        gt = sm > thr
        eq = sm == thr
        need = kf - count_gt(thr)
        rr = lax.broadcasted_iota(jnp.int32, (LANES, LANES), 0)
        cc = lax.broadcasted_iota(jnp.int32, (LANES, LANES), 1)
        prefix_ones = jnp.where(rr <= cc, 1.0, 0.0).astype(BF16)
        carry = jnp.zeros((tq, 1), F32)
        parts = []
        for j in range(ext // LANES):
            sl = slice(j * LANES, (j + 1) * LANES)
            eq_j = eq[:, sl]
            rank = _dot(jnp.where(eq_j, 1.0, 0.0).astype(BF16), prefix_ones) + carry
            parts.append(jnp.logical_or(gt[:, sl], jnp.logical_and(eq_j, rank <= need)))
            carry = rank[:, LANES - 1:LANES]
        mask = jnp.logical_and(jnp.concatenate(parts, axis=1), adm)
        mask2 = jnp.concatenate([mask, mask], axis=0)

        kk = k_ref[0, :ext, :]
        vv = v_ref[0, :ext, :]
        low_k = lax.broadcasted_iota(jnp.int32, (ext, LANES), 1) < HEAD_DIM
        vsw = pltpu.roll(vv.astype(F32), HEAD_DIM, 1).astype(BF16)
        one = jnp.ones_like(vv)
        n_kv = LANES // HEAD_DIM
        for g in range(n_kv):
            v_lo = jnp.where(low_k, vv if g == 0 else vsw, one)
            v_hi = jnp.where(low_k, one, vsw if g == 0 else vv)
            stacks = []
            for odd in range(2):
                rows = []
                for m in range(2):
                    c = 2 * g + m
                    qc = q[:, c * LANES:(c + 1) * LANES]
                    if (odd == 1) != (g == 1):
                        qc = pltpu.roll(qc, HEAD_DIM, 1)
                    rows.append(jnp.where(low, qc, 0.0) if g == 0 else jnp.where(low, 0.0, qc))
                qs = jnp.concatenate(rows, axis=0).astype(BF16)
                sc = jnp.where(mask2, _dot_nt(qs, kk), NEG_INF)
                p = jnp.exp(sc - jnp.max(sc, axis=1, keepdims=True))
                og = _dot(p.astype(BF16), v_hi if odd else v_lo)
                stacks.append(og / pltpu.roll(og, HEAD_DIM, 1))
            for m in range(2):
                c = 2 * g + m
                o_ref[0, :, c * LANES:(c + 1) * LANES] = jnp.where(
                    low, stacks[0][m * tq:(m + 1) * tq], stacks[1][m * tq:(m + 1) * tq])

    _for_tile_extent(i, tq, q_pos0, n_keys, extents, body)


def _dsa(q, qi, kiwi, k_all, v_all, ki_all, n_keys, q_pos0):
    b, t, dq = q.shape
    s_len = k_all.shape[1]
    tq = min(t, ATTN_TQ)
    n_sel = min(DSA_TOPK, n_keys // 4)
    extents = _key_extents(t // tq, tq, q_pos0, n_keys, s_len, 4 * LANES)
    kern = functools.partial(_dsa_kernel, tq=tq, extents=extents, n_keys=n_keys, q_pos0=q_pos0, n_sel=n_sel)
    qspec = lambda w: pl.BlockSpec((1, tq, w), lambda bi, i: (bi, i, 0))
    kspec = lambda w: pl.BlockSpec((1, s_len, w), lambda bi, i: (bi, 0, 0))
    return pl.pallas_call(
        kern,
        grid=(b, t // tq),
        in_specs=[qspec(dq), qspec(qi.shape[2]), qspec(kiwi.shape[2]),
                  kspec(k_all.shape[2]), kspec(v_all.shape[2]), kspec(ki_all.shape[2])],
        out_specs=qspec(dq),
        out_shape=jax.ShapeDtypeStruct((b, t, dq), F32),
        scratch_shapes=[pltpu.VMEM((tq, 1), F32), pltpu.VMEM((tq, 1), F32)],
        compiler_params=_cparams(("parallel", "parallel")),
        name="dsa",
    )(q, qi, kiwi, k_all, v_all, ki_all)


def _sb_kernel(q_ref, k_ref, v_ref, o_ref, acc_ref, run_ref, *, tq, n_keys, q_pos0):
    i = pl.program_id(1)
    pairs = q_ref.shape[2] // LANES
    first_q = q_pos0 + i * tq
    qpos = first_q + lax.broadcasted_iota(jnp.int32, (tq, 1), 0)
    n_blocks = (jnp.minimum(first_q + tq - 1, n_keys) + LANES - 1) // LANES
    n_full = jnp.minimum(first_q, n_keys) // LANES
    low_q = lax.broadcasted_iota(jnp.int32, (tq, LANES), 1) < HEAD_DIM
    low_k = lax.broadcasted_iota(jnp.int32, (LANES, LANES), 1) < HEAD_DIM
    q = q_ref[0] * HEAD_DIM ** -0.5
    qm = []
    for p in range(pairs):
        qp = q[:, p * LANES:(p + 1) * LANES]
        qm.append((jnp.where(low_q, qp, 0.0).astype(BF16), jnp.where(low_q, 0.0, qp).astype(BF16)))
    rr = lax.broadcasted_iota(jnp.int32, (2 * LANES, 2 * LANES), 0)
    cc = lax.broadcasted_iota(jnp.int32, (2 * LANES, 2 * LANES), 1)
    rk = jnp.where(rr >= LANES, rr - LANES, rr)
    cs_rhs = jnp.where(jnp.logical_or(cc >= LANES, rk >= cc), 1.0, 0.0).astype(BF16)
    acc_ref[...] = jnp.zeros_like(acc_ref)
    run_ref[...] = jnp.zeros_like(run_ref)

    def block(j, masked):
        ks = pl.multiple_of(j * LANES, LANES)
        if masked:
            kpos = ks + lax.broadcasted_iota(jnp.int32, (1, LANES), 1)
            causal = jnp.logical_and(kpos < qpos, kpos < n_keys)
        for p in range(pairs):
            ps = slice(p * LANES, (p + 1) * LANES)
            kb = k_ref[0, pl.ds(ks, LANES), ps]
            vb = v_ref[0, pl.ds(ks, LANES), ps]
            zero = jnp.zeros_like(vb)
            v_cat = jnp.concatenate([jnp.where(low_k, vb, zero), jnp.where(low_k, zero, vb)], axis=0)
            a_parts = []
            for c in range(2):
                hsl = slice((2 * p + c) * LANES, (2 * p + c + 1) * LANES)
                z = _dot_nt(qm[p][c], kb)
                sp = jnp.maximum(z, 0.0) + jnp.log(1.0 + jnp.exp(-jnp.abs(z)))
                if masked:
                    sp = jnp.where(causal, sp, 0.0)
                hi = sp.astype(BF16)
                lo = (sp - hi.astype(F32)).astype(BF16)
                cs = _dot(jnp.concatenate([hi, lo], axis=1), cs_rhs)
                run = run_ref[:, hsl]
                a = jnp.exp(z - cs[:, :LANES] - run)
                if masked:
                    a = jnp.where(causal, a, 0.0)
                run_ref[:, hsl] = run + cs[:, LANES:]
                a_parts.append(a.astype(BF16))
            acc_ref[:, ps] += _dot(jnp.concatenate(a_parts, axis=1), v_cat)

    def masked_step(jj, carry):
        block(n_blocks - 1 - jj, True)
        return carry

    def full_step(jj, carry):
        block(n_full - 1 - jj, False)
        return carry

    lax.fori_loop(0, n_blocks - n_full, masked_step, 0)
    lax.fori_loop(0, n_full, full_step, 0)
    o_ref[0] = acc_ref[...]


def _sb_attend(q, k_all, v_all, n_keys, q_pos0):
    b, t, d = q.shape
    s_len = k_all.shape[1]
    tq = min(t, ATTN_TQ)
    kern = functools.partial(_sb_kernel, tq=tq, n_keys=n_keys, q_pos0=q_pos0)
    return pl.pallas_call(
        kern,
        grid=(b, t // tq),
        in_specs=[pl.BlockSpec((1, tq, d), lambda bi, i: (bi, i, 0)),
                  pl.BlockSpec((1, s_len, d), lambda bi, i: (bi, 0, 0)),
                  pl.BlockSpec((1, s_len, d), lambda bi, i: (bi, 0, 0))],
        out_specs=pl.BlockSpec((1, tq, d), lambda bi, i: (bi, i, 0)),
        out_shape=jax.ShapeDtypeStruct((b, t, d), F32),
        scratch_shapes=[pltpu.VMEM((tq, d), F32), pltpu.VMEM((tq, 2 * d), F32)],
        compiler_params=_cparams(("parallel", "parallel")),
        name="sb_attend",
    )(q, k_all, v_all)


def _key_extents(nq, tq, q_pos0, n_keys, s_len, step):
    need = [min(n_keys, ((q_pos0 + (i + 1) * tq - 1) // CHUNK + 1) * CHUNK) for i in range(nq)]
    return tuple(sorted({min(s_len, -(-n // step) * step) for n in need}))


def _for_tile_extent(i, tq, q_pos0, n_keys, extents, body):
    need = jnp.minimum(n_keys, ((q_pos0 + (i + 1) * tq - 1) // CHUNK + 1) * CHUNK)
    prev = 0
    for ext in extents:
        pl.when(jnp.logical_and(need > prev, need <= ext))(functools.partial(body, ext))
        prev = ext


def _diff_kernel(q_ref, k_ref, v_ref, lam_ref, gain_ref, o_ref, *, tq, extents, n_keys, q_pos0, lam_init):
    i = pl.program_id(1)
    lp = lam_ref[...]
    lam = (jnp.exp(jnp.sum(lp[0:1] * lp[1:2], axis=1, keepdims=True))
           - jnp.exp(jnp.sum(lp[2:3] * lp[3:4], axis=1, keepdims=True)) + lam_init)
    qpos = q_pos0 + i * tq + lax.broadcasted_iota(jnp.int32, (tq, 1), 0)
    lim = jnp.minimum((qpos // CHUNK + 1) * CHUNK, n_keys)
    low = lax.broadcasted_iota(jnp.int32, (tq, LANES), 1) < HEAD_DIM
    heads = q_ref.shape[2] // LANES

    def body(ext):
        mask = lax.broadcasted_iota(jnp.int32, (tq, ext), 1) < lim
        for h in range(heads):
            hs = slice(h * LANES, (h + 1) * LANES)
            q = q_ref[0, :, hs] * HEAD_DIM ** -0.5
            kk = k_ref[0, :ext, hs]
            probs = []
            for c in range(2):
                qc = (jnp.where(low, q, 0.0) if c == 0 else jnp.where(low, 0.0, q)).astype(BF16)
                sc = jnp.where(mask, _dot_nt(qc, kk), NEG_INF)
                p = jnp.exp(sc - jnp.max(sc, axis=1, keepdims=True))
                probs.append(p * (1.0 / jnp.sum(p, axis=1, keepdims=True)))
            a = probs[0] - lam * probs[1]
            o = _dot(a.astype(BF16), v_ref[0, :ext, hs])
            o = o * lax.rsqrt(jnp.mean(o * o, axis=-1, keepdims=True) + NORM_EPS)
            o_ref[0, :, hs] = o * gain_ref[...] * (1.0 - lam_init)

    _for_tile_extent(i, tq, q_pos0, n_keys, extents, body)


def _diff_attend(q, k_all, v_all, diff_lambda, gain, n_keys, q_pos0, lam_init):
    b, t, d = q.shape
    s_len = k_all.shape[1]
    tq = min(t, ATTN_TQ)
    extents = _key_extents(t // tq, tq, q_pos0, n_keys, s_len, 2 * LANES)
    kern = functools.partial(_diff_kernel, tq=tq, extents=extents, n_keys=n_keys, q_pos0=q_pos0, lam_init=lam_init)
    return pl.pallas_call(
        kern,
        grid=(b, t // tq),
        in_specs=[pl.BlockSpec((1, tq, d), lambda bi, i: (bi, i, 0)),
                  pl.BlockSpec((1, s_len, d), lambda bi, i: (bi, 0, 0)),
                  pl.BlockSpec((1, s_len, d), lambda bi, i: (bi, 0, 0)),
                  pl.BlockSpec(diff_lambda.shape, lambda bi, i: (0, 0)),
                  pl.BlockSpec((1, LANES), lambda bi, i: (0, 0))],
        out_specs=pl.BlockSpec((1, tq, d), lambda bi, i: (bi, i, 0)),
        out_shape=jax.ShapeDtypeStruct((b, t, d), F32),
        compiler_params=_cparams(("parallel", "parallel")),
        name="diff_attend",
    )(q, k_all, v_all, diff_lambda, gain.reshape(1, LANES))


def _out_kernel(h_ref, a_ref, b_ref, wa_ref, wb_ref, o_ref):
    o_ref[...] = (h_ref[...] + _dot(a_ref[...].astype(BF16), wa_ref[...])
                  + _dot(b_ref[...].astype(BF16), wb_ref[...]))


def _out_proj(h, a, bmix, w_out, tm):
    n, d = h.shape
    ca = a.shape[1]
    wa, wb = w_out[:ca], w_out[ca:]
    row = lambda w: pl.BlockSpec((tm, w), lambda i: (i, 0))
    return pl.pallas_call(
        _out_kernel,
        grid=(n // tm,),
        in_specs=[row(d), row(ca), row(bmix.shape[1]),
                  pl.BlockSpec(wa.shape, lambda i: (0, 0)), pl.BlockSpec(wb.shape, lambda i: (0, 0))],
        out_specs=row(d),
        out_shape=jax.ShapeDtypeStruct((n, d), F32),
        compiler_params=_cparams(("parallel",)),
        name="out_proj",
    )(h, a, bmix, wa, wb)


def _router_kernel(h_ref, g_ref, w_ref, b_ref, xn_ref, gate_ref):
    xn = _rms(h_ref[...], g_ref[...]).astype(BF16)
    xn_ref[...] = xn
    logits = _dot(xn, w_ref[...]) + b_ref[...]
    lane = lax.broadcasted_iota(jnp.int32, logits.shape, 1)
    is_group = lane < MOE_GROUPS
    gl = jnp.where(is_group, logits, NEG_INF)
    g_max = jnp.max(gl, axis=1, keepdims=True)
    g_sel = jnp.min(jnp.where(gl == g_max, lane, LANES), axis=1, keepdims=True)
    g_gate = 1.0 / jnp.sum(jnp.where(is_group, jnp.exp(gl - g_max), 0.0), axis=1, keepdims=True)
    in_group = jnp.logical_and(lane >= GATE_COL0, (lane - GATE_COL0) // MOE_EPG == g_sel)
    in_group = jnp.logical_and(in_group, lane < GATE_COL0 + MOE_EXPERTS)
    el = jnp.where(in_group, logits, NEG_INF)
    top1 = jnp.max(el, axis=1, keepdims=True)
    i1 = jnp.min(jnp.where(jnp.logical_and(in_group, el == top1), lane, LANES), axis=1, keepdims=True)
    rest = jnp.logical_and(in_group, lane != i1)
    el2 = jnp.where(rest, logits, NEG_INF)
    top2 = jnp.max(el2, axis=1, keepdims=True)
    i2 = jnp.min(jnp.where(jnp.logical_and(rest, el2 == top2), lane, LANES), axis=1, keepdims=True)
    e2 = jnp.exp(top2 - top1)
    w1 = g_gate / (1.0 + e2)
    gate_ref[...] = jnp.where(lane == i1, w1, jnp.where(lane == i2, w1 * e2, 0.0))


def _expert_kernel(xn_ref, gate_ref, w13_ref, w2_ref, h_ref, gf_ref, o_ref, acc_ref, *, final_norm):
    e = pl.program_id(1)

    @pl.when(e == 0)
    def _():
        acc_ref[...] = jnp.zeros_like(acc_ref)

    gate = gate_ref[...]
    lane = lax.broadcasted_iota(jnp.int32, gate.shape, 1)
    ge = jnp.sum(jnp.where(lane == GATE_COL0 + e, gate, 0.0), axis=1, keepdims=True)
    up = _dot(xn_ref[...], w13_ref[0])
    hid = w13_ref.shape[2] // 2
    a, b = up[:, :hid], up[:, hid:]
    act = (a * (1.0 / (1.0 + jnp.exp(-a))) * b).astype(BF16)
    acc_ref[...] += ge * _dot(act, w2_ref[0])

    @pl.when(e == pl.num_programs(1) - 1)
    def _():
        y = h_ref[...] + acc_ref[...]
        o_ref[...] = _rms(y, gf_ref[...]) if final_norm else y


def _moe(h, g_ffn, wr, br, w13, w2, g_final, final_norm, tm):
    n, d = h.shape
    row = lambda w: pl.BlockSpec((tm, w), lambda i: (i, 0))
    xn, gate = pl.pallas_call(
        _router_kernel,
        grid=(n // tm,),
        in_specs=[row(d), pl.BlockSpec((1, d), lambda i: (0, 0)),
                  pl.BlockSpec(wr.shape, lambda i: (0, 0)), pl.BlockSpec((1, LANES), lambda i: (0, 0))],
        out_specs=[row(d), row(LANES)],
        out_shape=[jax.ShapeDtypeStruct((n, d), BF16), jax.ShapeDtypeStruct((n, LANES), F32)],
        compiler_params=_cparams(("parallel",)),
        name="moe_router",
    )(h, g_ffn.reshape(1, d), wr, br)
    n_e = w13.shape[0]
    tme = MOE_TM if n % MOE_TM == 0 else tm
    row2 = lambda w: pl.BlockSpec((tme, w), lambda i, e: (i, 0))
    kern = functools.partial(_expert_kernel, final_norm=final_norm)
    return pl.pallas_call(
        kern,
        grid=(n // tme, n_e),
        in_specs=[row2(d), row2(LANES),
                  pl.BlockSpec((1,) + w13.shape[1:], lambda i, e: (e, 0, 0)),
                  pl.BlockSpec((1,) + w2.shape[1:], lambda i, e: (e, 0, 0)),
                  row2(d), pl.BlockSpec((1, d), lambda i, e: (0, 0))],
        out_specs=row2(d),
        out_shape=jax.ShapeDtypeStruct((n, d), F32),
        scratch_shapes=[pltpu.VMEM((tme, d), F32)],
        compiler_params=_cparams(("parallel", "arbitrary")),
        name="moe_experts",
    )(xn, gate, w13, w2, h, g_final.reshape(1, d))


EVEN_SEGS = ((0, 512, None), (512, 512, "full"), (1024, 128, "full"), (1152, 128, None),
             (1280, 256, "full"), (1536, 128, "half"))
ODD_SEGS = ((0, 512, None), (512, 512, None), (1024, 512, None),
            (1536, 512, "full"), (2048, 512, "full"), (2560, 512, None))


def _cat_keys(hist, new):
    allk = jnp.concatenate([hist, new], axis=1) if hist is not None else new
    n_keys = allk.shape[1]
    pad = -n_keys % LANES
    if pad:
        allk = jnp.pad(allk, ((0, 0), (0, pad), (0, 0)))
    return allk.astype(BF16), n_keys


def kernel(x_prompt, x_sample, cache_pool, cache_dsa_k, cache_dsa_v, cache_idx_k, cache_sb_k, cache_sb_v,
           cache_diff_k, cache_diff_v, norm_mix, norm_ffn, norm_final, w_in_even, w_pool, pool_scale,
           w_out_even, w_in_odd, diff_lambda, diff_subln, w_out_odd, moe_w_group, moe_b_group,
           moe_w_expert, moe_b_expert, moe_w1, moe_w3, moe_w2):
    b, t, d = x_prompt.shape
    bd, td, _ = x_sample.shape
    past = cache_dsa_k.shape[2]
    depth = norm_mix.shape[0]
    groups = ((b, t, 0, min(512, b * t)), (bd, td, past, bd * td))

    tabs = []
    for (gb, gt, p0, tm) in groups:
        tab = _rope_tables(p0 + jnp.arange(gt, dtype=jnp.int32))
        if tm > gt:
            tab = jnp.tile(tab, (tm // gt, 1))
        tabs.append(tab)

    hs = [x_prompt.reshape(b * t, d), x_sample.reshape(bd * td, d)]
    outs = [dict(), dict()]
    for l in range(depth):
        li = l // 2
        last = l == depth - 1
        if l % 2 == 0:
            n_in = w_in_even.shape[2]
            w_in = jnp.pad(w_in_even[li], ((0, 0), (0, -n_in % LANES))).astype(BF16)
            w_out = w_out_even[li].astype(BF16)
            wp = w_pool[li].astype(BF16)
        else:
            w_in = w_in_odd[li].astype(BF16)
            w_out = w_out_odd[li].astype(BF16)
            lam_init = 0.8 - 0.6 * math.exp(-0.3 * l)
        wr = jnp.concatenate([moe_w_group[l]] + [moe_w_expert[l, g] for g in range(MOE_GROUPS)], axis=1)
        wr = jnp.pad(wr, ((0, 0), (0, LANES - wr.shape[1]))).astype(BF16)
        br = jnp.concatenate([moe_b_group[l], moe_b_expert[l].reshape(-1)])
        br = jnp.pad(br, (0, LANES - br.shape[0])).reshape(1, LANES).astype(F32)
        w13 = jnp.concatenate([moe_w1[l], moe_w3[l]], axis=2).astype(BF16)
        w2 = moe_w2[l].astype(BF16)

        for gi, (gb, gt, p0, tm) in enumerate(groups):
            h = hs[gi]
            o = outs[gi]
            sample = gi == 1
            r3 = lambda x: x.reshape(gb, gt, x.shape[-1])
            if l % 2 == 0:
                u, q, k, v, qi, kiwi = [r3(x) for x in _project(h, norm_mix[l], w_in, tabs[gi], EVEN_SEGS, tm)]
                ki = kiwi[..., :HEAD_DIM]
                hist = cache_pool[li] if sample else jnp.zeros((gb, POOL_HIST, u.shape[2]), F32)
                a_out = _pool_mix(u, hist, wp, pool_scale[li], p0)
                if sample:
                    k_all, n_keys = _cat_keys(cache_dsa_k[li].reshape(gb, past, -1), k)
                    v_all, _ = _cat_keys(cache_dsa_v[li].reshape(gb, past, -1), v)
                    ki_all, _ = _cat_keys(cache_idx_k[li], ki)
                else:
                    (k_all, n_keys), (v_all, _), (ki_all, _) = _cat_keys(None, k), _cat_keys(None, v), _cat_keys(None, ki)
                b_out = _dsa(q, qi, kiwi, k_all, v_all, ki_all, n_keys, p0)
                o.setdefault("pool", []).append(jnp.concatenate([hist, u], axis=1)[:, -POOL_HIST:])
                o.setdefault("dsa_k", []).append(k.reshape(gb, gt, -1, HEAD_DIM))
                o.setdefault("dsa_v", []).append(v.reshape(gb, gt, -1, HEAD_DIM))
                o.setdefault("idx_k", []).append(ki)
                mix_a, mix_b = a_out, b_out
            else:
                sq, sk, sv, dq, dk, dv = [r3(x) for x in _project(h, norm_mix[l], w_in, tabs[gi], ODD_SEGS, tm)]
                if sample:
                    sk_all, n_keys = _cat_keys(cache_sb_k[li].reshape(gb, past, -1), sk)
                    sv_all, _ = _cat_keys(cache_sb_v[li].reshape(gb, past, -1), sv)
                    dk_all, _ = _cat_keys(cache_diff_k[li].reshape(gb, past, -1), dk)
                    dv_all, _ = _cat_keys(cache_diff_v[li].reshape(gb, past, -1), dv)
                else:
                    (sk_all, n_keys), (sv_all, _) = _cat_keys(None, sk), _cat_keys(None, sv)
                    (dk_all, _), (dv_all, _) = _cat_keys(None, dk), _cat_keys(None, dv)
                c_out = _sb_attend(sq, sk_all, sv_all, n_keys, p0)
                d_out = _diff_attend(dq, dk_all, dv_all, diff_lambda[li], diff_subln[li], n_keys, p0, lam_init)
                n_sb = sk.shape[2] // HEAD_DIM
                n_df = dk.shape[2] // (2 * HEAD_DIM)
                o.setdefault("sb_k", []).append(sk.reshape(gb, gt, n_sb, HEAD_DIM))
                o.setdefault("sb_v", []).append(sv.reshape(gb, gt, n_sb, HEAD_DIM))
                o.setdefault("diff_k", []).append(dk.reshape(gb, gt, n_df, 2, HEAD_DIM))
                o.setdefault("diff_v", []).append(dv.reshape(gb, gt, n_df, 2 * HEAD_DIM))
                mix_a, mix_b = c_out, d_out
            h = _out_proj(h, mix_a.reshape(gb * gt, -1), mix_b.reshape(gb * gt, -1), w_out, tm)
            hs[gi] = _moe(h, norm_ffn[l], wr, br, w13, w2, norm_final, last, tm)

    names = ("pool", "dsa_k", "dsa_v", "idx_k", "sb_k", "sb_v", "diff_k", "diff_v")
    res = [hs[0].reshape(b, t, d), hs[1].reshape(bd, td, d)]
    for o in outs:
        res += [jnp.stack(o[nm]) for nm in names]
    return tuple(res)
```

```python
import functools
import math

import jax
import jax.numpy as jnp
import numpy as np
from jax import lax
from jax.experimental import pallas as pl
from jax.experimental.pallas import tpu as pltpu

F32 = jnp.float32
BF16 = jnp.bfloat16

LANES = 128
HEAD_DIM = 64
CHUNK = 64
ROPE_THETA = 10000.0
NORM_EPS = 1e-6
NEG_INF = -1e30
PAD_SCORE = -3e38
BIG_POS = 3e38
POOL_WINDOWS = (2, 4, 8, 16)
POOL_HIST = 15
POOL_HIST_PAD = 16
DSA_TOPK = 256
IDX_HEADS = 4
MOE_GROUPS = 4
MOE_EPG = 4
MOE_EXPERTS = 16
GATE_COL0 = MOE_GROUPS
VMEM_LIMIT = 56 * 1024 * 1024
BISECT_STEPS = 8
BISECT_ROUNDS = 48
ATTN_TQ = 256
DSA_TQ = 128
MOE_TM = 1024


def _cparams(sem):
    return pltpu.CompilerParams(dimension_semantics=sem, vmem_limit_bytes=VMEM_LIMIT)


def _dot(a, b):
    return jnp.dot(a, b, preferred_element_type=F32)


def _dot_nt(a, b):
    return lax.dot_general(a, b, (((1,), (1,)), ((), ())), preferred_element_type=F32)


def _rms(x, g):
    ms = jnp.mean(x * x, axis=-1, keepdims=True)
    return x * lax.rsqrt(ms + NORM_EPS) * g


def _proj_kernel(x_ref, g_ref, w_ref, tab_ref, *out_refs, segs):
    xn = _rms(x_ref[...], g_ref[...]).astype(BF16)
    for o_ref, (c0, width, mode) in zip(out_refs, segs):
        y = _dot(xn, w_ref[:, c0:c0 + width])
        if mode is None:
            o_ref[...] = y
            continue
        t0 = 0 if mode == "full" else 3 * LANES
        cos = tab_ref[:, t0:t0 + LANES]
        sin_a = tab_ref[:, t0 + LANES:t0 + 2 * LANES]
        sin_b = tab_ref[:, t0 + 2 * LANES:t0 + 3 * LANES]
        for c in range(0, width, LANES):
            yc = y[:, c:c + LANES]
            o_ref[:, c:c + LANES] = (yc * cos + pltpu.roll(yc, LANES - HEAD_DIM // 2, 1) * sin_a
                                     + pltpu.roll(yc, HEAD_DIM // 2, 1) * sin_b)


def _rope_tables(pos):
    half = HEAD_DIM // 2
    inv = ROPE_THETA ** (-jnp.arange(half, dtype=F32) / half)
    ang = pos.astype(F32)[:, None] * inv[None, :]
    cos, sin = jnp.cos(ang), jnp.sin(ang)
    zero, one = jnp.zeros_like(sin), jnp.ones_like(cos)
    cos_h = jnp.concatenate([cos, cos], axis=1)
    sa_h = jnp.concatenate([-sin, zero], axis=1)
    sb_h = jnp.concatenate([zero, sin], axis=1)
    one_h = jnp.concatenate([one, one], axis=1)
    zero_h = jnp.concatenate([zero, zero], axis=1)
    return jnp.concatenate([cos_h, cos_h, sa_h, sa_h, sb_h, sb_h,
                            cos_h, one_h, sa_h, zero_h, sb_h, zero_h], axis=1)


def _project(x, g, w, tab, segs, tm):
    n, d = x.shape
    tt = tab.shape[0]
    nt = tt // tm
    kern = functools.partial(_proj_kernel, segs=segs)
    return pl.pallas_call(
        kern,
        grid=(n // tm,),
        in_specs=[pl.BlockSpec((tm, d), lambda i: (i, 0)),
                  pl.BlockSpec((1, d), lambda i: (0, 0)),
                  pl.BlockSpec(w.shape, lambda i: (0, 0)),
                  pl.BlockSpec((tm, tab.shape[1]), lambda i: (i % nt, 0))],
        out_specs=[pl.BlockSpec((tm, wd), lambda i: (i, 0)) for _, wd, _ in segs],
        out_shape=[jax.ShapeDtypeStruct((n, wd), F32) for _, wd, _ in segs],
        compiler_params=_cparams(("parallel",)),
        name="proj",
    )(x, g.reshape(1, d), w, tab)


def _pool_kernel(u_ref, h_ref, w_ref, s_ref, o_ref, ext_ref, *, t, pos0, rc):
    ext_ref[0:POOL_HIST_PAD, :] = h_ref[0]
    ext_ref[POOL_HIST_PAD:POOL_HIST_PAD + t, :] = u_ref[0]
    for r0 in range(0, t, rc):
        pos = pos0 + r0 + lax.broadcasted_iota(jnp.int32, (rc, 1), 0)
        for g, win in enumerate(POOL_WINDOWS):
            c0 = g * LANES
            u_new = ext_ref[POOL_HIST_PAD + r0:POOL_HIST_PAD + r0 + rc, c0:c0 + LANES]
            s = u_new
            for k in range(1, win):
                s = s + ext_ref[POOL_HIST_PAD + r0 - k:POOL_HIST_PAD + r0 - k + rc, c0:c0 + LANES]
            cnt = jnp.minimum(pos + 1, win).astype(F32)
            dlt = (s / cnt - u_new).astype(BF16)
            o_ref[0, r0:r0 + rc, c0:c0 + LANES] = _dot(dlt, w_ref[g]) * s_ref[:, c0:c0 + LANES]


def _pool_mix(u, hist, w_pool, pool_scale, pos0):
    b, t, c = u.shape
    rc = min(t, 256)
    hist16 = jnp.pad(hist, ((0, 0), (POOL_HIST_PAD - POOL_HIST, 0), (0, 0)))
    kern = functools.partial(_pool_kernel, t=t, pos0=pos0, rc=rc)
    return pl.pallas_call(
        kern,
        grid=(b,),
        in_specs=[pl.BlockSpec((1, t, c), lambda i: (i, 0, 0)),
                  pl.BlockSpec((1, POOL_HIST_PAD, c), lambda i: (i, 0, 0)),
                  pl.BlockSpec(w_pool.shape, lambda i: (0, 0, 0)),
                  pl.BlockSpec((1, c), lambda i: (0, 0))],
        out_specs=pl.BlockSpec((1, t, c), lambda i: (i, 0, 0)),
        out_shape=jax.ShapeDtypeStruct((b, t, c), F32),
        scratch_shapes=[pltpu.VMEM((POOL_HIST_PAD + t, c), F32)],
        compiler_params=_cparams(("parallel",)),
        name="pool_mix",
    )(u, hist16, w_pool, pool_scale.reshape(1, c))


def _dsa_kernel(q_ref, qi_ref, kw_ref, k_ref, v_ref, ki_ref, o_ref, lo_ref, hi_ref,
                *, tq, extents, n_keys, q_pos0, n_sel):
    i = pl.program_id(1)
    qpos = q_pos0 + i * tq + lax.broadcasted_iota(jnp.int32, (tq, 1), 0)
    lim = jnp.minimum((qpos // CHUNK + 1) * CHUNK, n_keys)
    kf = float(n_sel)
    low = lax.broadcasted_iota(jnp.int32, (tq, LANES), 1) < HEAD_DIM
    q = q_ref[0] * HEAD_DIM ** -0.5
    qi = qi_ref[0].astype(BF16)
    wi = kw_ref[0][:, HEAD_DIM:HEAD_DIM + IDX_HEADS] * (IDX_HEADS * HEAD_DIM) ** -0.5

    def body(ext):
        kpos = lax.broadcasted_iota(jnp.int32, (tq, ext), 1)
        adm = kpos < lim
        padded = ext > n_keys
        virt = float(max(n_keys - ext, 0))

        sidx = jnp.zeros((tq, ext), F32)
        ki = ki_ref[0, :ext, :]
        for h in range(IDX_HEADS):
            sh = _dot_nt(qi[:, h * HEAD_DIM:(h + 1) * HEAD_DIM], ki)
            sidx = sidx + jnp.maximum(sh, 0.0) * wi[:, h:h + 1]
        sm = jnp.where(adm, sidx, NEG_INF)
        if padded:
            real = kpos < n_keys
            sm = jnp.where(real, sm, PAD_SCORE)

        def count_gt(x):
            cnt = jnp.sum(jnp.where(sm > x, 1.0, 0.0), axis=1, keepdims=True)
            return cnt + jnp.where(x < NEG_INF, virt, 0.0) if virt else cnt

        def bracket(lo, hi):
            above = jnp.min(jnp.where(sm > lo, sm, BIG_POS), axis=1, keepdims=True)
            below = jnp.max(jnp.where(sm <= hi, sm, PAD_SCORE), axis=1, keepdims=True)
            if virt:
                above = jnp.minimum(above, jnp.where(lo < NEG_INF, NEG_INF, BIG_POS))
                below = jnp.maximum(below, jnp.where(hi >= NEG_INF, NEG_INF, PAD_SCORE))
            return above, below

        row_max = jnp.max(sm, axis=1, keepdims=True)
        row_min = jnp.min(jnp.where(real, sm, BIG_POS) if padded else sm, axis=1, keepdims=True)
        if virt:
            row_min = jnp.minimum(row_min, NEG_INF)
        adm_min = jnp.min(jnp.where(adm, sm, BIG_POS), axis=1, keepdims=True)
        few = count_gt(row_min) < kf
        tight = count_gt(adm_min) >= kf
        lo_ref[...] = jnp.where(few, PAD_SCORE, jnp.where(tight, adm_min, row_min))
        hi_ref[...] = jnp.where(few, row_min, jnp.where(tight, row_max, adm_min))

        def unresolved(lo, hi):
            above, below = bracket(lo, hi)
            return jnp.sum(jnp.where(above < below, 1, 0))

        def cond(carry):
            rounds, open_rows = carry
            return jnp.logical_and(open_rows > 0, rounds < BISECT_ROUNDS)

        def step(carry):
            rounds, _ = carry
            lo, hi = lo_ref[...], hi_ref[...]
            for _ in range(BISECT_STEPS):
                mid = 0.5 * lo + 0.5 * hi
                under = count_gt(mid) < kf
                hi = jnp.where(under, mid, hi)
                lo = jnp.where(under, lo, mid)
            lo_ref[...] = lo
            hi_ref[...] = hi
            return rounds + 1, unresolved(lo, hi)

        lax.while_loop(cond, step, (jnp.int32(0), unresolved(lo_ref[...], hi_ref[...])))
        _, thr = bracket(lo_ref[...], hi_ref[...])

        gt = sm > thr
        eq = sm == thr
        need = kf - count_gt(thr)
        rr = lax.broadcasted_iota(jnp.int32, (LANES, LANES), 0)
        cc = lax.broadcasted_iota(jnp.int32, (LANES, LANES), 1)
        prefix_ones = jnp.where(rr <= cc, 1.0, 0.0).astype(BF16)
        carry = jnp.zeros((tq, 1), F32)
        parts = []
        for j in range(ext // LANES):
            sl = slice(j * LANES, (j + 1) * LANES)
            eq_j = eq[:, sl]
            rank = _dot(jnp.where(eq_j, 1.0, 0.0).astype(BF16), prefix_ones) + carry
            parts.append(jnp.logical_or(gt[:, sl], jnp.logical_and(eq_j, rank <= need)))
            carry = rank[:, LANES - 1:LANES]
        mask = jnp.logical_and(jnp.concatenate(parts, axis=1), adm)
        mask2 = jnp.concatenate([mask, mask], axis=0)

        kk = k_ref[0, :ext, :]
        vv = v_ref[0, :ext, :]
        low_k = lax.broadcasted_iota(jnp.int32, (ext, LANES), 1) < HEAD_DIM
        vsw = pltpu.roll(vv.astype(F32), HEAD_DIM, 1).astype(BF16)
        one = jnp.ones_like(vv)
        n_kv = LANES // HEAD_DIM
        for g in range(n_kv):
            v_lo = jnp.where(low_k, vv if g == 0 else vsw, one)
            v_hi = jnp.where(low_k, one, vsw if g == 0 else vv)
            stacks = []
            for odd in range(2):
                rows = []
                for m in range(2):
                    c = 2 * g + m
                    qc = q[:, c * LANES:(c + 1) * LANES]
                    if (odd == 1) != (g == 1):
                        qc = pltpu.roll(qc, HEAD_DIM, 1)
                    rows.append(jnp.where(low, qc, 0.0) if g == 0 else jnp.where(low, 0.0, qc))
                qs = jnp.concatenate(rows, axis=0).astype(BF16)
                sc = jnp.where(mask2, _dot_nt(qs, kk), NEG_INF)
                p = jnp.exp(sc - jnp.max(sc, axis=1, keepdims=True))
                og = _dot(p.astype(BF16), v_hi if odd else v_lo)
                stacks.append(og / pltpu.roll(og, HEAD_DIM, 1))
            for m in range(2):
                c = 2 * g + m
                o_ref[0, :, c * LANES:(c + 1) * LANES] = jnp.where(
                    low, stacks[0][m * tq:(m + 1) * tq], stacks[1][m * tq:(m + 1) * tq])

    _for_tile_extent(i, tq, q_pos0, n_keys, extents, body)


def _dsa(q, qi, kiwi, k_all, v_all, ki_all, n_keys, q_pos0):
    b, t, dq = q.shape
    s_len = k_all.shape[1]
    tq = min(t, DSA_TQ)
    n_sel = min(DSA_TOPK, n_keys // 4)
    extents = _key_extents(t // tq, tq, q_pos0, n_keys, s_len, 4 * LANES)
    kern = functools.partial(_dsa_kernel, tq=tq, extents=extents, n_keys=n_keys, q_pos0=q_pos0, n_sel=n_sel)
    qspec = lambda w: pl.BlockSpec((1, tq, w), lambda bi, i: (bi, i, 0))
    kspec = lambda w: pl.BlockSpec((1, s_len, w), lambda bi, i: (bi, 0, 0))
    return pl.pallas_call(
        kern,
        grid=(b, t // tq),
        in_specs=[qspec(dq), qspec(qi.shape[2]), qspec(kiwi.shape[2]),
                  kspec(k_all.shape[2]), kspec(v_all.shape[2]), kspec(ki_all.shape[2])],
        out_specs=qspec(dq),
        out_shape=jax.ShapeDtypeStruct((b, t, dq), F32),
        scratch_shapes=[pltpu.VMEM((tq, 1), F32), pltpu.VMEM((tq, 1), F32)],
        compiler_params=_cparams(("parallel", "parallel")),
        name="dsa",
    )(q, qi, kiwi, k_all, v_all, ki_all)


def _sb_kernel(q_ref, k_ref, v_ref, o_ref, acc_ref, run_ref, *, tq, n_keys, q_pos0):
    i = pl.program_id(1)
    pairs = q_ref.shape[2] // LANES
    first_q = q_pos0 + i * tq
    qpos = first_q + lax.broadcasted_iota(jnp.int32, (tq, 1), 0)
    n_blocks = (jnp.minimum(first_q + tq - 1, n_keys) + LANES - 1) // LANES
    n_full = jnp.minimum(first_q, n_keys) // LANES
    low_q = lax.broadcasted_iota(jnp.int32, (tq, LANES), 1) < HEAD_DIM
    low_k = lax.broadcasted_iota(jnp.int32, (LANES, LANES), 1) < HEAD_DIM
    q = q_ref[0] * HEAD_DIM ** -0.5
    qm = []
    for p in range(pairs):
        qp = q[:, p * LANES:(p + 1) * LANES]
        qm.append((jnp.where(low_q, qp, 0.0).astype(BF16), jnp.where(low_q, 0.0, qp).astype(BF16)))
    rr = lax.broadcasted_iota(jnp.int32, (2 * LANES, 2 * LANES), 0)
    cc = lax.broadcasted_iota(jnp.int32, (2 * LANES, 2 * LANES), 1)
    rk = jnp.where(rr >= LANES, rr - LANES, rr)
    cs_rhs = jnp.where(jnp.logical_or(cc >= LANES, rk >= cc), 1.0, 0.0).astype(BF16)
    acc_ref[...] = jnp.zeros_like(acc_ref)
    run_ref[...] = jnp.zeros_like(run_ref)

    def block(j, masked):
        ks = pl.multiple_of(j * LANES, LANES)
        if masked:
            kpos = ks + lax.broadcasted_iota(jnp.int32, (1, LANES), 1)
            causal = jnp.logical_and(kpos < qpos, kpos < n_keys)
        for p in range(pairs):
            ps = slice(p * LANES, (p + 1) * LANES)
            kb = k_ref[0, pl.ds(ks, LANES), ps]
            vb = v_ref[0, pl.ds(ks, LANES), ps]
            zero = jnp.zeros_like(vb)
            v_cat = jnp.concatenate([jnp.where(low_k, vb, zero), jnp.where(low_k, zero, vb)], axis=0)
            a_parts = []
            for c in range(2):
                hsl = slice((2 * p + c) * LANES, (2 * p + c + 1) * LANES)
                z = _dot_nt(qm[p][c], kb)
                sp = jnp.maximum(z, 0.0) + jnp.log(1.0 + jnp.exp(-jnp.abs(z)))
                if masked:
                    sp = jnp.where(causal, sp, 0.0)
                hi = sp.astype(BF16)
                lo = (sp - hi.astype(F32)).astype(BF16)
                cs = _dot(jnp.concatenate([hi, lo], axis=1), cs_rhs)
                run = run_ref[:, hsl]
                a = jnp.exp(z - cs[:, :LANES] - run)
                if masked:
                    a = jnp.where(causal, a, 0.0)
                run_ref[:, hsl] = run + cs[:, LANES:]
                a_parts.append(a.astype(BF16))
            acc_ref[:, ps] += _dot(jnp.concatenate(a_parts, axis=1), v_cat)

    def masked_step(jj, carry):
        block(n_blocks - 1 - jj, True)
        return carry

    def full_step(jj, carry):
        block(n_full - 1 - jj, False)
        return carry

    lax.fori_loop(0, n_blocks - n_full, masked_step, 0)
    lax.fori_loop(0, n_full, full_step, 0)
    o_ref[0] = acc_ref[...]


def _sb_attend(q, k_all, v_all, n_keys, q_pos0):
    b, t, d = q.shape
    s_len = k_all.shape[1]
    tq = min(t, ATTN_TQ)
    kern = functools.partial(_sb_kernel, tq=tq, n_keys=n_keys, q_pos0=q_pos0)
    return pl.pallas_call(
        kern,
        grid=(b, t // tq),
        in_specs=[pl.BlockSpec((1, tq, d), lambda bi, i: (bi, i, 0)),
                  pl.BlockSpec((1, s_len, d), lambda bi, i: (bi, 0, 0)),
                  pl.BlockSpec((1, s_len, d), lambda bi, i: (bi, 0, 0))],
        out_specs=pl.BlockSpec((1, tq, d), lambda bi, i: (bi, i, 0)),
        out_shape=jax.ShapeDtypeStruct((b, t, d), F32),
        scratch_shapes=[pltpu.VMEM((tq, d), F32), pltpu.VMEM((tq, 2 * d), F32)],
        compiler_params=_cparams(("parallel", "parallel")),
        name="sb_attend",
    )(q, k_all, v_all)


def _key_extents(nq, tq, q_pos0, n_keys, s_len, step):
    need = [min(n_keys, ((q_pos0 + (i + 1) * tq - 1) // CHUNK + 1) * CHUNK) for i in range(nq)]
    return tuple(sorted({min(s_len, -(-n // step) * step) for n in need}))


def _for_tile_extent(i, tq, q_pos0, n_keys, extents, body):
    need = jnp.minimum(n_keys, ((q_pos0 + (i + 1) * tq - 1) // CHUNK + 1) * CHUNK)
    prev = 0
    for ext in extents:
        pl.when(jnp.logical_and(need > prev, need <= ext))(functools.partial(body, ext))
        prev = ext


def _diff_kernel(q_ref, k_ref, v_ref, lam_ref, gain_ref, o_ref, *, tq, extents, n_keys, q_pos0, lam_init):
    i = pl.program_id(2)
    lp = lam_ref[...]
    lam = (jnp.exp(jnp.sum(lp[0:1] * lp[1:2], axis=1, keepdims=True))
           - jnp.exp(jnp.sum(lp[2:3] * lp[3:4], axis=1, keepdims=True)) + lam_init)
    qpos = q_pos0 + i * tq + lax.broadcasted_iota(jnp.int32, (tq, 1), 0)
    lim = jnp.minimum((qpos // CHUNK + 1) * CHUNK, n_keys)
    low = lax.broadcasted_iota(jnp.int32, (tq, LANES), 1) < HEAD_DIM
    q = q_ref[0] * HEAD_DIM ** -0.5
    qm = (jnp.where(low, q, 0.0).astype(BF16), jnp.where(low, 0.0, q).astype(BF16))

    def body(ext):
        mask = lax.broadcasted_iota(jnp.int32, (tq, ext), 1) < lim
        kk = k_ref[0, :ext, :]
        probs = []
        for c in range(2):
            sc = jnp.where(mask, _dot_nt(qm[c], kk), NEG_INF)
            p = jnp.exp(sc - jnp.max(sc, axis=1, keepdims=True))
            probs.append(p * (1.0 / jnp.sum(p, axis=1, keepdims=True)))
        a = probs[0] - lam * probs[1]
        o = _dot(a.astype(BF16), v_ref[0, :ext, :])
        o = o * lax.rsqrt(jnp.mean(o * o, axis=-1, keepdims=True) + NORM_EPS) * gain_ref[...] * (1.0 - lam_init)
        o_ref[0] = o

    _for_tile_extent(i, tq, q_pos0, n_keys, extents, body)


def _diff_attend(q, k_all, v_all, diff_lambda, gain, n_keys, q_pos0, lam_init):
    b, t, d = q.shape
    s_len = k_all.shape[1]
    tq = min(t, ATTN_TQ)
    extents = _key_extents(t // tq, tq, q_pos0, n_keys, s_len, 2 * LANES)
    kern = functools.partial(_diff_kernel, tq=tq, extents=extents, n_keys=n_keys, q_pos0=q_pos0, lam_init=lam_init)
    return pl.pallas_call(
        kern,
        grid=(b, d // LANES, t // tq),
        in_specs=[pl.BlockSpec((1, tq, LANES), lambda bi, h, i: (bi, i, h)),
                  pl.BlockSpec((1, s_len, LANES), lambda bi, h, i: (bi, 0, h)),
                  pl.BlockSpec((1, s_len, LANES), lambda bi, h, i: (bi, 0, h)),
                  pl.BlockSpec(diff_lambda.shape, lambda bi, h, i: (0, 0)),
                  pl.BlockSpec((1, LANES), lambda bi, h, i: (0, 0))],
        out_specs=pl.BlockSpec((1, tq, LANES), lambda bi, h, i: (bi, i, h)),
        out_shape=jax.ShapeDtypeStruct((b, t, d), F32),
        compiler_params=_cparams(("parallel", "parallel", "parallel")),
        name="diff_attend",
    )(q, k_all, v_all, diff_lambda, gain.reshape(1, LANES))


def _out_kernel(h_ref, a_ref, b_ref, wa_ref, wb_ref, o_ref):
    o_ref[...] = (h_ref[...] + _dot(a_ref[...].astype(BF16), wa_ref[...])
                  + _dot(b_ref[...].astype(BF16), wb_ref[...]))


def _out_proj(h, a, bmix, w_out, tm):
    n, d = h.shape
    ca = a.shape[1]
    wa, wb = w_out[:ca], w_out[ca:]
    row = lambda w: pl.BlockSpec((tm, w), lambda i: (i, 0))
    return pl.pallas_call(
        _out_kernel,
        grid=(n // tm,),
        in_specs=[row(d), row(ca), row(bmix.shape[1]),
                  pl.BlockSpec(wa.shape, lambda i: (0, 0)), pl.BlockSpec(wb.shape, lambda i: (0, 0))],
        out_specs=row(d),
        out_shape=jax.ShapeDtypeStruct((n, d), F32),
        compiler_params=_cparams(("parallel",)),
        name="out_proj",
    )(h, a, bmix, wa, wb)


def _router_kernel(h_ref, g_ref, w_ref, b_ref, xn_ref, gate_ref):
    xn = _rms(h_ref[...], g_ref[...]).astype(BF16)
    xn_ref[...] = xn
    logits = _dot(xn, w_ref[...]) + b_ref[...]
    lane = lax.broadcasted_iota(jnp.int32, logits.shape, 1)
    is_group = lane < MOE_GROUPS
    gl = jnp.where(is_group, logits, NEG_INF)
    g_max = jnp.max(gl, axis=1, keepdims=True)
    g_sel = jnp.min(jnp.where(gl == g_max, lane, LANES), axis=1, keepdims=True)
    g_gate = 1.0 / jnp.sum(jnp.where(is_group, jnp.exp(gl - g_max), 0.0), axis=1, keepdims=True)
    in_group = jnp.logical_and(lane >= GATE_COL0, (lane - GATE_COL0) // MOE_EPG == g_sel)
    in_group = jnp.logical_and(in_group, lane < GATE_COL0 + MOE_EXPERTS)
    el = jnp.where(in_group, logits, NEG_INF)
    top1 = jnp.max(el, axis=1, keepdims=True)
    i1 = jnp.min(jnp.where(jnp.logical_and(in_group, el == top1), lane, LANES), axis=1, keepdims=True)
    rest = jnp.logical_and(in_group, lane != i1)
    el2 = jnp.where(rest, logits, NEG_INF)
    top2 = jnp.max(el2, axis=1, keepdims=True)
    i2 = jnp.min(jnp.where(jnp.logical_and(rest, el2 == top2), lane, LANES), axis=1, keepdims=True)
    e2 = jnp.exp(top2 - top1)
    w1 = g_gate / (1.0 + e2)
    gate_ref[...] = jnp.where(lane == i1, w1, jnp.where(lane == i2, w1 * e2, 0.0))


def _expert_kernel(xn_ref, gate_ref, w13_ref, w2_ref, h_ref, gf_ref, o_ref, acc_ref, *, final_norm):
    e = pl.program_id(1)

    @pl.when(e == 0)
    def _():
        acc_ref[...] = jnp.zeros_like(acc_ref)

    gate = gate_ref[...]
    lane = lax.broadcasted_iota(jnp.int32, gate.shape, 1)
    ge = jnp.sum(jnp.where(lane == GATE_COL0 + e, gate, 0.0), axis=1, keepdims=True)
    up = _dot(xn_ref[...], w13_ref[0])
    hid = w13_ref.shape[2] // 2
    a, b = up[:, :hid], up[:, hid:]
    act = (a * (1.0 / (1.0 + jnp.exp(-a))) * b).astype(BF16)
    acc_ref[...] += ge * _dot(act, w2_ref[0])

    @pl.when(e == pl.num_programs(1) - 1)
    def _():
        y = h_ref[...] + acc_ref[...]
        o_ref[...] = _rms(y, gf_ref[...]) if final_norm else y


def _moe(h, g_ffn, wr, br, w13, w2, g_final, final_norm, tm):
    n, d = h.shape
    row = lambda w: pl.BlockSpec((tm, w), lambda i: (i, 0))
    xn, gate = pl.pallas_call(
        _router_kernel,
        grid=(n // tm,),
        in_specs=[row(d), pl.BlockSpec((1, d), lambda i: (0, 0)),
                  pl.BlockSpec(wr.shape, lambda i: (0, 0)), pl.BlockSpec((1, LANES), lambda i: (0, 0))],
        out_specs=[row(d), row(LANES)],
        out_shape=[jax.ShapeDtypeStruct((n, d), BF16), jax.ShapeDtypeStruct((n, LANES), F32)],
        compiler_params=_cparams(("parallel",)),
        name="moe_router",
    )(h, g_ffn.reshape(1, d), wr, br)
    n_e = w13.shape[0]
    tme = MOE_TM if n % MOE_TM == 0 else tm
    row2 = lambda w: pl.BlockSpec((tme, w), lambda i, e: (i, 0))
    kern = functools.partial(_expert_kernel, final_norm=final_norm)
    return pl.pallas_call(
        kern,
        grid=(n // tme, n_e),
        in_specs=[row2(d), row2(LANES),
                  pl.BlockSpec((1,) + w13.shape[1:], lambda i, e: (e, 0, 0)),
                  pl.BlockSpec((1,) + w2.shape[1:], lambda i, e: (e, 0, 0)),
                  row2(d), pl.BlockSpec((1, d), lambda i, e: (0, 0))],
        out_specs=row2(d),
        out_shape=jax.ShapeDtypeStruct((n, d), F32),
        scratch_shapes=[pltpu.VMEM((tme, d), F32)],
        compiler_params=_cparams(("parallel", "arbitrary")),
        name="moe_experts",
    )(xn, gate, w13, w2, h, g_final.reshape(1, d))


EVEN_SEGS = ((0, 512, None), (512, 512, "full"), (1024, 128, "full"), (1152, 128, None),
             (1280, 256, "full"), (1536, 128, "half"))
ODD_SEGS = ((0, 512, None), (512, 512, None), (1024, 512, None),
            (1536, 512, "full"), (2048, 512, "full"), (2560, 512, None))


def _cat_keys(hist, new):
    allk = jnp.concatenate([hist, new], axis=1) if hist is not None else new
    n_keys = allk.shape[1]
    pad = -n_keys % LANES
    if pad:
        allk = jnp.pad(allk, ((0, 0), (0, pad), (0, 0)))
    return allk.astype(BF16), n_keys


def kernel(x_prompt, x_sample, cache_pool, cache_dsa_k, cache_dsa_v, cache_idx_k, cache_sb_k, cache_sb_v,
           cache_diff_k, cache_diff_v, norm_mix, norm_ffn, norm_final, w_in_even, w_pool, pool_scale,
           w_out_even, w_in_odd, diff_lambda, diff_subln, w_out_odd, moe_w_group, moe_b_group,
           moe_w_expert, moe_b_expert, moe_w1, moe_w3, moe_w2):
    b, t, d = x_prompt.shape
    bd, td, _ = x_sample.shape
    past = cache_dsa_k.shape[2]
    depth = norm_mix.shape[0]
    groups = ((b, t, 0, min(512, b * t)), (bd, td, past, bd * td))

    tabs = []
    for (gb, gt, p0, tm) in groups:
        tab = _rope_tables(p0 + jnp.arange(gt, dtype=jnp.int32))
        if tm > gt:
            tab = jnp.tile(tab, (tm // gt, 1))
        tabs.append(tab)

    hs = [x_prompt.reshape(b * t, d), x_sample.reshape(bd * td, d)]
    outs = [dict(), dict()]
    for l in range(depth):
        li = l // 2
        last = l == depth - 1
        if l % 2 == 0:
            n_in = w_in_even.shape[2]
            w_in = jnp.pad(w_in_even[li], ((0, 0), (0, -n_in % LANES))).astype(BF16)
            w_out = w_out_even[li].astype(BF16)
            wp = w_pool[li].astype(BF16)
        else:
            w_in = w_in_odd[li].astype(BF16)
            w_out = w_out_odd[li].astype(BF16)
            lam_init = 0.8 - 0.6 * math.exp(-0.3 * l)
        wr = jnp.concatenate([moe_w_group[l]] + [moe_w_expert[l, g] for g in range(MOE_GROUPS)], axis=1)
        wr = jnp.pad(wr, ((0, 0), (0, LANES - wr.shape[1]))).astype(BF16)
        br = jnp.concatenate([moe_b_group[l], moe_b_expert[l].reshape(-1)])
        br = jnp.pad(br, (0, LANES - br.shape[0])).reshape(1, LANES).astype(F32)
        w13 = jnp.concatenate([moe_w1[l], moe_w3[l]], axis=2).astype(BF16)
        w2 = moe_w2[l].astype(BF16)

        for gi, (gb, gt, p0, tm) in enumerate(groups):
            h = hs[gi]
            o = outs[gi]
            sample = gi == 1
            r3 = lambda x: x.reshape(gb, gt, x.shape[-1])
            if l % 2 == 0:
                u, q, k, v, qi, kiwi = [r3(x) for x in _project(h, norm_mix[l], w_in, tabs[gi], EVEN_SEGS, tm)]
                ki = kiwi[..., :HEAD_DIM]
                hist = cache_pool[li] if sample else jnp.zeros((gb, POOL_HIST, u.shape[2]), F32)
                a_out = _pool_mix(u, hist, wp, pool_scale[li], p0)
                if sample:
                    k_all, n_keys = _cat_keys(cache_dsa_k[li].reshape(gb, past, -1), k)
                    v_all, _ = _cat_keys(cache_dsa_v[li].reshape(gb, past, -1), v)
                    ki_all, _ = _cat_keys(cache_idx_k[li], ki)
                else:
                    (k_all, n_keys), (v_all, _), (ki_all, _) = _cat_keys(None, k), _cat_keys(None, v), _cat_keys(None, ki)
                b_out = _dsa(q, qi, kiwi, k_all, v_all, ki_all, n_keys, p0)
                o.setdefault("pool", []).append(jnp.concatenate([hist, u], axis=1)[:, -POOL_HIST:])
                o.setdefault("dsa_k", []).append(k.reshape(gb, gt, -1, HEAD_DIM))
                o.setdefault("dsa_v", []).append(v.reshape(gb, gt, -1, HEAD_DIM))
                o.setdefault("idx_k", []).append(ki)
                mix_a, mix_b = a_out, b_out
            else:
                sq, sk, sv, dq, dk, dv = [r3(x) for x in _project(h, norm_mix[l], w_in, tabs[gi], ODD_SEGS, tm)]
                if sample:
                    sk_all, n_keys = _cat_keys(cache_sb_k[li].reshape(gb, past, -1), sk)
                    sv_all, _ = _cat_keys(cache_sb_v[li].reshape(gb, past, -1), sv)
                    dk_all, _ = _cat_keys(cache_diff_k[li].reshape(gb, past, -1), dk)
                    dv_all, _ = _cat_keys(cache_diff_v[li].reshape(gb, past, -1), dv)
                else:
                    (sk_all, n_keys), (sv_all, _) = _cat_keys(None, sk), _cat_keys(None, sv)
                    (dk_all, _), (dv_all, _) = _cat_keys(None, dk), _cat_keys(None, dv)
                c_out = _sb_attend(sq, sk_all, sv_all, n_keys, p0)
                d_out = _diff_attend(dq, dk_all, dv_all, diff_lambda[li], diff_subln[li], n_keys, p0, lam_init)
                n_sb = sk.shape[2] // HEAD_DIM
                n_df = dk.shape[2] // (2 * HEAD_DIM)
                o.setdefault("sb_k", []).append(sk.reshape(gb, gt, n_sb, HEAD_DIM))
                o.setdefault("sb_v", []).append(sv.reshape(gb, gt, n_sb, HEAD_DIM))
                o.setdefault("diff_k", []).append(dk.reshape(gb, gt, n_df, 2, HEAD_DIM))
                o.setdefault("diff_v", []).append(dv.reshape(gb, gt, n_df, 2 * HEAD_DIM))
                mix_a, mix_b = c_out, d_out
            h = _out_proj(h, mix_a.reshape(gb * gt, -1), mix_b.reshape(gb * gt, -1), w_out, tm)
            hs[gi] = _moe(h, norm_ffn[l], wr, br, w13, w2, norm_final, last, tm)

    names = ("pool", "dsa_k", "dsa_v", "idx_k", "sb_k", "sb_v", "diff_k", "diff_v")
    res = [hs[0].reshape(b, t, d), hs[1].reshape(bd, td, d)]
    for o in outs:
        res += [jnp.stack(o[nm]) for nm in names]
    return tuple(res)
```

```python
import functools
import math

import jax
import jax.numpy as jnp
import numpy as np
from jax import lax
from jax.experimental import pallas as pl
from jax.experimental.pallas import tpu as pltpu

F32 = jnp.float32
BF16 = jnp.bfloat16

LANES = 128
HEAD_DIM = 64
CHUNK = 64
ROPE_THETA = 10000.0
NORM_EPS = 1e-6
NEG_INF = -1e30
PAD_SCORE = -3e38
BIG_POS = 3e38
POOL_WINDOWS = (2, 4, 8, 16)
POOL_HIST = 15
POOL_HIST_PAD = 16
DSA_TOPK = 256
IDX_HEADS = 4
MOE_GROUPS = 4
MOE_EPG = 4
MOE_EXPERTS = 16
GATE_COL0 = MOE_GROUPS
VMEM_LIMIT = 56 * 1024 * 1024
BISECT_STEPS = 8
BISECT_ROUNDS = 48
ATTN_TQ = 256
DSA_TQ = 128
MOE_TM = 1024


def _cparams(sem):
    return pltpu.CompilerParams(dimension_semantics=sem, vmem_limit_bytes=VMEM_LIMIT)


def _dot(a, b):
    return jnp.dot(a, b, preferred_element_type=F32)


def _dot_nt(a, b):
    return lax.dot_general(a, b, (((1,), (1,)), ((), ())), preferred_element_type=F32)


def _rms(x, g):
    ms = jnp.mean(x * x, axis=-1, keepdims=True)
    return x * lax.rsqrt(ms + NORM_EPS) * g


def _proj_kernel(x_ref, g_ref, w_ref, tab_ref, *out_refs, segs):
    xn = _rms(x_ref[...], g_ref[...]).astype(BF16)
    copies = iter(out_refs[len(segs):])
    for o_ref, (c0, width, mode, twin) in zip(out_refs, segs):
        t_ref = next(copies) if twin else None
        y = _dot(xn, w_ref[:, c0:c0 + width])
        if mode is None:
            o_ref[...] = y
            if twin:
                t_ref[...] = y.astype(BF16)
            continue
        t0 = 0 if mode == "full" else 3 * LANES
        cos = tab_ref[:, t0:t0 + LANES]
        sin_a = tab_ref[:, t0 + LANES:t0 + 2 * LANES]
        sin_b = tab_ref[:, t0 + 2 * LANES:t0 + 3 * LANES]
        for c in range(0, width, LANES):
            yc = y[:, c:c + LANES]
            yr = (yc * cos + pltpu.roll(yc, LANES - HEAD_DIM // 2, 1) * sin_a
                  + pltpu.roll(yc, HEAD_DIM // 2, 1) * sin_b)
            o_ref[:, c:c + LANES] = yr
            if twin:
                t_ref[:, c:c + LANES] = yr.astype(BF16)


def _rope_tables(pos):
    half = HEAD_DIM // 2
    inv = ROPE_THETA ** (-jnp.arange(half, dtype=F32) / half)
    ang = pos.astype(F32)[:, None] * inv[None, :]
    cos, sin = jnp.cos(ang), jnp.sin(ang)
    zero, one = jnp.zeros_like(sin), jnp.ones_like(cos)
    cos_h = jnp.concatenate([cos, cos], axis=1)
    sa_h = jnp.concatenate([-sin, zero], axis=1)
    sb_h = jnp.concatenate([zero, sin], axis=1)
    one_h = jnp.concatenate([one, one], axis=1)
    zero_h = jnp.concatenate([zero, zero], axis=1)
    return jnp.concatenate([cos_h, cos_h, sa_h, sa_h, sb_h, sb_h,
                            cos_h, one_h, sa_h, zero_h, sb_h, zero_h], axis=1)


def _project(x, g, w, tab, segs, tm):
    n, d = x.shape
    tt = tab.shape[0]
    nt = tt // tm
    kern = functools.partial(_proj_kernel, segs=segs)
    return pl.pallas_call(
        kern,
        grid=(n // tm,),
        in_specs=[pl.BlockSpec((tm, d), lambda i: (i, 0)),
                  pl.BlockSpec((1, d), lambda i: (0, 0)),
                  pl.BlockSpec(w.shape, lambda i: (0, 0)),
                  pl.BlockSpec((tm, tab.shape[1]), lambda i: (i % nt, 0))],
        out_specs=[pl.BlockSpec((tm, wd), lambda i: (i, 0)) for _, wd, _, _ in segs]
        + [pl.BlockSpec((tm, wd), lambda i: (i, 0)) for _, wd, _, twin in segs if twin],
        out_shape=[jax.ShapeDtypeStruct((n, wd), F32) for _, wd, _, _ in segs]
        + [jax.ShapeDtypeStruct((n, wd), BF16) for _, wd, _, twin in segs if twin],
        compiler_params=_cparams(("parallel",)),
        name="proj",
    )(x, g.reshape(1, d), w, tab)


def _pool_kernel(u_ref, h_ref, w_ref, s_ref, o_ref, ext_ref, *, t, pos0, rc):
    ext_ref[0:POOL_HIST_PAD, :] = h_ref[0]
    ext_ref[POOL_HIST_PAD:POOL_HIST_PAD + t, :] = u_ref[0]
    for r0 in range(0, t, rc):
        pos = pos0 + r0 + lax.broadcasted_iota(jnp.int32, (rc, 1), 0)
        for g, win in enumerate(POOL_WINDOWS):
            c0 = g * LANES
            u_new = ext_ref[POOL_HIST_PAD + r0:POOL_HIST_PAD + r0 + rc, c0:c0 + LANES]
            s = u_new
            for k in range(1, win):
                s = s + ext_ref[POOL_HIST_PAD + r0 - k:POOL_HIST_PAD + r0 - k + rc, c0:c0 + LANES]
            cnt = jnp.minimum(pos + 1, win).astype(F32)
            dlt = (s / cnt - u_new).astype(BF16)
            o_ref[0, r0:r0 + rc, c0:c0 + LANES] = _dot(dlt, w_ref[g]) * s_ref[:, c0:c0 + LANES]


def _pool_mix(u, hist, w_pool, pool_scale, pos0):
    b, t, c = u.shape
    rc = min(t, 256)
    hist16 = jnp.pad(hist, ((0, 0), (POOL_HIST_PAD - POOL_HIST, 0), (0, 0)))
    kern = functools.partial(_pool_kernel, t=t, pos0=pos0, rc=rc)
    return pl.pallas_call(
        kern,
        grid=(b,),
        in_specs=[pl.BlockSpec((1, t, c), lambda i: (i, 0, 0)),
                  pl.BlockSpec((1, POOL_HIST_PAD, c), lambda i: (i, 0, 0)),
                  pl.BlockSpec(w_pool.shape, lambda i: (0, 0, 0)),
                  pl.BlockSpec((1, c), lambda i: (0, 0))],
        out_specs=pl.BlockSpec((1, t, c), lambda i: (i, 0, 0)),
        out_shape=jax.ShapeDtypeStruct((b, t, c), F32),
        scratch_shapes=[pltpu.VMEM((POOL_HIST_PAD + t, c), F32)],
        compiler_params=_cparams(("parallel",)),
        name="pool_mix",
    )(u, hist16, w_pool, pool_scale.reshape(1, c))


def _dsa_kernel(q_ref, qi_ref, kw_ref, k_ref, v_ref, ki_ref, o_ref, lo_ref, hi_ref, bias_ref,
                *, tq, extents, n_keys, q_pos0, n_sel):
    i = pl.program_id(1)
    qpos = q_pos0 + i * tq + lax.broadcasted_iota(jnp.int32, (tq, 1), 0)
    lim = jnp.minimum((qpos // CHUNK + 1) * CHUNK, n_keys)
    kf = float(n_sel)
    low = lax.broadcasted_iota(jnp.int32, (tq, LANES), 1) < HEAD_DIM
    q = q_ref[0] * HEAD_DIM ** -0.5
    qi = qi_ref[0].astype(BF16)
    wi = kw_ref[0][:, HEAD_DIM:HEAD_DIM + IDX_HEADS] * (IDX_HEADS * HEAD_DIM) ** -0.5

    def body(ext):
        kpos = lax.broadcasted_iota(jnp.int32, (tq, ext), 1)
        adm = kpos < lim
        padded = ext > n_keys
        virt = float(max(n_keys - ext, 0))

        sidx = jnp.zeros((tq, ext), F32)
        ki = ki_ref[0, :ext, :]
        for h in range(IDX_HEADS):
            sh = _dot_nt(qi[:, h * HEAD_DIM:(h + 1) * HEAD_DIM], ki)
            sidx = sidx + jnp.maximum(sh, 0.0) * wi[:, h:h + 1]
        sm = jnp.where(adm, sidx, NEG_INF)
        if padded:
            real = kpos < n_keys
            sm = jnp.where(real, sm, PAD_SCORE)

        def count_gt(x):
            cnt = jnp.sum(jnp.where(sm > x, 1.0, 0.0), axis=1, keepdims=True)
            return cnt + jnp.where(x < NEG_INF, virt, 0.0) if virt else cnt

        def bracket(lo, hi):
            above = jnp.min(jnp.where(sm > lo, sm, BIG_POS), axis=1, keepdims=True)
            below = jnp.max(jnp.where(sm <= hi, sm, PAD_SCORE), axis=1, keepdims=True)
            if virt:
                above = jnp.minimum(above, jnp.where(lo < NEG_INF, NEG_INF, BIG_POS))
                below = jnp.maximum(below, jnp.where(hi >= NEG_INF, NEG_INF, PAD_SCORE))
            return above, below

        row_max = jnp.max(sm, axis=1, keepdims=True)
        row_min = jnp.min(jnp.where(real, sm, BIG_POS) if padded else sm, axis=1, keepdims=True)
        if virt:
            row_min = jnp.minimum(row_min, NEG_INF)
        adm_min = jnp.min(jnp.where(adm, sm, BIG_POS), axis=1, keepdims=True)
        few = count_gt(row_min) < kf
        tight = count_gt(adm_min) >= kf
        lo_ref[...] = jnp.where(few, PAD_SCORE, jnp.where(tight, adm_min, row_min))
        hi_ref[...] = jnp.where(few, row_min, jnp.where(tight, row_max, adm_min))

        def unresolved(lo, hi):
            above, below = bracket(lo, hi)
            return jnp.sum(jnp.where(above < below, 1, 0))

        def cond(carry):
            rounds, open_rows = carry
            return jnp.logical_and(open_rows > 0, rounds < BISECT_ROUNDS)

        def step(carry):
            rounds, _ = carry
            lo, hi = lo_ref[...], hi_ref[...]
            for _ in range(BISECT_STEPS):
                mid = 0.5 * lo + 0.5 * hi
                under = count_gt(mid) < kf
                hi = jnp.where(under, mid, hi)
                lo = jnp.where(under, lo, mid)
            lo_ref[...] = lo
            hi_ref[...] = hi
            return rounds + 1, unresolved(lo, hi)

        lax.while_loop(cond, step, (jnp.int32(0), unresolved(lo_ref[...], hi_ref[...])))
        _, thr = bracket(lo_ref[...], hi_ref[...])

        gt = sm > thr
        eq = sm == thr
        need = kf - count_gt(thr)
        n_eq = jnp.sum(jnp.where(eq, 1.0, 0.0), axis=1, keepdims=True)
        crowded = jnp.sum(jnp.where(n_eq > need, 1, 0))

        @pl.when(crowded == 0)
        def _():
            keep = jnp.logical_and(jnp.logical_or(gt, eq), adm)
            bias_ref[:, :ext] = jnp.where(keep, 0.0, NEG_INF)

        @pl.when(crowded > 0)
        def _():
            rr = lax.broadcasted_iota(jnp.int32, (LANES, LANES), 0)
            cc = lax.broadcasted_iota(jnp.int32, (LANES, LANES), 1)
            prefix_ones = jnp.where(rr <= cc, 1.0, 0.0).astype(BF16)
            carry = jnp.zeros((tq, 1), F32)
            for j in range(ext // LANES):
                sl = slice(j * LANES, (j + 1) * LANES)
                eq_j = eq[:, sl]
                rank = _dot(jnp.where(eq_j, 1.0, 0.0).astype(BF16), prefix_ones) + carry
                keep = jnp.logical_or(gt[:, sl], jnp.logical_and(eq_j, rank <= need))
                bias_ref[:, sl] = jnp.where(jnp.logical_and(keep, adm[:, sl]), 0.0, NEG_INF)
                carry = rank[:, LANES - 1:LANES]

        bias = bias_ref[:, :ext]
        bias2 = jnp.concatenate([bias, bias], axis=0)

        kk = k_ref[0, :ext, :]
        vv = v_ref[0, :ext, :]
        low_k = lax.broadcasted_iota(jnp.int32, (ext, LANES), 1) < HEAD_DIM
        vsw = pltpu.roll(vv.astype(F32), HEAD_DIM, 1).astype(BF16)
        one = jnp.ones_like(vv)
        n_kv = LANES // HEAD_DIM
        for g in range(n_kv):
            v_lo = jnp.where(low_k, vv if g == 0 else vsw, one)
            v_hi = jnp.where(low_k, one, vsw if g == 0 else vv)
            stacks = []
            for odd in range(2):
                rows = []
                for m in range(2):
                    c = 2 * g + m
                    qc = q[:, c * LANES:(c + 1) * LANES]
                    if (odd == 1) != (g == 1):
                        qc = pltpu.roll(qc, HEAD_DIM, 1)
                    rows.append(jnp.where(low, qc, 0.0) if g == 0 else jnp.where(low, 0.0, qc))
                qs = jnp.concatenate(rows, axis=0).astype(BF16)
                sc = _dot_nt(qs, kk) + bias2
                p = jnp.exp(sc - jnp.max(sc, axis=1, keepdims=True))
                og = _dot(p.astype(BF16), v_hi if odd else v_lo)
                stacks.append(og / pltpu.roll(og, HEAD_DIM, 1))
            for m in range(2):
                c = 2 * g + m
                o_ref[0, :, c * LANES:(c + 1) * LANES] = jnp.where(
                    low, stacks[0][m * tq:(m + 1) * tq], stacks[1][m * tq:(m + 1) * tq])

    _for_tile_extent(i, tq, q_pos0, n_keys, extents, body)


def _dsa(q, qi, kiwi, k_all, v_all, ki_all, n_keys, q_pos0):
    b, t, dq = q.shape
    s_len = k_all.shape[1]
    tq = min(t, DSA_TQ)
    n_sel = min(DSA_TOPK, n_keys // 4)
    extents = _key_extents(t // tq, tq, q_pos0, n_keys, s_len, 4 * LANES)
    kern = functools.partial(_dsa_kernel, tq=tq, extents=extents, n_keys=n_keys, q_pos0=q_pos0, n_sel=n_sel)
    qspec = lambda w: pl.BlockSpec((1, tq, w), lambda bi, i: (bi, i, 0))
    kspec = lambda w: pl.BlockSpec((1, s_len, w), lambda bi, i: (bi, 0, 0))
    return pl.pallas_call(
        kern,
        grid=(b, t // tq),
        in_specs=[qspec(dq), qspec(qi.shape[2]), qspec(kiwi.shape[2]),
                  kspec(k_all.shape[2]), kspec(v_all.shape[2]), kspec(ki_all.shape[2])],
        out_specs=qspec(dq),
        out_shape=jax.ShapeDtypeStruct((b, t, dq), F32),
        scratch_shapes=[pltpu.VMEM((tq, 1), F32), pltpu.VMEM((tq, 1), F32), pltpu.VMEM((tq, s_len), F32)],
        compiler_params=_cparams(("parallel", "parallel")),
        name="dsa",
    )(q, qi, kiwi, k_all, v_all, ki_all)


def _sb_kernel(q_ref, k_ref, v_ref, o_ref, acc_ref, run_ref, *, tq, n_keys, q_pos0):
    i = pl.program_id(1)
    pairs = q_ref.shape[2] // LANES
    first_q = q_pos0 + i * tq
    qpos = first_q + lax.broadcasted_iota(jnp.int32, (tq, 1), 0)
    n_blocks = (jnp.minimum(first_q + tq - 1, n_keys) + LANES - 1) // LANES
    n_full = jnp.minimum(first_q, n_keys) // LANES
    low_q = lax.broadcasted_iota(jnp.int32, (tq, LANES), 1) < HEAD_DIM
    low_k = lax.broadcasted_iota(jnp.int32, (LANES, LANES), 1) < HEAD_DIM
    q = q_ref[0] * HEAD_DIM ** -0.5
    qm = []
    for p in range(pairs):
        qp = q[:, p * LANES:(p + 1) * LANES]
        qm.append((jnp.where(low_q, qp, 0.0).astype(BF16), jnp.where(low_q, 0.0, qp).astype(BF16)))
    rr = lax.broadcasted_iota(jnp.int32, (2 * LANES, 2 * LANES), 0)
    cc = lax.broadcasted_iota(jnp.int32, (2 * LANES, 2 * LANES), 1)
    rk = jnp.where(rr >= LANES, rr - LANES, rr)
    cs_rhs = jnp.where(jnp.logical_or(cc >= LANES, rk >= cc), 1.0, 0.0).astype(BF16)
    acc_ref[...] = jnp.zeros_like(acc_ref)
    run_ref[...] = jnp.zeros_like(run_ref)

    def block(j, masked):
        ks = pl.multiple_of(j * LANES, LANES)
        if masked:
            kpos = ks + lax.broadcasted_iota(jnp.int32, (1, LANES), 1)
            causal = jnp.logical_and(kpos < qpos, kpos < n_keys)
        for p in range(pairs):
            ps = slice(p * LANES, (p + 1) * LANES)
            kb = k_ref[0, pl.ds(ks, LANES), ps]
            vb = v_ref[0, pl.ds(ks, LANES), ps]
            zero = jnp.zeros_like(vb)
            v_cat = jnp.concatenate([jnp.where(low_k, vb, zero), jnp.where(low_k, zero, vb)], axis=0)
            a_parts = []
            for c in range(2):
                hsl = slice((2 * p + c) * LANES, (2 * p + c + 1) * LANES)
                z = _dot_nt(qm[p][c], kb)
                sp = jnp.maximum(z, 0.0) + jnp.log(1.0 + jnp.exp(-jnp.abs(z)))
                if masked:
                    sp = jnp.where(causal, sp, 0.0)
                hi = sp.astype(BF16)
                lo = (sp - hi.astype(F32)).astype(BF16)
                cs = _dot(jnp.concatenate([hi, lo], axis=1), cs_rhs)
                run = run_ref[:, hsl]
                a = jnp.exp(z - cs[:, :LANES] - run)
                if masked:
                    a = jnp.where(causal, a, 0.0)
                run_ref[:, hsl] = run + cs[:, LANES:]
                a_parts.append(a.astype(BF16))
            acc_ref[:, ps] += _dot(jnp.concatenate(a_parts, axis=1), v_cat)

    def masked_step(jj, carry):
        block(n_blocks - 1 - jj, True)
        return carry

    def full_pair(jj, carry):
        block(n_full - 1 - 2 * jj, False)
        block(n_full - 2 - 2 * jj, False)
        return carry

    lax.fori_loop(0, n_blocks - n_full, masked_step, 0)
    lax.fori_loop(0, n_full // 2, full_pair, 0)

    @pl.when(n_full % 2 == 1)
    def _():
        block(jnp.int32(0), False)

    o_ref[0] = acc_ref[...]


def _sb_attend(q, k_all, v_all, n_keys, q_pos0):
    b, t, d = q.shape
    s_len = k_all.shape[1]
    tq = min(t, ATTN_TQ)
    kern = functools.partial(_sb_kernel, tq=tq, n_keys=n_keys, q_pos0=q_pos0)
    return pl.pallas_call(
        kern,
        grid=(b, t // tq),
        in_specs=[pl.BlockSpec((1, tq, d), lambda bi, i: (bi, i, 0)),
                  pl.BlockSpec((1, s_len, d), lambda bi, i: (bi, 0, 0)),
                  pl.BlockSpec((1, s_len, d), lambda bi, i: (bi, 0, 0))],
        out_specs=pl.BlockSpec((1, tq, d), lambda bi, i: (bi, i, 0)),
        out_shape=jax.ShapeDtypeStruct((b, t, d), F32),
        scratch_shapes=[pltpu.VMEM((tq, d), F32), pltpu.VMEM((tq, 2 * d), F32)],
        compiler_params=_cparams(("parallel", "parallel")),
        name="sb_attend",
    )(q, k_all, v_all)


def _key_extents(nq, tq, q_pos0, n_keys, s_len, step):
    need = [min(n_keys, ((q_pos0 + (i + 1) * tq - 1) // CHUNK + 1) * CHUNK) for i in range(nq)]
    return tuple(sorted({min(s_len, -(-n // step) * step) for n in need}))


def _for_tile_extent(i, tq, q_pos0, n_keys, extents, body):
    need = jnp.minimum(n_keys, ((q_pos0 + (i + 1) * tq - 1) // CHUNK + 1) * CHUNK)
    prev = 0
    for ext in extents:
        pl.when(jnp.logical_and(need > prev, need <= ext))(functools.partial(body, ext))
        prev = ext


def _diff_kernel(q_ref, k_ref, v_ref, lam_ref, gain_ref, o_ref, *, tq, extents, n_keys, q_pos0, lam_init):
    i = pl.program_id(2)
    lp = lam_ref[...]
    lam = (jnp.exp(jnp.sum(lp[0:1] * lp[1:2], axis=1, keepdims=True))
           - jnp.exp(jnp.sum(lp[2:3] * lp[3:4], axis=1, keepdims=True)) + lam_init)
    qpos = q_pos0 + i * tq + lax.broadcasted_iota(jnp.int32, (tq, 1), 0)
    lim = jnp.minimum((qpos // CHUNK + 1) * CHUNK, n_keys)
    low = lax.broadcasted_iota(jnp.int32, (tq, LANES), 1) < HEAD_DIM
    q = q_ref[0] * HEAD_DIM ** -0.5
    qm = (jnp.where(low, q, 0.0).astype(BF16), jnp.where(low, 0.0, q).astype(BF16))

    def body(ext):
        mask = lax.broadcasted_iota(jnp.int32, (tq, ext), 1) < lim
        kk = k_ref[0, :ext, :]
        probs = []
        for c in range(2):
            sc = jnp.where(mask, _dot_nt(qm[c], kk), NEG_INF)
            p = jnp.exp(sc - jnp.max(sc, axis=1, keepdims=True))
            probs.append(p * (1.0 / jnp.sum(p, axis=1, keepdims=True)))
        a = probs[0] - lam * probs[1]
        o = _dot(a.astype(BF16), v_ref[0, :ext, :])
        o = o * lax.rsqrt(jnp.mean(o * o, axis=-1, keepdims=True) + NORM_EPS) * gain_ref[...] * (1.0 - lam_init)
        o_ref[0] = o

    _for_tile_extent(i, tq, q_pos0, n_keys, extents, body)


def _diff_attend(q, k_all, v_all, diff_lambda, gain, n_keys, q_pos0, lam_init):
    b, t, d = q.shape
    s_len = k_all.shape[1]
    tq = min(t, ATTN_TQ)
    extents = _key_extents(t // tq, tq, q_pos0, n_keys, s_len, 2 * LANES)
    kern = functools.partial(_diff_kernel, tq=tq, extents=extents, n_keys=n_keys, q_pos0=q_pos0, lam_init=lam_init)
    return pl.pallas_call(
        kern,
        grid=(b, d // LANES, t // tq),
        in_specs=[pl.BlockSpec((1, tq, LANES), lambda bi, h, i: (bi, i, h)),
                  pl.BlockSpec((1, s_len, LANES), lambda bi, h, i: (bi, 0, h)),
                  pl.BlockSpec((1, s_len, LANES), lambda bi, h, i: (bi, 0, h)),
                  pl.BlockSpec(diff_lambda.shape, lambda bi, h, i: (0, 0)),
                  pl.BlockSpec((1, LANES), lambda bi, h, i: (0, 0))],
        out_specs=pl.BlockSpec((1, tq, LANES), lambda bi, h, i: (bi, i, h)),
        out_shape=jax.ShapeDtypeStruct((b, t, d), F32),
        compiler_params=_cparams(("parallel", "parallel", "parallel")),
        name="diff_attend",
    )(q, k_all, v_all, diff_lambda, gain.reshape(1, LANES))


def _out_kernel(h_ref, a_ref, b_ref, wa_ref, wb_ref, o_ref):
    o_ref[...] = (h_ref[...] + _dot(a_ref[...].astype(BF16), wa_ref[...])
                  + _dot(b_ref[...].astype(BF16), wb_ref[...]))


def _out_proj(h, a, bmix, w_out, tm):
    n, d = h.shape
    ca = a.shape[1]
    wa, wb = w_out[:ca], w_out[ca:]
    row = lambda w: pl.BlockSpec((tm, w), lambda i: (i, 0))
    return pl.pallas_call(
        _out_kernel,
        grid=(n // tm,),
        in_specs=[row(d), row(ca), row(bmix.shape[1]),
                  pl.BlockSpec(wa.shape, lambda i: (0, 0)), pl.BlockSpec(wb.shape, lambda i: (0, 0))],
        out_specs=row(d),
        out_shape=jax.ShapeDtypeStruct((n, d), F32),
        compiler_params=_cparams(("parallel",)),
        name="out_proj",
    )(h, a, bmix, wa, wb)


def _router_kernel(h_ref, g_ref, w_ref, b_ref, xn_ref, gate_ref):
    xn = _rms(h_ref[...], g_ref[...]).astype(BF16)
    xn_ref[...] = xn
    logits = _dot(xn, w_ref[...]) + b_ref[...]
    lane = lax.broadcasted_iota(jnp.int32, logits.shape, 1)
    is_group = lane < MOE_GROUPS
    gl = jnp.where(is_group, logits, NEG_INF)
    g_max = jnp.max(gl, axis=1, keepdims=True)
    g_sel = jnp.min(jnp.where(gl == g_max, lane, LANES), axis=1, keepdims=True)
    g_gate = 1.0 / jnp.sum(jnp.where(is_group, jnp.exp(gl - g_max), 0.0), axis=1, keepdims=True)
    in_group = jnp.logical_and(lane >= GATE_COL0, (lane - GATE_COL0) // MOE_EPG == g_sel)
    in_group = jnp.logical_and(in_group, lane < GATE_COL0 + MOE_EXPERTS)
    el = jnp.where(in_group, logits, NEG_INF)
    top1 = jnp.max(el, axis=1, keepdims=True)
    i1 = jnp.min(jnp.where(jnp.logical_and(in_group, el == top1), lane, LANES), axis=1, keepdims=True)
    rest = jnp.logical_and(in_group, lane != i1)
    el2 = jnp.where(rest, logits, NEG_INF)
    top2 = jnp.max(el2, axis=1, keepdims=True)
    i2 = jnp.min(jnp.where(jnp.logical_and(rest, el2 == top2), lane, LANES), axis=1, keepdims=True)
    e2 = jnp.exp(top2 - top1)
    w1 = g_gate / (1.0 + e2)
    gate_ref[...] = jnp.where(lane == i1, w1, jnp.where(lane == i2, w1 * e2, 0.0))


def _expert_kernel(xn_ref, gate_ref, w1_ref, w3_ref, w2_ref, h_ref, gf_ref, o_ref, acc_ref, *, final_norm):
    e = pl.program_id(1)

    @pl.when(e == 0)
    def _():
        acc_ref[...] = jnp.zeros_like(acc_ref)

    gate = gate_ref[...]
    lane = lax.broadcasted_iota(jnp.int32, gate.shape, 1)
    ge = jnp.sum(jnp.where(lane == GATE_COL0 + e, gate, 0.0), axis=1, keepdims=True)
    xn = xn_ref[...]
    a = _dot(xn, w1_ref[0].astype(BF16))
    b = _dot(xn, w3_ref[0].astype(BF16))
    act = (a * (1.0 / (1.0 + jnp.exp(-a))) * b).astype(BF16)
    acc_ref[...] += ge * _dot(act, w2_ref[0].astype(BF16))

    @pl.when(e == pl.num_programs(1) - 1)
    def _():
        y = h_ref[...] + acc_ref[...]
        o_ref[...] = _rms(y, gf_ref[...]) if final_norm else y


def _moe(h, g_ffn, wr, br, w1, w3, w2, g_final, final_norm, tm):
    n, d = h.shape
    row = lambda w: pl.BlockSpec((tm, w), lambda i: (i, 0))
    xn, gate = pl.pallas_call(
        _router_kernel,
        grid=(n // tm,),
        in_specs=[row(d), pl.BlockSpec((1, d), lambda i: (0, 0)),
                  pl.BlockSpec(wr.shape, lambda i: (0, 0)), pl.BlockSpec((1, LANES), lambda i: (0, 0))],
        out_specs=[row(d), row(LANES)],
        out_shape=[jax.ShapeDtypeStruct((n, d), BF16), jax.ShapeDtypeStruct((n, LANES), F32)],
        compiler_params=_cparams(("parallel",)),
        name="moe_router",
    )(h, g_ffn.reshape(1, d), wr, br)
    n_e = w1.shape[0]
    tme = MOE_TM if n % MOE_TM == 0 else tm
    row2 = lambda w: pl.BlockSpec((tme, w), lambda i, e: (i, 0))
    kern = functools.partial(_expert_kernel, final_norm=final_norm)
    return pl.pallas_call(
        kern,
        grid=(n // tme, n_e),
        in_specs=[row2(d), row2(LANES),
                  pl.BlockSpec((1,) + w1.shape[1:], lambda i, e: (e, 0, 0)),
                  pl.BlockSpec((1,) + w3.shape[1:], lambda i, e: (e, 0, 0)),
                  pl.BlockSpec((1,) + w2.shape[1:], lambda i, e: (e, 0, 0)),
                  row2(d), pl.BlockSpec((1, d), lambda i, e: (0, 0))],
        out_specs=row2(d),
        out_shape=jax.ShapeDtypeStruct((n, d), F32),
        scratch_shapes=[pltpu.VMEM((tme, d), F32)],
        compiler_params=_cparams(("parallel", "arbitrary")),
        name="moe_experts",
    )(xn, gate, w1, w3, w2, h, g_final.reshape(1, d))


EVEN_SEGS = ((0, 512, None, False), (512, 512, "full", False), (1024, 128, "full", True), (1152, 128, None, True),
             (1280, 256, "full", False), (1536, 128, "half", False))
ODD_SEGS = ((0, 512, None, False), (512, 512, None, True), (1024, 512, None, True),
            (1536, 512, "full", False), (2048, 512, "full", True), (2560, 512, None, True))


def _cat_keys(hist, new):
    allk = jnp.concatenate([hist, new], axis=1) if hist is not None else new
    n_keys = allk.shape[1]
    pad = -n_keys % LANES
    if pad:
        allk = jnp.pad(allk, ((0, 0), (0, pad), (0, 0)))
    return allk.astype(BF16), n_keys


def kernel(x_prompt, x_sample, cache_pool, cache_dsa_k, cache_dsa_v, cache_idx_k, cache_sb_k, cache_sb_v,
           cache_diff_k, cache_diff_v, norm_mix, norm_ffn, norm_final, w_in_even, w_pool, pool_scale,
           w_out_even, w_in_odd, diff_lambda, diff_subln, w_out_odd, moe_w_group, moe_b_group,
           moe_w_expert, moe_b_expert, moe_w1, moe_w3, moe_w2):
    b, t, d = x_prompt.shape
    bd, td, _ = x_sample.shape
    past = cache_dsa_k.shape[2]
    depth = norm_mix.shape[0]
    groups = ((b, t, 0, min(512, b * t)), (bd, td, past, bd * td))

    tabs = []
    for (gb, gt, p0, tm) in groups:
        tab = _rope_tables(p0 + jnp.arange(gt, dtype=jnp.int32))
        if tm > gt:
            tab = jnp.tile(tab, (tm // gt, 1))
        tabs.append(tab)

    hs = [x_prompt.reshape(b * t, d), x_sample.reshape(bd * td, d)]
    outs = [dict(), dict()]
    for l in range(depth):
        li = l // 2
        last = l == depth - 1
        if l % 2 == 0:
            n_in = w_in_even.shape[2]
            w_in = jnp.pad(w_in_even[li], ((0, 0), (0, -n_in % LANES))).astype(BF16)
            w_out = w_out_even[li].astype(BF16)
            wp = w_pool[li].astype(BF16)
        else:
            w_in = w_in_odd[li].astype(BF16)
            w_out = w_out_odd[li].astype(BF16)
            lam_init = 0.8 - 0.6 * math.exp(-0.3 * l)
        wr = jnp.concatenate([moe_w_group[l]] + [moe_w_expert[l, g] for g in range(MOE_GROUPS)], axis=1)
        wr = jnp.pad(wr, ((0, 0), (0, LANES - wr.shape[1]))).astype(BF16)
        br = jnp.concatenate([moe_b_group[l], moe_b_expert[l].reshape(-1)])
        br = jnp.pad(br, (0, LANES - br.shape[0])).reshape(1, LANES).astype(F32)

        for gi, (gb, gt, p0, tm) in enumerate(groups):
            h = hs[gi]
            o = outs[gi]
            sample = gi == 1
            r3 = lambda x: x.reshape(gb, gt, x.shape[-1])
            if l % 2 == 0:
                u, q, k, v, qi, kiwi, k16, v16 = [
                    r3(x) for x in _project(h, norm_mix[l], w_in, tabs[gi], EVEN_SEGS, tm)]
                ki = kiwi[..., :HEAD_DIM]
                hist = cache_pool[li] if sample else jnp.zeros((gb, POOL_HIST, u.shape[2]), F32)
                a_out = _pool_mix(u, hist, wp, pool_scale[li], p0)
                if sample:
                    k_all, n_keys = _cat_keys(cache_dsa_k[li].reshape(gb, past, -1), k)
                    v_all, _ = _cat_keys(cache_dsa_v[li].reshape(gb, past, -1), v)
                    ki_all, _ = _cat_keys(cache_idx_k[li], ki)
                else:
                    (k_all, n_keys), (v_all, _), (ki_all, _) = _cat_keys(None, k16), _cat_keys(None, v16), _cat_keys(None, ki)
                b_out = _dsa(q, qi, kiwi, k_all, v_all, ki_all, n_keys, p0)
                o.setdefault("pool", []).append(jnp.concatenate([hist, u], axis=1)[:, -POOL_HIST:])
                o.setdefault("dsa_k", []).append(k.reshape(gb, gt, -1, HEAD_DIM))
                o.setdefault("dsa_v", []).append(v.reshape(gb, gt, -1, HEAD_DIM))
                o.setdefault("idx_k", []).append(ki)
                mix_a, mix_b = a_out, b_out
            else:
                sq, sk, sv, dq, dk, dv, sk16, sv16, dk16, dv16 = [
                    r3(x) for x in _project(h, norm_mix[l], w_in, tabs[gi], ODD_SEGS, tm)]
                if sample:
                    sk_all, n_keys = _cat_keys(cache_sb_k[li].reshape(gb, past, -1), sk)
                    sv_all, _ = _cat_keys(cache_sb_v[li].reshape(gb, past, -1), sv)
                    dk_all, _ = _cat_keys(cache_diff_k[li].reshape(gb, past, -1), dk)
                    dv_all, _ = _cat_keys(cache_diff_v[li].reshape(gb, past, -1), dv)
                else:
                    (sk_all, n_keys), (sv_all, _) = _cat_keys(None, sk16), _cat_keys(None, sv16)
                    (dk_all, _), (dv_all, _) = _cat_keys(None, dk16), _cat_keys(None, dv16)
                c_out = _sb_attend(sq, sk_all, sv_all, n_keys, p0)
                d_out = _diff_attend(dq, dk_all, dv_all, diff_lambda[li], diff_subln[li], n_keys, p0, lam_init)
                n_sb = sk.shape[2] // HEAD_DIM
                n_df = dk.shape[2] // (2 * HEAD_DIM)
                o.setdefault("sb_k", []).append(sk.reshape(gb, gt, n_sb, HEAD_DIM))
                o.setdefault("sb_v", []).append(sv.reshape(gb, gt, n_sb, HEAD_DIM))
                o.setdefault("diff_k", []).append(dk.reshape(gb, gt, n_df, 2, HEAD_DIM))
                o.setdefault("diff_v", []).append(dv.reshape(gb, gt, n_df, 2 * HEAD_DIM))
                mix_a, mix_b = c_out, d_out
            h = _out_proj(h, mix_a.reshape(gb * gt, -1), mix_b.reshape(gb * gt, -1), w_out, tm)
            hs[gi] = _moe(h, norm_ffn[l], wr, br, moe_w1[l], moe_w3[l], moe_w2[l], norm_final, last, tm)

    names = ("pool", "dsa_k", "dsa_v", "idx_k", "sb_k", "sb_v", "diff_k", "diff_v")
    res = [hs[0].reshape(b, t, d), hs[1].reshape(bd, td, d)]
    for o in outs:
        res += [jnp.stack(o[nm]) for nm in names]
    return tuple(res)
```

```python
import functools
import math

import jax
import jax.numpy as jnp
import numpy as np
from jax import lax
from jax.experimental import pallas as pl
from jax.experimental.pallas import tpu as pltpu

F32 = jnp.float32
BF16 = jnp.bfloat16

LANES = 128
HEAD_DIM = 64
CHUNK = 64
ROPE_THETA = 10000.0
NORM_EPS = 1e-6
NEG_INF = -1e30
PAD_SCORE = -3e38
BIG_POS = 3e38
POOL_WINDOWS = (2, 4, 8, 16)
POOL_HIST = 15
POOL_HIST_PAD = 16
DSA_TOPK = 256
IDX_HEADS = 4
MOE_GROUPS = 4
MOE_EPG = 4
MOE_EXPERTS = 16
GATE_COL0 = MOE_GROUPS
VMEM_LIMIT = 56 * 1024 * 1024
BISECT_STEPS = 8
BISECT_ROUNDS = 48
ATTN_TQ = 256
DSA_TQ = 128
MOE_TM = 1024


def _cparams(sem):
    return pltpu.CompilerParams(dimension_semantics=sem, vmem_limit_bytes=VMEM_LIMIT)


def _dot(a, b):
    return jnp.dot(a, b, preferred_element_type=F32)


def _dot_nt(a, b):
    return lax.dot_general(a, b, (((1,), (1,)), ((), ())), preferred_element_type=F32)


def _rms(x, g):
    ms = jnp.mean(x * x, axis=-1, keepdims=True)
    return x * lax.rsqrt(ms + NORM_EPS) * g


def _proj_kernel(x_ref, g_ref, w_ref, tab_ref, *out_refs, segs):
    xn = _rms(x_ref[...], g_ref[...]).astype(BF16)
    copies = iter(out_refs[len(segs):])
    for o_ref, (c0, width, mode, twin) in zip(out_refs, segs):
        t_ref = next(copies) if twin else None
        y = _dot(xn, w_ref[:, c0:c0 + width])
        if mode is None:
            o_ref[...] = y
            if twin:
                t_ref[...] = y.astype(BF16)
            continue
        t0 = 0 if mode == "full" else 3 * LANES
        cos = tab_ref[:, t0:t0 + LANES]
        sin_a = tab_ref[:, t0 + LANES:t0 + 2 * LANES]
        sin_b = tab_ref[:, t0 + 2 * LANES:t0 + 3 * LANES]
        for c in range(0, width, LANES):
            yc = y[:, c:c + LANES]
            yr = (yc * cos + pltpu.roll(yc, LANES - HEAD_DIM // 2, 1) * sin_a
                  + pltpu.roll(yc, HEAD_DIM // 2, 1) * sin_b)
            o_ref[:, c:c + LANES] = yr
            if twin:
                t_ref[:, c:c + LANES] = yr.astype(BF16)


def _rope_tables(pos):
    half = HEAD_DIM // 2
    inv = ROPE_THETA ** (-jnp.arange(half, dtype=F32) / half)
    ang = pos.astype(F32)[:, None] * inv[None, :]
    cos, sin = jnp.cos(ang), jnp.sin(ang)
    zero, one = jnp.zeros_like(sin), jnp.ones_like(cos)
    cos_h = jnp.concatenate([cos, cos], axis=1)
    sa_h = jnp.concatenate([-sin, zero], axis=1)
    sb_h = jnp.concatenate([zero, sin], axis=1)
    one_h = jnp.concatenate([one, one], axis=1)
    zero_h = jnp.concatenate([zero, zero], axis=1)
    return jnp.concatenate([cos_h, cos_h, sa_h, sa_h, sb_h, sb_h,
                            cos_h, one_h, sa_h, zero_h, sb_h, zero_h], axis=1)


def _project(x, g, w, tab, segs, tm):
    n, d = x.shape
    tt = tab.shape[0]
    nt = tt // tm
    kern = functools.partial(_proj_kernel, segs=segs)
    return pl.pallas_call(
        kern,
        grid=(n // tm,),
        in_specs=[pl.BlockSpec((tm, d), lambda i: (i, 0)),
                  pl.BlockSpec((1, d), lambda i: (0, 0)),
                  pl.BlockSpec(w.shape, lambda i: (0, 0)),
                  pl.BlockSpec((tm, tab.shape[1]), lambda i: (i % nt, 0))],
        out_specs=[pl.BlockSpec((tm, wd), lambda i: (i, 0)) for _, wd, _, _ in segs]
        + [pl.BlockSpec((tm, wd), lambda i: (i, 0)) for _, wd, _, twin in segs if twin],
        out_shape=[jax.ShapeDtypeStruct((n, wd), F32) for _, wd, _, _ in segs]
        + [jax.ShapeDtypeStruct((n, wd), BF16) for _, wd, _, twin in segs if twin],
        compiler_params=_cparams(("parallel",)),
        name="proj",
    )(x, g.reshape(1, d), w, tab)


def _pool_kernel(u_ref, h_ref, w_ref, s_ref, o_ref, ext_ref, *, t, pos0, rc):
    ext_ref[0:POOL_HIST_PAD, :] = h_ref[0]
    ext_ref[POOL_HIST_PAD:POOL_HIST_PAD + t, :] = u_ref[0]
    for r0 in range(0, t, rc):
        pos = pos0 + r0 + lax.broadcasted_iota(jnp.int32, (rc, 1), 0)
        for g, win in enumerate(POOL_WINDOWS):
            c0 = g * LANES
            u_new = ext_ref[POOL_HIST_PAD + r0:POOL_HIST_PAD + r0 + rc, c0:c0 + LANES]
            s = u_new
            for k in range(1, win):
                s = s + ext_ref[POOL_HIST_PAD + r0 - k:POOL_HIST_PAD + r0 - k + rc, c0:c0 + LANES]
            cnt = jnp.minimum(pos + 1, win).astype(F32)
            dlt = (s / cnt - u_new).astype(BF16)
            o_ref[0, r0:r0 + rc, c0:c0 + LANES] = _dot(dlt, w_ref[g]) * s_ref[:, c0:c0 + LANES]


def _pool_mix(u, hist, w_pool, pool_scale, pos0):
    b, t, c = u.shape
    rc = min(t, 256)
    hist16 = jnp.pad(hist, ((0, 0), (POOL_HIST_PAD - POOL_HIST, 0), (0, 0)))
    kern = functools.partial(_pool_kernel, t=t, pos0=pos0, rc=rc)
    return pl.pallas_call(
        kern,
        grid=(b,),
        in_specs=[pl.BlockSpec((1, t, c), lambda i: (i, 0, 0)),
                  pl.BlockSpec((1, POOL_HIST_PAD, c), lambda i: (i, 0, 0)),
                  pl.BlockSpec(w_pool.shape, lambda i: (0, 0, 0)),
                  pl.BlockSpec((1, c), lambda i: (0, 0))],
        out_specs=pl.BlockSpec((1, t, c), lambda i: (i, 0, 0)),
        out_shape=jax.ShapeDtypeStruct((b, t, c), F32),
        scratch_shapes=[pltpu.VMEM((POOL_HIST_PAD + t, c), F32)],
        compiler_params=_cparams(("parallel",)),
        name="pool_mix",
    )(u, hist16, w_pool, pool_scale.reshape(1, c))


def _dsa_kernel(q_ref, qi_ref, kw_ref, k_ref, v_ref, ki_ref, o_ref, lo_ref, hi_ref, bias_ref,
                *, tq, extents, n_keys, q_pos0, n_sel):
    i = pl.program_id(1)
    qpos = q_pos0 + i * tq + lax.broadcasted_iota(jnp.int32, (tq, 1), 0)
    lim = jnp.minimum((qpos // CHUNK + 1) * CHUNK, n_keys)
    kf = float(n_sel)
    low = lax.broadcasted_iota(jnp.int32, (tq, LANES), 1) < HEAD_DIM
    q = q_ref[0] * HEAD_DIM ** -0.5
    qi = qi_ref[0].astype(BF16)
    wi = kw_ref[0][:, HEAD_DIM:HEAD_DIM + IDX_HEADS] * (IDX_HEADS * HEAD_DIM) ** -0.5

    def body(ext):
        kpos = lax.broadcasted_iota(jnp.int32, (tq, ext), 1)
        adm = kpos < lim
        padded = ext > n_keys
        virt = float(max(n_keys - ext, 0))

        sidx = jnp.zeros((tq, ext), F32)
        ki = ki_ref[0, :ext, :]
        for h in range(IDX_HEADS):
            sh = _dot_nt(qi[:, h * HEAD_DIM:(h + 1) * HEAD_DIM], ki)
            sidx = sidx + jnp.maximum(sh, 0.0) * wi[:, h:h + 1]
        sm = jnp.where(adm, sidx, NEG_INF)
        if padded:
            real = kpos < n_keys
            sm = jnp.where(real, sm, PAD_SCORE)

        def count_gt(x):
            cnt = jnp.sum(jnp.where(sm > x, 1.0, 0.0), axis=1, keepdims=True)
            return cnt + jnp.where(x < NEG_INF, virt, 0.0) if virt else cnt

        def bracket(lo, hi):
            above = jnp.min(jnp.where(sm > lo, sm, BIG_POS), axis=1, keepdims=True)
            below = jnp.max(jnp.where(sm <= hi, sm, PAD_SCORE), axis=1, keepdims=True)
            if virt:
                above = jnp.minimum(above, jnp.where(lo < NEG_INF, NEG_INF, BIG_POS))
                below = jnp.maximum(below, jnp.where(hi >= NEG_INF, NEG_INF, PAD_SCORE))
            return above, below

        row_max = jnp.max(sm, axis=1, keepdims=True)
        row_min = jnp.min(jnp.where(real, sm, BIG_POS) if padded else sm, axis=1, keepdims=True)
        if virt:
            row_min = jnp.minimum(row_min, NEG_INF)
        adm_min = jnp.min(jnp.where(adm, sm, BIG_POS), axis=1, keepdims=True)
        few = count_gt(row_min) < kf
        tight = count_gt(adm_min) >= kf
        lo_ref[...] = jnp.where(few, PAD_SCORE, jnp.where(tight, adm_min, row_min))
        hi_ref[...] = jnp.where(few, row_min, jnp.where(tight, row_max, adm_min))

        def unresolved(lo, hi):
            above, below = bracket(lo, hi)
            return jnp.sum(jnp.where(above < below, 1, 0))

        def cond(carry):
            rounds, open_rows = carry
            return jnp.logical_and(open_rows > 0, rounds < BISECT_ROUNDS)

        def count_gt_wide(x):
            part = jnp.where(sm[:, :LANES] > x, 1.0, 0.0)
            for j in range(1, ext // LANES):
                part = part + jnp.where(sm[:, j * LANES:(j + 1) * LANES] > x, 1.0, 0.0)
            cnt = jnp.broadcast_to(jnp.sum(part, axis=1, keepdims=True), (tq, LANES))
            return cnt + jnp.where(x < NEG_INF, virt, 0.0) if virt else cnt

        def step(carry):
            rounds, _ = carry
            lo = jnp.broadcast_to(lo_ref[...], (tq, LANES))
            hi = jnp.broadcast_to(hi_ref[...], (tq, LANES))
            for _ in range(BISECT_STEPS):
                mid = 0.5 * lo + 0.5 * hi
                under = count_gt_wide(mid) < kf
                hi = jnp.where(under, mid, hi)
                lo = jnp.where(under, lo, mid)
            lo_ref[...] = lo[:, :1]
            hi_ref[...] = hi[:, :1]
            return rounds + 1, unresolved(lo_ref[...], hi_ref[...])

        lax.while_loop(cond, step, (jnp.int32(0), unresolved(lo_ref[...], hi_ref[...])))
        _, thr = bracket(lo_ref[...], hi_ref[...])

        gt = sm > thr
        eq = sm == thr
        need = kf - count_gt(thr)
        n_eq = jnp.sum(jnp.where(eq, 1.0, 0.0), axis=1, keepdims=True)
        crowded = jnp.sum(jnp.where(n_eq > need, 1, 0))

        @pl.when(crowded == 0)
        def _():
            keep = jnp.logical_and(jnp.logical_or(gt, eq), adm)
            bias_ref[:, :ext] = jnp.where(keep, 0.0, NEG_INF)

        @pl.when(crowded > 0)
        def _():
            rr = lax.broadcasted_iota(jnp.int32, (LANES, LANES), 0)
            cc = lax.broadcasted_iota(jnp.int32, (LANES, LANES), 1)
            prefix_ones = jnp.where(rr <= cc, 1.0, 0.0).astype(BF16)
            carry = jnp.zeros((tq, 1), F32)
            for j in range(ext // LANES):
                sl = slice(j * LANES, (j + 1) * LANES)
                eq_j = eq[:, sl]
                rank = _dot(jnp.where(eq_j, 1.0, 0.0).astype(BF16), prefix_ones) + carry
                keep = jnp.logical_or(gt[:, sl], jnp.logical_and(eq_j, rank <= need))
                bias_ref[:, sl] = jnp.where(jnp.logical_and(keep, adm[:, sl]), 0.0, NEG_INF)
                carry = rank[:, LANES - 1:LANES]

        bias = bias_ref[:, :ext]
        bias2 = jnp.concatenate([bias, bias], axis=0)

        kk = k_ref[0, :ext, :]
        vv = v_ref[0, :ext, :]
        low_k = lax.broadcasted_iota(jnp.int32, (ext, LANES), 1) < HEAD_DIM
        vsw = pltpu.roll(vv.astype(F32), HEAD_DIM, 1).astype(BF16)
        one = jnp.ones_like(vv)
        n_kv = LANES // HEAD_DIM
        for g in range(n_kv):
            v_lo = jnp.where(low_k, vv if g == 0 else vsw, one)
            v_hi = jnp.where(low_k, one, vsw if g == 0 else vv)
            stacks = []
            for odd in range(2):
                rows = []
                for m in range(2):
                    c = 2 * g + m
                    qc = q[:, c * LANES:(c + 1) * LANES]
                    if (odd == 1) != (g == 1):
                        qc = pltpu.roll(qc, HEAD_DIM, 1)
                    rows.append(jnp.where(low, qc, 0.0) if g == 0 else jnp.where(low, 0.0, qc))
                qs = jnp.concatenate(rows, axis=0).astype(BF16)
                sc = _dot_nt(qs, kk) + bias2
                p = jnp.exp(sc - jnp.max(sc, axis=1, keepdims=True))
                og = _dot(p.astype(BF16), v_hi if odd else v_lo)
                stacks.append(og / pltpu.roll(og, HEAD_DIM, 1))
            for m in range(2):
                c = 2 * g + m
                o_ref[0, :, c * LANES:(c + 1) * LANES] = jnp.where(
                    low, stacks[0][m * tq:(m + 1) * tq], stacks[1][m * tq:(m + 1) * tq])

    _for_tile_extent(i, tq, q_pos0, n_keys, extents, body)


def _dsa(q, qi, kiwi, k_all, v_all, ki_all, n_keys, q_pos0):
    b, t, dq = q.shape
    s_len = k_all.shape[1]
    tq = min(t, DSA_TQ)
    n_sel = min(DSA_TOPK, n_keys // 4)
    extents = _key_extents(t // tq, tq, q_pos0, n_keys, s_len, 4 * LANES)
    kern = functools.partial(_dsa_kernel, tq=tq, extents=extents, n_keys=n_keys, q_pos0=q_pos0, n_sel=n_sel)
    qspec = lambda w: pl.BlockSpec((1, tq, w), lambda bi, i: (bi, i, 0))
    kspec = lambda w: pl.BlockSpec((1, s_len, w), lambda bi, i: (bi, 0, 0))
    return pl.pallas_call(
        kern,
        grid=(b, t // tq),
        in_specs=[qspec(dq), qspec(qi.shape[2]), qspec(kiwi.shape[2]),
                  kspec(k_all.shape[2]), kspec(v_all.shape[2]), kspec(ki_all.shape[2])],
        out_specs=qspec(dq),
        out_shape=jax.ShapeDtypeStruct((b, t, dq), F32),
        scratch_shapes=[pltpu.VMEM((tq, 1), F32), pltpu.VMEM((tq, 1), F32), pltpu.VMEM((tq, s_len), F32)],
        compiler_params=_cparams(("parallel", "parallel")),
        name="dsa",
    )(q, qi, kiwi, k_all, v_all, ki_all)


def _sb_kernel(q_ref, k_ref, v_ref, o_ref, acc_ref, run_ref, *, tq, n_keys, q_pos0):
    i = pl.program_id(1)
    pairs = q_ref.shape[2] // LANES
    first_q = q_pos0 + i * tq
    qpos = first_q + lax.broadcasted_iota(jnp.int32, (tq, 1), 0)
    n_blocks = (jnp.minimum(first_q + tq - 1, n_keys) + LANES - 1) // LANES
    n_full = jnp.minimum(first_q, n_keys) // LANES
    low_q = lax.broadcasted_iota(jnp.int32, (tq, LANES), 1) < HEAD_DIM
    low_k = lax.broadcasted_iota(jnp.int32, (LANES, LANES), 1) < HEAD_DIM
    q = q_ref[0] * HEAD_DIM ** -0.5
    qm = []
    for p in range(pairs):
        qp = q[:, p * LANES:(p + 1) * LANES]
        qm.append((jnp.where(low_q, qp, 0.0).astype(BF16), jnp.where(low_q, 0.0, qp).astype(BF16)))
    rr = lax.broadcasted_iota(jnp.int32, (2 * LANES, 2 * LANES), 0)
    cc = lax.broadcasted_iota(jnp.int32, (2 * LANES, 2 * LANES), 1)
    rk = jnp.where(rr >= LANES, rr - LANES, rr)
    cs_rhs = jnp.where(jnp.logical_or(cc >= LANES, rk >= cc), 1.0, 0.0).astype(BF16)
    acc_ref[...] = jnp.zeros_like(acc_ref)
    run_ref[...] = jnp.zeros_like(run_ref)

    def block(j, masked):
        ks = pl.multiple_of(j * LANES, LANES)
        if masked:
            kpos = ks + lax.broadcasted_iota(jnp.int32, (1, LANES), 1)
            causal = jnp.logical_and(kpos < qpos, kpos < n_keys)
        for p in range(pairs):
            ps = slice(p * LANES, (p + 1) * LANES)
            kb = k_ref[0, pl.ds(ks, LANES), ps]
            vb = v_ref[0, pl.ds(ks, LANES), ps]
            zero = jnp.zeros_like(vb)
            v_cat = jnp.concatenate([jnp.where(low_k, vb, zero), jnp.where(low_k, zero, vb)], axis=0)
            a_parts = []
            for c in range(2):
                hsl = slice((2 * p + c) * LANES, (2 * p + c + 1) * LANES)
                z = _dot_nt(qm[p][c], kb)
                sp = jnp.maximum(z, 0.0) + jnp.log(1.0 + jnp.exp(-jnp.abs(z)))
                if masked:
                    sp = jnp.where(causal, sp, 0.0)
                hi = sp.astype(BF16)
                lo = (sp - hi.astype(F32)).astype(BF16)
                cs = _dot(jnp.concatenate([hi, lo], axis=1), cs_rhs)
                run = run_ref[:, hsl]
                a = jnp.exp(z - cs[:, :LANES] - run)
                if masked:
                    a = jnp.where(causal, a, 0.0)
                run_ref[:, hsl] = run + cs[:, LANES:]
                a_parts.append(a.astype(BF16))
            acc_ref[:, ps] += _dot(jnp.concatenate(a_parts, axis=1), v_cat)

    def masked_step(jj, carry):
        block(n_blocks - 1 - jj, True)
        return carry

    def full_pair(jj, carry):
        block(n_full - 1 - 2 * jj, False)
        block(n_full - 2 - 2 * jj, False)
        return carry

    lax.fori_loop(0, n_blocks - n_full, masked_step, 0)
    lax.fori_loop(0, n_full // 2, full_pair, 0)

    @pl.when(n_full % 2 == 1)
    def _():
        block(jnp.int32(0), False)

    o_ref[0] = acc_ref[...]


def _sb_attend(q, k_all, v_all, n_keys, q_pos0):
    b, t, d = q.shape
    s_len = k_all.shape[1]
    tq = min(t, ATTN_TQ)
    kern = functools.partial(_sb_kernel, tq=tq, n_keys=n_keys, q_pos0=q_pos0)
    return pl.pallas_call(
        kern,
        grid=(b, t // tq),
        in_specs=[pl.BlockSpec((1, tq, d), lambda bi, i: (bi, i, 0)),
                  pl.BlockSpec((1, s_len, d), lambda bi, i: (bi, 0, 0)),
                  pl.BlockSpec((1, s_len, d), lambda bi, i: (bi, 0, 0))],
        out_specs=pl.BlockSpec((1, tq, d), lambda bi, i: (bi, i, 0)),
        out_shape=jax.ShapeDtypeStruct((b, t, d), F32),
        scratch_shapes=[pltpu.VMEM((tq, d), F32), pltpu.VMEM((tq, 2 * d), F32)],
        compiler_params=_cparams(("parallel", "parallel")),
        name="sb_attend",
    )(q, k_all, v_all)


def _key_extents(nq, tq, q_pos0, n_keys, s_len, step):
    need = [min(n_keys, ((q_pos0 + (i + 1) * tq - 1) // CHUNK + 1) * CHUNK) for i in range(nq)]
    return tuple(sorted({min(s_len, -(-n // step) * step) for n in need}))


def _for_tile_extent(i, tq, q_pos0, n_keys, extents, body):
    need = jnp.minimum(n_keys, ((q_pos0 + (i + 1) * tq - 1) // CHUNK + 1) * CHUNK)
    prev = 0
    for ext in extents:
        pl.when(jnp.logical_and(need > prev, need <= ext))(functools.partial(body, ext))
        prev = ext


def _diff_kernel(q_ref, k_ref, v_ref, lam_ref, gain_ref, o_ref, *, tq, extents, n_keys, q_pos0, lam_init):
    i = pl.program_id(2)
    lp = lam_ref[...]
    lam = (jnp.exp(jnp.sum(lp[0:1] * lp[1:2], axis=1, keepdims=True))
           - jnp.exp(jnp.sum(lp[2:3] * lp[3:4], axis=1, keepdims=True)) + lam_init)
    qpos = q_pos0 + i * tq + lax.broadcasted_iota(jnp.int32, (tq, 1), 0)
    lim = jnp.minimum((qpos // CHUNK + 1) * CHUNK, n_keys)
    low = lax.broadcasted_iota(jnp.int32, (tq, LANES), 1) < HEAD_DIM
    q = q_ref[0] * HEAD_DIM ** -0.5
    qm = (jnp.where(low, q, 0.0).astype(BF16), jnp.where(low, 0.0, q).astype(BF16))

    def body(ext):
        mask = lax.broadcasted_iota(jnp.int32, (tq, ext), 1) < lim
        kk = k_ref[0, :ext, :]
        probs = []
        for c in range(2):
            sc = jnp.where(mask, _dot_nt(qm[c], kk), NEG_INF)
            p = jnp.exp(sc - jnp.max(sc, axis=1, keepdims=True))
            probs.append(p * (1.0 / jnp.sum(p, axis=1, keepdims=True)))
        a = probs[0] - lam * probs[1]
        o = _dot(a.astype(BF16), v_ref[0, :ext, :])
        o = o * lax.rsqrt(jnp.mean(o * o, axis=-1, keepdims=True) + NORM_EPS) * gain_ref[...] * (1.0 - lam_init)
        o_ref[0] = o

    _for_tile_extent(i, tq, q_pos0, n_keys, extents, body)


def _diff_attend(q, k_all, v_all, diff_lambda, gain, n_keys, q_pos0, lam_init):
    b, t, d = q.shape
    s_len = k_all.shape[1]
    tq = min(t, ATTN_TQ)
    extents = _key_extents(t // tq, tq, q_pos0, n_keys, s_len, 2 * LANES)
    kern = functools.partial(_diff_kernel, tq=tq, extents=extents, n_keys=n_keys, q_pos0=q_pos0, lam_init=lam_init)
    return pl.pallas_call(
        kern,
        grid=(b, d // LANES, t // tq),
        in_specs=[pl.BlockSpec((1, tq, LANES), lambda bi, h, i: (bi, i, h)),
                  pl.BlockSpec((1, s_len, LANES), lambda bi, h, i: (bi, 0, h)),
                  pl.BlockSpec((1, s_len, LANES), lambda bi, h, i: (bi, 0, h)),
                  pl.BlockSpec(diff_lambda.shape, lambda bi, h, i: (0, 0)),
                  pl.BlockSpec((1, LANES), lambda bi, h, i: (0, 0))],
        out_specs=pl.BlockSpec((1, tq, LANES), lambda bi, h, i: (bi, i, h)),
        out_shape=jax.ShapeDtypeStruct((b, t, d), F32),
        compiler_params=_cparams(("parallel", "parallel", "parallel")),
        name="diff_attend",
    )(q, k_all, v_all, diff_lambda, gain.reshape(1, LANES))


def _out_kernel(h_ref, a_ref, b_ref, wa_ref, wb_ref, o_ref):
    o_ref[...] = (h_ref[...] + _dot(a_ref[...].astype(BF16), wa_ref[...])
                  + _dot(b_ref[...].astype(BF16), wb_ref[...]))


def _out_proj(h, a, bmix, w_out, tm):
    n, d = h.shape
    ca = a.shape[1]
    wa, wb = w_out[:ca], w_out[ca:]
    row = lambda w: pl.BlockSpec((tm, w), lambda i: (i, 0))
    return pl.pallas_call(
        _out_kernel,
        grid=(n // tm,),
        in_specs=[row(d), row(ca), row(bmix.shape[1]),
                  pl.BlockSpec(wa.shape, lambda i: (0, 0)), pl.BlockSpec(wb.shape, lambda i: (0, 0))],
        out_specs=row(d),
        out_shape=jax.ShapeDtypeStruct((n, d), F32),
        compiler_params=_cparams(("parallel",)),
        name="out_proj",
    )(h, a, bmix, wa, wb)


def _router_kernel(h_ref, g_ref, w_ref, b_ref, xn_ref, gate_ref):
    xn = _rms(h_ref[...], g_ref[...]).astype(BF16)
    xn_ref[...] = xn
    logits = _dot(xn, w_ref[...]) + b_ref[...]
    lane = lax.broadcasted_iota(jnp.int32, logits.shape, 1)
    is_group = lane < MOE_GROUPS
    gl = jnp.where(is_group, logits, NEG_INF)
    g_max = jnp.max(gl, axis=1, keepdims=True)
    g_sel = jnp.min(jnp.where(gl == g_max, lane, LANES), axis=1, keepdims=True)
    g_gate = 1.0 / jnp.sum(jnp.where(is_group, jnp.exp(gl - g_max), 0.0), axis=1, keepdims=True)
    in_group = jnp.logical_and(lane >= GATE_COL0, (lane - GATE_COL0) // MOE_EPG == g_sel)
    in_group = jnp.logical_and(in_group, lane < GATE_COL0 + MOE_EXPERTS)
    el = jnp.where(in_group, logits, NEG_INF)
    top1 = jnp.max(el, axis=1, keepdims=True)
    i1 = jnp.min(jnp.where(jnp.logical_and(in_group, el == top1), lane, LANES), axis=1, keepdims=True)
    rest = jnp.logical_and(in_group, lane != i1)
    el2 = jnp.where(rest, logits, NEG_INF)
    top2 = jnp.max(el2, axis=1, keepdims=True)
    i2 = jnp.min(jnp.where(jnp.logical_and(rest, el2 == top2), lane, LANES), axis=1, keepdims=True)
    e2 = jnp.exp(top2 - top1)
    w1 = g_gate / (1.0 + e2)
    gate_ref[...] = jnp.where(lane == i1, w1, jnp.where(lane == i2, w1 * e2, 0.0))


def _expert_kernel(xn_ref, gate_ref, w1_ref, w3_ref, w2_ref, h_ref, gf_ref, o_ref, acc_ref, *, final_norm):
    e = pl.program_id(1)

    @pl.when(e == 0)
    def _():
        acc_ref[...] = jnp.zeros_like(acc_ref)

    gate = gate_ref[...]
    lane = lax.broadcasted_iota(jnp.int32, gate.shape, 1)
    ge = jnp.sum(jnp.where(lane == GATE_COL0 + e, gate, 0.0), axis=1, keepdims=True)
    xn = xn_ref[...]
    a = _dot(xn, w1_ref[0].astype(BF16))
    b = _dot(xn, w3_ref[0].astype(BF16))
    act = (a * (1.0 / (1.0 + jnp.exp(-a))) * b).astype(BF16)
    acc_ref[...] += ge * _dot(act, w2_ref[0].astype(BF16))

    @pl.when(e == pl.num_programs(1) - 1)
    def _():
        y = h_ref[...] + acc_ref[...]
        o_ref[...] = _rms(y, gf_ref[...]) if final_norm else y


def _moe(h, g_ffn, wr, br, w1, w3, w2, g_final, final_norm, tm):
    n, d = h.shape
    row = lambda w: pl.BlockSpec((tm, w), lambda i: (i, 0))
    xn, gate = pl.pallas_call(
        _router_kernel,
        grid=(n // tm,),
        in_specs=[row(d), pl.BlockSpec((1, d), lambda i: (0, 0)),
                  pl.BlockSpec(wr.shape, lambda i: (0, 0)), pl.BlockSpec((1, LANES), lambda i: (0, 0))],
        out_specs=[row(d), row(LANES)],
        out_shape=[jax.ShapeDtypeStruct((n, d), BF16), jax.ShapeDtypeStruct((n, LANES), F32)],
        compiler_params=_cparams(("parallel",)),
        name="moe_router",
    )(h, g_ffn.reshape(1, d), wr, br)
    n_e = w1.shape[0]
    tme = MOE_TM if n % MOE_TM == 0 else tm
    row2 = lambda w: pl.BlockSpec((tme, w), lambda i, e: (i, 0))
    kern = functools.partial(_expert_kernel, final_norm=final_norm)
    return pl.pallas_call(
        kern,
        grid=(n // tme, n_e),
        in_specs=[row2(d), row2(LANES),
                  pl.BlockSpec((1,) + w1.shape[1:], lambda i, e: (e, 0, 0)),
                  pl.BlockSpec((1,) + w3.shape[1:], lambda i, e: (e, 0, 0)),
                  pl.BlockSpec((1,) + w2.shape[1:], lambda i, e: (e, 0, 0)),
                  row2(d), pl.BlockSpec((1, d), lambda i, e: (0, 0))],
        out_specs=row2(d),
        out_shape=jax.ShapeDtypeStruct((n, d), F32),
        scratch_shapes=[pltpu.VMEM((tme, d), F32)],
        compiler_params=_cparams(("parallel", "arbitrary")),
        name="moe_experts",
    )(xn, gate, w1, w3, w2, h, g_final.reshape(1, d))


EVEN_SEGS = ((0, 512, None, False), (512, 512, "full", False), (1024, 128, "full", True), (1152, 128, None, True),
             (1280, 256, "full", False), (1536, 128, "half", False))
ODD_SEGS = ((0, 512, None, False), (512, 512, None, True), (1024, 512, None, True),
            (1536, 512, "full", False), (2048, 512, "full", True), (2560, 512, None, True))


def _cat_keys(hist, new):
    allk = jnp.concatenate([hist, new], axis=1) if hist is not None else new
    n_keys = allk.shape[1]
    pad = -n_keys % LANES
    if pad:
        allk = jnp.pad(allk, ((0, 0), (0, pad), (0, 0)))
    return allk.astype(BF16), n_keys


def kernel(x_prompt, x_sample, cache_pool, cache_dsa_k, cache_dsa_v, cache_idx_k, cache_sb_k, cache_sb_v,
           cache_diff_k, cache_diff_v, norm_mix, norm_ffn, norm_final, w_in_even, w_pool, pool_scale,
           w_out_even, w_in_odd, diff_lambda, diff_subln, w_out_odd, moe_w_group, moe_b_group,
           moe_w_expert, moe_b_expert, moe_w1, moe_w3, moe_w2):
    b, t, d = x_prompt.shape
    bd, td, _ = x_sample.shape
    past = cache_dsa_k.shape[2]
    depth = norm_mix.shape[0]
    groups = ((b, t, 0, min(512, b * t)), (bd, td, past, bd * td))

    tabs = []
    for (gb, gt, p0, tm) in groups:
        tab = _rope_tables(p0 + jnp.arange(gt, dtype=jnp.int32))
        if tm > gt:
            tab = jnp.tile(tab, (tm // gt, 1))
        tabs.append(tab)

    hs = [x_prompt.reshape(b * t, d), x_sample.reshape(bd * td, d)]
    outs = [dict(), dict()]
    for l in range(depth):
        li = l // 2
        last = l == depth - 1
        if l % 2 == 0:
            n_in = w_in_even.shape[2]
            w_in = jnp.pad(w_in_even[li], ((0, 0), (0, -n_in % LANES))).astype(BF16)
            w_out = w_out_even[li].astype(BF16)
            wp = w_pool[li].astype(BF16)
        else:
            w_in = w_in_odd[li].astype(BF16)
            w_out = w_out_odd[li].astype(BF16)
            lam_init = 0.8 - 0.6 * math.exp(-0.3 * l)
        wr = jnp.concatenate([moe_w_group[l]] + [moe_w_expert[l, g] for g in range(MOE_GROUPS)], axis=1)
        wr = jnp.pad(wr, ((0, 0), (0, LANES - wr.shape[1]))).astype(BF16)
        br = jnp.concatenate([moe_b_group[l], moe_b_expert[l].reshape(-1)])
        br = jnp.pad(br, (0, LANES - br.shape[0])).reshape(1, LANES).astype(F32)

        for gi, (gb, gt, p0, tm) in enumerate(groups):
            h = hs[gi]
            o = outs[gi]
            sample = gi == 1
            r3 = lambda x: x.reshape(gb, gt, x.shape[-1])
            if l % 2 == 0:
                u, q, k, v, qi, kiwi, k16, v16 = [
                    r3(x) for x in _project(h, norm_mix[l], w_in, tabs[gi], EVEN_SEGS, tm)]
                ki = kiwi[..., :HEAD_DIM]
                hist = cache_pool[li] if sample else jnp.zeros((gb, POOL_HIST, u.shape[2]), F32)
                a_out = _pool_mix(u, hist, wp, pool_scale[li], p0)
                if sample:
                    k_all, n_keys = _cat_keys(cache_dsa_k[li].reshape(gb, past, -1), k)
                    v_all, _ = _cat_keys(cache_dsa_v[li].reshape(gb, past, -1), v)
                    ki_all, _ = _cat_keys(cache_idx_k[li], ki)
                else:
                    (k_all, n_keys), (v_all, _), (ki_all, _) = _cat_keys(None, k16), _cat_keys(None, v16), _cat_keys(None, ki)
                b_out = _dsa(q, qi, kiwi, k_all, v_all, ki_all, n_keys, p0)
                o.setdefault("pool", []).append(jnp.concatenate([hist, u], axis=1)[:, -POOL_HIST:])
                o.setdefault("dsa_k", []).append(k.reshape(gb, gt, -1, HEAD_DIM))
                o.setdefault("dsa_v", []).append(v.reshape(gb, gt, -1, HEAD_DIM))
                o.setdefault("idx_k", []).append(ki)
                mix_a, mix_b = a_out, b_out
            else:
                sq, sk, sv, dq, dk, dv, sk16, sv16, dk16, dv16 = [
                    r3(x) for x in _project(h, norm_mix[l], w_in, tabs[gi], ODD_SEGS, tm)]
                if sample:
                    sk_all, n_keys = _cat_keys(cache_sb_k[li].reshape(gb, past, -1), sk)
                    sv_all, _ = _cat_keys(cache_sb_v[li].reshape(gb, past, -1), sv)
                    dk_all, _ = _cat_keys(cache_diff_k[li].reshape(gb, past, -1), dk)
                    dv_all, _ = _cat_keys(cache_diff_v[li].reshape(gb, past, -1), dv)
                else:
                    (sk_all, n_keys), (sv_all, _) = _cat_keys(None, sk16), _cat_keys(None, sv16)
                    (dk_all, _), (dv_all, _) = _cat_keys(None, dk16), _cat_keys(None, dv16)
                c_out = _sb_attend(sq, sk_all, sv_all, n_keys, p0)
                d_out = _diff_attend(dq, dk_all, dv_all, diff_lambda[li], diff_subln[li], n_keys, p0, lam_init)
                n_sb = sk.shape[2] // HEAD_DIM
                n_df = dk.shape[2] // (2 * HEAD_DIM)
                o.setdefault("sb_k", []).append(sk.reshape(gb, gt, n_sb, HEAD_DIM))
                o.setdefault("sb_v", []).append(sv.reshape(gb, gt, n_sb, HEAD_DIM))
                o.setdefault("diff_k", []).append(dk.reshape(gb, gt, n_df, 2, HEAD_DIM))
                o.setdefault("diff_v", []).append(dv.reshape(gb, gt, n_df, 2 * HEAD_DIM))
                mix_a, mix_b = c_out, d_out
            h = _out_proj(h, mix_a.reshape(gb * gt, -1), mix_b.reshape(gb * gt, -1), w_out, tm)
            hs[gi] = _moe(h, norm_ffn[l], wr, br, moe_w1[l], moe_w3[l], moe_w2[l], norm_final, last, tm)

    names = ("pool", "dsa_k", "dsa_v", "idx_k", "sb_k", "sb_v", "diff_k", "diff_v")
    res = [hs[0].reshape(b, t, d), hs[1].reshape(bd, td, d)]
    for o in outs:
        res += [jnp.stack(o[nm]) for nm in names]
    return tuple(res)
```

```python
import functools
import math

import jax
import jax.numpy as jnp
import numpy as np
from jax import lax
from jax.experimental import pallas as pl
from jax.experimental.pallas import tpu as pltpu

F32 = jnp.float32
BF16 = jnp.bfloat16

LANES = 128
HEAD_DIM = 64
CHUNK = 64
ROPE_THETA = 10000.0
NORM_EPS = 1e-6
NEG_INF = -1e30
PAD_SCORE = -3e38
BIG_POS = 3e38
POOL_WINDOWS = (2, 4, 8, 16)
POOL_HIST = 15
POOL_HIST_PAD = 16
DSA_TOPK = 256
IDX_HEADS = 4
MOE_GROUPS = 4
MOE_EPG = 4
MOE_EXPERTS = 16
GATE_COL0 = MOE_GROUPS
VMEM_LIMIT = 56 * 1024 * 1024
BISECT_STEPS = 8
BISECT_ROUNDS = 48
ATTN_TQ = 256
DSA_TQ = 128
MOE_TM = 1024
DIFF_LANES = 2 * LANES


def _cparams(sem):
    return pltpu.CompilerParams(dimension_semantics=sem, vmem_limit_bytes=VMEM_LIMIT)


def _dot(a, b):
    return jnp.dot(a, b, preferred_element_type=F32)


def _dot_nt(a, b):
    return lax.dot_general(a, b, (((1,), (1,)), ((), ())), preferred_element_type=F32)


def _rms(x, g):
    ms = jnp.mean(x * x, axis=-1, keepdims=True)
    return x * lax.rsqrt(ms + NORM_EPS) * g


def _proj_kernel(x_ref, g_ref, w_ref, tab_ref, *out_refs, segs):
    xn = _rms(x_ref[...], g_ref[...]).astype(BF16)
    copies = iter(out_refs[len(segs):])
    for o_ref, (c0, width, mode, twin) in zip(out_refs, segs):
        t_ref = next(copies) if twin else None
        y = _dot(xn, w_ref[:, c0:c0 + width])
        if mode is None:
            o_ref[...] = y
            if twin:
                t_ref[...] = y.astype(BF16)
            continue
        t0 = 0 if mode == "full" else 3 * LANES
        cos = tab_ref[:, t0:t0 + LANES]
        sin_a = tab_ref[:, t0 + LANES:t0 + 2 * LANES]
        sin_b = tab_ref[:, t0 + 2 * LANES:t0 + 3 * LANES]
        for c in range(0, width, LANES):
            yc = y[:, c:c + LANES]
            yr = (yc * cos + pltpu.roll(yc, LANES - HEAD_DIM // 2, 1) * sin_a
                  + pltpu.roll(yc, HEAD_DIM // 2, 1) * sin_b)
            o_ref[:, c:c + LANES] = yr
            if twin:
                t_ref[:, c:c + LANES] = yr.astype(BF16)


def _rope_tables(pos):
    half = HEAD_DIM // 2
    inv = ROPE_THETA ** (-jnp.arange(half, dtype=F32) / half)
    ang = pos.astype(F32)[:, None] * inv[None, :]
    cos, sin = jnp.cos(ang), jnp.sin(ang)
    zero, one = jnp.zeros_like(sin), jnp.ones_like(cos)
    cos_h = jnp.concatenate([cos, cos], axis=1)
    sa_h = jnp.concatenate([-sin, zero], axis=1)
    sb_h = jnp.concatenate([zero, sin], axis=1)
    one_h = jnp.concatenate([one, one], axis=1)
    zero_h = jnp.concatenate([zero, zero], axis=1)
    return jnp.concatenate([cos_h, cos_h, sa_h, sa_h, sb_h, sb_h,
                            cos_h, one_h, sa_h, zero_h, sb_h, zero_h], axis=1)


def _project(x, g, w, tab, segs, tm):
    n, d = x.shape
    tt = tab.shape[0]
    nt = tt // tm
    kern = functools.partial(_proj_kernel, segs=segs)
    return pl.pallas_call(
        kern,
        grid=(n // tm,),
        in_specs=[pl.BlockSpec((tm, d), lambda i: (i, 0)),
                  pl.BlockSpec((1, d), lambda i: (0, 0)),
                  pl.BlockSpec(w.shape, lambda i: (0, 0)),
                  pl.BlockSpec((tm, tab.shape[1]), lambda i: (i % nt, 0))],
        out_specs=[pl.BlockSpec((tm, wd), lambda i: (i, 0)) for _, wd, _, _ in segs]
        + [pl.BlockSpec((tm, wd), lambda i: (i, 0)) for _, wd, _, twin in segs if twin],
        out_shape=[jax.ShapeDtypeStruct((n, wd), F32) for _, wd, _, _ in segs]
        + [jax.ShapeDtypeStruct((n, wd), BF16) for _, wd, _, twin in segs if twin],
        compiler_params=_cparams(("parallel",)),
        name="proj",
    )(x, g.reshape(1, d), w, tab)


def _pool_kernel(u_ref, h_ref, w_ref, s_ref, o_ref, ext_ref, *, t, pos0, rc):
    ext_ref[0:POOL_HIST_PAD, :] = h_ref[0]
    ext_ref[POOL_HIST_PAD:POOL_HIST_PAD + t, :] = u_ref[0]
    for r0 in range(0, t, rc):
        pos = pos0 + r0 + lax.broadcasted_iota(jnp.int32, (rc, 1), 0)
        for g, win in enumerate(POOL_WINDOWS):
            c0 = g * LANES
            u_new = ext_ref[POOL_HIST_PAD + r0:POOL_HIST_PAD + r0 + rc, c0:c0 + LANES]
            s = u_new
            for k in range(1, win):
                s = s + ext_ref[POOL_HIST_PAD + r0 - k:POOL_HIST_PAD + r0 - k + rc, c0:c0 + LANES]
            cnt = jnp.minimum(pos + 1, win).astype(F32)
            dlt = (s / cnt - u_new).astype(BF16)
            o_ref[0, r0:r0 + rc, c0:c0 + LANES] = _dot(dlt, w_ref[g]) * s_ref[:, c0:c0 + LANES]


def _pool_mix(u, hist, w_pool, pool_scale, pos0):
    b, t, c = u.shape
    rc = min(t, 256)
    hist16 = jnp.pad(hist, ((0, 0), (POOL_HIST_PAD - POOL_HIST, 0), (0, 0)))
    kern = functools.partial(_pool_kernel, t=t, pos0=pos0, rc=rc)
    return pl.pallas_call(
        kern,
        grid=(b,),
        in_specs=[pl.BlockSpec((1, t, c), lambda i: (i, 0, 0)),
                  pl.BlockSpec((1, POOL_HIST_PAD, c), lambda i: (i, 0, 0)),
                  pl.BlockSpec(w_pool.shape, lambda i: (0, 0, 0)),
                  pl.BlockSpec((1, c), lambda i: (0, 0))],
        out_specs=pl.BlockSpec((1, t, c), lambda i: (i, 0, 0)),
        out_shape=jax.ShapeDtypeStruct((b, t, c), F32),
        scratch_shapes=[pltpu.VMEM((POOL_HIST_PAD + t, c), F32)],
        compiler_params=_cparams(("parallel",)),
        name="pool_mix",
    )(u, hist16, w_pool, pool_scale.reshape(1, c))


def _dsa_kernel(q_ref, qi_ref, kw_ref, k_ref, v_ref, ki_ref, o_ref, lo_ref, hi_ref, bias_ref,
                *, tq, extents, n_keys, q_pos0, n_sel):
    i = pl.program_id(1)
    qpos = q_pos0 + i * tq + lax.broadcasted_iota(jnp.int32, (tq, 1), 0)
    lim = jnp.minimum((qpos // CHUNK + 1) * CHUNK, n_keys)
    kf = float(n_sel)
    low = lax.broadcasted_iota(jnp.int32, (tq, LANES), 1) < HEAD_DIM
    q = q_ref[0] * HEAD_DIM ** -0.5
    qi = qi_ref[0].astype(BF16)
    wi = kw_ref[0][:, HEAD_DIM:HEAD_DIM + IDX_HEADS] * (IDX_HEADS * HEAD_DIM) ** -0.5

    def body(ext):
        kpos = lax.broadcasted_iota(jnp.int32, (tq, ext), 1)
        adm = kpos < lim
        padded = ext > n_keys
        virt = float(max(n_keys - ext, 0))

        sidx = jnp.zeros((tq, ext), F32)
        ki = ki_ref[0, :ext, :]
        for h in range(IDX_HEADS):
            sh = _dot_nt(qi[:, h * HEAD_DIM:(h + 1) * HEAD_DIM], ki)
            sidx = sidx + jnp.maximum(sh, 0.0) * wi[:, h:h + 1]
        sm = jnp.where(adm, sidx, NEG_INF)
        if padded:
            real = kpos < n_keys
            sm = jnp.where(real, sm, PAD_SCORE)

        def count_gt(x):
            cnt = jnp.sum(jnp.where(sm > x, 1.0, 0.0), axis=1, keepdims=True)
            return cnt + jnp.where(x < NEG_INF, virt, 0.0) if virt else cnt

        def bracket(lo, hi):
            above = jnp.min(jnp.where(sm > lo, sm, BIG_POS), axis=1, keepdims=True)
            below = jnp.max(jnp.where(sm <= hi, sm, PAD_SCORE), axis=1, keepdims=True)
            if virt:
                above = jnp.minimum(above, jnp.where(lo < NEG_INF, NEG_INF, BIG_POS))
                below = jnp.maximum(below, jnp.where(hi >= NEG_INF, NEG_INF, PAD_SCORE))
            return above, below

        row_max = jnp.max(sm, axis=1, keepdims=True)
        row_min = jnp.min(jnp.where(real, sm, BIG_POS) if padded else sm, axis=1, keepdims=True)
        if virt:
            row_min = jnp.minimum(row_min, NEG_INF)
        adm_min = jnp.min(jnp.where(adm, sm, BIG_POS), axis=1, keepdims=True)
        few = count_gt(row_min) < kf
        tight = count_gt(adm_min) >= kf
        lo_ref[...] = jnp.where(few, PAD_SCORE, jnp.where(tight, adm_min, row_min))
        hi_ref[...] = jnp.where(few, row_min, jnp.where(tight, row_max, adm_min))

        def unresolved(lo, hi):
            above, below = bracket(lo, hi)
            return jnp.sum(jnp.where(above < below, 1, 0))

        def cond(carry):
            rounds, open_rows = carry
            return jnp.logical_and(open_rows > 0, rounds < BISECT_ROUNDS)

        def count_gt_wide(x):
            part = jnp.where(sm[:, :LANES] > x, 1.0, 0.0)
            for j in range(1, ext // LANES):
                part = part + jnp.where(sm[:, j * LANES:(j + 1) * LANES] > x, 1.0, 0.0)
            cnt = jnp.broadcast_to(jnp.sum(part, axis=1, keepdims=True), (tq, LANES))
            return cnt + jnp.where(x < NEG_INF, virt, 0.0) if virt else cnt

        def step(carry):
            rounds, _ = carry
            lo = jnp.broadcast_to(lo_ref[...], (tq, LANES))
            hi = jnp.broadcast_to(hi_ref[...], (tq, LANES))
            for _ in range(BISECT_STEPS):
                mid = 0.5 * lo + 0.5 * hi
                under = count_gt_wide(mid) < kf
                hi = jnp.where(under, mid, hi)
                lo = jnp.where(under, lo, mid)
            lo_ref[...] = lo[:, :1]
            hi_ref[...] = hi[:, :1]
            return rounds + 1, unresolved(lo_ref[...], hi_ref[...])

        lax.while_loop(cond, step, (jnp.int32(0), unresolved(lo_ref[...], hi_ref[...])))
        _, thr = bracket(lo_ref[...], hi_ref[...])

        gt = sm > thr
        eq = sm == thr
        need = kf - count_gt(thr)
        n_eq = jnp.sum(jnp.where(eq, 1.0, 0.0), axis=1, keepdims=True)
        crowded = jnp.sum(jnp.where(n_eq > need, 1, 0))

        @pl.when(crowded == 0)
        def _():
            keep = jnp.logical_and(jnp.logical_or(gt, eq), adm)
            bias_ref[:, :ext] = jnp.where(keep, 0.0, NEG_INF)

        @pl.when(crowded > 0)
        def _():
            rr = lax.broadcasted_iota(jnp.int32, (LANES, LANES), 0)
            cc = lax.broadcasted_iota(jnp.int32, (LANES, LANES), 1)
            prefix_ones = jnp.where(rr <= cc, 1.0, 0.0).astype(BF16)
            carry = jnp.zeros((tq, 1), F32)
            for j in range(ext // LANES):
                sl = slice(j * LANES, (j + 1) * LANES)
                eq_j = eq[:, sl]
                rank = _dot(jnp.where(eq_j, 1.0, 0.0).astype(BF16), prefix_ones) + carry
                keep = jnp.logical_or(gt[:, sl], jnp.logical_and(eq_j, rank <= need))
                bias_ref[:, sl] = jnp.where(jnp.logical_and(keep, adm[:, sl]), 0.0, NEG_INF)
                carry = rank[:, LANES - 1:LANES]

        bias = bias_ref[:, :ext]
        bias2 = jnp.concatenate([bias, bias], axis=0)

        kk = k_ref[0, :ext, :]
        vv = v_ref[0, :ext, :]
        low_k = lax.broadcasted_iota(jnp.int32, (ext, LANES), 1) < HEAD_DIM
        vsw = pltpu.roll(vv.astype(F32), HEAD_DIM, 1).astype(BF16)
        one = jnp.ones_like(vv)
        n_kv = LANES // HEAD_DIM
        for g in range(n_kv):
            v_lo = jnp.where(low_k, vv if g == 0 else vsw, one)
            v_hi = jnp.where(low_k, one, vsw if g == 0 else vv)
            stacks = []
            for odd in range(2):
                rows = []
                for m in range(2):
                    c = 2 * g + m
                    qc = q[:, c * LANES:(c + 1) * LANES]
                    if (odd == 1) != (g == 1):
                        qc = pltpu.roll(qc, HEAD_DIM, 1)
                    rows.append(jnp.where(low, qc, 0.0) if g == 0 else jnp.where(low, 0.0, qc))
                qs = jnp.concatenate(rows, axis=0).astype(BF16)
                sc = _dot_nt(qs, kk) + bias2
                p = jnp.exp(sc - jnp.max(sc, axis=1, keepdims=True))
                og = _dot(p.astype(BF16), v_hi if odd else v_lo)
                stacks.append(og / pltpu.roll(og, HEAD_DIM, 1))
            for m in range(2):
                c = 2 * g + m
                o_ref[0, :, c * LANES:(c + 1) * LANES] = jnp.where(
                    low, stacks[0][m * tq:(m + 1) * tq], stacks[1][m * tq:(m + 1) * tq])

    _for_tile_extent(i, tq, q_pos0, n_keys, extents, body)


def _dsa(q, qi, kiwi, k_all, v_all, ki_all, n_keys, q_pos0):
    b, t, dq = q.shape
    s_len = k_all.shape[1]
    tq = min(t, DSA_TQ)
    n_sel = min(DSA_TOPK, n_keys // 4)
    extents = _key_extents(t // tq, tq, q_pos0, n_keys, s_len, 4 * LANES)
    kern = functools.partial(_dsa_kernel, tq=tq, extents=extents, n_keys=n_keys, q_pos0=q_pos0, n_sel=n_sel)
    qspec = lambda w: pl.BlockSpec((1, tq, w), lambda bi, i: (bi, i, 0))
    kspec = lambda w: pl.BlockSpec((1, s_len, w), lambda bi, i: (bi, 0, 0))
    return pl.pallas_call(
        kern,
        grid=(b, t // tq),
        in_specs=[qspec(dq), qspec(qi.shape[2]), qspec(kiwi.shape[2]),
                  kspec(k_all.shape[2]), kspec(v_all.shape[2]), kspec(ki_all.shape[2])],
        out_specs=qspec(dq),
        out_shape=jax.ShapeDtypeStruct((b, t, dq), F32),
        scratch_shapes=[pltpu.VMEM((tq, 1), F32), pltpu.VMEM((tq, 1), F32), pltpu.VMEM((tq, s_len), F32)],
        compiler_params=_cparams(("parallel", "parallel")),
        name="dsa",
    )(q, qi, kiwi, k_all, v_all, ki_all)


def _sb_kernel(q_ref, k_ref, v_ref, o_ref, acc_ref, run_ref, *, tq, n_keys, q_pos0):
    i = pl.program_id(1)
    pairs = q_ref.shape[2] // LANES
    first_q = q_pos0 + i * tq
    qpos = first_q + lax.broadcasted_iota(jnp.int32, (tq, 1), 0)
    n_blocks = (jnp.minimum(first_q + tq - 1, n_keys) + LANES - 1) // LANES
    n_full = jnp.minimum(first_q, n_keys) // LANES
    low_q = lax.broadcasted_iota(jnp.int32, (tq, LANES), 1) < HEAD_DIM
    low_k = lax.broadcasted_iota(jnp.int32, (LANES, LANES), 1) < HEAD_DIM
    q = q_ref[0] * HEAD_DIM ** -0.5
    qm = []
    for p in range(pairs):
        qp = q[:, p * LANES:(p + 1) * LANES]
        qm.append((jnp.where(low_q, qp, 0.0).astype(BF16), jnp.where(low_q, 0.0, qp).astype(BF16)))
    rr = lax.broadcasted_iota(jnp.int32, (2 * LANES, 2 * LANES), 0)
    cc = lax.broadcasted_iota(jnp.int32, (2 * LANES, 2 * LANES), 1)
    rk = jnp.where(rr >= LANES, rr - LANES, rr)
    cs_rhs = jnp.where(jnp.logical_or(cc >= LANES, rk >= cc), 1.0, 0.0).astype(BF16)
    acc_ref[...] = jnp.zeros_like(acc_ref)
    run_ref[...] = jnp.zeros_like(run_ref)

    def block(j, masked):
        ks = pl.multiple_of(j * LANES, LANES)
        if masked:
            kpos = ks + lax.broadcasted_iota(jnp.int32, (1, LANES), 1)
            causal = jnp.logical_and(kpos < qpos, kpos < n_keys)
        heads = [(p, c) for p in range(pairs) for c in range(2)]
        lanes = lambda n: slice(n * LANES, (n + 1) * LANES)
        zs = [_dot_nt(qm[p][c], k_ref[0, pl.ds(ks, LANES), lanes(p)]) for p, c in heads]
        css = []
        for z in zs:
            sp = jnp.maximum(z, 0.0) + jnp.log(1.0 + jnp.exp(-jnp.abs(z)))
            if masked:
                sp = jnp.where(causal, sp, 0.0)
            hi = sp.astype(BF16)
            lo = (sp - hi.astype(F32)).astype(BF16)
            css.append(_dot(jnp.concatenate([hi, lo], axis=1), cs_rhs))
        probs = []
        for n, (z, cs) in enumerate(zip(zs, css)):
            run = run_ref[:, lanes(n)]
            a = jnp.exp(z - cs[:, :LANES] - run)
            if masked:
                a = jnp.where(causal, a, 0.0)
            run_ref[:, lanes(n)] = run + cs[:, LANES:]
            probs.append(a.astype(BF16))
        for p in range(pairs):
            vb = v_ref[0, pl.ds(ks, LANES), lanes(p)]
            zero = jnp.zeros_like(vb)
            v_cat = jnp.concatenate([jnp.where(low_k, vb, zero), jnp.where(low_k, zero, vb)], axis=0)
            acc_ref[:, lanes(p)] += _dot(jnp.concatenate(probs[2 * p:2 * p + 2], axis=1), v_cat)

    def masked_step(jj, carry):
        block(n_blocks - 1 - jj, True)
        return carry

    def full_pair(jj, carry):
        block(n_full - 1 - 2 * jj, False)
        block(n_full - 2 - 2 * jj, False)
        return carry

    lax.fori_loop(0, n_blocks - n_full, masked_step, 0)
    lax.fori_loop(0, n_full // 2, full_pair, 0)

    @pl.when(n_full % 2 == 1)
    def _():
        block(jnp.int32(0), False)

    o_ref[0] = acc_ref[...]


def _sb_attend(q, k_all, v_all, n_keys, q_pos0):
    b, t, d = q.shape
    s_len = k_all.shape[1]
    tq = min(t, ATTN_TQ)
    kern = functools.partial(_sb_kernel, tq=tq, n_keys=n_keys, q_pos0=q_pos0)
    return pl.pallas_call(
        kern,
        grid=(b, t // tq),
        in_specs=[pl.BlockSpec((1, tq, d), lambda bi, i: (bi, i, 0)),
                  pl.BlockSpec((1, s_len, d), lambda bi, i: (bi, 0, 0)),
                  pl.BlockSpec((1, s_len, d), lambda bi, i: (bi, 0, 0))],
        out_specs=pl.BlockSpec((1, tq, d), lambda bi, i: (bi, i, 0)),
        out_shape=jax.ShapeDtypeStruct((b, t, d), F32),
        scratch_shapes=[pltpu.VMEM((tq, d), F32), pltpu.VMEM((tq, 2 * d), F32)],
        compiler_params=_cparams(("parallel", "parallel")),
        name="sb_attend",
    )(q, k_all, v_all)


def _key_extents(nq, tq, q_pos0, n_keys, s_len, step):
    need = [min(n_keys, ((q_pos0 + (i + 1) * tq - 1) // CHUNK + 1) * CHUNK) for i in range(nq)]
    return tuple(sorted({min(s_len, -(-n // step) * step) for n in need}))


def _for_tile_extent(i, tq, q_pos0, n_keys, extents, body):
    need = jnp.minimum(n_keys, ((q_pos0 + (i + 1) * tq - 1) // CHUNK + 1) * CHUNK)
    prev = 0
    for ext in extents:
        pl.when(jnp.logical_and(need > prev, need <= ext))(functools.partial(body, ext))
        prev = ext


def _diff_kernel(q_ref, k_ref, v_ref, lam_ref, gain_ref, o_ref, *, tq, extents, n_keys, q_pos0, lam_init):
    i = pl.program_id(2)
    lp = lam_ref[...]
    lam = (jnp.exp(jnp.sum(lp[0:1] * lp[1:2], axis=1, keepdims=True))
           - jnp.exp(jnp.sum(lp[2:3] * lp[3:4], axis=1, keepdims=True)) + lam_init)
    qpos = q_pos0 + i * tq + lax.broadcasted_iota(jnp.int32, (tq, 1), 0)
    lim = jnp.minimum((qpos // CHUNK + 1) * CHUNK, n_keys)
    low = lax.broadcasted_iota(jnp.int32, (tq, LANES), 1) < HEAD_DIM

    def body(ext):
        mask = lax.broadcasted_iota(jnp.int32, (tq, ext), 1) < lim
        for h in range(q_ref.shape[2] // LANES):
            hs = slice(h * LANES, (h + 1) * LANES)
            q = q_ref[0, :, hs] * HEAD_DIM ** -0.5
            kk = k_ref[0, :ext, hs]
            probs = []
            for c in range(2):
                qc = (jnp.where(low, q, 0.0) if c == 0 else jnp.where(low, 0.0, q)).astype(BF16)
                sc = jnp.where(mask, _dot_nt(qc, kk), NEG_INF)
                p = jnp.exp(sc - jnp.max(sc, axis=1, keepdims=True))
                probs.append(p * (1.0 / jnp.sum(p, axis=1, keepdims=True)))
            a = probs[0] - lam * probs[1]
            o = _dot(a.astype(BF16), v_ref[0, :ext, hs])
            o = o * lax.rsqrt(jnp.mean(o * o, axis=-1, keepdims=True) + NORM_EPS)
            o_ref[0, :, hs] = o * gain_ref[...] * (1.0 - lam_init)

    _for_tile_extent(i, tq, q_pos0, n_keys, extents, body)


def _diff_attend(q, k_all, v_all, diff_lambda, gain, n_keys, q_pos0, lam_init):
    b, t, d = q.shape
    s_len = k_all.shape[1]
    tq = min(t, ATTN_TQ)
    extents = _key_extents(t // tq, tq, q_pos0, n_keys, s_len, 2 * LANES)
    kern = functools.partial(_diff_kernel, tq=tq, extents=extents, n_keys=n_keys, q_pos0=q_pos0, lam_init=lam_init)
    return pl.pallas_call(
        kern,
        grid=(b, d // DIFF_LANES, t // tq),
        in_specs=[pl.BlockSpec((1, tq, DIFF_LANES), lambda bi, h, i: (bi, i, h)),
                  pl.BlockSpec((1, s_len, DIFF_LANES), lambda bi, h, i: (bi, 0, h)),
                  pl.BlockSpec((1, s_len, DIFF_LANES), lambda bi, h, i: (bi, 0, h)),
                  pl.BlockSpec(diff_lambda.shape, lambda bi, h, i: (0, 0)),
                  pl.BlockSpec((1, LANES), lambda bi, h, i: (0, 0))],
        out_specs=pl.BlockSpec((1, tq, DIFF_LANES), lambda bi, h, i: (bi, i, h)),
        out_shape=jax.ShapeDtypeStruct((b, t, d), F32),
        compiler_params=_cparams(("parallel", "parallel", "parallel")),
        name="diff_attend",
    )(q, k_all, v_all, diff_lambda, gain.reshape(1, LANES))


def _route(logits):
    lane = lax.broadcasted_iota(jnp.int32, logits.shape, 1)
    is_group = lane < MOE_GROUPS
    gl = jnp.where(is_group, logits, NEG_INF)
    g_max = jnp.max(gl, axis=1, keepdims=True)
    g_sel = jnp.min(jnp.where(gl == g_max, lane, LANES), axis=1, keepdims=True)
    g_gate = 1.0 / jnp.sum(jnp.where(is_group, jnp.exp(gl - g_max), 0.0), axis=1, keepdims=True)
    in_group = jnp.logical_and(lane >= GATE_COL0, (lane - GATE_COL0) // MOE_EPG == g_sel)
    in_group = jnp.logical_and(in_group, lane < GATE_COL0 + MOE_EXPERTS)
    el = jnp.where(in_group, logits, NEG_INF)
    top1 = jnp.max(el, axis=1, keepdims=True)
    i1 = jnp.min(jnp.where(jnp.logical_and(in_group, el == top1), lane, LANES), axis=1, keepdims=True)
    rest = jnp.logical_and(in_group, lane != i1)
    el2 = jnp.where(rest, logits, NEG_INF)
    top2 = jnp.max(el2, axis=1, keepdims=True)
    i2 = jnp.min(jnp.where(jnp.logical_and(rest, el2 == top2), lane, LANES), axis=1, keepdims=True)
    e2 = jnp.exp(top2 - top1)
    w1 = g_gate / (1.0 + e2)
    return jnp.where(lane == i1, w1, jnp.where(lane == i2, w1 * e2, 0.0))


def _out_kernel(h_ref, a_ref, b_ref, wa_ref, wb_ref, g_ref, wr_ref, br_ref, o_ref, xn_ref, gate_ref):
    h = (h_ref[...] + _dot(a_ref[...].astype(BF16), wa_ref[...])
         + _dot(b_ref[...].astype(BF16), wb_ref[...]))
    o_ref[...] = h
    xn = _rms(h, g_ref[...]).astype(BF16)
    xn_ref[...] = xn
    gate_ref[...] = _route(_dot(xn, wr_ref[...]) + br_ref[...])


def _out_proj_route(h, a, bmix, w_out, g_ffn, wr, br, tm):
    n, d = h.shape
    ca = a.shape[1]
    wa, wb = w_out[:ca], w_out[ca:]
    row = lambda w: pl.BlockSpec((tm, w), lambda i: (i, 0))
    whole = lambda x: pl.BlockSpec(x.shape, lambda i: (0, 0))
    return pl.pallas_call(
        _out_kernel,
        grid=(n // tm,),
        in_specs=[row(d), row(ca), row(bmix.shape[1]), whole(wa), whole(wb),
                  pl.BlockSpec((1, d), lambda i: (0, 0)), whole(wr), whole(br)],
        out_specs=[row(d), row(d), row(LANES)],
        out_shape=[jax.ShapeDtypeStruct((n, d), F32), jax.ShapeDtypeStruct((n, d), BF16),
                   jax.ShapeDtypeStruct((n, LANES), F32)],
        compiler_params=_cparams(("parallel",)),
        name="out_proj_route",
    )(h, a, bmix, wa, wb, g_ffn.reshape(1, d), wr, br)


def _expert_kernel(xn_ref, gate_ref, w1_ref, w3_ref, w2_ref, h_ref, gf_ref, o_ref, acc_ref, *, final_norm):
    e = pl.program_id(1)

    @pl.when(e == 0)
    def _():
        acc_ref[...] = jnp.zeros_like(acc_ref)

    gate = gate_ref[...]
    lane = lax.broadcasted_iota(jnp.int32, gate.shape, 1)
    ge = jnp.sum(jnp.where(lane == GATE_COL0 + e, gate, 0.0), axis=1, keepdims=True)
    xn = xn_ref[...]
    a = _dot(xn, w1_ref[0].astype(BF16))
    b = _dot(xn, w3_ref[0].astype(BF16))
    act = (a * (1.0 / (1.0 + jnp.exp(-a))) * b).astype(BF16)
    acc_ref[...] += ge * _dot(act, w2_ref[0].astype(BF16))

    @pl.when(e == pl.num_programs(1) - 1)
    def _():
        y = h_ref[...] + acc_ref[...]
        o_ref[...] = _rms(y, gf_ref[...]) if final_norm else y


def _moe(h, xn, gate, w1, w3, w2, g_final, final_norm, tm):
    n, d = h.shape
    n_e = w1.shape[0]
    tme = MOE_TM if n % MOE_TM == 0 else tm
    row2 = lambda w: pl.BlockSpec((tme, w), lambda i, e: (i, 0))
    kern = functools.partial(_expert_kernel, final_norm=final_norm)
    return pl.pallas_call(
        kern,
        grid=(n // tme, n_e),
        in_specs=[row2(d), row2(LANES),
                  pl.BlockSpec((1,) + w1.shape[1:], lambda i, e: (e, 0, 0)),
                  pl.BlockSpec((1,) + w3.shape[1:], lambda i, e: (e, 0, 0)),
                  pl.BlockSpec((1,) + w2.shape[1:], lambda i, e: (e, 0, 0)),
                  row2(d), pl.BlockSpec((1, d), lambda i, e: (0, 0))],
        out_specs=row2(d),
        out_shape=jax.ShapeDtypeStruct((n, d), F32),
        scratch_shapes=[pltpu.VMEM((tme, d), F32)],
        compiler_params=_cparams(("parallel", "arbitrary")),
        name="moe_experts",
    )(xn, gate, w1, w3, w2, h, g_final.reshape(1, d))


EVEN_SEGS = ((0, 512, None, False), (512, 512, "full", False), (1024, 128, "full", True), (1152, 128, None, True),
             (1280, 256, "full", False), (1536, 128, "half", False))
ODD_SEGS = ((0, 512, None, False), (512, 512, None, True), (1024, 512, None, True),
            (1536, 512, "full", False), (2048, 512, "full", True), (2560, 512, None, True))


def _cat_keys(hist, new):
    allk = jnp.concatenate([hist, new], axis=1) if hist is not None else new
    n_keys = allk.shape[1]
    pad = -n_keys % LANES
    if pad:
        allk = jnp.pad(allk, ((0, 0), (0, pad), (0, 0)))
    return allk.astype(BF16), n_keys


def kernel(x_prompt, x_sample, cache_pool, cache_dsa_k, cache_dsa_v, cache_idx_k, cache_sb_k, cache_sb_v,
           cache_diff_k, cache_diff_v, norm_mix, norm_ffn, norm_final, w_in_even, w_pool, pool_scale,
           w_out_even, w_in_odd, diff_lambda, diff_subln, w_out_odd, moe_w_group, moe_b_group,
           moe_w_expert, moe_b_expert, moe_w1, moe_w3, moe_w2):
    b, t, d = x_prompt.shape
    bd, td, _ = x_sample.shape
    past = cache_dsa_k.shape[2]
    depth = norm_mix.shape[0]
    groups = ((b, t, 0, min(512, b * t)), (bd, td, past, bd * td))

    tabs = []
    for (gb, gt, p0, tm) in groups:
        tab = _rope_tables(p0 + jnp.arange(gt, dtype=jnp.int32))
        if tm > gt:
            tab = jnp.tile(tab, (tm // gt, 1))
        tabs.append(tab)

    hs = [x_prompt.reshape(b * t, d), x_sample.reshape(bd * td, d)]
    outs = [dict(), dict()]
    for l in range(depth):
        li = l // 2
        last = l == depth - 1
        if l % 2 == 0:
            n_in = w_in_even.shape[2]
            w_in = jnp.pad(w_in_even[li], ((0, 0), (0, -n_in % LANES))).astype(BF16)
            w_out = w_out_even[li].astype(BF16)
            wp = w_pool[li].astype(BF16)
        else:
            w_in = w_in_odd[li].astype(BF16)
            w_out = w_out_odd[li].astype(BF16)
            lam_init = 0.8 - 0.6 * math.exp(-0.3 * l)
        wr = jnp.concatenate([moe_w_group[l]] + [moe_w_expert[l, g] for g in range(MOE_GROUPS)], axis=1)
        wr = jnp.pad(wr, ((0, 0), (0, LANES - wr.shape[1]))).astype(BF16)
        br = jnp.concatenate([moe_b_group[l], moe_b_expert[l].reshape(-1)])
        br = jnp.pad(br, (0, LANES - br.shape[0])).reshape(1, LANES).astype(F32)

        for gi, (gb, gt, p0, tm) in enumerate(groups):
            h = hs[gi]
            o = outs[gi]
            sample = gi == 1
            r3 = lambda x: x.reshape(gb, gt, x.shape[-1])
            if l % 2 == 0:
                u, q, k, v, qi, kiwi, k16, v16 = [
                    r3(x) for x in _project(h, norm_mix[l], w_in, tabs[gi], EVEN_SEGS, tm)]
                ki = kiwi[..., :HEAD_DIM]
                hist = cache_pool[li] if sample else jnp.zeros((gb, POOL_HIST, u.shape[2]), F32)
                a_out = _pool_mix(u, hist, wp, pool_scale[li], p0)
                if sample:
                    k_all, n_keys = _cat_keys(cache_dsa_k[li].reshape(gb, past, -1), k)
                    v_all, _ = _cat_keys(cache_dsa_v[li].reshape(gb, past, -1), v)
                    ki_all, _ = _cat_keys(cache_idx_k[li], ki)
                else:
                    (k_all, n_keys), (v_all, _), (ki_all, _) = _cat_keys(None, k16), _cat_keys(None, v16), _cat_keys(None, ki)
                b_out = _dsa(q, qi, kiwi, k_all, v_all, ki_all, n_keys, p0)
                o.setdefault("pool", []).append(jnp.concatenate([hist, u], axis=1)[:, -POOL_HIST:])
                o.setdefault("dsa_k", []).append(k.reshape(gb, gt, -1, HEAD_DIM))
                o.setdefault("dsa_v", []).append(v.reshape(gb, gt, -1, HEAD_DIM))
                o.setdefault("idx_k", []).append(ki)
                mix_a, mix_b = a_out, b_out
            else:
                sq, sk, sv, dq, dk, dv, sk16, sv16, dk16, dv16 = [
                    r3(x) for x in _project(h, norm_mix[l], w_in, tabs[gi], ODD_SEGS, tm)]
                if sample:
                    sk_all, n_keys = _cat_keys(cache_sb_k[li].reshape(gb, past, -1), sk)
                    sv_all, _ = _cat_keys(cache_sb_v[li].reshape(gb, past, -1), sv)
                    dk_all, _ = _cat_keys(cache_diff_k[li].reshape(gb, past, -1), dk)
                    dv_all, _ = _cat_keys(cache_diff_v[li].reshape(gb, past, -1), dv)
                else:
                    (sk_all, n_keys), (sv_all, _) = _cat_keys(None, sk16), _cat_keys(None, sv16)
                    (dk_all, _), (dv_all, _) = _cat_keys(None, dk16), _cat_keys(None, dv16)
                c_out = _sb_attend(sq, sk_all, sv_all, n_keys, p0)
                d_out = _diff_attend(dq, dk_all, dv_all, diff_lambda[li], diff_subln[li], n_keys, p0, lam_init)
                n_sb = sk.shape[2] // HEAD_DIM
                n_df = dk.shape[2] // (2 * HEAD_DIM)
                o.setdefault("sb_k", []).append(sk.reshape(gb, gt, n_sb, HEAD_DIM))
                o.setdefault("sb_v", []).append(sv.reshape(gb, gt, n_sb, HEAD_DIM))
                o.setdefault("diff_k", []).append(dk.reshape(gb, gt, n_df, 2, HEAD_DIM))
                o.setdefault("diff_v", []).append(dv.reshape(gb, gt, n_df, 2 * HEAD_DIM))
                mix_a, mix_b = c_out, d_out
            h, xn, gate = _out_proj_route(h, mix_a.reshape(gb * gt, -1), mix_b.reshape(gb * gt, -1), w_out,
                                          norm_ffn[l], wr, br, tm)
            hs[gi] = _moe(h, xn, gate, moe_w1[l], moe_w3[l], moe_w2[l], norm_final, last, tm)

    names = ("pool", "dsa_k", "dsa_v", "idx_k", "sb_k", "sb_v", "diff_k", "diff_v")
    res = [hs[0].reshape(b, t, d), hs[1].reshape(bd, td, d)]
    for o in outs:
        res += [jnp.stack(o[nm]) for nm in names]
    return tuple(res)
```

```python
import functools
import math

import jax
import jax.numpy as jnp
import numpy as np
from jax import lax
from jax.experimental import pallas as pl
from jax.experimental.pallas import tpu as pltpu

F32 = jnp.float32
BF16 = jnp.bfloat16

LANES = 128
HEAD_DIM = 64
CHUNK = 64
ROPE_THETA = 10000.0
NORM_EPS = 1e-6
NEG_INF = -1e30
PAD_SCORE = -3e38
BIG_POS = 3e38
POOL_WINDOWS = (2, 4, 8, 16)
POOL_HIST = 15
POOL_HIST_PAD = 16
DSA_TOPK = 256
IDX_HEADS = 4
MOE_GROUPS = 4
MOE_EPG = 4
MOE_EXPERTS = 16
GATE_COL0 = MOE_GROUPS
GROUP_LANE = 0
BF16_ROWS = 16
SORT_ROWS = 256
VMEM_LIMIT = 56 * 1024 * 1024
BISECT_STEPS = 8
BISECT_ROUNDS = 48
ATTN_TQ = 256
DSA_TQ = 128
MOE_TM = 1024
DIFF_LANES = 2 * LANES


def _cparams(sem):
    return pltpu.CompilerParams(dimension_semantics=sem, vmem_limit_bytes=VMEM_LIMIT)


def _dot(a, b):
    return jnp.dot(a, b, preferred_element_type=F32)


def _dot_nt(a, b):
    return lax.dot_general(a, b, (((1,), (1,)), ((), ())), preferred_element_type=F32)


def _dot_tn(a, b):
    return lax.dot_general(a, b, (((0,), (0,)), ((), ())), preferred_element_type=F32)


def _rms(x, g):
    ms = jnp.mean(x * x, axis=-1, keepdims=True)
    return x * lax.rsqrt(ms + NORM_EPS) * g


def _proj_kernel(x_ref, g_ref, w_ref, tab_ref, *out_refs, segs):
    xn = _rms(x_ref[...], g_ref[...]).astype(BF16)
    copies = iter(out_refs[len(segs):])
    for o_ref, (c0, width, mode, twin) in zip(out_refs, segs):
        t_ref = next(copies) if twin else None
        y = _dot(xn, w_ref[:, c0:c0 + width])
        if mode is None:
            o_ref[...] = y
            if twin:
                t_ref[...] = y.astype(BF16)
            continue
        t0 = 0 if mode == "full" else 3 * LANES
        cos = tab_ref[:, t0:t0 + LANES]
        sin_a = tab_ref[:, t0 + LANES:t0 + 2 * LANES]
        sin_b = tab_ref[:, t0 + 2 * LANES:t0 + 3 * LANES]
        for c in range(0, width, LANES):
            yc = y[:, c:c + LANES]
            yr = (yc * cos + pltpu.roll(yc, LANES - HEAD_DIM // 2, 1) * sin_a
                  + pltpu.roll(yc, HEAD_DIM // 2, 1) * sin_b)
            o_ref[:, c:c + LANES] = yr
            if twin:
                t_ref[:, c:c + LANES] = yr.astype(BF16)


def _rope_tables(pos):
    half = HEAD_DIM // 2
    inv = ROPE_THETA ** (-jnp.arange(half, dtype=F32) / half)
    ang = pos.astype(F32)[:, None] * inv[None, :]
    cos, sin = jnp.cos(ang), jnp.sin(ang)
    zero, one = jnp.zeros_like(sin), jnp.ones_like(cos)
    cos_h = jnp.concatenate([cos, cos], axis=1)
    sa_h = jnp.concatenate([-sin, zero], axis=1)
    sb_h = jnp.concatenate([zero, sin], axis=1)
    one_h = jnp.concatenate([one, one], axis=1)
    zero_h = jnp.concatenate([zero, zero], axis=1)
    return jnp.concatenate([cos_h, cos_h, sa_h, sa_h, sb_h, sb_h,
                            cos_h, one_h, sa_h, zero_h, sb_h, zero_h], axis=1)


def _project(x, g, w, tab, segs, tm):
    n, d = x.shape
    tt = tab.shape[0]
    nt = tt // tm
    kern = functools.partial(_proj_kernel, segs=segs)
    return pl.pallas_call(
        kern,
        grid=(n // tm,),
        in_specs=[pl.BlockSpec((tm, d), lambda i: (i, 0)),
                  pl.BlockSpec((1, d), lambda i: (0, 0)),
                  pl.BlockSpec(w.shape, lambda i: (0, 0)),
                  pl.BlockSpec((tm, tab.shape[1]), lambda i: (i % nt, 0))],
        out_specs=[pl.BlockSpec((tm, wd), lambda i: (i, 0)) for _, wd, _, _ in segs]
        + [pl.BlockSpec((tm, wd), lambda i: (i, 0)) for _, wd, _, twin in segs if twin],
        out_shape=[jax.ShapeDtypeStruct((n, wd), F32) for _, wd, _, _ in segs]
        + [jax.ShapeDtypeStruct((n, wd), BF16) for _, wd, _, twin in segs if twin],
        compiler_params=_cparams(("parallel",)),
        name="proj",
    )(x, g.reshape(1, d), w, tab)


def _pool_kernel(u_ref, h_ref, w_ref, s_ref, o_ref, ext_ref, *, t, pos0, rc):
    ext_ref[0:POOL_HIST_PAD, :] = h_ref[0]
    ext_ref[POOL_HIST_PAD:POOL_HIST_PAD + t, :] = u_ref[0]
    for r0 in range(0, t, rc):
        pos = pos0 + r0 + lax.broadcasted_iota(jnp.int32, (rc, 1), 0)
        for g, win in enumerate(POOL_WINDOWS):
            c0 = g * LANES
            u_new = ext_ref[POOL_HIST_PAD + r0:POOL_HIST_PAD + r0 + rc, c0:c0 + LANES]
            s = u_new
            for k in range(1, win):
                s = s + ext_ref[POOL_HIST_PAD + r0 - k:POOL_HIST_PAD + r0 - k + rc, c0:c0 + LANES]
            cnt = jnp.minimum(pos + 1, win).astype(F32)
            dlt = (s / cnt - u_new).astype(BF16)
            o_ref[0, r0:r0 + rc, c0:c0 + LANES] = _dot(dlt, w_ref[g]) * s_ref[:, c0:c0 + LANES]


def _pool_mix(u, hist, w_pool, pool_scale, pos0):
    b, t, c = u.shape
    rc = min(t, 256)
    hist16 = jnp.pad(hist, ((0, 0), (POOL_HIST_PAD - POOL_HIST, 0), (0, 0)))
    kern = functools.partial(_pool_kernel, t=t, pos0=pos0, rc=rc)
    return pl.pallas_call(
        kern,
        grid=(b,),
        in_specs=[pl.BlockSpec((1, t, c), lambda i: (i, 0, 0)),
                  pl.BlockSpec((1, POOL_HIST_PAD, c), lambda i: (i, 0, 0)),
                  pl.BlockSpec(w_pool.shape, lambda i: (0, 0, 0)),
                  pl.BlockSpec((1, c), lambda i: (0, 0))],
        out_specs=pl.BlockSpec((1, t, c), lambda i: (i, 0, 0)),
        out_shape=jax.ShapeDtypeStruct((b, t, c), F32),
        scratch_shapes=[pltpu.VMEM((POOL_HIST_PAD + t, c), F32)],
        compiler_params=_cparams(("parallel",)),
        name="pool_mix",
    )(u, hist16, w_pool, pool_scale.reshape(1, c))


def _dsa_kernel(q_ref, qi_ref, kw_ref, k_ref, v_ref, ki_ref, o_ref, lo_ref, hi_ref, bias_ref,
                *, tq, extents, n_keys, q_pos0, n_sel):
    i = pl.program_id(1)
    qpos = q_pos0 + i * tq + lax.broadcasted_iota(jnp.int32, (tq, 1), 0)
    lim = jnp.minimum((qpos // CHUNK + 1) * CHUNK, n_keys)
    kf = float(n_sel)
    low = lax.broadcasted_iota(jnp.int32, (tq, LANES), 1) < HEAD_DIM
    q = q_ref[0] * HEAD_DIM ** -0.5
    qi = qi_ref[0].astype(BF16)
    wi = kw_ref[0][:, HEAD_DIM:HEAD_DIM + IDX_HEADS] * (IDX_HEADS * HEAD_DIM) ** -0.5

    def body(ext):
        kpos = lax.broadcasted_iota(jnp.int32, (tq, ext), 1)
        adm = kpos < lim
        padded = ext > n_keys
        virt = float(max(n_keys - ext, 0))

        sidx = jnp.zeros((tq, ext), F32)
        ki = ki_ref[0, :ext, :]
        for h in range(IDX_HEADS):
            sh = _dot_nt(qi[:, h * HEAD_DIM:(h + 1) * HEAD_DIM], ki)
            sidx = sidx + jnp.maximum(sh, 0.0) * wi[:, h:h + 1]
        sm = jnp.where(adm, sidx, NEG_INF)
        if padded:
            real = kpos < n_keys
            sm = jnp.where(real, sm, PAD_SCORE)

        def count_gt(x):
            cnt = jnp.sum(jnp.where(sm > x, 1.0, 0.0), axis=1, keepdims=True)
            return cnt + jnp.where(x < NEG_INF, virt, 0.0) if virt else cnt

        def bracket(lo, hi):
            above = jnp.min(jnp.where(sm > lo, sm, BIG_POS), axis=1, keepdims=True)
            below = jnp.max(jnp.where(sm <= hi, sm, PAD_SCORE), axis=1, keepdims=True)
            if virt:
                above = jnp.minimum(above, jnp.where(lo < NEG_INF, NEG_INF, BIG_POS))
                below = jnp.maximum(below, jnp.where(hi >= NEG_INF, NEG_INF, PAD_SCORE))
            return above, below

        row_max = jnp.max(sm, axis=1, keepdims=True)
        row_min = jnp.min(jnp.where(real, sm, BIG_POS) if padded else sm, axis=1, keepdims=True)
        if virt:
            row_min = jnp.minimum(row_min, NEG_INF)
        adm_min = jnp.min(jnp.where(adm, sm, BIG_POS), axis=1, keepdims=True)
        few = count_gt(row_min) < kf
        tight = count_gt(adm_min) >= kf
        lo_ref[...] = jnp.where(few, PAD_SCORE, jnp.where(tight, adm_min, row_min))
        hi_ref[...] = jnp.where(few, row_min, jnp.where(tight, row_max, adm_min))

        def unresolved(lo, hi):
            above, below = bracket(lo, hi)
            return jnp.sum(jnp.where(above < below, 1, 0))

        def cond(carry):
            rounds, open_rows = carry
            return jnp.logical_and(open_rows > 0, rounds < BISECT_ROUNDS)

        def count_gt_wide(x):
            part = jnp.where(sm[:, :LANES] > x, 1.0, 0.0)
            for j in range(1, ext // LANES):
                part = part + jnp.where(sm[:, j * LANES:(j + 1) * LANES] > x, 1.0, 0.0)
            cnt = jnp.broadcast_to(jnp.sum(part, axis=1, keepdims=True), (tq, LANES))
            return cnt + jnp.where(x < NEG_INF, virt, 0.0) if virt else cnt

        def step(carry):
            rounds, _ = carry
            lo = jnp.broadcast_to(lo_ref[...], (tq, LANES))
            hi = jnp.broadcast_to(hi_ref[...], (tq, LANES))
            for _ in range(BISECT_STEPS):
                mid = 0.5 * lo + 0.5 * hi
                under = count_gt_wide(mid) < kf
                hi = jnp.where(under, mid, hi)
                lo = jnp.where(under, lo, mid)
            lo_ref[...] = lo[:, :1]
            hi_ref[...] = hi[:, :1]
            return rounds + 1, unresolved(lo_ref[...], hi_ref[...])

        lax.while_loop(cond, step, (jnp.int32(0), unresolved(lo_ref[...], hi_ref[...])))
        _, thr = bracket(lo_ref[...], hi_ref[...])

        gt = sm > thr
        eq = sm == thr
        need = kf - count_gt(thr)
        n_eq = jnp.sum(jnp.where(eq, 1.0, 0.0), axis=1, keepdims=True)
        crowded = jnp.sum(jnp.where(n_eq > need, 1, 0))

        @pl.when(crowded == 0)
        def _():
            keep = jnp.logical_and(jnp.logical_or(gt, eq), adm)
            bias_ref[:, :ext] = jnp.where(keep, 0.0, NEG_INF)

        @pl.when(crowded > 0)
        def _():
            rr = lax.broadcasted_iota(jnp.int32, (LANES, LANES), 0)
            cc = lax.broadcasted_iota(jnp.int32, (LANES, LANES), 1)
            prefix_ones = jnp.where(rr <= cc, 1.0, 0.0).astype(BF16)
            carry = jnp.zeros((tq, 1), F32)
            for j in range(ext // LANES):
                sl = slice(j * LANES, (j + 1) * LANES)
                eq_j = eq[:, sl]
                rank = _dot(jnp.where(eq_j, 1.0, 0.0).astype(BF16), prefix_ones) + carry
                keep = jnp.logical_or(gt[:, sl], jnp.logical_and(eq_j, rank <= need))
                bias_ref[:, sl] = jnp.where(jnp.logical_and(keep, adm[:, sl]), 0.0, NEG_INF)
                carry = rank[:, LANES - 1:LANES]

        bias = bias_ref[:, :ext]
        bias2 = jnp.concatenate([bias, bias], axis=0)

        kk = k_ref[0, :ext, :]
        vv = v_ref[0, :ext, :]
        low_k = lax.broadcasted_iota(jnp.int32, (ext, LANES), 1) < HEAD_DIM
        vsw = pltpu.roll(vv.astype(F32), HEAD_DIM, 1).astype(BF16)
        one = jnp.ones_like(vv)
        n_kv = LANES // HEAD_DIM
        for g in range(n_kv):
            v_lo = jnp.where(low_k, vv if g == 0 else vsw, one)
            v_hi = jnp.where(low_k, one, vsw if g == 0 else vv)
            stacks = []
            for odd in range(2):
                rows = []
                for m in range(2):
                    c = 2 * g + m
                    qc = q[:, c * LANES:(c + 1) * LANES]
                    if (odd == 1) != (g == 1):
                        qc = pltpu.roll(qc, HEAD_DIM, 1)
                    rows.append(jnp.where(low, qc, 0.0) if g == 0 else jnp.where(low, 0.0, qc))
                qs = jnp.concatenate(rows, axis=0).astype(BF16)
                sc = _dot_nt(qs, kk) + bias2
                p = jnp.exp(sc - jnp.max(sc, axis=1, keepdims=True))
                og = _dot(p.astype(BF16), v_hi if odd else v_lo)
                stacks.append(og / pltpu.roll(og, HEAD_DIM, 1))
            for m in range(2):
                c = 2 * g + m
                o_ref[0, :, c * LANES:(c + 1) * LANES] = jnp.where(
                    low, stacks[0][m * tq:(m + 1) * tq], stacks[1][m * tq:(m + 1) * tq])

    _for_tile_extent(i, tq, q_pos0, n_keys, extents, body)


def _dsa(q, qi, kiwi, k_all, v_all, ki_all, n_keys, q_pos0):
    b, t, dq = q.shape
    s_len = k_all.shape[1]
    tq = min(t, DSA_TQ)
    n_sel = min(DSA_TOPK, n_keys // 4)
    extents = _key_extents(t // tq, tq, q_pos0, n_keys, s_len, 4 * LANES)
    kern = functools.partial(_dsa_kernel, tq=tq, extents=extents, n_keys=n_keys, q_pos0=q_pos0, n_sel=n_sel)
    qspec = lambda w: pl.BlockSpec((1, tq, w), lambda bi, i: (bi, i, 0))
    kspec = lambda w: pl.BlockSpec((1, s_len, w), lambda bi, i: (bi, 0, 0))
    return pl.pallas_call(
        kern,
        grid=(b, t // tq),
        in_specs=[qspec(dq), qspec(qi.shape[2]), qspec(kiwi.shape[2]),
                  kspec(k_all.shape[2]), kspec(v_all.shape[2]), kspec(ki_all.shape[2])],
        out_specs=qspec(dq),
        out_shape=jax.ShapeDtypeStruct((b, t, dq), F32),
        scratch_shapes=[pltpu.VMEM((tq, 1), F32), pltpu.VMEM((tq, 1), F32), pltpu.VMEM((tq, s_len), F32)],
        compiler_params=_cparams(("parallel", "parallel")),
        name="dsa",
    )(q, qi, kiwi, k_all, v_all, ki_all)


def _sb_kernel(q_ref, k_ref, v_ref, o_ref, acc_ref, run_ref, *, tq, n_keys, q_pos0):
    i = pl.program_id(1)
    pairs = q_ref.shape[2] // LANES
    first_q = q_pos0 + i * tq
    qpos = first_q + lax.broadcasted_iota(jnp.int32, (tq, 1), 0)
    n_blocks = (jnp.minimum(first_q + tq - 1, n_keys) + LANES - 1) // LANES
    n_full = jnp.minimum(first_q, n_keys) // LANES
    low_q = lax.broadcasted_iota(jnp.int32, (tq, LANES), 1) < HEAD_DIM
    low_k = lax.broadcasted_iota(jnp.int32, (LANES, LANES), 1) < HEAD_DIM
    q = q_ref[0] * HEAD_DIM ** -0.5
    qm = []
    for p in range(pairs):
        qp = q[:, p * LANES:(p + 1) * LANES]
        qm.append((jnp.where(low_q, qp, 0.0).astype(BF16), jnp.where(low_q, 0.0, qp).astype(BF16)))
    rr = lax.broadcasted_iota(jnp.int32, (2 * LANES, 2 * LANES), 0)
    cc = lax.broadcasted_iota(jnp.int32, (2 * LANES, 2 * LANES), 1)
    rk = jnp.where(rr >= LANES, rr - LANES, rr)
    cs_rhs = jnp.where(jnp.logical_or(cc >= LANES, rk >= cc), 1.0, 0.0).astype(BF16)
    acc_ref[...] = jnp.zeros_like(acc_ref)
    run_ref[...] = jnp.zeros_like(run_ref)

    def block(j, masked):
        ks = pl.multiple_of(j * LANES, LANES)
        if masked:
            kpos = ks + lax.broadcasted_iota(jnp.int32, (1, LANES), 1)
            causal = jnp.logical_and(kpos < qpos, kpos < n_keys)
        heads = [(p, c) for p in range(pairs) for c in range(2)]
        lanes = lambda n: slice(n * LANES, (n + 1) * LANES)
        zs = [_dot_nt(qm[p][c], k_ref[0, pl.ds(ks, LANES), lanes(p)]) for p, c in heads]
        css = []
        for z in zs:
            sp = jnp.maximum(z, 0.0) + jnp.log(1.0 + jnp.exp(-jnp.abs(z)))
            if masked:
                sp = jnp.where(causal, sp, 0.0)
            hi = sp.astype(BF16)
            lo = (sp - hi.astype(F32)).astype(BF16)
            css.append(_dot(jnp.concatenate([hi, lo], axis=1), cs_rhs))
        probs = []
        for n, (z, cs) in enumerate(zip(zs, css)):
            run = run_ref[:, lanes(n)]
            a = jnp.exp(z - cs[:, :LANES] - run)
            if masked:
                a = jnp.where(causal, a, 0.0)
            run_ref[:, lanes(n)] = run + cs[:, LANES:]
            probs.append(a.astype(BF16))
        for p in range(pairs):
            vb = v_ref[0, pl.ds(ks, LANES), lanes(p)]
            zero = jnp.zeros_like(vb)
            v_cat = jnp.concatenate([jnp.where(low_k, vb, zero), jnp.where(low_k, zero, vb)], axis=0)
            acc_ref[:, lanes(p)] += _dot(jnp.concatenate(probs[2 * p:2 * p + 2], axis=1), v_cat)

    def masked_step(jj, carry):
        block(n_blocks - 1 - jj, True)
        return carry

    def full_pair(jj, carry):
        block(n_full - 1 - 2 * jj, False)
        block(n_full - 2 - 2 * jj, False)
        return carry

    lax.fori_loop(0, n_blocks - n_full, masked_step, 0)
    lax.fori_loop(0, n_full // 2, full_pair, 0)

    @pl.when(n_full % 2 == 1)
    def _():
        block(jnp.int32(0), False)

    o_ref[0] = acc_ref[...]


def _sb_attend(q, k_all, v_all, n_keys, q_pos0):
    b, t, d = q.shape
    s_len = k_all.shape[1]
    tq = min(t, ATTN_TQ)
    kern = functools.partial(_sb_kernel, tq=tq, n_keys=n_keys, q_pos0=q_pos0)
    return pl.pallas_call(
        kern,
        grid=(b, t // tq),
        in_specs=[pl.BlockSpec((1, tq, d), lambda bi, i: (bi, i, 0)),
                  pl.BlockSpec((1, s_len, d), lambda bi, i: (bi, 0, 0)),
                  pl.BlockSpec((1, s_len, d), lambda bi, i: (bi, 0, 0))],
        out_specs=pl.BlockSpec((1, tq, d), lambda bi, i: (bi, i, 0)),
        out_shape=jax.ShapeDtypeStruct((b, t, d), F32),
        scratch_shapes=[pltpu.VMEM((tq, d), F32), pltpu.VMEM((tq, 2 * d), F32)],
        compiler_params=_cparams(("parallel", "parallel")),
        name="sb_attend",
    )(q, k_all, v_all)


def _key_extents(nq, tq, q_pos0, n_keys, s_len, step):
    need = [min(n_keys, ((q_pos0 + (i + 1) * tq - 1) // CHUNK + 1) * CHUNK) for i in range(nq)]
    return tuple(sorted({min(s_len, -(-n // step) * step) for n in need}))


def _for_tile_extent(i, tq, q_pos0, n_keys, extents, body):
    need = jnp.minimum(n_keys, ((q_pos0 + (i + 1) * tq - 1) // CHUNK + 1) * CHUNK)
    prev = 0
    for ext in extents:
        pl.when(jnp.logical_and(need > prev, need <= ext))(functools.partial(body, ext))
        prev = ext


def _diff_kernel(q_ref, k_ref, v_ref, lam_ref, gain_ref, o_ref, *, tq, extents, n_keys, q_pos0, lam_init):
    i = pl.program_id(2)
    lp = lam_ref[...]
    lam = (jnp.exp(jnp.sum(lp[0:1] * lp[1:2], axis=1, keepdims=True))
           - jnp.exp(jnp.sum(lp[2:3] * lp[3:4], axis=1, keepdims=True)) + lam_init)
    qpos = q_pos0 + i * tq + lax.broadcasted_iota(jnp.int32, (tq, 1), 0)
    lim = jnp.minimum((qpos // CHUNK + 1) * CHUNK, n_keys)
    low = lax.broadcasted_iota(jnp.int32, (tq, LANES), 1) < HEAD_DIM

    def body(ext):
        mask = lax.broadcasted_iota(jnp.int32, (tq, ext), 1) < lim
        for h in range(q_ref.shape[2] // LANES):
            hs = slice(h * LANES, (h + 1) * LANES)
            q = q_ref[0, :, hs] * HEAD_DIM ** -0.5
            kk = k_ref[0, :ext, hs]
            probs = []
            for c in range(2):
                qc = (jnp.where(low, q, 0.0) if c == 0 else jnp.where(low, 0.0, q)).astype(BF16)
                sc = jnp.where(mask, _dot_nt(qc, kk), NEG_INF)
                p = jnp.exp(sc - jnp.max(sc, axis=1, keepdims=True))
                probs.append(p * (1.0 / jnp.sum(p, axis=1, keepdims=True)))
            a = probs[0] - lam * probs[1]
            o = _dot(a.astype(BF16), v_ref[0, :ext, hs])
            o = o * lax.rsqrt(jnp.mean(o * o, axis=-1, keepdims=True) + NORM_EPS)
            o_ref[0, :, hs] = o * gain_ref[...] * (1.0 - lam_init)

    _for_tile_extent(i, tq, q_pos0, n_keys, extents, body)


def _diff_attend(q, k_all, v_all, diff_lambda, gain, n_keys, q_pos0, lam_init):
    b, t, d = q.shape
    s_len = k_all.shape[1]
    tq = min(t, ATTN_TQ)
    extents = _key_extents(t // tq, tq, q_pos0, n_keys, s_len, 2 * LANES)
    kern = functools.partial(_diff_kernel, tq=tq, extents=extents, n_keys=n_keys, q_pos0=q_pos0, lam_init=lam_init)
    return pl.pallas_call(
        kern,
        grid=(b, d // DIFF_LANES, t // tq),
        in_specs=[pl.BlockSpec((1, tq, DIFF_LANES), lambda bi, h, i: (bi, i, h)),
                  pl.BlockSpec((1, s_len, DIFF_LANES), lambda bi, h, i: (bi, 0, h)),
                  pl.BlockSpec((1, s_len, DIFF_LANES), lambda bi, h, i: (bi, 0, h)),
                  pl.BlockSpec(diff_lambda.shape, lambda bi, h, i: (0, 0)),
                  pl.BlockSpec((1, LANES), lambda bi, h, i: (0, 0))],
        out_specs=pl.BlockSpec((1, tq, DIFF_LANES), lambda bi, h, i: (bi, i, h)),
        out_shape=jax.ShapeDtypeStruct((b, t, d), F32),
        compiler_params=_cparams(("parallel", "parallel", "parallel")),
        name="diff_attend",
    )(q, k_all, v_all, diff_lambda, gain.reshape(1, LANES))


def _route(logits):
    lane = lax.broadcasted_iota(jnp.int32, logits.shape, 1)
    is_group = lane < MOE_GROUPS
    gl = jnp.where(is_group, logits, NEG_INF)
    g_max = jnp.max(gl, axis=1, keepdims=True)
    g_sel = jnp.min(jnp.where(gl == g_max, lane, LANES), axis=1, keepdims=True)
    g_gate = 1.0 / jnp.sum(jnp.where(is_group, jnp.exp(gl - g_max), 0.0), axis=1, keepdims=True)
    in_group = jnp.logical_and(lane >= GATE_COL0, (lane - GATE_COL0) // MOE_EPG == g_sel)
    in_group = jnp.logical_and(in_group, lane < GATE_COL0 + MOE_EXPERTS)
    el = jnp.where(in_group, logits, NEG_INF)
    top1 = jnp.max(el, axis=1, keepdims=True)
    i1 = jnp.min(jnp.where(jnp.logical_and(in_group, el == top1), lane, LANES), axis=1, keepdims=True)
    rest = jnp.logical_and(in_group, lane != i1)
    el2 = jnp.where(rest, logits, NEG_INF)
    top2 = jnp.max(el2, axis=1, keepdims=True)
    i2 = jnp.min(jnp.where(jnp.logical_and(rest, el2 == top2), lane, LANES), axis=1, keepdims=True)
    e2 = jnp.exp(top2 - top1)
    w1 = g_gate / (1.0 + e2)
    gates = jnp.where(lane == i1, w1, jnp.where(lane == i2, w1 * e2, 0.0))
    return jnp.where(lane == GROUP_LANE, g_sel.astype(F32), gates)


def _out_kernel(h_ref, a_ref, b_ref, wa_ref, wb_ref, g_ref, wr_ref, br_ref, o_ref, xn_ref, gate_ref):
    h = (h_ref[...] + _dot(a_ref[...].astype(BF16), wa_ref[...])
         + _dot(b_ref[...].astype(BF16), wb_ref[...]))
    o_ref[...] = h
    xn = _rms(h, g_ref[...]).astype(BF16)
    xn_ref[...] = xn
    gate_ref[...] = _route(_dot(xn, wr_ref[...]) + br_ref[...])


def _out_proj_route(h, a, bmix, w_out, g_ffn, wr, br, tm):
    n, d = h.shape
    ca = a.shape[1]
    wa, wb = w_out[:ca], w_out[ca:]
    row = lambda w: pl.BlockSpec((tm, w), lambda i: (i, 0))
    whole = lambda x: pl.BlockSpec(x.shape, lambda i: (0, 0))
    return pl.pallas_call(
        _out_kernel,
        grid=(n // tm,),
        in_specs=[row(d), row(ca), row(bmix.shape[1]), whole(wa), whole(wb),
                  pl.BlockSpec((1, d), lambda i: (0, 0)), whole(wr), whole(br)],
        out_specs=[row(d), row(d), row(LANES)],
        out_shape=[jax.ShapeDtypeStruct((n, d), F32), jax.ShapeDtypeStruct((n, d), BF16),
                   jax.ShapeDtypeStruct((n, LANES), F32)],
        compiler_params=_cparams(("parallel",)),
        name="out_proj_route",
    )(h, a, bmix, wa, wb, g_ffn.reshape(1, d), wr, br)


def _split3(x):
    hi = x.astype(BF16)
    r1 = x - hi.astype(F32)
    mid = r1.astype(BF16)
    lo = (r1 - mid.astype(F32)).astype(BF16)
    return hi, mid, lo


def _expert_kernel(xn_ref, gate_ref, w1_ref, w3_ref, w2_ref, h_ref, gf_ref, o_ref,
                   acc_ref, xs_ref, gs_ref, pt_ref, tri_ref, seg_ref, *, final_norm, win):
    i = pl.program_id(0)
    e = pl.program_id(1)
    tm = xn_ref.shape[0]

    @pl.when(jnp.logical_and(i == 0, e == 0))
    def _():
        rr = lax.broadcasted_iota(jnp.int32, (tm, tm), 0)
        cc = lax.broadcasted_iota(jnp.int32, (tm, tm), 1)
        tri_ref[...] = jnp.where(cc < rr, 1.0, 0.0).astype(BF16)

    @pl.when(e == 0)
    def _():
        gate = gate_ref[...]
        lane = lax.broadcasted_iota(jnp.int32, gate.shape, 1)
        lane_row = lax.broadcasted_iota(jnp.int32, (1, LANES), 1)
        in_grp = jnp.logical_and(lane < MOE_GROUPS, lane.astype(F32) == gate[:, GROUP_LANE:GROUP_LANE + 1])
        onehot = jnp.where(in_grp, 1.0, 0.0)
        before = _dot(tri_ref[...], onehot.astype(BF16))
        count = jnp.sum(onehot, axis=0, keepdims=True)
        first = jnp.zeros((1, LANES), F32)
        start = jnp.float32(0.0)
        for g in range(MOE_GROUPS):
            n_g = jnp.sum(jnp.where(lane_row == g, count, 0.0))
            seg_ref[g] = start.astype(jnp.int32)
            seg_ref[MOE_GROUPS + g] = n_g.astype(jnp.int32)
            first = jnp.where(lane_row == g, start, first)
            start = start + n_g
        dest = jnp.sum(onehot * (before + first), axis=1, keepdims=True).astype(jnp.int32)
        g_hi, g_mid, g_lo = _split3(gate)
        sr = min(SORT_ROWS, tm)
        for r in range(0, tm, sr):
            rs = slice(r, r + sr)
            pt_ref[rs, :] = jnp.where(lax.broadcasted_iota(jnp.int32, (sr, tm), 1) == dest[rs],
                                      1.0, 0.0).astype(BF16)
        for r in range(0, tm, sr):
            rs = slice(r, r + sr)
            pt_cols = pt_ref[:, rs]
            xs_ref[rs, :] = _dot_tn(pt_cols, xn_ref[...]).astype(BF16)
            gs_ref[rs, :] = _dot_tn(pt_cols, g_hi) + _dot_tn(pt_cols, g_mid) + _dot_tn(pt_cols, g_lo)
        xs_ref[tm:, :] = jnp.zeros((win, xs_ref.shape[1]), BF16)
        gs_ref[tm:, :] = jnp.zeros((win, LANES), F32)
        acc_ref[...] = jnp.zeros_like(acc_ref)

    grp = e // MOE_EPG
    seg_first = seg_ref[grp]
    seg_rows = seg_ref[MOE_GROUPS + grp]
    row0 = (seg_first // BF16_ROWS) * BF16_ROWS
    n_win = (seg_first + seg_rows - row0 + win - 1) // win
    w1 = w1_ref[0].astype(BF16)
    w3 = w3_ref[0].astype(BF16)
    w2 = w2_ref[0].astype(BF16)

    def window(w, carry):
        rows = pl.ds(pl.multiple_of(row0 + w * win, BF16_ROWS), win)
        x = xs_ref[rows, :]
        gsw = gs_ref[rows, :]
        lane = lax.broadcasted_iota(jnp.int32, gsw.shape, 1)
        ge = jnp.sum(jnp.where(lane == GATE_COL0 + e, gsw, 0.0), axis=1, keepdims=True)
        a = _dot(x, w1)
        b = _dot(x, w3)
        act = (a * (1.0 / (1.0 + jnp.exp(-a))) * b).astype(BF16)
        acc_ref[rows, :] += ge * _dot(act, w2)
        return carry

    lax.fori_loop(0, n_win, window, 0)

    @pl.when(e == pl.num_programs(1) - 1)
    def _():
        a_hi, a_mid, a_lo = _split3(acc_ref[0:tm, :])
        sr = min(SORT_ROWS, tm)
        for r in range(0, tm, sr):
            rs = slice(r, r + sr)
            pt = pt_ref[rs, :]
            y = h_ref[rs, :] + (_dot(pt, a_hi) + _dot(pt, a_mid) + _dot(pt, a_lo))
            o_ref[rs, :] = _rms(y, gf_ref[...]) if final_norm else y


def _moe(h, xn, gate, w1, w3, w2, g_final, final_norm, tm):
    n, d = h.shape
    n_e = w1.shape[0]
    tme = MOE_TM if n % MOE_TM == 0 else tm
    win = -(-(tme * 5 // 16) // BF16_ROWS) * BF16_ROWS
    row2 = lambda w: pl.BlockSpec((tme, w), lambda i, e: (i, 0), pipeline_mode=pl.Buffered(1))
    kern = functools.partial(_expert_kernel, final_norm=final_norm, win=win)
    return pl.pallas_call(
        kern,
        grid=(n // tme, n_e),
        in_specs=[row2(d), row2(LANES),
                  pl.BlockSpec((1,) + w1.shape[1:], lambda i, e: (e, 0, 0)),
                  pl.BlockSpec((1,) + w3.shape[1:], lambda i, e: (e, 0, 0)),
                  pl.BlockSpec((1,) + w2.shape[1:], lambda i, e: (e, 0, 0)),
                  row2(d), pl.BlockSpec((1, d), lambda i, e: (0, 0))],
        out_specs=row2(d),
        out_shape=jax.ShapeDtypeStruct((n, d), F32),
        scratch_shapes=[pltpu.VMEM((tme + win, d), F32), pltpu.VMEM((tme + win, d), BF16),
                        pltpu.VMEM((tme + win, LANES), F32), pltpu.VMEM((tme, tme), BF16),
                        pltpu.VMEM((tme, tme), BF16), pltpu.SMEM((2 * MOE_GROUPS,), jnp.int32)],
        compiler_params=_cparams(("arbitrary", "arbitrary")),
        name="moe_experts",
    )(xn, gate, w1, w3, w2, h, g_final.reshape(1, d))


EVEN_SEGS = ((0, 512, None, False), (512, 512, "full", False), (1024, 128, "full", True), (1152, 128, None, True),
             (1280, 256, "full", False), (1536, 128, "half", False))
ODD_SEGS = ((0, 512, None, False), (512, 512, None, True), (1024, 512, None, True),
            (1536, 512, "full", False), (2048, 512, "full", True), (2560, 512, None, True))


def _cat_keys(hist, new):
    allk = jnp.concatenate([hist, new], axis=1) if hist is not None else new
    n_keys = allk.shape[1]
    pad = -n_keys % LANES
    if pad:
        allk = jnp.pad(allk, ((0, 0), (0, pad), (0, 0)))
    return allk.astype(BF16), n_keys


def kernel(x_prompt, x_sample, cache_pool, cache_dsa_k, cache_dsa_v, cache_idx_k, cache_sb_k, cache_sb_v,
           cache_diff_k, cache_diff_v, norm_mix, norm_ffn, norm_final, w_in_even, w_pool, pool_scale,
           w_out_even, w_in_odd, diff_lambda, diff_subln, w_out_odd, moe_w_group, moe_b_group,
           moe_w_expert, moe_b_expert, moe_w1, moe_w3, moe_w2):
    b, t, d = x_prompt.shape
    bd, td, _ = x_sample.shape
    past = cache_dsa_k.shape[2]
    depth = norm_mix.shape[0]
    groups = ((b, t, 0, min(512, b * t)), (bd, td, past, bd * td))

    tabs = []
    for (gb, gt, p0, tm) in groups:
        tab = _rope_tables(p0 + jnp.arange(gt, dtype=jnp.int32))
        if tm > gt:
            tab = jnp.tile(tab, (tm // gt, 1))
        tabs.append(tab)

    hs = [x_prompt.reshape(b * t, d), x_sample.reshape(bd * td, d)]
    outs = [dict(), dict()]
    for l in range(depth):
        li = l // 2
        last = l == depth - 1
        if l % 2 == 0:
            n_in = w_in_even.shape[2]
            w_in = jnp.pad(w_in_even[li], ((0, 0), (0, -n_in % LANES))).astype(BF16)
            w_out = w_out_even[li].astype(BF16)
            wp = w_pool[li].astype(BF16)
        else:
            w_in = w_in_odd[li].astype(BF16)
            w_out = w_out_odd[li].astype(BF16)
            lam_init = 0.8 - 0.6 * math.exp(-0.3 * l)
        wr = jnp.concatenate([moe_w_group[l]] + [moe_w_expert[l, g] for g in range(MOE_GROUPS)], axis=1)
        wr = jnp.pad(wr, ((0, 0), (0, LANES - wr.shape[1]))).astype(BF16)
        br = jnp.concatenate([moe_b_group[l], moe_b_expert[l].reshape(-1)])
        br = jnp.pad(br, (0, LANES - br.shape[0])).reshape(1, LANES).astype(F32)

        for gi, (gb, gt, p0, tm) in enumerate(groups):
            h = hs[gi]
            o = outs[gi]
            sample = gi == 1
            r3 = lambda x: x.reshape(gb, gt, x.shape[-1])
            if l % 2 == 0:
                u, q, k, v, qi, kiwi, k16, v16 = [
                    r3(x) for x in _project(h, norm_mix[l], w_in, tabs[gi], EVEN_SEGS, tm)]
                ki = kiwi[..., :HEAD_DIM]
                hist = cache_pool[li] if sample else jnp.zeros((gb, POOL_HIST, u.shape[2]), F32)
                a_out = _pool_mix(u, hist, wp, pool_scale[li], p0)
                if sample:
                    k_all, n_keys = _cat_keys(cache_dsa_k[li].reshape(gb, past, -1), k)
                    v_all, _ = _cat_keys(cache_dsa_v[li].reshape(gb, past, -1), v)
                    ki_all, _ = _cat_keys(cache_idx_k[li], ki)
                else:
                    (k_all, n_keys), (v_all, _), (ki_all, _) = _cat_keys(None, k16), _cat_keys(None, v16), _cat_keys(None, ki)
                b_out = _dsa(q, qi, kiwi, k_all, v_all, ki_all, n_keys, p0)
                o.setdefault("pool", []).append(jnp.concatenate([hist, u], axis=1)[:, -POOL_HIST:])
                o.setdefault("dsa_k", []).append(k.reshape(gb, gt, -1, HEAD_DIM))
                o.setdefault("dsa_v", []).append(v.reshape(gb, gt, -1, HEAD_DIM))
                o.setdefault("idx_k", []).append(ki)
                mix_a, mix_b = a_out, b_out
            else:
                sq, sk, sv, dq, dk, dv, sk16, sv16, dk16, dv16 = [
                    r3(x) for x in _project(h, norm_mix[l], w_in, tabs[gi], ODD_SEGS, tm)]
                if sample:
                    sk_all, n_keys = _cat_keys(cache_sb_k[li].reshape(gb, past, -1), sk)
                    sv_all, _ = _cat_keys(cache_sb_v[li].reshape(gb, past, -1), sv)
                    dk_all, _ = _cat_keys(cache_diff_k[li].reshape(gb, past, -1), dk)
                    dv_all, _ = _cat_keys(cache_diff_v[li].reshape(gb, past, -1), dv)
                else:
                    (sk_all, n_keys), (sv_all, _) = _cat_keys(None, sk16), _cat_keys(None, sv16)
                    (dk_all, _), (dv_all, _) = _cat_keys(None, dk16), _cat_keys(None, dv16)
                c_out = _sb_attend(sq, sk_all, sv_all, n_keys, p0)
                d_out = _diff_attend(dq, dk_all, dv_all, diff_lambda[li], diff_subln[li], n_keys, p0, lam_init)
                n_sb = sk.shape[2] // HEAD_DIM
                n_df = dk.shape[2] // (2 * HEAD_DIM)
                o.setdefault("sb_k", []).append(sk.reshape(gb, gt, n_sb, HEAD_DIM))
                o.setdefault("sb_v", []).append(sv.reshape(gb, gt, n_sb, HEAD_DIM))
                o.setdefault("diff_k", []).append(dk.reshape(gb, gt, n_df, 2, HEAD_DIM))
                o.setdefault("diff_v", []).append(dv.reshape(gb, gt, n_df, 2 * HEAD_DIM))
                mix_a, mix_b = c_out, d_out
            h, xn, gate = _out_proj_route(h, mix_a.reshape(gb * gt, -1), mix_b.reshape(gb * gt, -1), w_out,
                                          norm_ffn[l], wr, br, tm)
            hs[gi] = _moe(h, xn, gate, moe_w1[l], moe_w3[l], moe_w2[l], norm_final, last, tm)

    names = ("pool", "dsa_k", "dsa_v", "idx_k", "sb_k", "sb_v", "diff_k", "diff_v")
    res = [hs[0].reshape(b, t, d), hs[1].reshape(bd, td, d)]
    for o in outs:
        res += [jnp.stack(o[nm]) for nm in names]
    return tuple(res)
```

```python
import functools
import math

import jax
import jax.numpy as jnp
import numpy as np
from jax import lax
from jax.experimental import pallas as pl
from jax.experimental.pallas import tpu as pltpu

F32 = jnp.float32
BF16 = jnp.bfloat16

LANES = 128
HEAD_DIM = 64
CHUNK = 64
ROPE_THETA = 10000.0
NORM_EPS = 1e-6
NEG_INF = -1e30
PAD_SCORE = -3e38
BIG_POS = 3e38
POOL_WINDOWS = (2, 4, 8, 16)
POOL_HIST = 15
POOL_HIST_PAD = 16
DSA_TOPK = 256
IDX_HEADS = 4
MOE_GROUPS = 4
MOE_EPG = 4
MOE_EXPERTS = 16
GATE_COL0 = MOE_GROUPS
GROUP_LANE = 0
BF16_ROWS = 16
SORT_ROWS = 256
VMEM_LIMIT = 56 * 1024 * 1024
BISECT_STEPS = 8
BISECT_ROUNDS = 48
ATTN_TQ = 256
DSA_TQ = 128
MOE_TM = 1024
DIFF_LANES = 2 * LANES


def _cparams(sem):
    return pltpu.CompilerParams(dimension_semantics=sem, vmem_limit_bytes=VMEM_LIMIT)


def _dot(a, b):
    return jnp.dot(a, b, preferred_element_type=F32)


def _dot_nt(a, b):
    return lax.dot_general(a, b, (((1,), (1,)), ((), ())), preferred_element_type=F32)


def _dot_tn(a, b):
    return lax.dot_general(a, b, (((0,), (0,)), ((), ())), preferred_element_type=F32)


def _rms(x, g):
    ms = jnp.mean(x * x, axis=-1, keepdims=True)
    return x * lax.rsqrt(ms + NORM_EPS) * g


def _proj_kernel(x_ref, g_ref, w_ref, tab_ref, *out_refs, segs):
    xn = _rms(x_ref[...], g_ref[...]).astype(BF16)
    copies = iter(out_refs[len(segs):])
    for o_ref, (c0, width, mode, twin) in zip(out_refs, segs):
        t_ref = next(copies) if twin else None
        y = _dot(xn, w_ref[:, c0:c0 + width])
        if mode is None:
            o_ref[...] = y
            if twin:
                t_ref[...] = y.astype(BF16)
            continue
        t0 = 0 if mode == "full" else 3 * LANES
        cos = tab_ref[:, t0:t0 + LANES]
        sin_a = tab_ref[:, t0 + LANES:t0 + 2 * LANES]
        sin_b = tab_ref[:, t0 + 2 * LANES:t0 + 3 * LANES]
        for c in range(0, width, LANES):
            yc = y[:, c:c + LANES]
            yr = (yc * cos + pltpu.roll(yc, LANES - HEAD_DIM // 2, 1) * sin_a
                  + pltpu.roll(yc, HEAD_DIM // 2, 1) * sin_b)
            o_ref[:, c:c + LANES] = yr
            if twin:
                t_ref[:, c:c + LANES] = yr.astype(BF16)


def _rope_tables(pos):
    half = HEAD_DIM // 2
    inv = ROPE_THETA ** (-jnp.arange(half, dtype=F32) / half)
    ang = pos.astype(F32)[:, None] * inv[None, :]
    cos, sin = jnp.cos(ang), jnp.sin(ang)
    zero, one = jnp.zeros_like(sin), jnp.ones_like(cos)
    cos_h = jnp.concatenate([cos, cos], axis=1)
    sa_h = jnp.concatenate([-sin, zero], axis=1)
    sb_h = jnp.concatenate([zero, sin], axis=1)
    one_h = jnp.concatenate([one, one], axis=1)
    zero_h = jnp.concatenate([zero, zero], axis=1)
    return jnp.concatenate([cos_h, cos_h, sa_h, sa_h, sb_h, sb_h,
                            cos_h, one_h, sa_h, zero_h, sb_h, zero_h], axis=1)


def _project(x, g, w, tab, segs, tm):
    n, d = x.shape
    tt = tab.shape[0]
    nt = tt // tm
    kern = functools.partial(_proj_kernel, segs=segs)
    return pl.pallas_call(
        kern,
        grid=(n // tm,),
        in_specs=[pl.BlockSpec((tm, d), lambda i: (i, 0)),
                  pl.BlockSpec((1, d), lambda i: (0, 0)),
                  pl.BlockSpec(w.shape, lambda i: (0, 0)),
                  pl.BlockSpec((tm, tab.shape[1]), lambda i: (i % nt, 0))],
        out_specs=[pl.BlockSpec((tm, wd), lambda i: (i, 0)) for _, wd, _, _ in segs]
        + [pl.BlockSpec((tm, wd), lambda i: (i, 0)) for _, wd, _, twin in segs if twin],
        out_shape=[jax.ShapeDtypeStruct((n, wd), F32) for _, wd, _, _ in segs]
        + [jax.ShapeDtypeStruct((n, wd), BF16) for _, wd, _, twin in segs if twin],
        compiler_params=_cparams(("parallel",)),
        name="proj",
    )(x, g.reshape(1, d), w, tab)


def _pool_kernel(u_ref, h_ref, w_ref, s_ref, o_ref, ext_ref, *, t, pos0, rc):
    ext_ref[0:POOL_HIST_PAD, :] = h_ref[0]
    ext_ref[POOL_HIST_PAD:POOL_HIST_PAD + t, :] = u_ref[0]
    for r0 in range(0, t, rc):
        pos = pos0 + r0 + lax.broadcasted_iota(jnp.int32, (rc, 1), 0)
        for g, win in enumerate(POOL_WINDOWS):
            c0 = g * LANES
            u_new = ext_ref[POOL_HIST_PAD + r0:POOL_HIST_PAD + r0 + rc, c0:c0 + LANES]
            s = u_new
            for k in range(1, win):
                s = s + ext_ref[POOL_HIST_PAD + r0 - k:POOL_HIST_PAD + r0 - k + rc, c0:c0 + LANES]
            cnt = jnp.minimum(pos + 1, win).astype(F32)
            dlt = (s / cnt - u_new).astype(BF16)
            o_ref[0, r0:r0 + rc, c0:c0 + LANES] = _dot(dlt, w_ref[g]) * s_ref[:, c0:c0 + LANES]


def _pool_mix(u, hist, w_pool, pool_scale, pos0):
    b, t, c = u.shape
    rc = min(t, 256)
    hist16 = jnp.pad(hist, ((0, 0), (POOL_HIST_PAD - POOL_HIST, 0), (0, 0)))
    kern = functools.partial(_pool_kernel, t=t, pos0=pos0, rc=rc)
    return pl.pallas_call(
        kern,
        grid=(b,),
        in_specs=[pl.BlockSpec((1, t, c), lambda i: (i, 0, 0)),
                  pl.BlockSpec((1, POOL_HIST_PAD, c), lambda i: (i, 0, 0)),
                  pl.BlockSpec(w_pool.shape, lambda i: (0, 0, 0)),
                  pl.BlockSpec((1, c), lambda i: (0, 0))],
        out_specs=pl.BlockSpec((1, t, c), lambda i: (i, 0, 0)),
        out_shape=jax.ShapeDtypeStruct((b, t, c), F32),
        scratch_shapes=[pltpu.VMEM((POOL_HIST_PAD + t, c), F32)],
        compiler_params=_cparams(("parallel",)),
        name="pool_mix",
    )(u, hist16, w_pool, pool_scale.reshape(1, c))


def _dsa_kernel(q_ref, qi_ref, kw_ref, k_ref, v_ref, ki_ref, o_ref, lo_ref, hi_ref, bias_ref,
                *, tq, extents, n_keys, q_pos0, n_sel):
    i = pl.program_id(1)
    qpos = q_pos0 + i * tq + lax.broadcasted_iota(jnp.int32, (tq, 1), 0)
    lim = jnp.minimum((qpos // CHUNK + 1) * CHUNK, n_keys)
    kf = float(n_sel)
    low = lax.broadcasted_iota(jnp.int32, (tq, LANES), 1) < HEAD_DIM
    q = q_ref[0] * HEAD_DIM ** -0.5
    qi = qi_ref[0].astype(BF16)
    wi = kw_ref[0][:, HEAD_DIM:HEAD_DIM + IDX_HEADS] * (IDX_HEADS * HEAD_DIM) ** -0.5

    def body(ext):
        kpos = lax.broadcasted_iota(jnp.int32, (tq, ext), 1)
        adm = kpos < lim
        padded = ext > n_keys
        virt = float(max(n_keys - ext, 0))

        sidx = jnp.zeros((tq, ext), F32)
        ki = ki_ref[0, :ext, :]
        for h in range(IDX_HEADS):
            sh = _dot_nt(qi[:, h * HEAD_DIM:(h + 1) * HEAD_DIM], ki)
            sidx = sidx + jnp.maximum(sh, 0.0) * wi[:, h:h + 1]
        sm = jnp.where(adm, sidx, NEG_INF)
        if padded:
            real = kpos < n_keys
            sm = jnp.where(real, sm, PAD_SCORE)

        def count_gt(x):
            cnt = jnp.sum(jnp.where(sm > x, 1.0, 0.0), axis=1, keepdims=True)
            return cnt + jnp.where(x < NEG_INF, virt, 0.0) if virt else cnt

        def bracket(lo, hi):
            above = jnp.min(jnp.where(sm > lo, sm, BIG_POS), axis=1, keepdims=True)
            below = jnp.max(jnp.where(sm <= hi, sm, PAD_SCORE), axis=1, keepdims=True)
            if virt:
                above = jnp.minimum(above, jnp.where(lo < NEG_INF, NEG_INF, BIG_POS))
                below = jnp.maximum(below, jnp.where(hi >= NEG_INF, NEG_INF, PAD_SCORE))
            return above, below

        row_max = jnp.max(sm, axis=1, keepdims=True)
        row_min = jnp.min(jnp.where(real, sm, BIG_POS) if padded else sm, axis=1, keepdims=True)
        if virt:
            row_min = jnp.minimum(row_min, NEG_INF)
        adm_min = jnp.min(jnp.where(adm, sm, BIG_POS), axis=1, keepdims=True)
        few = count_gt(row_min) < kf
        tight = count_gt(adm_min) >= kf
        lo_ref[...] = jnp.where(few, PAD_SCORE, jnp.where(tight, adm_min, row_min))
        hi_ref[...] = jnp.where(few, row_min, jnp.where(tight, row_max, adm_min))

        def unresolved(lo, hi):
            above, below = bracket(lo, hi)
            return jnp.sum(jnp.where(above < below, 1, 0))

        def cond(carry):
            rounds, open_rows = carry
            return jnp.logical_and(open_rows > 0, rounds < BISECT_ROUNDS)

        def count_gt_wide(x):
            part = jnp.where(sm[:, :LANES] > x, 1.0, 0.0)
            for j in range(1, ext // LANES):
                part = part + jnp.where(sm[:, j * LANES:(j + 1) * LANES] > x, 1.0, 0.0)
            cnt = jnp.broadcast_to(jnp.sum(part, axis=1, keepdims=True), (tq, LANES))
            return cnt + jnp.where(x < NEG_INF, virt, 0.0) if virt else cnt

        def step(carry):
            rounds, _ = carry
            lo = jnp.broadcast_to(lo_ref[...], (tq, LANES))
            hi = jnp.broadcast_to(hi_ref[...], (tq, LANES))
            for _ in range(BISECT_STEPS):
                mid = 0.5 * lo + 0.5 * hi
                under = count_gt_wide(mid) < kf
                hi = jnp.where(under, mid, hi)
                lo = jnp.where(under, lo, mid)
            lo_ref[...] = lo[:, :1]
            hi_ref[...] = hi[:, :1]
            return rounds + 1, unresolved(lo_ref[...], hi_ref[...])

        lax.while_loop(cond, step, (jnp.int32(0), unresolved(lo_ref[...], hi_ref[...])))
        _, thr = bracket(lo_ref[...], hi_ref[...])

        gt = sm > thr
        eq = sm == thr
        need = kf - count_gt(thr)
        n_eq = jnp.sum(jnp.where(eq, 1.0, 0.0), axis=1, keepdims=True)
        crowded = jnp.sum(jnp.where(n_eq > need, 1, 0))

        @pl.when(crowded == 0)
        def _():
            keep = jnp.logical_and(jnp.logical_or(gt, eq), adm)
            bias_ref[:, :ext] = jnp.where(keep, 0.0, NEG_INF)

        @pl.when(crowded > 0)
        def _():
            rr = lax.broadcasted_iota(jnp.int32, (LANES, LANES), 0)
            cc = lax.broadcasted_iota(jnp.int32, (LANES, LANES), 1)
            prefix_ones = jnp.where(rr <= cc, 1.0, 0.0).astype(BF16)
            carry = jnp.zeros((tq, 1), F32)
            for j in range(ext // LANES):
                sl = slice(j * LANES, (j + 1) * LANES)
                eq_j = eq[:, sl]
                rank = _dot(jnp.where(eq_j, 1.0, 0.0).astype(BF16), prefix_ones) + carry
                keep = jnp.logical_or(gt[:, sl], jnp.logical_and(eq_j, rank <= need))
                bias_ref[:, sl] = jnp.where(jnp.logical_and(keep, adm[:, sl]), 0.0, NEG_INF)
                carry = rank[:, LANES - 1:LANES]

        bias = bias_ref[:, :ext]
        bias2 = jnp.concatenate([bias, bias], axis=0)

        kk = k_ref[0, :ext, :]
        vv = v_ref[0, :ext, :]
        low_k = lax.broadcasted_iota(jnp.int32, (ext, LANES), 1) < HEAD_DIM
        vsw = pltpu.roll(vv.astype(F32), HEAD_DIM, 1).astype(BF16)
        one = jnp.ones_like(vv)
        n_kv = LANES // HEAD_DIM
        for g in range(n_kv):
            v_lo = jnp.where(low_k, vv if g == 0 else vsw, one)
            v_hi = jnp.where(low_k, one, vsw if g == 0 else vv)
            stacks = []
            for odd in range(2):
                rows = []
                for m in range(2):
                    c = 2 * g + m
                    qc = q[:, c * LANES:(c + 1) * LANES]
                    if (odd == 1) != (g == 1):
                        qc = pltpu.roll(qc, HEAD_DIM, 1)
                    rows.append(jnp.where(low, qc, 0.0) if g == 0 else jnp.where(low, 0.0, qc))
                qs = jnp.concatenate(rows, axis=0).astype(BF16)
                sc = _dot_nt(qs, kk) + bias2
                p = jnp.exp(sc - jnp.max(sc, axis=1, keepdims=True))
                og = _dot(p.astype(BF16), v_hi if odd else v_lo)
                stacks.append(og / pltpu.roll(og, HEAD_DIM, 1))
            for m in range(2):
                c = 2 * g + m
                o_ref[0, :, c * LANES:(c + 1) * LANES] = jnp.where(
                    low, stacks[0][m * tq:(m + 1) * tq], stacks[1][m * tq:(m + 1) * tq])

    _for_tile_extent(i, tq, q_pos0, n_keys, extents, body)


def _dsa(q, qi, kiwi, k_all, v_all, ki_all, n_keys, q_pos0):
    b, t, dq = q.shape
    s_len = k_all.shape[1]
    tq = min(t, DSA_TQ)
    n_sel = min(DSA_TOPK, n_keys // 4)
    extents = _key_extents(t // tq, tq, q_pos0, n_keys, s_len, 4 * LANES)
    kern = functools.partial(_dsa_kernel, tq=tq, extents=extents, n_keys=n_keys, q_pos0=q_pos0, n_sel=n_sel)
    qspec = lambda w: pl.BlockSpec((1, tq, w), lambda bi, i: (bi, i, 0))
    kspec = lambda w: pl.BlockSpec((1, s_len, w), lambda bi, i: (bi, 0, 0))
    return pl.pallas_call(
        kern,
        grid=(b, t // tq),
        in_specs=[qspec(dq), qspec(qi.shape[2]), qspec(kiwi.shape[2]),
                  kspec(k_all.shape[2]), kspec(v_all.shape[2]), kspec(ki_all.shape[2])],
        out_specs=qspec(dq),
        out_shape=jax.ShapeDtypeStruct((b, t, dq), F32),
        scratch_shapes=[pltpu.VMEM((tq, 1), F32), pltpu.VMEM((tq, 1), F32), pltpu.VMEM((tq, s_len), F32)],
        compiler_params=_cparams(("parallel", "parallel")),
        name="dsa",
    )(q, qi, kiwi, k_all, v_all, ki_all)


def _sb_kernel(q_ref, k_ref, v_ref, o_ref, acc_ref, run_ref, *, tq, n_keys, q_pos0):
    i = pl.program_id(1)
    pairs = q_ref.shape[2] // LANES
    first_q = q_pos0 + i * tq
    qpos = first_q + lax.broadcasted_iota(jnp.int32, (tq, 1), 0)
    n_blocks = (jnp.minimum(first_q + tq - 1, n_keys) + LANES - 1) // LANES
    n_full = jnp.minimum(first_q, n_keys) // LANES
    low_q = lax.broadcasted_iota(jnp.int32, (tq, LANES), 1) < HEAD_DIM
    low_k = lax.broadcasted_iota(jnp.int32, (LANES, LANES), 1) < HEAD_DIM
    q = q_ref[0] * HEAD_DIM ** -0.5
    qm = []
    for p in range(pairs):
        qp = q[:, p * LANES:(p + 1) * LANES]
        qm.append((jnp.where(low_q, qp, 0.0).astype(BF16), jnp.where(low_q, 0.0, qp).astype(BF16)))
    rr = lax.broadcasted_iota(jnp.int32, (2 * LANES, 2 * LANES), 0)
    cc = lax.broadcasted_iota(jnp.int32, (2 * LANES, 2 * LANES), 1)
    rk = jnp.where(rr >= LANES, rr - LANES, rr)
    cs_rhs = jnp.where(jnp.logical_or(cc >= LANES, rk >= cc), 1.0, 0.0).astype(BF16)
    acc_ref[...] = jnp.zeros_like(acc_ref)
    run_ref[...] = jnp.zeros_like(run_ref)

    def block(j, masked):
        ks = pl.multiple_of(j * LANES, LANES)
        if masked:
            kpos = ks + lax.broadcasted_iota(jnp.int32, (1, LANES), 1)
            causal = jnp.logical_and(kpos < qpos, kpos < n_keys)
        heads = [(p, c) for p in range(pairs) for c in range(2)]
        lanes = lambda n: slice(n * LANES, (n + 1) * LANES)
        zs = [_dot_nt(qm[p][c], k_ref[0, pl.ds(ks, LANES), lanes(p)]) for p, c in heads]
        css = []
        for z in zs:
            sp = jnp.maximum(z, 0.0) + jnp.log(1.0 + jnp.exp(-jnp.abs(z)))
            if masked:
                sp = jnp.where(causal, sp, 0.0)
            hi = sp.astype(BF16)
            lo = (sp - hi.astype(F32)).astype(BF16)
            css.append(_dot(jnp.concatenate([hi, lo], axis=1), cs_rhs))
        probs = []
        for n, (z, cs) in enumerate(zip(zs, css)):
            run = run_ref[:, lanes(n)]
            a = jnp.exp(z - cs[:, :LANES] - run)
            if masked:
                a = jnp.where(causal, a, 0.0)
            run_ref[:, lanes(n)] = run + cs[:, LANES:]
            probs.append(a.astype(BF16))
        for p in range(pairs):
            vb = v_ref[0, pl.ds(ks, LANES), lanes(p)]
            zero = jnp.zeros_like(vb)
            v_cat = jnp.concatenate([jnp.where(low_k, vb, zero), jnp.where(low_k, zero, vb)], axis=0)
            acc_ref[:, lanes(p)] += _dot(jnp.concatenate(probs[2 * p:2 * p + 2], axis=1), v_cat)

    def masked_step(jj, carry):
        block(n_blocks - 1 - jj, True)
        return carry

    def full_pair(jj, carry):
        block(n_full - 1 - 2 * jj, False)
        block(n_full - 2 - 2 * jj, False)
        return carry

    lax.fori_loop(0, n_blocks - n_full, masked_step, 0)
    lax.fori_loop(0, n_full // 2, full_pair, 0)

    @pl.when(n_full % 2 == 1)
    def _():
        block(jnp.int32(0), False)

    o_ref[0] = acc_ref[...]


def _sb_attend(q, k_all, v_all, n_keys, q_pos0):
    b, t, d = q.shape
    s_len = k_all.shape[1]
    tq = min(t, ATTN_TQ)
    kern = functools.partial(_sb_kernel, tq=tq, n_keys=n_keys, q_pos0=q_pos0)
    return pl.pallas_call(
        kern,
        grid=(b, t // tq),
        in_specs=[pl.BlockSpec((1, tq, d), lambda bi, i: (bi, i, 0)),
                  pl.BlockSpec((1, s_len, d), lambda bi, i: (bi, 0, 0)),
                  pl.BlockSpec((1, s_len, d), lambda bi, i: (bi, 0, 0))],
        out_specs=pl.BlockSpec((1, tq, d), lambda bi, i: (bi, i, 0)),
        out_shape=jax.ShapeDtypeStruct((b, t, d), F32),
        scratch_shapes=[pltpu.VMEM((tq, d), F32), pltpu.VMEM((tq, 2 * d), F32)],
        compiler_params=_cparams(("parallel", "parallel")),
        name="sb_attend",
    )(q, k_all, v_all)


def _key_extents(nq, tq, q_pos0, n_keys, s_len, step):
    need = [min(n_keys, ((q_pos0 + (i + 1) * tq - 1) // CHUNK + 1) * CHUNK) for i in range(nq)]
    return tuple(sorted({min(s_len, -(-n // step) * step) for n in need}))


def _for_tile_extent(i, tq, q_pos0, n_keys, extents, body):
    need = jnp.minimum(n_keys, ((q_pos0 + (i + 1) * tq - 1) // CHUNK + 1) * CHUNK)
    prev = 0
    for ext in extents:
        pl.when(jnp.logical_and(need > prev, need <= ext))(functools.partial(body, ext))
        prev = ext


def _diff_kernel(q_ref, k_ref, v_ref, lam_ref, gain_ref, o_ref, *, tq, extents, n_keys, q_pos0, lam_init):
    i = pl.program_id(2)
    lp = lam_ref[...]
    lam = (jnp.exp(jnp.sum(lp[0:1] * lp[1:2], axis=1, keepdims=True))
           - jnp.exp(jnp.sum(lp[2:3] * lp[3:4], axis=1, keepdims=True)) + lam_init)
    qpos = q_pos0 + i * tq + lax.broadcasted_iota(jnp.int32, (tq, 1), 0)
    lim = jnp.minimum((qpos // CHUNK + 1) * CHUNK, n_keys)
    low = lax.broadcasted_iota(jnp.int32, (tq, LANES), 1) < HEAD_DIM

    def body(ext):
        mask = lax.broadcasted_iota(jnp.int32, (tq, ext), 1) < lim
        for h in range(q_ref.shape[2] // LANES):
            hs = slice(h * LANES, (h + 1) * LANES)
            q = q_ref[0, :, hs] * HEAD_DIM ** -0.5
            kk = k_ref[0, :ext, hs]
            probs = []
            for c in range(2):
                qc = (jnp.where(low, q, 0.0) if c == 0 else jnp.where(low, 0.0, q)).astype(BF16)
                sc = jnp.where(mask, _dot_nt(qc, kk), NEG_INF)
                p = jnp.exp(sc - jnp.max(sc, axis=1, keepdims=True))
                probs.append(p * (1.0 / jnp.sum(p, axis=1, keepdims=True)))
            a = probs[0] - lam * probs[1]
            o = _dot(a.astype(BF16), v_ref[0, :ext, hs])
            o = o * lax.rsqrt(jnp.mean(o * o, axis=-1, keepdims=True) + NORM_EPS)
            o_ref[0, :, hs] = o * gain_ref[...] * (1.0 - lam_init)

    _for_tile_extent(i, tq, q_pos0, n_keys, extents, body)


def _diff_attend(q, k_all, v_all, diff_lambda, gain, n_keys, q_pos0, lam_init):
    b, t, d = q.shape
    s_len = k_all.shape[1]
    tq = min(t, ATTN_TQ)
    extents = _key_extents(t // tq, tq, q_pos0, n_keys, s_len, 2 * LANES)
    kern = functools.partial(_diff_kernel, tq=tq, extents=extents, n_keys=n_keys, q_pos0=q_pos0, lam_init=lam_init)
    return pl.pallas_call(
        kern,
        grid=(b, d // DIFF_LANES, t // tq),
        in_specs=[pl.BlockSpec((1, tq, DIFF_LANES), lambda bi, h, i: (bi, i, h)),
                  pl.BlockSpec((1, s_len, DIFF_LANES), lambda bi, h, i: (bi, 0, h)),
                  pl.BlockSpec((1, s_len, DIFF_LANES), lambda bi, h, i: (bi, 0, h)),
                  pl.BlockSpec(diff_lambda.shape, lambda bi, h, i: (0, 0)),
                  pl.BlockSpec((1, LANES), lambda bi, h, i: (0, 0))],
        out_specs=pl.BlockSpec((1, tq, DIFF_LANES), lambda bi, h, i: (bi, i, h)),
        out_shape=jax.ShapeDtypeStruct((b, t, d), F32),
        compiler_params=_cparams(("parallel", "parallel", "parallel")),
        name="diff_attend",
    )(q, k_all, v_all, diff_lambda, gain.reshape(1, LANES))


def _route(logits):
    lane = lax.broadcasted_iota(jnp.int32, logits.shape, 1)
    is_group = lane < MOE_GROUPS
    gl = jnp.where(is_group, logits, NEG_INF)
    g_max = jnp.max(gl, axis=1, keepdims=True)
    g_sel = jnp.min(jnp.where(gl == g_max, lane, LANES), axis=1, keepdims=True)
    g_gate = 1.0 / jnp.sum(jnp.where(is_group, jnp.exp(gl - g_max), 0.0), axis=1, keepdims=True)
    in_group = jnp.logical_and(lane >= GATE_COL0, (lane - GATE_COL0) // MOE_EPG == g_sel)
    in_group = jnp.logical_and(in_group, lane < GATE_COL0 + MOE_EXPERTS)
    el = jnp.where(in_group, logits, NEG_INF)
    top1 = jnp.max(el, axis=1, keepdims=True)
    i1 = jnp.min(jnp.where(jnp.logical_and(in_group, el == top1), lane, LANES), axis=1, keepdims=True)
    rest = jnp.logical_and(in_group, lane != i1)
    el2 = jnp.where(rest, logits, NEG_INF)
    top2 = jnp.max(el2, axis=1, keepdims=True)
    i2 = jnp.min(jnp.where(jnp.logical_and(rest, el2 == top2), lane, LANES), axis=1, keepdims=True)
    e2 = jnp.exp(top2 - top1)
    w1 = g_gate / (1.0 + e2)
    gates = jnp.where(lane == i1, w1, jnp.where(lane == i2, w1 * e2, 0.0))
    return jnp.where(lane == GROUP_LANE, g_sel.astype(F32), gates)


def _out_kernel(h_ref, a_ref, b_ref, wa_ref, wb_ref, g_ref, wr_ref, br_ref, o_ref, xn_ref, gate_ref):
    h = (h_ref[...] + _dot(a_ref[...].astype(BF16), wa_ref[...])
         + _dot(b_ref[...].astype(BF16), wb_ref[...]))
    o_ref[...] = h
    xn = _rms(h, g_ref[...]).astype(BF16)
    xn_ref[...] = xn
    gate_ref[...] = _route(_dot(xn, wr_ref[...]) + br_ref[...])


def _out_proj_route(h, a, bmix, w_out, g_ffn, wr, br, tm):
    n, d = h.shape
    ca = a.shape[1]
    wa, wb = w_out[:ca], w_out[ca:]
    row = lambda w: pl.BlockSpec((tm, w), lambda i: (i, 0))
    whole = lambda x: pl.BlockSpec(x.shape, lambda i: (0, 0))
    return pl.pallas_call(
        _out_kernel,
        grid=(n // tm,),
        in_specs=[row(d), row(ca), row(bmix.shape[1]), whole(wa), whole(wb),
                  pl.BlockSpec((1, d), lambda i: (0, 0)), whole(wr), whole(br)],
        out_specs=[row(d), row(d), row(LANES)],
        out_shape=[jax.ShapeDtypeStruct((n, d), F32), jax.ShapeDtypeStruct((n, d), BF16),
                   jax.ShapeDtypeStruct((n, LANES), F32)],
        compiler_params=_cparams(("parallel",)),
        name="out_proj_route",
    )(h, a, bmix, wa, wb, g_ffn.reshape(1, d), wr, br)


def _split3(x):
    hi = x.astype(BF16)
    r1 = x - hi.astype(F32)
    mid = r1.astype(BF16)
    lo = (r1 - mid.astype(F32)).astype(BF16)
    return hi, mid, lo


def _expert_kernel(xn_ref, gate_ref, w1_ref, w3_ref, w2_ref, h_ref, gf_ref, o_ref,
                   acc_ref, xs_ref, gs_ref, pt_ref, tri_ref, seg_ref, *, final_norm, win):
    i = pl.program_id(0)
    e = pl.program_id(1)
    tm = xn_ref.shape[0]

    @pl.when(jnp.logical_and(i == 0, e == 0))
    def _():
        rr = lax.broadcasted_iota(jnp.int32, (tm, tm), 0)
        cc = lax.broadcasted_iota(jnp.int32, (tm, tm), 1)
        tri_ref[...] = jnp.where(cc < rr, 1.0, 0.0).astype(BF16)

    @pl.when(e == 0)
    def _():
        gate = gate_ref[...]
        lane = lax.broadcasted_iota(jnp.int32, gate.shape, 1)
        lane_row = lax.broadcasted_iota(jnp.int32, (1, LANES), 1)
        in_grp = jnp.logical_and(lane < MOE_GROUPS, lane.astype(F32) == gate[:, GROUP_LANE:GROUP_LANE + 1])
        onehot = jnp.where(in_grp, 1.0, 0.0)
        before = _dot(tri_ref[...], onehot.astype(BF16))
        count = jnp.sum(onehot, axis=0, keepdims=True)
        first = jnp.zeros((1, LANES), F32)
        start = jnp.float32(0.0)
        for g in range(MOE_GROUPS):
            n_g = jnp.sum(jnp.where(lane_row == g, count, 0.0))
            seg_ref[g] = start.astype(jnp.int32)
            seg_ref[MOE_GROUPS + g] = n_g.astype(jnp.int32)
            first = jnp.where(lane_row == g, start, first)
            start = start + n_g
        dest = jnp.sum(onehot * (before + first), axis=1, keepdims=True).astype(jnp.int32)
        g_hi, g_mid, g_lo = _split3(gate)
        sr = min(SORT_ROWS, tm)
        for r in range(0, tm, sr):
            rs = slice(r, r + sr)
            pt_ref[rs, :] = jnp.where(lax.broadcasted_iota(jnp.int32, (sr, tm), 1) == dest[rs],
                                      1.0, 0.0).astype(BF16)
        for r in range(0, tm, sr):
            rs = slice(r, r + sr)
            pt_cols = pt_ref[:, rs]
            xs_ref[rs, :] = _dot_tn(pt_cols, xn_ref[...]).astype(BF16)
            gs_ref[rs, :] = _dot_tn(pt_cols, g_hi) + _dot_tn(pt_cols, g_mid) + _dot_tn(pt_cols, g_lo)
        xs_ref[tm:, :] = jnp.zeros((win, xs_ref.shape[1]), BF16)
        gs_ref[tm:, :] = jnp.zeros((win, LANES), F32)
        acc_ref[...] = jnp.zeros_like(acc_ref)

    grp = e // MOE_EPG
    seg_first = seg_ref[grp]
    seg_rows = seg_ref[MOE_GROUPS + grp]
    row0 = (seg_first // BF16_ROWS) * BF16_ROWS
    n_win = (seg_first + seg_rows - row0 + win - 1) // win

    def window(w, carry):
        rows = pl.ds(pl.multiple_of(row0 + w * win, BF16_ROWS), win)
        x = xs_ref[rows, :]
        gsw = gs_ref[rows, :]
        lane = lax.broadcasted_iota(jnp.int32, gsw.shape, 1)
        ge = jnp.sum(jnp.where(lane == GATE_COL0 + e, gsw, 0.0), axis=1, keepdims=True)
        a = _dot(x, w1_ref[0])
        b = _dot(x, w3_ref[0])
        act = (a * (1.0 / (1.0 + jnp.exp(-a))) * b).astype(BF16)
        acc_ref[rows, :] += ge * _dot(act, w2_ref[0])
        return carry

    lax.fori_loop(0, n_win, window, 0)

    @pl.when(e == pl.num_programs(1) - 1)
    def _():
        a_hi, a_mid, a_lo = _split3(acc_ref[0:tm, :])
        sr = min(SORT_ROWS, tm)
        for r in range(0, tm, sr):
            rs = slice(r, r + sr)
            pt = pt_ref[rs, :]
            y = h_ref[rs, :] + (_dot(pt, a_hi) + _dot(pt, a_mid) + _dot(pt, a_lo))
            o_ref[rs, :] = _rms(y, gf_ref[...]) if final_norm else y


def _moe(h, xn, gate, w1, w3, w2, g_final, final_norm, tm):
    n, d = h.shape
    n_e = w1.shape[0]
    tme = MOE_TM if n % MOE_TM == 0 else tm
    win = -(-(tme * 5 // 16) // BF16_ROWS) * BF16_ROWS
    row2 = lambda w: pl.BlockSpec((tme, w), lambda i, e: (i, 0), pipeline_mode=pl.Buffered(1))
    kern = functools.partial(_expert_kernel, final_norm=final_norm, win=win)
    return pl.pallas_call(
        kern,
        grid=(n // tme, n_e),
        in_specs=[row2(d), row2(LANES),
                  pl.BlockSpec((1,) + w1.shape[1:], lambda i, e: (e, 0, 0)),
                  pl.BlockSpec((1,) + w3.shape[1:], lambda i, e: (e, 0, 0)),
                  pl.BlockSpec((1,) + w2.shape[1:], lambda i, e: (e, 0, 0)),
                  row2(d), pl.BlockSpec((1, d), lambda i, e: (0, 0))],
        out_specs=row2(d),
        out_shape=jax.ShapeDtypeStruct((n, d), F32),
        scratch_shapes=[pltpu.VMEM((tme + win, d), F32), pltpu.VMEM((tme + win, d), BF16),
                        pltpu.VMEM((tme + win, LANES), F32), pltpu.VMEM((tme, tme), BF16),
                        pltpu.VMEM((tme, tme), BF16), pltpu.SMEM((2 * MOE_GROUPS,), jnp.int32)],
        compiler_params=_cparams(("arbitrary", "arbitrary")),
        name="moe_experts",
    )(xn, gate, w1, w3, w2, h, g_final.reshape(1, d))


EVEN_SEGS = ((0, 512, None, False), (512, 512, "full", False), (1024, 128, "full", True), (1152, 128, None, True),
             (1280, 256, "full", False), (1536, 128, "half", False))
ODD_SEGS = ((0, 512, None, False), (512, 512, None, True), (1024, 512, None, True),
            (1536, 512, "full", False), (2048, 512, "full", True), (2560, 512, None, True))


def _cat_keys(hist, new):
    allk = jnp.concatenate([hist, new], axis=1) if hist is not None else new
    n_keys = allk.shape[1]
    pad = -n_keys % LANES
    if pad:
        allk = jnp.pad(allk, ((0, 0), (0, pad), (0, 0)))
    return allk.astype(BF16), n_keys


def kernel(x_prompt, x_sample, cache_pool, cache_dsa_k, cache_dsa_v, cache_idx_k, cache_sb_k, cache_sb_v,
           cache_diff_k, cache_diff_v, norm_mix, norm_ffn, norm_final, w_in_even, w_pool, pool_scale,
           w_out_even, w_in_odd, diff_lambda, diff_subln, w_out_odd, moe_w_group, moe_b_group,
           moe_w_expert, moe_b_expert, moe_w1, moe_w3, moe_w2):
    b, t, d = x_prompt.shape
    bd, td, _ = x_sample.shape
    past = cache_dsa_k.shape[2]
    depth = norm_mix.shape[0]
    groups = ((b, t, 0, min(512, b * t)), (bd, td, past, bd * td))

    tabs = []
    for (gb, gt, p0, tm) in groups:
        tab = _rope_tables(p0 + jnp.arange(gt, dtype=jnp.int32))
        if tm > gt:
            tab = jnp.tile(tab, (tm // gt, 1))
        tabs.append(tab)

    hs = [x_prompt.reshape(b * t, d), x_sample.reshape(bd * td, d)]
    outs = [dict(), dict()]
    for l in range(depth):
        li = l // 2
        last = l == depth - 1
        if l % 2 == 0:
            n_in = w_in_even.shape[2]
            w_in = jnp.pad(w_in_even[li], ((0, 0), (0, -n_in % LANES))).astype(BF16)
            w_out = w_out_even[li].astype(BF16)
            wp = w_pool[li].astype(BF16)
        else:
            w_in = w_in_odd[li].astype(BF16)
            w_out = w_out_odd[li].astype(BF16)
            lam_init = 0.8 - 0.6 * math.exp(-0.3 * l)
        wr = jnp.concatenate([moe_w_group[l]] + [moe_w_expert[l, g] for g in range(MOE_GROUPS)], axis=1)
        wr = jnp.pad(wr, ((0, 0), (0, LANES - wr.shape[1]))).astype(BF16)
        br = jnp.concatenate([moe_b_group[l], moe_b_expert[l].reshape(-1)])
        br = jnp.pad(br, (0, LANES - br.shape[0])).reshape(1, LANES).astype(F32)
        w1, w3, w2 = moe_w1[l].astype(BF16), moe_w3[l].astype(BF16), moe_w2[l].astype(BF16)

        for gi, (gb, gt, p0, tm) in enumerate(groups):
            h = hs[gi]
            o = outs[gi]
            sample = gi == 1
            r3 = lambda x: x.reshape(gb, gt, x.shape[-1])
            if l % 2 == 0:
                u, q, k, v, qi, kiwi, k16, v16 = [
                    r3(x) for x in _project(h, norm_mix[l], w_in, tabs[gi], EVEN_SEGS, tm)]
                ki = kiwi[..., :HEAD_DIM]
                hist = cache_pool[li] if sample else jnp.zeros((gb, POOL_HIST, u.shape[2]), F32)
                a_out = _pool_mix(u, hist, wp, pool_scale[li], p0)
                if sample:
                    k_all, n_keys = _cat_keys(cache_dsa_k[li].reshape(gb, past, -1), k)
                    v_all, _ = _cat_keys(cache_dsa_v[li].reshape(gb, past, -1), v)
                    ki_all, _ = _cat_keys(cache_idx_k[li], ki)
                else:
                    (k_all, n_keys), (v_all, _), (ki_all, _) = _cat_keys(None, k16), _cat_keys(None, v16), _cat_keys(None, ki)
                b_out = _dsa(q, qi, kiwi, k_all, v_all, ki_all, n_keys, p0)
                o.setdefault("pool", []).append(jnp.concatenate([hist, u], axis=1)[:, -POOL_HIST:])
                o.setdefault("dsa_k", []).append(k.reshape(gb, gt, -1, HEAD_DIM))
                o.setdefault("dsa_v", []).append(v.reshape(gb, gt, -1, HEAD_DIM))
                o.setdefault("idx_k", []).append(ki)
                mix_a, mix_b = a_out, b_out
            else:
                sq, sk, sv, dq, dk, dv, sk16, sv16, dk16, dv16 = [
                    r3(x) for x in _project(h, norm_mix[l], w_in, tabs[gi], ODD_SEGS, tm)]
                if sample:
                    sk_all, n_keys = _cat_keys(cache_sb_k[li].reshape(gb, past, -1), sk)
                    sv_all, _ = _cat_keys(cache_sb_v[li].reshape(gb, past, -1), sv)
                    dk_all, _ = _cat_keys(cache_diff_k[li].reshape(gb, past, -1), dk)
                    dv_all, _ = _cat_keys(cache_diff_v[li].reshape(gb, past, -1), dv)
                else:
                    (sk_all, n_keys), (sv_all, _) = _cat_keys(None, sk16), _cat_keys(None, sv16)
                    (dk_all, _), (dv_all, _) = _cat_keys(None, dk16), _cat_keys(None, dv16)
                c_out = _sb_attend(sq, sk_all, sv_all, n_keys, p0)
                d_out = _diff_attend(dq, dk_all, dv_all, diff_lambda[li], diff_subln[li], n_keys, p0, lam_init)
                n_sb = sk.shape[2] // HEAD_DIM
                n_df = dk.shape[2] // (2 * HEAD_DIM)
                o.setdefault("sb_k", []).append(sk.reshape(gb, gt, n_sb, HEAD_DIM))
                o.setdefault("sb_v", []).append(sv.reshape(gb, gt, n_sb, HEAD_DIM))
                o.setdefault("diff_k", []).append(dk.reshape(gb, gt, n_df, 2, HEAD_DIM))
                o.setdefault("diff_v", []).append(dv.reshape(gb, gt, n_df, 2 * HEAD_DIM))
                mix_a, mix_b = c_out, d_out
            h, xn, gate = _out_proj_route(h, mix_a.reshape(gb * gt, -1), mix_b.reshape(gb * gt, -1), w_out,
                                          norm_ffn[l], wr, br, tm)
            hs[gi] = _moe(h, xn, gate, w1, w3, w2, norm_final, last, tm)

    names = ("pool", "dsa_k", "dsa_v", "idx_k", "sb_k", "sb_v", "diff_k", "diff_v")
    res = [hs[0].reshape(b, t, d), hs[1].reshape(bd, td, d)]
    for o in outs:
        res += [jnp.stack(o[nm]) for nm in names]
    return tuple(res)
```

```python
import functools
import math

import jax
import jax.numpy as jnp
import numpy as np
from jax import lax
from jax.experimental import pallas as pl
from jax.experimental.pallas import tpu as pltpu

F32 = jnp.float32
BF16 = jnp.bfloat16

LANES = 128
HEAD_DIM = 64
CHUNK = 64
ROPE_THETA = 10000.0
NORM_EPS = 1e-6
NEG_INF = -1e30
PAD_SCORE = -3e38
BIG_POS = 3e38
POOL_WINDOWS = (2, 4, 8, 16)
POOL_HIST = 15
POOL_HIST_PAD = 16
DSA_TOPK = 256
IDX_HEADS = 4
MOE_GROUPS = 4
MOE_EPG = 4
MOE_EXPERTS = 16
GATE_COL0 = MOE_GROUPS
GROUP_LANE = 0
BF16_ROWS = 16
SORT_ROWS = 256
VMEM_LIMIT = 56 * 1024 * 1024
BISECT_STEPS = 8
BISECT_ROUNDS = 48
BISECT_UNCHECKED = 2
ATTN_TQ = 256
DSA_TQ = 128
MOE_TM = 1024
DIFF_LANES = 2 * LANES


def _cparams(sem):
    return pltpu.CompilerParams(dimension_semantics=sem, vmem_limit_bytes=VMEM_LIMIT)


def _dot(a, b):
    return jnp.dot(a, b, preferred_element_type=F32)


def _dot_nt(a, b):
    return lax.dot_general(a, b, (((1,), (1,)), ((), ())), preferred_element_type=F32)


def _dot_tn(a, b):
    return lax.dot_general(a, b, (((0,), (0,)), ((), ())), preferred_element_type=F32)


def _rms(x, g):
    ms = jnp.mean(x * x, axis=-1, keepdims=True)
    return x * lax.rsqrt(ms + NORM_EPS) * g


def _proj_kernel(x_ref, g_ref, w_ref, tab_ref, *out_refs, segs):
    xn = _rms(x_ref[...], g_ref[...]).astype(BF16)
    copies = iter(out_refs[len(segs):])
    for o_ref, (c0, width, mode, twin) in zip(out_refs, segs):
        t_ref = next(copies) if twin else None
        y = _dot(xn, w_ref[:, c0:c0 + width])
        if mode is None:
            o_ref[...] = y
            if twin:
                t_ref[...] = y.astype(BF16)
            continue
        t0 = 0 if mode == "full" else 3 * LANES
        cos = tab_ref[:, t0:t0 + LANES]
        sin_a = tab_ref[:, t0 + LANES:t0 + 2 * LANES]
        sin_b = tab_ref[:, t0 + 2 * LANES:t0 + 3 * LANES]
        for c in range(0, width, LANES):
            yc = y[:, c:c + LANES]
            yr = (yc * cos + pltpu.roll(yc, LANES - HEAD_DIM // 2, 1) * sin_a
                  + pltpu.roll(yc, HEAD_DIM // 2, 1) * sin_b)
            o_ref[:, c:c + LANES] = yr
            if twin:
                t_ref[:, c:c + LANES] = yr.astype(BF16)


def _rope_tables(pos):
    half = HEAD_DIM // 2
    inv = ROPE_THETA ** (-jnp.arange(half, dtype=F32) / half)
    ang = pos.astype(F32)[:, None] * inv[None, :]
    cos, sin = jnp.cos(ang), jnp.sin(ang)
    zero, one = jnp.zeros_like(sin), jnp.ones_like(cos)
    cos_h = jnp.concatenate([cos, cos], axis=1)
    sa_h = jnp.concatenate([-sin, zero], axis=1)
    sb_h = jnp.concatenate([zero, sin], axis=1)
    one_h = jnp.concatenate([one, one], axis=1)
    zero_h = jnp.concatenate([zero, zero], axis=1)
    return jnp.concatenate([cos_h, cos_h, sa_h, sa_h, sb_h, sb_h,
                            cos_h, one_h, sa_h, zero_h, sb_h, zero_h], axis=1)


def _project(x, g, w, tab, segs, tm):
    n, d = x.shape
    tt = tab.shape[0]
    nt = tt // tm
    kern = functools.partial(_proj_kernel, segs=segs)
    return pl.pallas_call(
        kern,
        grid=(n // tm,),
        in_specs=[pl.BlockSpec((tm, d), lambda i: (i, 0)),
                  pl.BlockSpec((1, d), lambda i: (0, 0)),
                  pl.BlockSpec(w.shape, lambda i: (0, 0)),
                  pl.BlockSpec((tm, tab.shape[1]), lambda i: (i % nt, 0))],
        out_specs=[pl.BlockSpec((tm, wd), lambda i: (i, 0)) for _, wd, _, _ in segs]
        + [pl.BlockSpec((tm, wd), lambda i: (i, 0)) for _, wd, _, twin in segs if twin],
        out_shape=[jax.ShapeDtypeStruct((n, wd), F32) for _, wd, _, _ in segs]
        + [jax.ShapeDtypeStruct((n, wd), BF16) for _, wd, _, twin in segs if twin],
        compiler_params=_cparams(("parallel",)),
        name="proj",
    )(x, g.reshape(1, d), w, tab)


def _pool_kernel(u_ref, h_ref, w_ref, s_ref, o_ref, ext_ref, *, t, pos0, rc):
    ext_ref[0:POOL_HIST_PAD, :] = h_ref[0]
    ext_ref[POOL_HIST_PAD:POOL_HIST_PAD + t, :] = u_ref[0]
    for r0 in range(0, t, rc):
        pos = pos0 + r0 + lax.broadcasted_iota(jnp.int32, (rc, 1), 0)
        for g, win in enumerate(POOL_WINDOWS):
            c0 = g * LANES
            u_new = ext_ref[POOL_HIST_PAD + r0:POOL_HIST_PAD + r0 + rc, c0:c0 + LANES]
            s = u_new
            for k in range(1, win):
                s = s + ext_ref[POOL_HIST_PAD + r0 - k:POOL_HIST_PAD + r0 - k + rc, c0:c0 + LANES]
            cnt = jnp.minimum(pos + 1, win).astype(F32)
            dlt = (s / cnt - u_new).astype(BF16)
            o_ref[0, r0:r0 + rc, c0:c0 + LANES] = _dot(dlt, w_ref[g]) * s_ref[:, c0:c0 + LANES]


def _pool_mix(u, hist, w_pool, pool_scale, pos0):
    b, t, c = u.shape
    rc = min(t, 256)
    hist16 = jnp.pad(hist, ((0, 0), (POOL_HIST_PAD - POOL_HIST, 0), (0, 0)))
    kern = functools.partial(_pool_kernel, t=t, pos0=pos0, rc=rc)
    return pl.pallas_call(
        kern,
        grid=(b,),
        in_specs=[pl.BlockSpec((1, t, c), lambda i: (i, 0, 0)),
                  pl.BlockSpec((1, POOL_HIST_PAD, c), lambda i: (i, 0, 0)),
                  pl.BlockSpec(w_pool.shape, lambda i: (0, 0, 0)),
                  pl.BlockSpec((1, c), lambda i: (0, 0))],
        out_specs=pl.BlockSpec((1, t, c), lambda i: (i, 0, 0)),
        out_shape=jax.ShapeDtypeStruct((b, t, c), F32),
        scratch_shapes=[pltpu.VMEM((POOL_HIST_PAD + t, c), F32)],
        compiler_params=_cparams(("parallel",)),
        name="pool_mix",
    )(u, hist16, w_pool, pool_scale.reshape(1, c))


def _dsa_kernel(q_ref, qi_ref, kw_ref, k_ref, v_ref, ki_ref, o_ref, lo_ref, hi_ref, bias_ref,
                *, tq, extents, n_keys, q_pos0, n_sel):
    i = pl.program_id(1)
    qpos = q_pos0 + i * tq + lax.broadcasted_iota(jnp.int32, (tq, 1), 0)
    lim = jnp.minimum((qpos // CHUNK + 1) * CHUNK, n_keys)
    kf = float(n_sel)
    low = lax.broadcasted_iota(jnp.int32, (tq, LANES), 1) < HEAD_DIM
    q = q_ref[0] * HEAD_DIM ** -0.5
    qi = qi_ref[0].astype(BF16)
    wi = kw_ref[0][:, HEAD_DIM:HEAD_DIM + IDX_HEADS] * (IDX_HEADS * HEAD_DIM) ** -0.5

    def body(ext):
        kpos = lax.broadcasted_iota(jnp.int32, (tq, ext), 1)
        adm = kpos < lim
        padded = ext > n_keys
        virt = float(max(n_keys - ext, 0))

        sidx = jnp.zeros((tq, ext), F32)
        ki = ki_ref[0, :ext, :]
        for h in range(IDX_HEADS):
            sh = _dot_nt(qi[:, h * HEAD_DIM:(h + 1) * HEAD_DIM], ki)
            sidx = sidx + jnp.maximum(sh, 0.0) * wi[:, h:h + 1]
        sm = jnp.where(adm, sidx, NEG_INF)
        if padded:
            real = kpos < n_keys
            sm = jnp.where(real, sm, PAD_SCORE)

        def count_gt(x):
            cnt = jnp.sum(jnp.where(sm > x, 1.0, 0.0), axis=1, keepdims=True)
            return cnt + jnp.where(x < NEG_INF, virt, 0.0) if virt else cnt

        def bracket(lo, hi):
            above = jnp.min(jnp.where(sm > lo, sm, BIG_POS), axis=1, keepdims=True)
            below = jnp.max(jnp.where(sm <= hi, sm, PAD_SCORE), axis=1, keepdims=True)
            if virt:
                above = jnp.minimum(above, jnp.where(lo < NEG_INF, NEG_INF, BIG_POS))
                below = jnp.maximum(below, jnp.where(hi >= NEG_INF, NEG_INF, PAD_SCORE))
            return above, below

        row_max = jnp.max(sm, axis=1, keepdims=True)
        row_min = jnp.min(jnp.where(real, sm, BIG_POS) if padded else sm, axis=1, keepdims=True)
        if virt:
            row_min = jnp.minimum(row_min, NEG_INF)
        adm_min = jnp.min(jnp.where(adm, sm, BIG_POS), axis=1, keepdims=True)
        few = count_gt(row_min) < kf
        tight = count_gt(adm_min) >= kf
        lo_ref[...] = jnp.where(few, PAD_SCORE, jnp.where(tight, adm_min, row_min))
        hi_ref[...] = jnp.where(few, row_min, jnp.where(tight, row_max, adm_min))

        def unresolved(lo, hi):
            above, below = bracket(lo, hi)
            return jnp.sum(jnp.where(above < below, 1, 0))

        def cond(carry):
            rounds, open_rows = carry
            return jnp.logical_and(open_rows > 0, rounds < BISECT_ROUNDS)

        def count_gt_wide(x):
            part = jnp.where(sm[:, :LANES] > x, 1.0, 0.0)
            for j in range(1, ext // LANES):
                part = part + jnp.where(sm[:, j * LANES:(j + 1) * LANES] > x, 1.0, 0.0)
            cnt = jnp.broadcast_to(jnp.sum(part, axis=1, keepdims=True), (tq, LANES))
            return cnt + jnp.where(x < NEG_INF, virt, 0.0) if virt else cnt

        def step(carry):
            rounds, _ = carry
            lo = jnp.broadcast_to(lo_ref[...], (tq, LANES))
            hi = jnp.broadcast_to(hi_ref[...], (tq, LANES))
            for _ in range(BISECT_STEPS):
                mid = 0.5 * lo + 0.5 * hi
                under = count_gt_wide(mid) < kf
                hi = jnp.where(under, mid, hi)
                lo = jnp.where(under, lo, mid)
            lo_ref[...] = lo[:, :1]
            hi_ref[...] = hi[:, :1]
            open_rows = lax.cond(rounds + 1 >= BISECT_UNCHECKED,
                                 lambda: unresolved(lo_ref[...], hi_ref[...]), lambda: jnp.int32(1))
            return rounds + 1, open_rows

        lax.while_loop(cond, step, (jnp.int32(0), jnp.int32(1)))
        _, thr = bracket(lo_ref[...], hi_ref[...])

        gt = sm > thr
        eq = sm == thr
        need = kf - count_gt(thr)
        n_eq = jnp.sum(jnp.where(eq, 1.0, 0.0), axis=1, keepdims=True)
        crowded = jnp.sum(jnp.where(n_eq > need, 1, 0))

        @pl.when(crowded == 0)
        def _():
            keep = jnp.logical_and(jnp.logical_or(gt, eq), adm)
            bias_ref[:, :ext] = jnp.where(keep, 0.0, NEG_INF)

        @pl.when(crowded > 0)
        def _():
            rr = lax.broadcasted_iota(jnp.int32, (LANES, LANES), 0)
            cc = lax.broadcasted_iota(jnp.int32, (LANES, LANES), 1)
            prefix_ones = jnp.where(rr <= cc, 1.0, 0.0).astype(BF16)
            carry = jnp.zeros((tq, 1), F32)
            for j in range(ext // LANES):
                sl = slice(j * LANES, (j + 1) * LANES)
                eq_j = eq[:, sl]
                rank = _dot(jnp.where(eq_j, 1.0, 0.0).astype(BF16), prefix_ones) + carry
                keep = jnp.logical_or(gt[:, sl], jnp.logical_and(eq_j, rank <= need))
                bias_ref[:, sl] = jnp.where(jnp.logical_and(keep, adm[:, sl]), 0.0, NEG_INF)
                carry = rank[:, LANES - 1:LANES]

        bias = bias_ref[:, :ext]
        bias2 = jnp.concatenate([bias, bias], axis=0)

        kk = k_ref[0, :ext, :]
        vv = v_ref[0, :ext, :]
        low_k = lax.broadcasted_iota(jnp.int32, (ext, LANES), 1) < HEAD_DIM
        vsw = pltpu.roll(vv.astype(F32), HEAD_DIM, 1).astype(BF16)
        one = jnp.ones_like(vv)
        n_kv = LANES // HEAD_DIM
        for g in range(n_kv):
            v_lo = jnp.where(low_k, vv if g == 0 else vsw, one)
            v_hi = jnp.where(low_k, one, vsw if g == 0 else vv)
            stacks = []
            for odd in range(2):
                rows = []
                for m in range(2):
                    c = 2 * g + m
                    qc = q[:, c * LANES:(c + 1) * LANES]
                    if (odd == 1) != (g == 1):
                        qc = pltpu.roll(qc, HEAD_DIM, 1)
                    rows.append(jnp.where(low, qc, 0.0) if g == 0 else jnp.where(low, 0.0, qc))
                qs = jnp.concatenate(rows, axis=0).astype(BF16)
                sc = _dot_nt(qs, kk) + bias2
                p = jnp.exp(sc - jnp.max(sc, axis=1, keepdims=True))
                og = _dot(p.astype(BF16), v_hi if odd else v_lo)
                stacks.append(og / pltpu.roll(og, HEAD_DIM, 1))
            for m in range(2):
                c = 2 * g + m
                o_ref[0, :, c * LANES:(c + 1) * LANES] = jnp.where(
                    low, stacks[0][m * tq:(m + 1) * tq], stacks[1][m * tq:(m + 1) * tq])

    _for_tile_extent(i, tq, q_pos0, n_keys, extents, body)


def _dsa(q, qi, kiwi, k_all, v_all, ki_all, n_keys, q_pos0):
    b, t, dq = q.shape
    s_len = k_all.shape[1]
    tq = min(t, DSA_TQ)
    n_sel = min(DSA_TOPK, n_keys // 4)
    extents = _key_extents(t // tq, tq, q_pos0, n_keys, s_len, 4 * LANES)
    kern = functools.partial(_dsa_kernel, tq=tq, extents=extents, n_keys=n_keys, q_pos0=q_pos0, n_sel=n_sel)
    qspec = lambda w: pl.BlockSpec((1, tq, w), lambda bi, i: (bi, i, 0))
    kspec = lambda w: pl.BlockSpec((1, s_len, w), lambda bi, i: (bi, 0, 0))
    return pl.pallas_call(
        kern,
        grid=(b, t // tq),
        in_specs=[qspec(dq), qspec(qi.shape[2]), qspec(kiwi.shape[2]),
                  kspec(k_all.shape[2]), kspec(v_all.shape[2]), kspec(ki_all.shape[2])],
        out_specs=qspec(dq),
        out_shape=jax.ShapeDtypeStruct((b, t, dq), F32),
        scratch_shapes=[pltpu.VMEM((tq, 1), F32), pltpu.VMEM((tq, 1), F32), pltpu.VMEM((tq, s_len), F32)],
        compiler_params=_cparams(("parallel", "parallel")),
        name="dsa",
    )(q, qi, kiwi, k_all, v_all, ki_all)


def _sb_kernel(q_ref, k_ref, v_ref, o_ref, acc_ref, run_ref, *, tq, n_keys, q_pos0):
    i = pl.program_id(1)
    pairs = q_ref.shape[2] // LANES
    first_q = q_pos0 + i * tq
    qpos = first_q + lax.broadcasted_iota(jnp.int32, (tq, 1), 0)
    n_blocks = (jnp.minimum(first_q + tq - 1, n_keys) + LANES - 1) // LANES
    n_full = jnp.minimum(first_q, n_keys) // LANES
    low_q = lax.broadcasted_iota(jnp.int32, (tq, LANES), 1) < HEAD_DIM
    low_k = lax.broadcasted_iota(jnp.int32, (LANES, LANES), 1) < HEAD_DIM
    q = q_ref[0] * HEAD_DIM ** -0.5
    qm = []
    for p in range(pairs):
        qp = q[:, p * LANES:(p + 1) * LANES]
        qm.append((jnp.where(low_q, qp, 0.0).astype(BF16), jnp.where(low_q, 0.0, qp).astype(BF16)))
    rr = lax.broadcasted_iota(jnp.int32, (2 * LANES, 2 * LANES), 0)
    cc = lax.broadcasted_iota(jnp.int32, (2 * LANES, 2 * LANES), 1)
    rk = jnp.where(rr >= LANES, rr - LANES, rr)
    cs_rhs = jnp.where(jnp.logical_or(cc >= LANES, rk >= cc), 1.0, 0.0).astype(BF16)
    acc_ref[...] = jnp.zeros_like(acc_ref)
    run_ref[...] = jnp.zeros_like(run_ref)

    def block(j, masked):
        ks = pl.multiple_of(j * LANES, LANES)
        if masked:
            kpos = ks + lax.broadcasted_iota(jnp.int32, (1, LANES), 1)
            causal = jnp.logical_and(kpos < qpos, kpos < n_keys)
        heads = [(p, c) for p in range(pairs) for c in range(2)]
        lanes = lambda n: slice(n * LANES, (n + 1) * LANES)
        zs = [_dot_nt(qm[p][c], k_ref[0, pl.ds(ks, LANES), lanes(p)]) for p, c in heads]
        css = []
        for z in zs:
            sp = jnp.maximum(z, 0.0) + jnp.log(1.0 + jnp.exp(-jnp.abs(z)))
            if masked:
                sp = jnp.where(causal, sp, 0.0)
            hi = sp.astype(BF16)
            lo = (sp - hi.astype(F32)).astype(BF16)
            css.append(_dot(jnp.concatenate([hi, lo], axis=1), cs_rhs))
        probs = []
        for n, (z, cs) in enumerate(zip(zs, css)):
            run = run_ref[:, lanes(n)]
            a = jnp.exp(z - cs[:, :LANES] - run)
            if masked:
                a = jnp.where(causal, a, 0.0)
            run_ref[:, lanes(n)] = run + cs[:, LANES:]
            probs.append(a.astype(BF16))
        for p in range(pairs):
            vb = v_ref[0, pl.ds(ks, LANES), lanes(p)]
            zero = jnp.zeros_like(vb)
            v_cat = jnp.concatenate([jnp.where(low_k, vb, zero), jnp.where(low_k, zero, vb)], axis=0)
            acc_ref[:, lanes(p)] += _dot(jnp.concatenate(probs[2 * p:2 * p + 2], axis=1), v_cat)

    def masked_step(jj, carry):
        block(n_blocks - 1 - jj, True)
        return carry

    def full_pair(jj, carry):
        block(n_full - 1 - 2 * jj, False)
        block(n_full - 2 - 2 * jj, False)
        return carry

    lax.fori_loop(0, n_blocks - n_full, masked_step, 0)
    lax.fori_loop(0, n_full // 2, full_pair, 0)

    @pl.when(n_full % 2 == 1)
    def _():
        block(jnp.int32(0), False)

    o_ref[0] = acc_ref[...]


def _sb_attend(q, k_all, v_all, n_keys, q_pos0):
    b, t, d = q.shape
    s_len = k_all.shape[1]
    tq = min(t, ATTN_TQ)
    kern = functools.partial(_sb_kernel, tq=tq, n_keys=n_keys, q_pos0=q_pos0)
    return pl.pallas_call(
        kern,
        grid=(b, t // tq),
        in_specs=[pl.BlockSpec((1, tq, d), lambda bi, i: (bi, i, 0)),
                  pl.BlockSpec((1, s_len, d), lambda bi, i: (bi, 0, 0)),
                  pl.BlockSpec((1, s_len, d), lambda bi, i: (bi, 0, 0))],
        out_specs=pl.BlockSpec((1, tq, d), lambda bi, i: (bi, i, 0)),
        out_shape=jax.ShapeDtypeStruct((b, t, d), F32),
        scratch_shapes=[pltpu.VMEM((tq, d), F32), pltpu.VMEM((tq, 2 * d), F32)],
        compiler_params=_cparams(("parallel", "parallel")),
        name="sb_attend",
    )(q, k_all, v_all)


def _key_extents(nq, tq, q_pos0, n_keys, s_len, step):
    need = [min(n_keys, ((q_pos0 + (i + 1) * tq - 1) // CHUNK + 1) * CHUNK) for i in range(nq)]
    return tuple(sorted({min(s_len, -(-n // step) * step) for n in need}))


def _for_tile_extent(i, tq, q_pos0, n_keys, extents, body):
    need = jnp.minimum(n_keys, ((q_pos0 + (i + 1) * tq - 1) // CHUNK + 1) * CHUNK)
    prev = 0
    for ext in extents:
        pl.when(jnp.logical_and(need > prev, need <= ext))(functools.partial(body, ext))
        prev = ext


def _diff_kernel(q_ref, k_ref, v_ref, lam_ref, gain_ref, o_ref, *, tq, extents, n_keys, q_pos0, lam_init):
    i = pl.program_id(2)
    lp = lam_ref[...]
    lam = (jnp.exp(jnp.sum(lp[0:1] * lp[1:2], axis=1, keepdims=True))
           - jnp.exp(jnp.sum(lp[2:3] * lp[3:4], axis=1, keepdims=True)) + lam_init)
    qpos = q_pos0 + i * tq + lax.broadcasted_iota(jnp.int32, (tq, 1), 0)
    lim = jnp.minimum((qpos // CHUNK + 1) * CHUNK, n_keys)
    low = lax.broadcasted_iota(jnp.int32, (tq, LANES), 1) < HEAD_DIM

    def body(ext):
        mask = lax.broadcasted_iota(jnp.int32, (tq, ext), 1) < lim
        for h in range(q_ref.shape[2] // LANES):
            hs = slice(h * LANES, (h + 1) * LANES)
            q = q_ref[0, :, hs] * HEAD_DIM ** -0.5
            kk = k_ref[0, :ext, hs]
            probs = []
            for c in range(2):
                qc = (jnp.where(low, q, 0.0) if c == 0 else jnp.where(low, 0.0, q)).astype(BF16)
                sc = jnp.where(mask, _dot_nt(qc, kk), NEG_INF)
                p = jnp.exp(sc - jnp.max(sc, axis=1, keepdims=True))
                probs.append(p * (1.0 / jnp.sum(p, axis=1, keepdims=True)))
            a = probs[0] - lam * probs[1]
            o = _dot(a.astype(BF16), v_ref[0, :ext, hs])
            o = o * lax.rsqrt(jnp.mean(o * o, axis=-1, keepdims=True) + NORM_EPS)
            o_ref[0, :, hs] = o * gain_ref[...] * (1.0 - lam_init)

    _for_tile_extent(i, tq, q_pos0, n_keys, extents, body)


def _diff_attend(q, k_all, v_all, diff_lambda, gain, n_keys, q_pos0, lam_init):
    b, t, d = q.shape
    s_len = k_all.shape[1]
    tq = min(t, ATTN_TQ)
    extents = _key_extents(t // tq, tq, q_pos0, n_keys, s_len, 2 * LANES)
    kern = functools.partial(_diff_kernel, tq=tq, extents=extents, n_keys=n_keys, q_pos0=q_pos0, lam_init=lam_init)
    return pl.pallas_call(
        kern,
        grid=(b, d // DIFF_LANES, t // tq),
        in_specs=[pl.BlockSpec((1, tq, DIFF_LANES), lambda bi, h, i: (bi, i, h)),
                  pl.BlockSpec((1, s_len, DIFF_LANES), lambda bi, h, i: (bi, 0, h)),
                  pl.BlockSpec((1, s_len, DIFF_LANES), lambda bi, h, i: (bi, 0, h)),
                  pl.BlockSpec(diff_lambda.shape, lambda bi, h, i: (0, 0)),
                  pl.BlockSpec((1, LANES), lambda bi, h, i: (0, 0))],
        out_specs=pl.BlockSpec((1, tq, DIFF_LANES), lambda bi, h, i: (bi, i, h)),
        out_shape=jax.ShapeDtypeStruct((b, t, d), F32),
        compiler_params=_cparams(("parallel", "parallel", "parallel")),
        name="diff_attend",
    )(q, k_all, v_all, diff_lambda, gain.reshape(1, LANES))


def _route(logits):
    lane = lax.broadcasted_iota(jnp.int32, logits.shape, 1)
    is_group = lane < MOE_GROUPS
    gl = jnp.where(is_group, logits, NEG_INF)
    g_max = jnp.max(gl, axis=1, keepdims=True)
    g_sel = jnp.min(jnp.where(gl == g_max, lane, LANES), axis=1, keepdims=True)
    g_gate = 1.0 / jnp.sum(jnp.where(is_group, jnp.exp(gl - g_max), 0.0), axis=1, keepdims=True)
    in_group = jnp.logical_and(lane >= GATE_COL0, (lane - GATE_COL0) // MOE_EPG == g_sel)
    in_group = jnp.logical_and(in_group, lane < GATE_COL0 + MOE_EXPERTS)
    el = jnp.where(in_group, logits, NEG_INF)
    top1 = jnp.max(el, axis=1, keepdims=True)
    i1 = jnp.min(jnp.where(jnp.logical_and(in_group, el == top1), lane, LANES), axis=1, keepdims=True)
    rest = jnp.logical_and(in_group, lane != i1)
    el2 = jnp.where(rest, logits, NEG_INF)
    top2 = jnp.max(el2, axis=1, keepdims=True)
    i2 = jnp.min(jnp.where(jnp.logical_and(rest, el2 == top2), lane, LANES), axis=1, keepdims=True)
    e2 = jnp.exp(top2 - top1)
    w1 = g_gate / (1.0 + e2)
    gates = jnp.where(lane == i1, w1, jnp.where(lane == i2, w1 * e2, 0.0))
    return jnp.where(lane == GROUP_LANE, g_sel.astype(F32), gates)


def _out_kernel(h_ref, a_ref, b_ref, wa_ref, wb_ref, g_ref, wr_ref, br_ref, o_ref, xn_ref, gate_ref):
    h = (h_ref[...] + _dot(a_ref[...].astype(BF16), wa_ref[...])
         + _dot(b_ref[...].astype(BF16), wb_ref[...]))
    o_ref[...] = h
    xn = _rms(h, g_ref[...]).astype(BF16)
    xn_ref[...] = xn
    gate_ref[...] = _route(_dot(xn, wr_ref[...]) + br_ref[...])


def _out_proj_route(h, a, bmix, w_out, g_ffn, wr, br, tm):
    n, d = h.shape
    ca = a.shape[1]
    wa, wb = w_out[:ca], w_out[ca:]
    row = lambda w: pl.BlockSpec((tm, w), lambda i: (i, 0))
    whole = lambda x: pl.BlockSpec(x.shape, lambda i: (0, 0))
    return pl.pallas_call(
        _out_kernel,
        grid=(n // tm,),
        in_specs=[row(d), row(ca), row(bmix.shape[1]), whole(wa), whole(wb),
                  pl.BlockSpec((1, d), lambda i: (0, 0)), whole(wr), whole(br)],
        out_specs=[row(d), row(d), row(LANES)],
        out_shape=[jax.ShapeDtypeStruct((n, d), F32), jax.ShapeDtypeStruct((n, d), BF16),
                   jax.ShapeDtypeStruct((n, LANES), F32)],
        compiler_params=_cparams(("parallel",)),
        name="out_proj_route",
    )(h, a, bmix, wa, wb, g_ffn.reshape(1, d), wr, br)


def _split3(x):
    hi = x.astype(BF16)
    r1 = x - hi.astype(F32)
    mid = r1.astype(BF16)
    lo = (r1 - mid.astype(F32)).astype(BF16)
    return hi, mid, lo


def _expert_kernel(xn_ref, gate_ref, w1_ref, w3_ref, w2_ref, h_ref, gf_ref, o_ref,
                   acc_ref, xs_ref, gs_ref, pt_ref, tri_ref, seg_ref, *, final_norm, win):
    i = pl.program_id(0)
    e = pl.program_id(1)
    tm = xn_ref.shape[0]

    @pl.when(jnp.logical_and(i == 0, e == 0))
    def _():
        rr = lax.broadcasted_iota(jnp.int32, (tm, tm), 0)
        cc = lax.broadcasted_iota(jnp.int32, (tm, tm), 1)
        tri_ref[...] = jnp.where(cc < rr, 1.0, 0.0).astype(BF16)

    @pl.when(e == 0)
    def _():
        gate = gate_ref[...]
        lane = lax.broadcasted_iota(jnp.int32, gate.shape, 1)
        lane_row = lax.broadcasted_iota(jnp.int32, (1, LANES), 1)
        in_grp = jnp.logical_and(lane < MOE_GROUPS, lane.astype(F32) == gate[:, GROUP_LANE:GROUP_LANE + 1])
        onehot = jnp.where(in_grp, 1.0, 0.0)
        before = _dot(tri_ref[...], onehot.astype(BF16))
        count = jnp.sum(onehot, axis=0, keepdims=True)
        first = jnp.zeros((1, LANES), F32)
        start = jnp.float32(0.0)
        for g in range(MOE_GROUPS):
            n_g = jnp.sum(jnp.where(lane_row == g, count, 0.0))
            seg_ref[g] = start.astype(jnp.int32)
            seg_ref[MOE_GROUPS + g] = n_g.astype(jnp.int32)
            first = jnp.where(lane_row == g, start, first)
            start = start + n_g
        dest = jnp.sum(onehot * (before + first), axis=1, keepdims=True).astype(jnp.int32)
        g_hi, g_mid, g_lo = _split3(gate)
        sr = min(SORT_ROWS, tm)
        for r in range(0, tm, sr):
            rs = slice(r, r + sr)
            pt_ref[rs, :] = jnp.where(lax.broadcasted_iota(jnp.int32, (sr, tm), 1) == dest[rs],
                                      1.0, 0.0).astype(BF16)
        for r in range(0, tm, sr):
            rs = slice(r, r + sr)
            pt_cols = pt_ref[:, rs]
            xs_ref[rs, :] = _dot_tn(pt_cols, xn_ref[...]).astype(BF16)
            gs_ref[rs, :] = _dot_tn(pt_cols, g_hi) + _dot_tn(pt_cols, g_mid) + _dot_tn(pt_cols, g_lo)
        xs_ref[tm:, :] = jnp.zeros((win, xs_ref.shape[1]), BF16)
        gs_ref[tm:, :] = jnp.zeros((win, LANES), F32)
        acc_ref[...] = jnp.zeros_like(acc_ref)

    grp = e // MOE_EPG
    seg_first = seg_ref[grp]
    seg_rows = seg_ref[MOE_GROUPS + grp]
    row0 = (seg_first // BF16_ROWS) * BF16_ROWS
    n_win = (seg_first + seg_rows - row0 + win - 1) // win

    def window(w, carry):
        rows = pl.ds(pl.multiple_of(row0 + w * win, BF16_ROWS), win)
        x = xs_ref[rows, :]
        gsw = gs_ref[rows, :]
        lane = lax.broadcasted_iota(jnp.int32, gsw.shape, 1)
        ge = jnp.sum(jnp.where(lane == GATE_COL0 + e, gsw, 0.0), axis=1, keepdims=True)
        a = _dot(x, w1_ref[0])
        b = _dot(x, w3_ref[0])
        act = (a * (1.0 / (1.0 + jnp.exp(-a))) * b).astype(BF16)
        acc_ref[rows, :] += ge * _dot(act, w2_ref[0])
        return carry

    lax.fori_loop(0, n_win, window, 0)

    @pl.when(e == pl.num_programs(1) - 1)
    def _():
        a_hi, a_mid, a_lo = _split3(acc_ref[0:tm, :])
        sr = min(SORT_ROWS, tm)
        for r in range(0, tm, sr):
            rs = slice(r, r + sr)
            pt = pt_ref[rs, :]
            y = h_ref[rs, :] + (_dot(pt, a_hi) + _dot(pt, a_mid) + _dot(pt, a_lo))
            o_ref[rs, :] = _rms(y, gf_ref[...]) if final_norm else y


def _moe(h, xn, gate, w1, w3, w2, g_final, final_norm, tm):
    n, d = h.shape
    n_e = w1.shape[0]
    tme = MOE_TM if n % MOE_TM == 0 else tm
    win = -(-(tme * 5 // 16) // BF16_ROWS) * BF16_ROWS
    row2 = lambda w: pl.BlockSpec((tme, w), lambda i, e: (i, 0))
    kern = functools.partial(_expert_kernel, final_norm=final_norm, win=win)
    return pl.pallas_call(
        kern,
        grid=(n // tme, n_e),
        in_specs=[row2(d), row2(LANES),
                  pl.BlockSpec((1,) + w1.shape[1:], lambda i, e: (e, 0, 0)),
                  pl.BlockSpec((1,) + w3.shape[1:], lambda i, e: (e, 0, 0)),
                  pl.BlockSpec((1,) + w2.shape[1:], lambda i, e: (e, 0, 0)),
                  row2(d), pl.BlockSpec((1, d), lambda i, e: (0, 0))],
        out_specs=pl.BlockSpec((tme, d), lambda i, e: (i, 0), pipeline_mode=pl.Buffered(1)),
        out_shape=jax.ShapeDtypeStruct((n, d), F32),
        scratch_shapes=[pltpu.VMEM((tme + win, d), F32), pltpu.VMEM((tme + win, d), BF16),
                        pltpu.VMEM((tme + win, LANES), F32), pltpu.VMEM((tme, tme), BF16),
                        pltpu.VMEM((tme, tme), BF16), pltpu.SMEM((2 * MOE_GROUPS,), jnp.int32)],
        compiler_params=_cparams(("arbitrary", "arbitrary")),
        name="moe_experts",
    )(xn, gate, w1, w3, w2, h, g_final.reshape(1, d))


EVEN_SEGS = ((0, 512, None, False), (512, 512, "full", False), (1024, 128, "full", True), (1152, 128, None, True),
             (1280, 256, "full", False), (1536, 128, "half", False))
ODD_SEGS = ((0, 512, None, False), (512, 512, None, True), (1024, 512, None, True),
            (1536, 512, "full", False), (2048, 512, "full", True), (2560, 512, None, True))


def _cat_keys(hist, new):
    allk = jnp.concatenate([hist, new], axis=1) if hist is not None else new
    n_keys = allk.shape[1]
    pad = -n_keys % LANES
    if pad:
        allk = jnp.pad(allk, ((0, 0), (0, pad), (0, 0)))
    return allk.astype(BF16), n_keys


def kernel(x_prompt, x_sample, cache_pool, cache_dsa_k, cache_dsa_v, cache_idx_k, cache_sb_k, cache_sb_v,
           cache_diff_k, cache_diff_v, norm_mix, norm_ffn, norm_final, w_in_even, w_pool, pool_scale,
           w_out_even, w_in_odd, diff_lambda, diff_subln, w_out_odd, moe_w_group, moe_b_group,
           moe_w_expert, moe_b_expert, moe_w1, moe_w3, moe_w2):
    b, t, d = x_prompt.shape
    bd, td, _ = x_sample.shape
    past = cache_dsa_k.shape[2]
    depth = norm_mix.shape[0]
    groups = ((b, t, 0, min(512, b * t)), (bd, td, past, bd * td))

    tabs = []
    for (gb, gt, p0, tm) in groups:
        tab = _rope_tables(p0 + jnp.arange(gt, dtype=jnp.int32))
        if tm > gt:
            tab = jnp.tile(tab, (tm // gt, 1))
        tabs.append(tab)

    hs = [x_prompt.reshape(b * t, d), x_sample.reshape(bd * td, d)]
    outs = [dict(), dict()]
    for l in range(depth):
        li = l // 2
        last = l == depth - 1
        if l % 2 == 0:
            n_in = w_in_even.shape[2]
            w_in = jnp.pad(w_in_even[li], ((0, 0), (0, -n_in % LANES))).astype(BF16)
            w_out = w_out_even[li].astype(BF16)
            wp = w_pool[li].astype(BF16)
        else:
            w_in = w_in_odd[li].astype(BF16)
            w_out = w_out_odd[li].astype(BF16)
            lam_init = 0.8 - 0.6 * math.exp(-0.3 * l)
        wr = jnp.concatenate([moe_w_group[l]] + [moe_w_expert[l, g] for g in range(MOE_GROUPS)], axis=1)
        wr = jnp.pad(wr, ((0, 0), (0, LANES - wr.shape[1]))).astype(BF16)
        br = jnp.concatenate([moe_b_group[l], moe_b_expert[l].reshape(-1)])
        br = jnp.pad(br, (0, LANES - br.shape[0])).reshape(1, LANES).astype(F32)
        w1, w3, w2 = moe_w1[l].astype(BF16), moe_w3[l].astype(BF16), moe_w2[l].astype(BF16)

        for gi, (gb, gt, p0, tm) in enumerate(groups):
            h = hs[gi]
            o = outs[gi]
            sample = gi == 1
            r3 = lambda x: x.reshape(gb, gt, x.shape[-1])
            if l % 2 == 0:
                u, q, k, v, qi, kiwi, k16, v16 = [
                    r3(x) for x in _project(h, norm_mix[l], w_in, tabs[gi], EVEN_SEGS, tm)]
                ki = kiwi[..., :HEAD_DIM]
                hist = cache_pool[li] if sample else jnp.zeros((gb, POOL_HIST, u.shape[2]), F32)
                a_out = _pool_mix(u, hist, wp, pool_scale[li], p0)
                if sample:
                    k_all, n_keys = _cat_keys(cache_dsa_k[li].reshape(gb, past, -1), k)
                    v_all, _ = _cat_keys(cache_dsa_v[li].reshape(gb, past, -1), v)
                    ki_all, _ = _cat_keys(cache_idx_k[li], ki)
                else:
                    (k_all, n_keys), (v_all, _), (ki_all, _) = _cat_keys(None, k16), _cat_keys(None, v16), _cat_keys(None, ki)
                b_out = _dsa(q, qi, kiwi, k_all, v_all, ki_all, n_keys, p0)
                o.setdefault("pool", []).append(jnp.concatenate([hist, u], axis=1)[:, -POOL_HIST:])
                o.setdefault("dsa_k", []).append(k.reshape(gb, gt, -1, HEAD_DIM))
                o.setdefault("dsa_v", []).append(v.reshape(gb, gt, -1, HEAD_DIM))
                o.setdefault("idx_k", []).append(ki)
                mix_a, mix_b = a_out, b_out
            else:
                sq, sk, sv, dq, dk, dv, sk16, sv16, dk16, dv16 = [
                    r3(x) for x in _project(h, norm_mix[l], w_in, tabs[gi], ODD_SEGS, tm)]
                if sample:
                    sk_all, n_keys = _cat_keys(cache_sb_k[li].reshape(gb, past, -1), sk)
                    sv_all, _ = _cat_keys(cache_sb_v[li].reshape(gb, past, -1), sv)
                    dk_all, _ = _cat_keys(cache_diff_k[li].reshape(gb, past, -1), dk)
                    dv_all, _ = _cat_keys(cache_diff_v[li].reshape(gb, past, -1), dv)
                else:
                    (sk_all, n_keys), (sv_all, _) = _cat_keys(None, sk16), _cat_keys(None, sv16)
                    (dk_all, _), (dv_all, _) = _cat_keys(None, dk16), _cat_keys(None, dv16)
                c_out = _sb_attend(sq, sk_all, sv_all, n_keys, p0)
                d_out = _diff_attend(dq, dk_all, dv_all, diff_lambda[li], diff_subln[li], n_keys, p0, lam_init)
                n_sb = sk.shape[2] // HEAD_DIM
                n_df = dk.shape[2] // (2 * HEAD_DIM)
                o.setdefault("sb_k", []).append(sk.reshape(gb, gt, n_sb, HEAD_DIM))
                o.setdefault("sb_v", []).append(sv.reshape(gb, gt, n_sb, HEAD_DIM))
                o.setdefault("diff_k", []).append(dk.reshape(gb, gt, n_df, 2, HEAD_DIM))
                o.setdefault("diff_v", []).append(dv.reshape(gb, gt, n_df, 2 * HEAD_DIM))
                mix_a, mix_b = c_out, d_out
            h, xn, gate = _out_proj_route(h, mix_a.reshape(gb * gt, -1), mix_b.reshape(gb * gt, -1), w_out,
                                          norm_ffn[l], wr, br, tm)
            hs[gi] = _moe(h, xn, gate, w1, w3, w2, norm_final, last, tm)

    names = ("pool", "dsa_k", "dsa_v", "idx_k", "sb_k", "sb_v", "diff_k", "diff_v")
    res = [hs[0].reshape(b, t, d), hs[1].reshape(bd, td, d)]
    for o in outs:
        res += [jnp.stack(o[nm]) for nm in names]
    return tuple(res)
```

```python
import functools
import math

import jax
import jax.numpy as jnp
import numpy as np
from jax import lax
from jax.experimental import pallas as pl
from jax.experimental.pallas import tpu as pltpu

F32 = jnp.float32
BF16 = jnp.bfloat16

LANES = 128
HEAD_DIM = 64
CHUNK = 64
ROPE_THETA = 10000.0
NORM_EPS = 1e-6
NEG_INF = -1e30
PAD_SCORE = -3e38
BIG_POS = 3e38
POOL_WINDOWS = (2, 4, 8, 16)
POOL_HIST = 15
POOL_HIST_PAD = 16
DSA_TOPK = 256
IDX_HEADS = 4
MOE_GROUPS = 4
MOE_EPG = 4
MOE_EXPERTS = 16
GATE_COL0 = MOE_GROUPS
GROUP_LANE = 0
BF16_ROWS = 16
SORT_ROWS = 256
VMEM_LIMIT = 56 * 1024 * 1024
BISECT_STEPS = 8
BISECT_ROUNDS = 48
BISECT_UNCHECKED = 2
ATTN_TQ = 256
DSA_TQ = 128
MOE_TM = 1024
DIFF_LANES = 2 * LANES


def _cparams(sem):
    return pltpu.CompilerParams(dimension_semantics=sem, vmem_limit_bytes=VMEM_LIMIT)


def _dot(a, b):
    return jnp.dot(a, b, preferred_element_type=F32)


def _dot_nt(a, b):
    return lax.dot_general(a, b, (((1,), (1,)), ((), ())), preferred_element_type=F32)


def _dot_tn(a, b):
    return lax.dot_general(a, b, (((0,), (0,)), ((), ())), preferred_element_type=F32)


def _rms(x, g):
    ms = jnp.mean(x * x, axis=-1, keepdims=True)
    return x * lax.rsqrt(ms + NORM_EPS) * g


def _proj_kernel(x_ref, g_ref, w_ref, tab_ref, *out_refs, segs):
    xn = _rms(x_ref[...], g_ref[...]).astype(BF16)
    copies = iter(out_refs[len(segs):])
    for o_ref, (c0, width, mode, twin) in zip(out_refs, segs):
        t_ref = next(copies) if twin else None
        y = _dot(xn, w_ref[:, c0:c0 + width])
        if mode is None:
            o_ref[...] = y
            if twin:
                t_ref[...] = y.astype(BF16)
            continue
        t0 = 0 if mode == "full" else 3 * LANES
        cos = tab_ref[:, t0:t0 + LANES]
        sin_a = tab_ref[:, t0 + LANES:t0 + 2 * LANES]
        sin_b = tab_ref[:, t0 + 2 * LANES:t0 + 3 * LANES]
        for c in range(0, width, LANES):
            yc = y[:, c:c + LANES]
            yr = (yc * cos + pltpu.roll(yc, LANES - HEAD_DIM // 2, 1) * sin_a
                  + pltpu.roll(yc, HEAD_DIM // 2, 1) * sin_b)
            o_ref[:, c:c + LANES] = yr
            if twin:
                t_ref[:, c:c + LANES] = yr.astype(BF16)


def _rope_tables(pos):
    half = HEAD_DIM // 2
    inv = ROPE_THETA ** (-jnp.arange(half, dtype=F32) / half)
    ang = pos.astype(F32)[:, None] * inv[None, :]
    cos, sin = jnp.cos(ang), jnp.sin(ang)
    zero, one = jnp.zeros_like(sin), jnp.ones_like(cos)
    cos_h = jnp.concatenate([cos, cos], axis=1)
    sa_h = jnp.concatenate([-sin, zero], axis=1)
    sb_h = jnp.concatenate([zero, sin], axis=1)
    one_h = jnp.concatenate([one, one], axis=1)
    zero_h = jnp.concatenate([zero, zero], axis=1)
    return jnp.concatenate([cos_h, cos_h, sa_h, sa_h, sb_h, sb_h,
                            cos_h, one_h, sa_h, zero_h, sb_h, zero_h], axis=1)


def _project(x, g, w, tab, segs, tm):
    n, d = x.shape
    tt = tab.shape[0]
    nt = tt // tm
    kern = functools.partial(_proj_kernel, segs=segs)
    return pl.pallas_call(
        kern,
        grid=(n // tm,),
        in_specs=[pl.BlockSpec((tm, d), lambda i: (i, 0)),
                  pl.BlockSpec((1, d), lambda i: (0, 0)),
                  pl.BlockSpec(w.shape, lambda i: (0, 0)),
                  pl.BlockSpec((tm, tab.shape[1]), lambda i: (i % nt, 0))],
        out_specs=[pl.BlockSpec((tm, wd), lambda i: (i, 0)) for _, wd, _, _ in segs]
        + [pl.BlockSpec((tm, wd), lambda i: (i, 0)) for _, wd, _, twin in segs if twin],
        out_shape=[jax.ShapeDtypeStruct((n, wd), F32) for _, wd, _, _ in segs]
        + [jax.ShapeDtypeStruct((n, wd), BF16) for _, wd, _, twin in segs if twin],
        compiler_params=_cparams(("parallel",)),
        name="proj",
    )(x, g.reshape(1, d), w, tab)


def _pool_kernel(u_ref, h_ref, w_ref, s_ref, o_ref, ext_ref, *, t, pos0, rc):
    ext_ref[0:POOL_HIST_PAD, :] = h_ref[0]
    ext_ref[POOL_HIST_PAD:POOL_HIST_PAD + t, :] = u_ref[0]
    for r0 in range(0, t, rc):
        pos = pos0 + r0 + lax.broadcasted_iota(jnp.int32, (rc, 1), 0)
        for g, win in enumerate(POOL_WINDOWS):
            c0 = g * LANES
            u_new = ext_ref[POOL_HIST_PAD + r0:POOL_HIST_PAD + r0 + rc, c0:c0 + LANES]
            s = u_new
            for k in range(1, win):
                s = s + ext_ref[POOL_HIST_PAD + r0 - k:POOL_HIST_PAD + r0 - k + rc, c0:c0 + LANES]
            cnt = jnp.minimum(pos + 1, win).astype(F32)
            dlt = (s / cnt - u_new).astype(BF16)
            o_ref[0, r0:r0 + rc, c0:c0 + LANES] = _dot(dlt, w_ref[g]) * s_ref[:, c0:c0 + LANES]


def _pool_mix(u, hist, w_pool, pool_scale, pos0):
    b, t, c = u.shape
    rc = min(t, 256)
    hist16 = jnp.pad(hist, ((0, 0), (POOL_HIST_PAD - POOL_HIST, 0), (0, 0)))
    kern = functools.partial(_pool_kernel, t=t, pos0=pos0, rc=rc)
    return pl.pallas_call(
        kern,
        grid=(b,),
        in_specs=[pl.BlockSpec((1, t, c), lambda i: (i, 0, 0)),
                  pl.BlockSpec((1, POOL_HIST_PAD, c), lambda i: (i, 0, 0)),
                  pl.BlockSpec(w_pool.shape, lambda i: (0, 0, 0)),
                  pl.BlockSpec((1, c), lambda i: (0, 0))],
        out_specs=pl.BlockSpec((1, t, c), lambda i: (i, 0, 0)),
        out_shape=jax.ShapeDtypeStruct((b, t, c), F32),
        scratch_shapes=[pltpu.VMEM((POOL_HIST_PAD + t, c), F32)],
        compiler_params=_cparams(("parallel",)),
        name="pool_mix",
    )(u, hist16, w_pool, pool_scale.reshape(1, c))


def _dsa_kernel(q_ref, qi_ref, kw_ref, k_ref, v_ref, ki_ref, o_ref, lo_ref, hi_ref, bias_ref,
                *, tq, extents, n_keys, q_pos0, n_sel):
    i = pl.program_id(1)
    qpos = q_pos0 + i * tq + lax.broadcasted_iota(jnp.int32, (tq, 1), 0)
    lim = jnp.minimum((qpos // CHUNK + 1) * CHUNK, n_keys)
    kf = float(n_sel)
    low = lax.broadcasted_iota(jnp.int32, (tq, LANES), 1) < HEAD_DIM
    q = q_ref[0] * HEAD_DIM ** -0.5
    qi = qi_ref[0].astype(BF16)
    wi = kw_ref[0][:, HEAD_DIM:HEAD_DIM + IDX_HEADS] * (IDX_HEADS * HEAD_DIM) ** -0.5

    def body(ext):
        kpos = lax.broadcasted_iota(jnp.int32, (tq, ext), 1)
        adm = kpos < lim
        padded = ext > n_keys
        virt = float(max(n_keys - ext, 0))

        sidx = jnp.zeros((tq, ext), F32)
        ki = ki_ref[0, :ext, :]
        for h in range(IDX_HEADS):
            sh = _dot_nt(qi[:, h * HEAD_DIM:(h + 1) * HEAD_DIM], ki)
            sidx = sidx + jnp.maximum(sh, 0.0) * wi[:, h:h + 1]
        sm = jnp.where(adm, sidx, NEG_INF)
        if padded:
            real = kpos < n_keys
            sm = jnp.where(real, sm, PAD_SCORE)

        def count_gt_wide(x):
            part = jnp.where(sm[:, :LANES] > x, 1.0, 0.0)
            for j in range(1, ext // LANES):
                part = part + jnp.where(sm[:, j * LANES:(j + 1) * LANES] > x, 1.0, 0.0)
            cnt = jnp.broadcast_to(jnp.sum(part, axis=1, keepdims=True), (tq, LANES))
            return cnt + jnp.where(x < NEG_INF, virt, 0.0) if virt else cnt

        def count_gt(x):
            return count_gt_wide(jnp.broadcast_to(x, (tq, LANES)))[:, :1]

        def bracket(lo, hi):
            above = jnp.min(jnp.where(sm > lo, sm, BIG_POS), axis=1, keepdims=True)
            below = jnp.max(jnp.where(sm <= hi, sm, PAD_SCORE), axis=1, keepdims=True)
            if virt:
                above = jnp.minimum(above, jnp.where(lo < NEG_INF, NEG_INF, BIG_POS))
                below = jnp.maximum(below, jnp.where(hi >= NEG_INF, NEG_INF, PAD_SCORE))
            return above, below

        row_max = jnp.max(sm, axis=1, keepdims=True)
        row_min = jnp.min(jnp.where(real, sm, BIG_POS) if padded else sm, axis=1, keepdims=True)
        if virt:
            row_min = jnp.minimum(row_min, NEG_INF)
        adm_min = jnp.min(jnp.where(adm, sm, BIG_POS), axis=1, keepdims=True)
        few = count_gt(row_min) < kf
        tight = count_gt(adm_min) >= kf
        lo_ref[...] = jnp.where(few, PAD_SCORE, jnp.where(tight, adm_min, row_min))
        hi_ref[...] = jnp.where(few, row_min, jnp.where(tight, row_max, adm_min))

        def unresolved(lo, hi):
            above, below = bracket(lo, hi)
            return jnp.sum(jnp.where(above < below, 1, 0))

        def cond(carry):
            rounds, open_rows = carry
            return jnp.logical_and(open_rows > 0, rounds < BISECT_ROUNDS)

        def step(carry):
            rounds, _ = carry
            lo = jnp.broadcast_to(lo_ref[...], (tq, LANES))
            hi = jnp.broadcast_to(hi_ref[...], (tq, LANES))
            for _ in range(BISECT_STEPS):
                mid = 0.5 * lo + 0.5 * hi
                under = count_gt_wide(mid) < kf
                hi = jnp.where(under, mid, hi)
                lo = jnp.where(under, lo, mid)
            lo_ref[...] = lo[:, :1]
            hi_ref[...] = hi[:, :1]
            open_rows = lax.cond(rounds + 1 >= BISECT_UNCHECKED,
                                 lambda: unresolved(lo_ref[...], hi_ref[...]), lambda: jnp.int32(1))
            return rounds + 1, open_rows

        lax.while_loop(cond, step, (jnp.int32(0), jnp.int32(1)))
        _, thr = bracket(lo_ref[...], hi_ref[...])

        gt = sm > thr
        eq = sm == thr
        need = kf - count_gt(thr)
        n_eq = jnp.sum(jnp.where(eq, 1.0, 0.0), axis=1, keepdims=True)
        crowded = jnp.sum(jnp.where(n_eq > need, 1, 0))

        @pl.when(crowded == 0)
        def _():
            keep = jnp.logical_and(jnp.logical_or(gt, eq), adm)
            bias_ref[:, :ext] = jnp.where(keep, 0.0, NEG_INF)

        @pl.when(crowded > 0)
        def _():
            rr = lax.broadcasted_iota(jnp.int32, (LANES, LANES), 0)
            cc = lax.broadcasted_iota(jnp.int32, (LANES, LANES), 1)
            prefix_ones = jnp.where(rr <= cc, 1.0, 0.0).astype(BF16)
            carry = jnp.zeros((tq, 1), F32)
            for j in range(ext // LANES):
                sl = slice(j * LANES, (j + 1) * LANES)
                eq_j = eq[:, sl]
                rank = _dot(jnp.where(eq_j, 1.0, 0.0).astype(BF16), prefix_ones) + carry
                keep = jnp.logical_or(gt[:, sl], jnp.logical_and(eq_j, rank <= need))
                bias_ref[:, sl] = jnp.where(jnp.logical_and(keep, adm[:, sl]), 0.0, NEG_INF)
                carry = rank[:, LANES - 1:LANES]

        bias = bias_ref[:, :ext]
        bias2 = jnp.concatenate([bias, bias], axis=0)

        kk = k_ref[0, :ext, :]
        vv = v_ref[0, :ext, :]
        low_k = lax.broadcasted_iota(jnp.int32, (ext, LANES), 1) < HEAD_DIM
        vsw = pltpu.roll(vv.astype(F32), HEAD_DIM, 1).astype(BF16)
        one = jnp.ones_like(vv)
        n_kv = LANES // HEAD_DIM
        stacks = [(g, odd) for g in range(n_kv) for odd in range(2)]
        scores = []
        for g, odd in stacks:
            rows = []
            for m in range(2):
                c = 2 * g + m
                qc = q[:, c * LANES:(c + 1) * LANES]
                if (odd == 1) != (g == 1):
                    qc = pltpu.roll(qc, HEAD_DIM, 1)
                rows.append(jnp.where(low, qc, 0.0) if g == 0 else jnp.where(low, 0.0, qc))
            qs = jnp.concatenate(rows, axis=0).astype(BF16)
            scores.append(_dot_nt(qs, kk) + bias2)
        probs = [jnp.exp(sc - jnp.max(sc, axis=1, keepdims=True)).astype(BF16) for sc in scores]
        outs = []
        for (g, odd), p in zip(stacks, probs):
            if odd:
                v_aug = jnp.where(low_k, one, vsw if g == 0 else vv)
            else:
                v_aug = jnp.where(low_k, vv if g == 0 else vsw, one)
            og = _dot(p, v_aug)
            outs.append(og / pltpu.roll(og, HEAD_DIM, 1))
        for g in range(n_kv):
            for m in range(2):
                c = 2 * g + m
                o_ref[0, :, c * LANES:(c + 1) * LANES] = jnp.where(
                    low, outs[2 * g][m * tq:(m + 1) * tq], outs[2 * g + 1][m * tq:(m + 1) * tq])

    _for_tile_extent(i, tq, q_pos0, n_keys, extents, body)


def _dsa(q, qi, kiwi, k_all, v_all, ki_all, n_keys, q_pos0):
    b, t, dq = q.shape
    s_len = k_all.shape[1]
    tq = min(t, DSA_TQ)
    n_sel = min(DSA_TOPK, n_keys // 4)
    extents = _key_extents(t // tq, tq, q_pos0, n_keys, s_len, 4 * LANES)
    kern = functools.partial(_dsa_kernel, tq=tq, extents=extents, n_keys=n_keys, q_pos0=q_pos0, n_sel=n_sel)
    qspec = lambda w: pl.BlockSpec((1, tq, w), lambda bi, i: (bi, i, 0))
    kspec = lambda w: pl.BlockSpec((1, s_len, w), lambda bi, i: (bi, 0, 0))
    return pl.pallas_call(
        kern,
        grid=(b, t // tq),
        in_specs=[qspec(dq), qspec(qi.shape[2]), qspec(kiwi.shape[2]),
                  kspec(k_all.shape[2]), kspec(v_all.shape[2]), kspec(ki_all.shape[2])],
        out_specs=qspec(dq),
        out_shape=jax.ShapeDtypeStruct((b, t, dq), F32),
        scratch_shapes=[pltpu.VMEM((tq, 1), F32), pltpu.VMEM((tq, 1), F32), pltpu.VMEM((tq, s_len), F32)],
        compiler_params=_cparams(("parallel", "parallel")),
        name="dsa",
    )(q, qi, kiwi, k_all, v_all, ki_all)


def _sb_kernel(q_ref, k_ref, v_ref, o_ref, acc_ref, run_ref, *, tq, n_keys, q_pos0):
    i = pl.program_id(1)
    pairs = q_ref.shape[2] // LANES
    first_q = q_pos0 + i * tq
    qpos = first_q + lax.broadcasted_iota(jnp.int32, (tq, 1), 0)
    n_blocks = (jnp.minimum(first_q + tq - 1, n_keys) + LANES - 1) // LANES
    n_full = jnp.minimum(first_q, n_keys) // LANES
    low_q = lax.broadcasted_iota(jnp.int32, (tq, LANES), 1) < HEAD_DIM
    low_k = lax.broadcasted_iota(jnp.int32, (LANES, LANES), 1) < HEAD_DIM
    q = q_ref[0] * HEAD_DIM ** -0.5
    qm = []
    for p in range(pairs):
        qp = q[:, p * LANES:(p + 1) * LANES]
        qm.append((jnp.where(low_q, qp, 0.0).astype(BF16), jnp.where(low_q, 0.0, qp).astype(BF16)))
    rr = lax.broadcasted_iota(jnp.int32, (2 * LANES, 2 * LANES), 0)
    cc = lax.broadcasted_iota(jnp.int32, (2 * LANES, 2 * LANES), 1)
    rk = jnp.where(rr >= LANES, rr - LANES, rr)
    cs_rhs = jnp.where(jnp.logical_or(cc >= LANES, rk >= cc), 1.0, 0.0).astype(BF16)
    acc_ref[...] = jnp.zeros_like(acc_ref)
    run_ref[...] = jnp.zeros_like(run_ref)

    def block(j, masked):
        ks = pl.multiple_of(j * LANES, LANES)
        if masked:
            kpos = ks + lax.broadcasted_iota(jnp.int32, (1, LANES), 1)
            causal = jnp.logical_and(kpos < qpos, kpos < n_keys)
        heads = [(p, c) for p in range(pairs) for c in range(2)]
        lanes = lambda n: slice(n * LANES, (n + 1) * LANES)
        zs = [_dot_nt(qm[p][c], k_ref[0, pl.ds(ks, LANES), lanes(p)]) for p, c in heads]
        css = []
        for z in zs:
            sp = jnp.maximum(z, 0.0) + jnp.log(1.0 + jnp.exp(-jnp.abs(z)))
            if masked:
                sp = jnp.where(causal, sp, 0.0)
            hi = sp.astype(BF16)
            lo = (sp - hi.astype(F32)).astype(BF16)
            css.append(_dot(jnp.concatenate([hi, lo], axis=1), cs_rhs))
        probs = []
        for n, (z, cs) in enumerate(zip(zs, css)):
            run = run_ref[:, lanes(n)]
            a = jnp.exp(z - cs[:, :LANES] - run)
            if masked:
                a = jnp.where(causal, a, 0.0)
            run_ref[:, lanes(n)] = run + cs[:, LANES:]
            probs.append(a.astype(BF16))
        for p in range(pairs):
            vb = v_ref[0, pl.ds(ks, LANES), lanes(p)]
            zero = jnp.zeros_like(vb)
            v_cat = jnp.concatenate([jnp.where(low_k, vb, zero), jnp.where(low_k, zero, vb)], axis=0)
            acc_ref[:, lanes(p)] += _dot(jnp.concatenate(probs[2 * p:2 * p + 2], axis=1), v_cat)

    def masked_step(jj, carry):
        block(n_blocks - 1 - jj, True)
        return carry

    def full_pair(jj, carry):
        block(n_full - 1 - 2 * jj, False)
        block(n_full - 2 - 2 * jj, False)
        return carry

    lax.fori_loop(0, n_blocks - n_full, masked_step, 0)
    lax.fori_loop(0, n_full // 2, full_pair, 0)

    @pl.when(n_full % 2 == 1)
    def _():
        block(jnp.int32(0), False)

    o_ref[0] = acc_ref[...]


def _sb_attend(q, k_all, v_all, n_keys, q_pos0):
    b, t, d = q.shape
    s_len = k_all.shape[1]
    tq = min(t, ATTN_TQ)
    kern = functools.partial(_sb_kernel, tq=tq, n_keys=n_keys, q_pos0=q_pos0)
    return pl.pallas_call(
        kern,
        grid=(b, t // tq),
        in_specs=[pl.BlockSpec((1, tq, d), lambda bi, i: (bi, i, 0)),
                  pl.BlockSpec((1, s_len, d), lambda bi, i: (bi, 0, 0)),
                  pl.BlockSpec((1, s_len, d), lambda bi, i: (bi, 0, 0))],
        out_specs=pl.BlockSpec((1, tq, d), lambda bi, i: (bi, i, 0)),
        out_shape=jax.ShapeDtypeStruct((b, t, d), F32),
        scratch_shapes=[pltpu.VMEM((tq, d), F32), pltpu.VMEM((tq, 2 * d), F32)],
        compiler_params=_cparams(("parallel", "parallel")),
        name="sb_attend",
    )(q, k_all, v_all)


def _key_extents(nq, tq, q_pos0, n_keys, s_len, step):
    need = [min(n_keys, ((q_pos0 + (i + 1) * tq - 1) // CHUNK + 1) * CHUNK) for i in range(nq)]
    return tuple(sorted({min(s_len, -(-n // step) * step) for n in need}))


def _for_tile_extent(i, tq, q_pos0, n_keys, extents, body):
    need = jnp.minimum(n_keys, ((q_pos0 + (i + 1) * tq - 1) // CHUNK + 1) * CHUNK)
    prev = 0
    for ext in extents:
        pl.when(jnp.logical_and(need > prev, need <= ext))(functools.partial(body, ext))
        prev = ext


def _diff_kernel(q_ref, k_ref, v_ref, lam_ref, gain_ref, o_ref, *, tq, extents, n_keys, q_pos0, lam_init):
    i = pl.program_id(2)
    lp = lam_ref[...]
    lam = (jnp.exp(jnp.sum(lp[0:1] * lp[1:2], axis=1, keepdims=True))
           - jnp.exp(jnp.sum(lp[2:3] * lp[3:4], axis=1, keepdims=True)) + lam_init)
    qpos = q_pos0 + i * tq + lax.broadcasted_iota(jnp.int32, (tq, 1), 0)
    lim = jnp.minimum((qpos // CHUNK + 1) * CHUNK, n_keys)
    low = lax.broadcasted_iota(jnp.int32, (tq, LANES), 1) < HEAD_DIM

    def body(ext):
        mask = lax.broadcasted_iota(jnp.int32, (tq, ext), 1) < lim
        heads = range(q_ref.shape[2] // LANES)
        lanes = lambda h: slice(h * LANES, (h + 1) * LANES)
        scores = []
        for h in heads:
            q = q_ref[0, :, lanes(h)] * HEAD_DIM ** -0.5
            kk = k_ref[0, :ext, lanes(h)]
            for c in range(2):
                qc = (jnp.where(low, q, 0.0) if c == 0 else jnp.where(low, 0.0, q)).astype(BF16)
                scores.append(jnp.where(mask, _dot_nt(qc, kk), NEG_INF))
        probs = []
        for sc in scores:
            p = jnp.exp(sc - jnp.max(sc, axis=1, keepdims=True))
            probs.append(p * (1.0 / jnp.sum(p, axis=1, keepdims=True)))
        for h in heads:
            a = probs[2 * h] - lam * probs[2 * h + 1]
            o = _dot(a.astype(BF16), v_ref[0, :ext, lanes(h)])
            o = o * lax.rsqrt(jnp.mean(o * o, axis=-1, keepdims=True) + NORM_EPS)
            o_ref[0, :, lanes(h)] = o * gain_ref[...] * (1.0 - lam_init)

    _for_tile_extent(i, tq, q_pos0, n_keys, extents, body)


def _diff_attend(q, k_all, v_all, diff_lambda, gain, n_keys, q_pos0, lam_init):
    b, t, d = q.shape
    s_len = k_all.shape[1]
    tq = min(t, ATTN_TQ)
    extents = _key_extents(t // tq, tq, q_pos0, n_keys, s_len, 2 * LANES)
    kern = functools.partial(_diff_kernel, tq=tq, extents=extents, n_keys=n_keys, q_pos0=q_pos0, lam_init=lam_init)
    return pl.pallas_call(
        kern,
        grid=(b, d // DIFF_LANES, t // tq),
        in_specs=[pl.BlockSpec((1, tq, DIFF_LANES), lambda bi, h, i: (bi, i, h)),
                  pl.BlockSpec((1, s_len, DIFF_LANES), lambda bi, h, i: (bi, 0, h)),
                  pl.BlockSpec((1, s_len, DIFF_LANES), lambda bi, h, i: (bi, 0, h)),
                  pl.BlockSpec(diff_lambda.shape, lambda bi, h, i: (0, 0)),
                  pl.BlockSpec((1, LANES), lambda bi, h, i: (0, 0))],
        out_specs=pl.BlockSpec((1, tq, DIFF_LANES), lambda bi, h, i: (bi, i, h)),
        out_shape=jax.ShapeDtypeStruct((b, t, d), F32),
        compiler_params=_cparams(("parallel", "parallel", "parallel")),
        name="diff_attend",
    )(q, k_all, v_all, diff_lambda, gain.reshape(1, LANES))


def _route(logits):
    lane = lax.broadcasted_iota(jnp.int32, logits.shape, 1)
    is_group = lane < MOE_GROUPS
    gl = jnp.where(is_group, logits, NEG_INF)
    g_max = jnp.max(gl, axis=1, keepdims=True)
    g_sel = jnp.min(jnp.where(gl == g_max, lane, LANES), axis=1, keepdims=True)
    g_gate = 1.0 / jnp.sum(jnp.where(is_group, jnp.exp(gl - g_max), 0.0), axis=1, keepdims=True)
    in_group = jnp.logical_and(lane >= GATE_COL0, (lane - GATE_COL0) // MOE_EPG == g_sel)
    in_group = jnp.logical_and(in_group, lane < GATE_COL0 + MOE_EXPERTS)
    el = jnp.where(in_group, logits, NEG_INF)
    top1 = jnp.max(el, axis=1, keepdims=True)
    i1 = jnp.min(jnp.where(jnp.logical_and(in_group, el == top1), lane, LANES), axis=1, keepdims=True)
    rest = jnp.logical_and(in_group, lane != i1)
    el2 = jnp.where(rest, logits, NEG_INF)
    top2 = jnp.max(el2, axis=1, keepdims=True)
    i2 = jnp.min(jnp.where(jnp.logical_and(rest, el2 == top2), lane, LANES), axis=1, keepdims=True)
    e2 = jnp.exp(top2 - top1)
    w1 = g_gate / (1.0 + e2)
    gates = jnp.where(lane == i1, w1, jnp.where(lane == i2, w1 * e2, 0.0))
    return jnp.where(lane == GROUP_LANE, g_sel.astype(F32), gates)


def _out_kernel(h_ref, a_ref, b_ref, wa_ref, wb_ref, g_ref, wr_ref, br_ref, o_ref, xn_ref, gate_ref):
    h = (h_ref[...] + _dot(a_ref[...].astype(BF16), wa_ref[...])
         + _dot(b_ref[...].astype(BF16), wb_ref[...]))
    o_ref[...] = h
    xn = _rms(h, g_ref[...]).astype(BF16)
    xn_ref[...] = xn
    gate_ref[...] = _route(_dot(xn, wr_ref[...]) + br_ref[...])


def _out_proj_route(h, a, bmix, w_out, g_ffn, wr, br, tm):
    n, d = h.shape
    ca = a.shape[1]
    wa, wb = w_out[:ca], w_out[ca:]
    row = lambda w: pl.BlockSpec((tm, w), lambda i: (i, 0))
    whole = lambda x: pl.BlockSpec(x.shape, lambda i: (0, 0))
    return pl.pallas_call(
        _out_kernel,
        grid=(n // tm,),
        in_specs=[row(d), row(ca), row(bmix.shape[1]), whole(wa), whole(wb),
                  pl.BlockSpec((1, d), lambda i: (0, 0)), whole(wr), whole(br)],
        out_specs=[row(d), row(d), row(LANES)],
        out_shape=[jax.ShapeDtypeStruct((n, d), F32), jax.ShapeDtypeStruct((n, d), BF16),
                   jax.ShapeDtypeStruct((n, LANES), F32)],
        compiler_params=_cparams(("parallel",)),
        name="out_proj_route",
    )(h, a, bmix, wa, wb, g_ffn.reshape(1, d), wr, br)


def _split3(x):
    hi = x.astype(BF16)
    r1 = x - hi.astype(F32)
    mid = r1.astype(BF16)
    lo = (r1 - mid.astype(F32)).astype(BF16)
    return hi, mid, lo


def _expert_kernel(xn_ref, gate_ref, w1_ref, w3_ref, w2_ref, h_ref, gf_ref, o_ref,
                   acc_ref, xs_ref, gs_ref, pt_ref, tri_ref, seg_ref, *, final_norm, win):
    i = pl.program_id(0)
    e = pl.program_id(1)
    tm = xn_ref.shape[0]

    @pl.when(jnp.logical_and(i == 0, e == 0))
    def _():
        rr = lax.broadcasted_iota(jnp.int32, (tm, tm), 0)
        cc = lax.broadcasted_iota(jnp.int32, (tm, tm), 1)
        tri_ref[...] = jnp.where(cc < rr, 1.0, 0.0).astype(BF16)

    @pl.when(e == 0)
    def _():
        gate = gate_ref[...]
        lane = lax.broadcasted_iota(jnp.int32, gate.shape, 1)
        lane_row = lax.broadcasted_iota(jnp.int32, (1, LANES), 1)
        in_grp = jnp.logical_and(lane < MOE_GROUPS, lane.astype(F32) == gate[:, GROUP_LANE:GROUP_LANE + 1])
        onehot = jnp.where(in_grp, 1.0, 0.0)
        before = _dot(tri_ref[...], onehot.astype(BF16))
        count = jnp.sum(onehot, axis=0, keepdims=True)
        first = jnp.zeros((1, LANES), F32)
        start = jnp.float32(0.0)
        for g in range(MOE_GROUPS):
            n_g = jnp.sum(jnp.where(lane_row == g, count, 0.0))
            seg_ref[g] = start.astype(jnp.int32)
            seg_ref[MOE_GROUPS + g] = n_g.astype(jnp.int32)
            first = jnp.where(lane_row == g, start, first)
            start = start + n_g
        dest = jnp.sum(onehot * (before + first), axis=1, keepdims=True).astype(jnp.int32)
        g_hi, g_mid, g_lo = _split3(gate)
        sr = min(SORT_ROWS, tm)
        for r in range(0, tm, sr):
            rs = slice(r, r + sr)
            pt_ref[rs, :] = jnp.where(lax.broadcasted_iota(jnp.int32, (sr, tm), 1) == dest[rs],
                                      1.0, 0.0).astype(BF16)
        for r in range(0, tm, sr):
            rs = slice(r, r + sr)
            pt_cols = pt_ref[:, rs]
            xs_ref[rs, :] = _dot_tn(pt_cols, xn_ref[...]).astype(BF16)
            gs_ref[rs, :] = _dot_tn(pt_cols, g_hi) + _dot_tn(pt_cols, g_mid) + _dot_tn(pt_cols, g_lo)
        xs_ref[tm:, :] = jnp.zeros((win, xs_ref.shape[1]), BF16)
        gs_ref[tm:, :] = jnp.zeros((win, LANES), F32)
        acc_ref[...] = jnp.zeros_like(acc_ref)

    grp = e // MOE_EPG
    seg_first = seg_ref[grp]
    seg_rows = seg_ref[MOE_GROUPS + grp]
    row0 = (seg_first // BF16_ROWS) * BF16_ROWS
    n_win = (seg_first + seg_rows - row0 + win - 1) // win

    def window(w, carry):
        rows = pl.ds(pl.multiple_of(row0 + w * win, BF16_ROWS), win)
        x = xs_ref[rows, :]
        gsw = gs_ref[rows, :]
        lane = lax.broadcasted_iota(jnp.int32, gsw.shape, 1)
        ge = jnp.sum(jnp.where(lane == GATE_COL0 + e, gsw, 0.0), axis=1, keepdims=True)
        a = _dot(x, w1_ref[0])
        b = _dot(x, w3_ref[0])
        act = (a * (1.0 / (1.0 + jnp.exp(-a))) * b).astype(BF16)
        acc_ref[rows, :] += ge * _dot(act, w2_ref[0])
        return carry

    lax.fori_loop(0, n_win, window, 0)

    @pl.when(e == pl.num_programs(1) - 1)
    def _():
        a_hi, a_mid, a_lo = _split3(acc_ref[0:tm, :])
        sr = min(SORT_ROWS, tm)
        for r in range(0, tm, sr):
            rs = slice(r, r + sr)
            pt = pt_ref[rs, :]
            y = h_ref[rs, :] + (_dot(pt, a_hi) + _dot(pt, a_mid) + _dot(pt, a_lo))
            o_ref[rs, :] = _rms(y, gf_ref[...]) if final_norm else y


def _moe(h, xn, gate, w1, w3, w2, g_final, final_norm, tm):
    n, d = h.shape
    n_e = w1.shape[0]
    tme = MOE_TM if n % MOE_TM == 0 else tm
    win = -(-(tme * 5 // 16) // BF16_ROWS) * BF16_ROWS
    row2 = lambda w: pl.BlockSpec((tme, w), lambda i, e: (i, 0))
    wspec = lambda w: pl.BlockSpec((1,) + w.shape[1:], lambda i, e: (e, 0, 0))
    kern = functools.partial(_expert_kernel, final_norm=final_norm, win=win)
    return pl.pallas_call(
        kern,
        grid=(n // tme, n_e),
        in_specs=[row2(d), row2(LANES),
                  wspec(w1), wspec(w3), wspec(w2),
                  row2(d), pl.BlockSpec((1, d), lambda i, e: (0, 0))],
        out_specs=pl.BlockSpec((tme, d), lambda i, e: (i, 0), pipeline_mode=pl.Buffered(1)),
        out_shape=jax.ShapeDtypeStruct((n, d), F32),
        scratch_shapes=[pltpu.VMEM((tme + win, d), F32), pltpu.VMEM((tme + win, d), BF16),
                        pltpu.VMEM((tme + win, LANES), F32), pltpu.VMEM((tme, tme), BF16),
                        pltpu.VMEM((tme, tme), BF16), pltpu.SMEM((2 * MOE_GROUPS,), jnp.int32)],
        compiler_params=_cparams(("arbitrary", "arbitrary")),
        name="moe_experts",
    )(xn, gate, w1, w3, w2, h, g_final.reshape(1, d))


EVEN_SEGS = ((0, 512, None, False), (512, 512, "full", False), (1024, 128, "full", True), (1152, 128, None, True),
             (1280, 256, "full", False), (1536, 128, "half", False))
ODD_SEGS = ((0, 512, None, False), (512, 512, None, True), (1024, 512, None, True),
            (1536, 512, "full", False), (2048, 512, "full", True), (2560, 512, None, True))


def _cat_keys(hist, new):
    allk = jnp.concatenate([hist, new], axis=1) if hist is not None else new
    n_keys = allk.shape[1]
    pad = -n_keys % LANES
    if pad:
        allk = jnp.pad(allk, ((0, 0), (0, pad), (0, 0)))
    return allk.astype(BF16), n_keys


def kernel(x_prompt, x_sample, cache_pool, cache_dsa_k, cache_dsa_v, cache_idx_k, cache_sb_k, cache_sb_v,
           cache_diff_k, cache_diff_v, norm_mix, norm_ffn, norm_final, w_in_even, w_pool, pool_scale,
           w_out_even, w_in_odd, diff_lambda, diff_subln, w_out_odd, moe_w_group, moe_b_group,
           moe_w_expert, moe_b_expert, moe_w1, moe_w3, moe_w2):
    b, t, d = x_prompt.shape
    bd, td, _ = x_sample.shape
    past = cache_dsa_k.shape[2]
    depth = norm_mix.shape[0]
    groups = ((b, t, 0, min(512, b * t)), (bd, td, past, bd * td))

    tabs = []
    for (gb, gt, p0, tm) in groups:
        tab = _rope_tables(p0 + jnp.arange(gt, dtype=jnp.int32))
        if tm > gt:
            tab = jnp.tile(tab, (tm // gt, 1))
        tabs.append(tab)

    hs = [x_prompt.reshape(b * t, d), x_sample.reshape(bd * td, d)]
    outs = [dict(), dict()]
    for l in range(depth):
        li = l // 2
        last = l == depth - 1
        if l % 2 == 0:
            n_in = w_in_even.shape[2]
            w_in = jnp.pad(w_in_even[li], ((0, 0), (0, -n_in % LANES))).astype(BF16)
            w_out = w_out_even[li].astype(BF16)
            wp = w_pool[li].astype(BF16)
        else:
            w_in = w_in_odd[li].astype(BF16)
            w_out = w_out_odd[li].astype(BF16)
            lam_init = 0.8 - 0.6 * math.exp(-0.3 * l)
        wr = jnp.concatenate([moe_w_group[l]] + [moe_w_expert[l, g] for g in range(MOE_GROUPS)], axis=1)
        wr = jnp.pad(wr, ((0, 0), (0, LANES - wr.shape[1]))).astype(BF16)
        br = jnp.concatenate([moe_b_group[l], moe_b_expert[l].reshape(-1)])
        br = jnp.pad(br, (0, LANES - br.shape[0])).reshape(1, LANES).astype(F32)
        w1, w3, w2 = moe_w1[l].astype(BF16), moe_w3[l].astype(BF16), moe_w2[l].astype(BF16)

        for gi, (gb, gt, p0, tm) in enumerate(groups):
            h = hs[gi]
            o = outs[gi]
            sample = gi == 1
            r3 = lambda x: x.reshape(gb, gt, x.shape[-1])
            if l % 2 == 0:
                u, q, k, v, qi, kiwi, k16, v16 = [
                    r3(x) for x in _project(h, norm_mix[l], w_in, tabs[gi], EVEN_SEGS, tm)]
                ki = kiwi[..., :HEAD_DIM]
                hist = cache_pool[li] if sample else jnp.zeros((gb, POOL_HIST, u.shape[2]), F32)
                a_out = _pool_mix(u, hist, wp, pool_scale[li], p0)
                if sample:
                    k_all, n_keys = _cat_keys(cache_dsa_k[li].reshape(gb, past, -1), k)
                    v_all, _ = _cat_keys(cache_dsa_v[li].reshape(gb, past, -1), v)
                    ki_all, _ = _cat_keys(cache_idx_k[li], ki)
                else:
                    (k_all, n_keys), (v_all, _), (ki_all, _) = _cat_keys(None, k16), _cat_keys(None, v16), _cat_keys(None, ki)
                b_out = _dsa(q, qi, kiwi, k_all, v_all, ki_all, n_keys, p0)
                o.setdefault("pool", []).append(jnp.concatenate([hist, u], axis=1)[:, -POOL_HIST:])
                o.setdefault("dsa_k", []).append(k.reshape(gb, gt, -1, HEAD_DIM))
                o.setdefault("dsa_v", []).append(v.reshape(gb, gt, -1, HEAD_DIM))
                o.setdefault("idx_k", []).append(ki)
                mix_a, mix_b = a_out, b_out
            else:
                sq, sk, sv, dq, dk, dv, sk16, sv16, dk16, dv16 = [
                    r3(x) for x in _project(h, norm_mix[l], w_in, tabs[gi], ODD_SEGS, tm)]
                if sample:
                    sk_all, n_keys = _cat_keys(cache_sb_k[li].reshape(gb, past, -1), sk)
                    sv_all, _ = _cat_keys(cache_sb_v[li].reshape(gb, past, -1), sv)
                    dk_all, _ = _cat_keys(cache_diff_k[li].reshape(gb, past, -1), dk)
                    dv_all, _ = _cat_keys(cache_diff_v[li].reshape(gb, past, -1), dv)
                else:
                    (sk_all, n_keys), (sv_all, _) = _cat_keys(None, sk16), _cat_keys(None, sv16)
                    (dk_all, _), (dv_all, _) = _cat_keys(None, dk16), _cat_keys(None, dv16)
                c_out = _sb_attend(sq, sk_all, sv_all, n_keys, p0)
                d_out = _diff_attend(dq, dk_all, dv_all, diff_lambda[li], diff_subln[li], n_keys, p0, lam_init)
                n_sb = sk.shape[2] // HEAD_DIM
                n_df = dk.shape[2] // (2 * HEAD_DIM)
                o.setdefault("sb_k", []).append(sk.reshape(gb, gt, n_sb, HEAD_DIM))
                o.setdefault("sb_v", []).append(sv.reshape(gb, gt, n_sb, HEAD_DIM))
                o.setdefault("diff_k", []).append(dk.reshape(gb, gt, n_df, 2, HEAD_DIM))
                o.setdefault("diff_v", []).append(dv.reshape(gb, gt, n_df, 2 * HEAD_DIM))
                mix_a, mix_b = c_out, d_out
            h, xn, gate = _out_proj_route(h, mix_a.reshape(gb * gt, -1), mix_b.reshape(gb * gt, -1), w_out,
                                          norm_ffn[l], wr, br, tm)
            hs[gi] = _moe(h, xn, gate, w1, w3, w2, norm_final, last, tm)

    names = ("pool", "dsa_k", "dsa_v", "idx_k", "sb_k", "sb_v", "diff_k", "diff_v")
    res = [hs[0].reshape(b, t, d), hs[1].reshape(bd, td, d)]
    for o in outs:
        res += [jnp.stack(o[nm]) for nm in names]
    return tuple(res)
```

```python
import functools
import math

import jax
import jax.numpy as jnp
import numpy as np
from jax import lax
from jax.experimental import pallas as pl
from jax.experimental.pallas import tpu as pltpu

F32 = jnp.float32
BF16 = jnp.bfloat16

LANES = 128
HEAD_DIM = 64
CHUNK = 64
ROPE_THETA = 10000.0
NORM_EPS = 1e-6
NEG_INF = -1e30
PAD_SCORE = -3e38
BIG_POS = 3e38
POOL_WINDOWS = (2, 4, 8, 16)
POOL_HIST = 15
POOL_HIST_PAD = 16
DSA_TOPK = 256
IDX_HEADS = 4
MOE_GROUPS = 4
MOE_EPG = 4
MOE_EXPERTS = 16
GATE_COL0 = MOE_GROUPS
GROUP_LANE = 0
BF16_ROWS = 16
SORT_ROWS = 256
VMEM_LIMIT = 56 * 1024 * 1024
BISECT_STEPS = 8
BISECT_ROUNDS = 48
BISECT_UNCHECKED = 2
ATTN_TQ = 256
DSA_TQ = 128
MOE_TM = 1024
MOE_SUBTILES = 2
MOE_VMEM_LIMIT = 62 * 1024 * 1024
DIFF_LANES = 2 * LANES


def _cparams(sem):
    return pltpu.CompilerParams(dimension_semantics=sem, vmem_limit_bytes=VMEM_LIMIT)


def _dot(a, b):
    return jnp.dot(a, b, preferred_element_type=F32)


def _dot_nt(a, b):
    return lax.dot_general(a, b, (((1,), (1,)), ((), ())), preferred_element_type=F32)


def _dot_tn(a, b):
    return lax.dot_general(a, b, (((0,), (0,)), ((), ())), preferred_element_type=F32)


def _rms(x, g):
    ms = jnp.mean(x * x, axis=-1, keepdims=True)
    return x * lax.rsqrt(ms + NORM_EPS) * g


def _proj_kernel(x_ref, g_ref, w_ref, tab_ref, *out_refs, segs):
    xn = _rms(x_ref[...], g_ref[...]).astype(BF16)
    copies = iter(out_refs[len(segs):])
    for o_ref, (c0, width, mode, twin) in zip(out_refs, segs):
        t_ref = next(copies) if twin else None
        y = _dot(xn, w_ref[:, c0:c0 + width])
        if mode is None:
            o_ref[...] = y
            if twin:
                t_ref[...] = y.astype(BF16)
            continue
        t0 = 0 if mode == "full" else 3 * LANES
        cos = tab_ref[:, t0:t0 + LANES]
        sin_a = tab_ref[:, t0 + LANES:t0 + 2 * LANES]
        sin_b = tab_ref[:, t0 + 2 * LANES:t0 + 3 * LANES]
        for c in range(0, width, LANES):
            yc = y[:, c:c + LANES]
            yr = (yc * cos + pltpu.roll(yc, LANES - HEAD_DIM // 2, 1) * sin_a
                  + pltpu.roll(yc, HEAD_DIM // 2, 1) * sin_b)
            o_ref[:, c:c + LANES] = yr
            if twin:
                t_ref[:, c:c + LANES] = yr.astype(BF16)


def _rope_tables(pos):
    half = HEAD_DIM // 2
    inv = ROPE_THETA ** (-jnp.arange(half, dtype=F32) / half)
    ang = pos.astype(F32)[:, None] * inv[None, :]
    cos, sin = jnp.cos(ang), jnp.sin(ang)
    zero, one = jnp.zeros_like(sin), jnp.ones_like(cos)
    cos_h = jnp.concatenate([cos, cos], axis=1)
    sa_h = jnp.concatenate([-sin, zero], axis=1)
    sb_h = jnp.concatenate([zero, sin], axis=1)
    one_h = jnp.concatenate([one, one], axis=1)
    zero_h = jnp.concatenate([zero, zero], axis=1)
    return jnp.concatenate([cos_h, cos_h, sa_h, sa_h, sb_h, sb_h,
                            cos_h, one_h, sa_h, zero_h, sb_h, zero_h], axis=1)


def _project(x, g, w, tab, segs, tm):
    n, d = x.shape
    tt = tab.shape[0]
    nt = tt // tm
    kern = functools.partial(_proj_kernel, segs=segs)
    return pl.pallas_call(
        kern,
        grid=(n // tm,),
        in_specs=[pl.BlockSpec((tm, d), lambda i: (i, 0)),
                  pl.BlockSpec((1, d), lambda i: (0, 0)),
                  pl.BlockSpec(w.shape, lambda i: (0, 0)),
                  pl.BlockSpec((tm, tab.shape[1]), lambda i: (i % nt, 0))],
        out_specs=[pl.BlockSpec((tm, wd), lambda i: (i, 0)) for _, wd, _, _ in segs]
        + [pl.BlockSpec((tm, wd), lambda i: (i, 0)) for _, wd, _, twin in segs if twin],
        out_shape=[jax.ShapeDtypeStruct((n, wd), F32) for _, wd, _, _ in segs]
        + [jax.ShapeDtypeStruct((n, wd), BF16) for _, wd, _, twin in segs if twin],
        compiler_params=_cparams(("parallel",)),
        name="proj",
    )(x, g.reshape(1, d), w, tab)


def _pool_kernel(u_ref, h_ref, w_ref, s_ref, o_ref, ext_ref, *, t, pos0, rc):
    ext_ref[0:POOL_HIST_PAD, :] = h_ref[0]
    ext_ref[POOL_HIST_PAD:POOL_HIST_PAD + t, :] = u_ref[0]
    for r0 in range(0, t, rc):
        pos = pos0 + r0 + lax.broadcasted_iota(jnp.int32, (rc, 1), 0)
        for g, win in enumerate(POOL_WINDOWS):
            c0 = g * LANES
            u_new = ext_ref[POOL_HIST_PAD + r0:POOL_HIST_PAD + r0 + rc, c0:c0 + LANES]
            s = u_new
            for k in range(1, win):
                s = s + ext_ref[POOL_HIST_PAD + r0 - k:POOL_HIST_PAD + r0 - k + rc, c0:c0 + LANES]
            cnt = jnp.minimum(pos + 1, win).astype(F32)
            dlt = (s / cnt - u_new).astype(BF16)
            o_ref[0, r0:r0 + rc, c0:c0 + LANES] = _dot(dlt, w_ref[g]) * s_ref[:, c0:c0 + LANES]


def _pool_mix(u, hist, w_pool, pool_scale, pos0):
    b, t, c = u.shape
    rc = min(t, 256)
    hist16 = jnp.pad(hist, ((0, 0), (POOL_HIST_PAD - POOL_HIST, 0), (0, 0)))
    kern = functools.partial(_pool_kernel, t=t, pos0=pos0, rc=rc)
    return pl.pallas_call(
        kern,
        grid=(b,),
        in_specs=[pl.BlockSpec((1, t, c), lambda i: (i, 0, 0)),
                  pl.BlockSpec((1, POOL_HIST_PAD, c), lambda i: (i, 0, 0)),
                  pl.BlockSpec(w_pool.shape, lambda i: (0, 0, 0)),
                  pl.BlockSpec((1, c), lambda i: (0, 0))],
        out_specs=pl.BlockSpec((1, t, c), lambda i: (i, 0, 0)),
        out_shape=jax.ShapeDtypeStruct((b, t, c), F32),
        scratch_shapes=[pltpu.VMEM((POOL_HIST_PAD + t, c), F32)],
        compiler_params=_cparams(("parallel",)),
        name="pool_mix",
    )(u, hist16, w_pool, pool_scale.reshape(1, c))


def _dsa_kernel(q_ref, qi_ref, kw_ref, k_ref, v_ref, ki_ref, o_ref, lo_ref, hi_ref, bias_ref,
                *, tq, extents, n_keys, q_pos0, n_sel):
    i = pl.program_id(1)
    qpos = q_pos0 + i * tq + lax.broadcasted_iota(jnp.int32, (tq, 1), 0)
    lim = jnp.minimum((qpos // CHUNK + 1) * CHUNK, n_keys)
    kf = float(n_sel)
    low = lax.broadcasted_iota(jnp.int32, (tq, LANES), 1) < HEAD_DIM
    q = q_ref[0] * HEAD_DIM ** -0.5
    qi = qi_ref[0].astype(BF16)
    wi = kw_ref[0][:, HEAD_DIM:HEAD_DIM + IDX_HEADS] * (IDX_HEADS * HEAD_DIM) ** -0.5

    def body(ext):
        kpos = lax.broadcasted_iota(jnp.int32, (tq, ext), 1)
        adm = kpos < lim
        padded = ext > n_keys
        virt = float(max(n_keys - ext, 0))

        sidx = jnp.zeros((tq, ext), F32)
        ki = ki_ref[0, :ext, :]
        for h in range(IDX_HEADS):
            sh = _dot_nt(qi[:, h * HEAD_DIM:(h + 1) * HEAD_DIM], ki)
            sidx = sidx + jnp.maximum(sh, 0.0) * wi[:, h:h + 1]
        sm = jnp.where(adm, sidx, NEG_INF)
        if padded:
            real = kpos < n_keys
            sm = jnp.where(real, sm, PAD_SCORE)

        def count_gt_wide(x):
            part = jnp.where(sm[:, :LANES] > x, 1.0, 0.0)
            for j in range(1, ext // LANES):
                part = part + jnp.where(sm[:, j * LANES:(j + 1) * LANES] > x, 1.0, 0.0)
            cnt = jnp.broadcast_to(jnp.sum(part, axis=1, keepdims=True), (tq, LANES))
            return cnt + jnp.where(x < NEG_INF, virt, 0.0) if virt else cnt

        def count_gt(x):
            return count_gt_wide(jnp.broadcast_to(x, (tq, LANES)))[:, :1]

        def bracket(lo, hi):
            above = jnp.min(jnp.where(sm > lo, sm, BIG_POS), axis=1, keepdims=True)
            below = jnp.max(jnp.where(sm <= hi, sm, PAD_SCORE), axis=1, keepdims=True)
            if virt:
                above = jnp.minimum(above, jnp.where(lo < NEG_INF, NEG_INF, BIG_POS))
                below = jnp.maximum(below, jnp.where(hi >= NEG_INF, NEG_INF, PAD_SCORE))
            return above, below

        row_max = jnp.max(sm, axis=1, keepdims=True)
        row_min = jnp.min(jnp.where(real, sm, BIG_POS) if padded else sm, axis=1, keepdims=True)
        if virt:
            row_min = jnp.minimum(row_min, NEG_INF)
        adm_min = jnp.min(jnp.where(adm, sm, BIG_POS), axis=1, keepdims=True)
        few = count_gt(row_min) < kf
        tight = count_gt(adm_min) >= kf
        lo_ref[...] = jnp.where(few, PAD_SCORE, jnp.where(tight, adm_min, row_min))
        hi_ref[...] = jnp.where(few, row_min, jnp.where(tight, row_max, adm_min))

        def unresolved(lo, hi):
            above, below = bracket(lo, hi)
            return jnp.sum(jnp.where(above < below, 1, 0))

        def cond(carry):
            rounds, open_rows = carry
            return jnp.logical_and(open_rows > 0, rounds < BISECT_ROUNDS)

        def step(carry):
            rounds, _ = carry
            lo = jnp.broadcast_to(lo_ref[...], (tq, LANES))
            hi = jnp.broadcast_to(hi_ref[...], (tq, LANES))
            for _ in range(BISECT_STEPS):
                mid = 0.5 * lo + 0.5 * hi
                under = count_gt_wide(mid) < kf
                hi = jnp.where(under, mid, hi)
                lo = jnp.where(under, lo, mid)
            lo_ref[...] = lo[:, :1]
            hi_ref[...] = hi[:, :1]
            open_rows = lax.cond(rounds + 1 >= BISECT_UNCHECKED,
                                 lambda: unresolved(lo_ref[...], hi_ref[...]), lambda: jnp.int32(1))
            return rounds + 1, open_rows

        lax.while_loop(cond, step, (jnp.int32(0), jnp.int32(1)))
        _, thr = bracket(lo_ref[...], hi_ref[...])

        gt = sm > thr
        eq = sm == thr
        need = kf - count_gt(thr)
        n_eq = jnp.sum(jnp.where(eq, 1.0, 0.0), axis=1, keepdims=True)
        crowded = jnp.sum(jnp.where(n_eq > need, 1, 0))

        @pl.when(crowded == 0)
        def _():
            keep = jnp.logical_and(jnp.logical_or(gt, eq), adm)
            bias_ref[:, :ext] = jnp.where(keep, 0.0, NEG_INF)

        @pl.when(crowded > 0)
        def _():
            rr = lax.broadcasted_iota(jnp.int32, (LANES, LANES), 0)
            cc = lax.broadcasted_iota(jnp.int32, (LANES, LANES), 1)
            prefix_ones = jnp.where(rr <= cc, 1.0, 0.0).astype(BF16)
            carry = jnp.zeros((tq, 1), F32)
            for j in range(ext // LANES):
                sl = slice(j * LANES, (j + 1) * LANES)
                eq_j = eq[:, sl]
                rank = _dot(jnp.where(eq_j, 1.0, 0.0).astype(BF16), prefix_ones) + carry
                keep = jnp.logical_or(gt[:, sl], jnp.logical_and(eq_j, rank <= need))
                bias_ref[:, sl] = jnp.where(jnp.logical_and(keep, adm[:, sl]), 0.0, NEG_INF)
                carry = rank[:, LANES - 1:LANES]

        bias = bias_ref[:, :ext]
        bias2 = jnp.concatenate([bias, bias], axis=0)

        kk = k_ref[0, :ext, :]
        vv = v_ref[0, :ext, :]
        low_k = lax.broadcasted_iota(jnp.int32, (ext, LANES), 1) < HEAD_DIM
        vsw = pltpu.roll(vv.astype(F32), HEAD_DIM, 1).astype(BF16)
        one = jnp.ones_like(vv)
        n_kv = LANES // HEAD_DIM
        stacks = [(g, odd) for g in range(n_kv) for odd in range(2)]
        scores = []
        for g, odd in stacks:
            rows = []
            for m in range(2):
                c = 2 * g + m
                qc = q[:, c * LANES:(c + 1) * LANES]
                if (odd == 1) != (g == 1):
                    qc = pltpu.roll(qc, HEAD_DIM, 1)
                rows.append(jnp.where(low, qc, 0.0) if g == 0 else jnp.where(low, 0.0, qc))
            qs = jnp.concatenate(rows, axis=0).astype(BF16)
            scores.append(_dot_nt(qs, kk) + bias2)
        probs = [jnp.exp(sc - jnp.max(sc, axis=1, keepdims=True)).astype(BF16) for sc in scores]
        outs = []
        for (g, odd), p in zip(stacks, probs):
            if odd:
                v_aug = jnp.where(low_k, one, vsw if g == 0 else vv)
            else:
                v_aug = jnp.where(low_k, vv if g == 0 else vsw, one)
            og = _dot(p, v_aug)
            outs.append(og / pltpu.roll(og, HEAD_DIM, 1))
        for g in range(n_kv):
            for m in range(2):
                c = 2 * g + m
                o_ref[0, :, c * LANES:(c + 1) * LANES] = jnp.where(
                    low, outs[2 * g][m * tq:(m + 1) * tq], outs[2 * g + 1][m * tq:(m + 1) * tq])

    _for_tile_extent(i, tq, q_pos0, n_keys, extents, body)


def _dsa(q, qi, kiwi, k_all, v_all, ki_all, n_keys, q_pos0):
    b, t, dq = q.shape
    s_len = k_all.shape[1]
    tq = min(t, DSA_TQ)
    n_sel = min(DSA_TOPK, n_keys // 4)
    extents = _key_extents(t // tq, tq, q_pos0, n_keys, s_len, 4 * LANES)
    kern = functools.partial(_dsa_kernel, tq=tq, extents=extents, n_keys=n_keys, q_pos0=q_pos0, n_sel=n_sel)
    qspec = lambda w: pl.BlockSpec((1, tq, w), lambda bi, i: (bi, i, 0))
    kspec = lambda w: pl.BlockSpec((1, s_len, w), lambda bi, i: (bi, 0, 0))
    return pl.pallas_call(
        kern,
        grid=(b, t // tq),
        in_specs=[qspec(dq), qspec(qi.shape[2]), qspec(kiwi.shape[2]),
                  kspec(k_all.shape[2]), kspec(v_all.shape[2]), kspec(ki_all.shape[2])],
        out_specs=qspec(dq),
        out_shape=jax.ShapeDtypeStruct((b, t, dq), F32),
        scratch_shapes=[pltpu.VMEM((tq, 1), F32), pltpu.VMEM((tq, 1), F32), pltpu.VMEM((tq, s_len), F32)],
        compiler_params=_cparams(("parallel", "parallel")),
        name="dsa",
    )(q, qi, kiwi, k_all, v_all, ki_all)


def _sb_kernel(q_ref, k_ref, v_ref, o_ref, acc_ref, run_ref, *, tq, n_keys, q_pos0):
    i = pl.program_id(1)
    pairs = q_ref.shape[2] // LANES
    first_q = q_pos0 + i * tq
    qpos = first_q + lax.broadcasted_iota(jnp.int32, (tq, 1), 0)
    n_blocks = (jnp.minimum(first_q + tq - 1, n_keys) + LANES - 1) // LANES
    n_full = jnp.minimum(first_q, n_keys) // LANES
    low_q = lax.broadcasted_iota(jnp.int32, (tq, LANES), 1) < HEAD_DIM
    low_k = lax.broadcasted_iota(jnp.int32, (LANES, LANES), 1) < HEAD_DIM
    q = q_ref[0] * HEAD_DIM ** -0.5
    qm = []
    for p in range(pairs):
        qp = q[:, p * LANES:(p + 1) * LANES]
        qm.append((jnp.where(low_q, qp, 0.0).astype(BF16), jnp.where(low_q, 0.0, qp).astype(BF16)))
    rr = lax.broadcasted_iota(jnp.int32, (2 * LANES, 2 * LANES), 0)
    cc = lax.broadcasted_iota(jnp.int32, (2 * LANES, 2 * LANES), 1)
    rk = jnp.where(rr >= LANES, rr - LANES, rr)
    cs_rhs = jnp.where(jnp.logical_or(cc >= LANES, rk >= cc), 1.0, 0.0).astype(BF16)
    acc_ref[...] = jnp.zeros_like(acc_ref)
    run_ref[...] = jnp.zeros_like(run_ref)

    def block(j, masked):
        ks = pl.multiple_of(j * LANES, LANES)
        if masked:
            kpos = ks + lax.broadcasted_iota(jnp.int32, (1, LANES), 1)
            causal = jnp.logical_and(kpos < qpos, kpos < n_keys)
        heads = [(p, c) for p in range(pairs) for c in range(2)]
        lanes = lambda n: slice(n * LANES, (n + 1) * LANES)
        zs = [_dot_nt(qm[p][c], k_ref[0, pl.ds(ks, LANES), lanes(p)]) for p, c in heads]
        css = []
        for z in zs:
            sp = jnp.maximum(z, 0.0) + jnp.log(1.0 + jnp.exp(-jnp.abs(z)))
            if masked:
                sp = jnp.where(causal, sp, 0.0)
            hi = sp.astype(BF16)
            lo = (sp - hi.astype(F32)).astype(BF16)
            css.append(_dot(jnp.concatenate([hi, lo], axis=1), cs_rhs))
        probs = []
        for n, (z, cs) in enumerate(zip(zs, css)):
            run = run_ref[:, lanes(n)]
            a = jnp.exp(z - cs[:, :LANES] - run)
            if masked:
                a = jnp.where(causal, a, 0.0)
            run_ref[:, lanes(n)] = run + cs[:, LANES:]
            probs.append(a.astype(BF16))
        for p in range(pairs):
            vb = v_ref[0, pl.ds(ks, LANES), lanes(p)]
            zero = jnp.zeros_like(vb)
            v_cat = jnp.concatenate([jnp.where(low_k, vb, zero), jnp.where(low_k, zero, vb)], axis=0)
            acc_ref[:, lanes(p)] += _dot(jnp.concatenate(probs[2 * p:2 * p + 2], axis=1), v_cat)

    def masked_step(jj, carry):
        block(n_blocks - 1 - jj, True)
        return carry

    def full_pair(jj, carry):
        block(n_full - 1 - 2 * jj, False)
        block(n_full - 2 - 2 * jj, False)
        return carry

    lax.fori_loop(0, n_blocks - n_full, masked_step, 0)
    lax.fori_loop(0, n_full // 2, full_pair, 0)

    @pl.when(n_full % 2 == 1)
    def _():
        block(jnp.int32(0), False)

    o_ref[0] = acc_ref[...]


def _sb_attend(q, k_all, v_all, n_keys, q_pos0):
    b, t, d = q.shape
    s_len = k_all.shape[1]
    tq = min(t, ATTN_TQ)
    kern = functools.partial(_sb_kernel, tq=tq, n_keys=n_keys, q_pos0=q_pos0)
    return pl.pallas_call(
        kern,
        grid=(b, t // tq),
        in_specs=[pl.BlockSpec((1, tq, d), lambda bi, i: (bi, i, 0)),
                  pl.BlockSpec((1, s_len, d), lambda bi, i: (bi, 0, 0)),
                  pl.BlockSpec((1, s_len, d), lambda bi, i: (bi, 0, 0))],
        out_specs=pl.BlockSpec((1, tq, d), lambda bi, i: (bi, i, 0)),
        out_shape=jax.ShapeDtypeStruct((b, t, d), F32),
        scratch_shapes=[pltpu.VMEM((tq, d), F32), pltpu.VMEM((tq, 2 * d), F32)],
        compiler_params=_cparams(("parallel", "parallel")),
        name="sb_attend",
    )(q, k_all, v_all)


def _key_extents(nq, tq, q_pos0, n_keys, s_len, step):
    need = [min(n_keys, ((q_pos0 + (i + 1) * tq - 1) // CHUNK + 1) * CHUNK) for i in range(nq)]
    return tuple(sorted({min(s_len, -(-n // step) * step) for n in need}))


def _for_tile_extent(i, tq, q_pos0, n_keys, extents, body):
    need = jnp.minimum(n_keys, ((q_pos0 + (i + 1) * tq - 1) // CHUNK + 1) * CHUNK)
    prev = 0
    for ext in extents:
        pl.when(jnp.logical_and(need > prev, need <= ext))(functools.partial(body, ext))
        prev = ext


def _diff_kernel(q_ref, k_ref, v_ref, lam_ref, gain_ref, o_ref, *, tq, extents, n_keys, q_pos0, lam_init):
    i = pl.program_id(2)
    lp = lam_ref[...]
    lam = (jnp.exp(jnp.sum(lp[0:1] * lp[1:2], axis=1, keepdims=True))
           - jnp.exp(jnp.sum(lp[2:3] * lp[3:4], axis=1, keepdims=True)) + lam_init)
    qpos = q_pos0 + i * tq + lax.broadcasted_iota(jnp.int32, (tq, 1), 0)
    lim = jnp.minimum((qpos // CHUNK + 1) * CHUNK, n_keys)
    low = lax.broadcasted_iota(jnp.int32, (tq, LANES), 1) < HEAD_DIM

    def body(ext):
        mask = lax.broadcasted_iota(jnp.int32, (tq, ext), 1) < lim
        heads = range(q_ref.shape[2] // LANES)
        lanes = lambda h: slice(h * LANES, (h + 1) * LANES)
        scores = []
        for h in heads:
            q = q_ref[0, :, lanes(h)] * HEAD_DIM ** -0.5
            kk = k_ref[0, :ext, lanes(h)]
            for c in range(2):
                qc = (jnp.where(low, q, 0.0) if c == 0 else jnp.where(low, 0.0, q)).astype(BF16)
                scores.append(jnp.where(mask, _dot_nt(qc, kk), NEG_INF))
        probs = []
        for sc in scores:
            p = jnp.exp(sc - jnp.max(sc, axis=1, keepdims=True))
            probs.append(p * (1.0 / jnp.sum(p, axis=1, keepdims=True)))
        for h in heads:
            a = probs[2 * h] - lam * probs[2 * h + 1]
            o = _dot(a.astype(BF16), v_ref[0, :ext, lanes(h)])
            o = o * lax.rsqrt(jnp.mean(o * o, axis=-1, keepdims=True) + NORM_EPS)
            o_ref[0, :, lanes(h)] = o * gain_ref[...] * (1.0 - lam_init)

    _for_tile_extent(i, tq, q_pos0, n_keys, extents, body)


def _diff_attend(q, k_all, v_all, diff_lambda, gain, n_keys, q_pos0, lam_init):
    b, t, d = q.shape
    s_len = k_all.shape[1]
    tq = min(t, ATTN_TQ)
    extents = _key_extents(t // tq, tq, q_pos0, n_keys, s_len, 2 * LANES)
    kern = functools.partial(_diff_kernel, tq=tq, extents=extents, n_keys=n_keys, q_pos0=q_pos0, lam_init=lam_init)
    return pl.pallas_call(
        kern,
        grid=(b, d // DIFF_LANES, t // tq),
        in_specs=[pl.BlockSpec((1, tq, DIFF_LANES), lambda bi, h, i: (bi, i, h)),
                  pl.BlockSpec((1, s_len, DIFF_LANES), lambda bi, h, i: (bi, 0, h)),
                  pl.BlockSpec((1, s_len, DIFF_LANES), lambda bi, h, i: (bi, 0, h)),
                  pl.BlockSpec(diff_lambda.shape, lambda bi, h, i: (0, 0)),
                  pl.BlockSpec((1, LANES), lambda bi, h, i: (0, 0))],
        out_specs=pl.BlockSpec((1, tq, DIFF_LANES), lambda bi, h, i: (bi, i, h)),
        out_shape=jax.ShapeDtypeStruct((b, t, d), F32),
        compiler_params=_cparams(("parallel", "parallel", "parallel")),
        name="diff_attend",
    )(q, k_all, v_all, diff_lambda, gain.reshape(1, LANES))


def _route(logits):
    lane = lax.broadcasted_iota(jnp.int32, logits.shape, 1)
    is_group = lane < MOE_GROUPS
    gl = jnp.where(is_group, logits, NEG_INF)
    g_max = jnp.max(gl, axis=1, keepdims=True)
    g_sel = jnp.min(jnp.where(gl == g_max, lane, LANES), axis=1, keepdims=True)
    g_gate = 1.0 / jnp.sum(jnp.where(is_group, jnp.exp(gl - g_max), 0.0), axis=1, keepdims=True)
    in_group = jnp.logical_and(lane >= GATE_COL0, (lane - GATE_COL0) // MOE_EPG == g_sel)
    in_group = jnp.logical_and(in_group, lane < GATE_COL0 + MOE_EXPERTS)
    el = jnp.where(in_group, logits, NEG_INF)
    top1 = jnp.max(el, axis=1, keepdims=True)
    i1 = jnp.min(jnp.where(jnp.logical_and(in_group, el == top1), lane, LANES), axis=1, keepdims=True)
    rest = jnp.logical_and(in_group, lane != i1)
    el2 = jnp.where(rest, logits, NEG_INF)
    top2 = jnp.max(el2, axis=1, keepdims=True)
    i2 = jnp.min(jnp.where(jnp.logical_and(rest, el2 == top2), lane, LANES), axis=1, keepdims=True)
    e2 = jnp.exp(top2 - top1)
    w1 = g_gate / (1.0 + e2)
    gates = jnp.where(lane == i1, w1, jnp.where(lane == i2, w1 * e2, 0.0))
    return jnp.where(lane == GROUP_LANE, g_sel.astype(F32), gates)


def _out_kernel(h_ref, a_ref, b_ref, wa_ref, wb_ref, g_ref, wr_ref, br_ref, o_ref, xn_ref, gate_ref):
    h = (h_ref[...] + _dot(a_ref[...].astype(BF16), wa_ref[...])
         + _dot(b_ref[...].astype(BF16), wb_ref[...]))
    o_ref[...] = h
    xn = _rms(h, g_ref[...]).astype(BF16)
    xn_ref[...] = xn
    gate_ref[...] = _route(_dot(xn, wr_ref[...]) + br_ref[...])


def _out_proj_route(h, a, bmix, w_out, g_ffn, wr, br, tm):
    n, d = h.shape
    ca = a.shape[1]
    wa, wb = w_out[:ca], w_out[ca:]
    row = lambda w: pl.BlockSpec((tm, w), lambda i: (i, 0))
    whole = lambda x: pl.BlockSpec(x.shape, lambda i: (0, 0))
    return pl.pallas_call(
        _out_kernel,
        grid=(n // tm,),
        in_specs=[row(d), row(ca), row(bmix.shape[1]), whole(wa), whole(wb),
                  pl.BlockSpec((1, d), lambda i: (0, 0)), whole(wr), whole(br)],
        out_specs=[row(d), row(d), row(LANES)],
        out_shape=[jax.ShapeDtypeStruct((n, d), F32), jax.ShapeDtypeStruct((n, d), BF16),
                   jax.ShapeDtypeStruct((n, LANES), F32)],
        compiler_params=_cparams(("parallel",)),
        name="out_proj_route",
    )(h, a, bmix, wa, wb, g_ffn.reshape(1, d), wr, br)


def _split3(x):
    hi = x.astype(BF16)
    r1 = x - hi.astype(F32)
    mid = r1.astype(BF16)
    lo = (r1 - mid.astype(F32)).astype(BF16)
    return hi, mid, lo


def _expert_kernel(xn_ref, gate_ref, w1_ref, w3_ref, w2_ref, h_ref, gf_ref, o_ref,
                   acc_ref, xs_ref, gs_ref, pt_ref, tri_ref, seg_ref, *, final_norm, tm, win):
    i = pl.program_id(0)
    e = pl.program_id(1)
    n_sub = xn_ref.shape[0] // tm
    sr = min(SORT_ROWS, tm)
    tile_rows = lambda s: pl.ds(pl.multiple_of(s * tm, tm), tm)

    @pl.when(jnp.logical_and(i == 0, e == 0))
    def _():
        rr = lax.broadcasted_iota(jnp.int32, (tm, tm), 0)
        cc = lax.broadcasted_iota(jnp.int32, (tm, tm), 1)
        tri_ref[...] = jnp.where(cc < rr, 1.0, 0.0).astype(BF16)

    def sort_tile(s, carry):
        gate = gate_ref[tile_rows(s), :]
        lane = lax.broadcasted_iota(jnp.int32, gate.shape, 1)
        lane_row = lax.broadcasted_iota(jnp.int32, (1, LANES), 1)
        in_grp = jnp.logical_and(lane < MOE_GROUPS, lane.astype(F32) == gate[:, GROUP_LANE:GROUP_LANE + 1])
        onehot = jnp.where(in_grp, 1.0, 0.0)
        before = _dot(tri_ref[...], onehot.astype(BF16))
        count = jnp.sum(onehot, axis=0, keepdims=True)
        first = jnp.zeros((1, LANES), F32)
        start = jnp.float32(0.0)
        for g in range(MOE_GROUPS):
            n_g = jnp.sum(jnp.where(lane_row == g, count, 0.0))
            seg_ref[s, g] = start.astype(jnp.int32)
            seg_ref[s, MOE_GROUPS + g] = n_g.astype(jnp.int32)
            first = jnp.where(lane_row == g, start, first)
            start = start + n_g
        dest = jnp.sum(onehot * (before + first), axis=1, keepdims=True).astype(jnp.int32)
        g_hi, g_mid, g_lo = _split3(gate)
        xn = xn_ref[tile_rows(s), :]
        for r in range(0, tm, sr):
            rs = slice(r, r + sr)
            pt_ref[s, rs, :] = jnp.where(lax.broadcasted_iota(jnp.int32, (sr, tm), 1) == dest[rs],
                                         1.0, 0.0).astype(BF16)
        for r in range(0, tm, sr):
            rs = slice(r, r + sr)
            pt_cols = pt_ref[s, :, rs]
            xs_ref[s, rs, :] = _dot_tn(pt_cols, xn).astype(BF16)
            gs_ref[s, rs, :] = _dot_tn(pt_cols, g_hi) + _dot_tn(pt_cols, g_mid) + _dot_tn(pt_cols, g_lo)
        xs_ref[s, tm:, :] = jnp.zeros((win, xs_ref.shape[2]), BF16)
        gs_ref[s, tm:, :] = jnp.zeros((win, LANES), F32)
        acc_ref[s] = jnp.zeros(acc_ref.shape[1:], F32)
        return carry

    @pl.when(e == 0)
    def _():
        lax.fori_loop(0, n_sub, sort_tile, 0)

    grp = e // MOE_EPG
    for s in range(n_sub):
        seg_first = seg_ref[s, grp]
        seg_rows = seg_ref[s, MOE_GROUPS + grp]
        row0 = (seg_first // BF16_ROWS) * BF16_ROWS
        n_win = (seg_first + seg_rows - row0 + win - 1) // win

        def window(w, carry, s=s, row0=row0):
            rows = pl.ds(pl.multiple_of(row0 + w * win, BF16_ROWS), win)
            x = xs_ref[s, rows, :]
            gsw = gs_ref[s, rows, :]
            lane = lax.broadcasted_iota(jnp.int32, gsw.shape, 1)
            ge = jnp.sum(jnp.where(lane == GATE_COL0 + e, gsw, 0.0), axis=1, keepdims=True)
            a = _dot(x, w1_ref[0])
            b = _dot(x, w3_ref[0])
            act = (a * (1.0 / (1.0 + jnp.exp(-a))) * b).astype(BF16)
            acc_ref[s, rows, :] += ge * _dot(act, w2_ref[0])
            return carry

        lax.fori_loop(0, n_win, window, 0)

    def unsort_tile(s, carry):
        a_hi, a_mid, a_lo = _split3(acc_ref[s, 0:tm, :])
        for r in range(0, tm, sr):
            rows = pl.ds(pl.multiple_of(s * tm + r, sr), sr)
            pt = pt_ref[s, r:r + sr, :]
            y = h_ref[rows, :] + (_dot(pt, a_hi) + _dot(pt, a_mid) + _dot(pt, a_lo))
            o_ref[rows, :] = _rms(y, gf_ref[...]) if final_norm else y
        return carry

    @pl.when(e == pl.num_programs(1) - 1)
    def _():
        lax.fori_loop(0, n_sub, unsort_tile, 0)


def _moe(h, xn, gate, w1, w3, w2, g_final, final_norm, tm):
    n, d = h.shape
    n_e = w1.shape[0]
    tme = MOE_TM if n % MOE_TM == 0 else tm
    n_sub = MOE_SUBTILES if n % (MOE_SUBTILES * tme) == 0 else 1
    blk = n_sub * tme
    win = -(-(tme * 5 // 16) // BF16_ROWS) * BF16_ROWS
    row2 = lambda w: pl.BlockSpec((blk, w), lambda i, e: (i, 0), pipeline_mode=pl.Buffered(1))
    wspec = lambda w: pl.BlockSpec((1,) + w.shape[1:], lambda i, e: (e, 0, 0))
    kern = functools.partial(_expert_kernel, final_norm=final_norm, tm=tme, win=win)
    return pl.pallas_call(
        kern,
        grid=(n // blk, n_e),
        in_specs=[row2(d), row2(LANES),
                  wspec(w1), wspec(w3), wspec(w2),
                  row2(d), pl.BlockSpec((1, d), lambda i, e: (0, 0))],
        out_specs=row2(d),
        out_shape=jax.ShapeDtypeStruct((n, d), F32),
        scratch_shapes=[pltpu.VMEM((n_sub, tme + win, d), F32), pltpu.VMEM((n_sub, tme + win, d), BF16),
                        pltpu.VMEM((n_sub, tme + win, LANES), F32), pltpu.VMEM((n_sub, tme, tme), BF16),
                        pltpu.VMEM((tme, tme), BF16), pltpu.SMEM((n_sub, 2 * MOE_GROUPS), jnp.int32)],
        compiler_params=pltpu.CompilerParams(dimension_semantics=("arbitrary", "arbitrary"),
                                             vmem_limit_bytes=MOE_VMEM_LIMIT),
        name="moe_experts",
    )(xn, gate, w1, w3, w2, h, g_final.reshape(1, d))


EVEN_SEGS = ((0, 512, None, False), (512, 512, "full", False), (1024, 128, "full", True), (1152, 128, None, True),
             (1280, 256, "full", False), (1536, 128, "half", False))
ODD_SEGS = ((0, 512, None, False), (512, 512, None, True), (1024, 512, None, True),
            (1536, 512, "full", False), (2048, 512, "full", True), (2560, 512, None, True))


def _cat_keys(hist, new):
    allk = jnp.concatenate([hist, new], axis=1) if hist is not None else new
    n_keys = allk.shape[1]
    pad = -n_keys % LANES
    if pad:
        allk = jnp.pad(allk, ((0, 0), (0, pad), (0, 0)))
    return allk.astype(BF16), n_keys


def kernel(x_prompt, x_sample, cache_pool, cache_dsa_k, cache_dsa_v, cache_idx_k, cache_sb_k, cache_sb_v,
           cache_diff_k, cache_diff_v, norm_mix, norm_ffn, norm_final, w_in_even, w_pool, pool_scale,
           w_out_even, w_in_odd, diff_lambda, diff_subln, w_out_odd, moe_w_group, moe_b_group,
           moe_w_expert, moe_b_expert, moe_w1, moe_w3, moe_w2):
    b, t, d = x_prompt.shape
    bd, td, _ = x_sample.shape
    past = cache_dsa_k.shape[2]
    depth = norm_mix.shape[0]
    groups = ((b, t, 0, min(512, b * t)), (bd, td, past, bd * td))

    tabs = []
    for (gb, gt, p0, tm) in groups:
        tab = _rope_tables(p0 + jnp.arange(gt, dtype=jnp.int32))
        if tm > gt:
            tab = jnp.tile(tab, (tm // gt, 1))
        tabs.append(tab)

    hs = [x_prompt.reshape(b * t, d), x_sample.reshape(bd * td, d)]
    outs = [dict(), dict()]
    for l in range(depth):
        li = l // 2
        last = l == depth - 1
        if l % 2 == 0:
            n_in = w_in_even.shape[2]
            w_in = jnp.pad(w_in_even[li], ((0, 0), (0, -n_in % LANES))).astype(BF16)
            w_out = w_out_even[li].astype(BF16)
            wp = w_pool[li].astype(BF16)
        else:
            w_in = w_in_odd[li].astype(BF16)
            w_out = w_out_odd[li].astype(BF16)
            lam_init = 0.8 - 0.6 * math.exp(-0.3 * l)
        wr = jnp.concatenate([moe_w_group[l]] + [moe_w_expert[l, g] for g in range(MOE_GROUPS)], axis=1)
        wr = jnp.pad(wr, ((0, 0), (0, LANES - wr.shape[1]))).astype(BF16)
        br = jnp.concatenate([moe_b_group[l], moe_b_expert[l].reshape(-1)])
        br = jnp.pad(br, (0, LANES - br.shape[0])).reshape(1, LANES).astype(F32)
        w1, w3, w2 = moe_w1[l].astype(BF16), moe_w3[l].astype(BF16), moe_w2[l].astype(BF16)

        for gi, (gb, gt, p0, tm) in enumerate(groups):
            h = hs[gi]
            o = outs[gi]
            sample = gi == 1
            r3 = lambda x: x.reshape(gb, gt, x.shape[-1])
            if l % 2 == 0:
                u, q, k, v, qi, kiwi, k16, v16 = [
                    r3(x) for x in _project(h, norm_mix[l], w_in, tabs[gi], EVEN_SEGS, tm)]
                ki = kiwi[..., :HEAD_DIM]
                hist = cache_pool[li] if sample else jnp.zeros((gb, POOL_HIST, u.shape[2]), F32)
                a_out = _pool_mix(u, hist, wp, pool_scale[li], p0)
                if sample:
                    k_all, n_keys = _cat_keys(cache_dsa_k[li].reshape(gb, past, -1), k)
                    v_all, _ = _cat_keys(cache_dsa_v[li].reshape(gb, past, -1), v)
                    ki_all, _ = _cat_keys(cache_idx_k[li], ki)
                else:
                    (k_all, n_keys), (v_all, _), (ki_all, _) = _cat_keys(None, k16), _cat_keys(None, v16), _cat_keys(None, ki)
                b_out = _dsa(q, qi, kiwi, k_all, v_all, ki_all, n_keys, p0)
                o.setdefault("pool", []).append(jnp.concatenate([hist, u], axis=1)[:, -POOL_HIST:])
                o.setdefault("dsa_k", []).append(k.reshape(gb, gt, -1, HEAD_DIM))
                o.setdefault("dsa_v", []).append(v.reshape(gb, gt, -1, HEAD_DIM))
                o.setdefault("idx_k", []).append(ki)
                mix_a, mix_b = a_out, b_out
            else:
                sq, sk, sv, dq, dk, dv, sk16, sv16, dk16, dv16 = [
                    r3(x) for x in _project(h, norm_mix[l], w_in, tabs[gi], ODD_SEGS, tm)]
                if sample:
                    sk_all, n_keys = _cat_keys(cache_sb_k[li].reshape(gb, past, -1), sk)
                    sv_all, _ = _cat_keys(cache_sb_v[li].reshape(gb, past, -1), sv)
                    dk_all, _ = _cat_keys(cache_diff_k[li].reshape(gb, past, -1), dk)
                    dv_all, _ = _cat_keys(cache_diff_v[li].reshape(gb, past, -1), dv)
                else:
                    (sk_all, n_keys), (sv_all, _) = _cat_keys(None, sk16), _cat_keys(None, sv16)
                    (dk_all, _), (dv_all, _) = _cat_keys(None, dk16), _cat_keys(None, dv16)
                c_out = _sb_attend(sq, sk_all, sv_all, n_keys, p0)
                d_out = _diff_attend(dq, dk_all, dv_all, diff_lambda[li], diff_subln[li], n_keys, p0, lam_init)
                n_sb = sk.shape[2] // HEAD_DIM
                n_df = dk.shape[2] // (2 * HEAD_DIM)
                o.setdefault("sb_k", []).append(sk.reshape(gb, gt, n_sb, HEAD_DIM))
                o.setdefault("sb_v", []).append(sv.reshape(gb, gt, n_sb, HEAD_DIM))
                o.setdefault("diff_k", []).append(dk.reshape(gb, gt, n_df, 2, HEAD_DIM))
                o.setdefault("diff_v", []).append(dv.reshape(gb, gt, n_df, 2 * HEAD_DIM))
                mix_a, mix_b = c_out, d_out
            h, xn, gate = _out_proj_route(h, mix_a.reshape(gb * gt, -1), mix_b.reshape(gb * gt, -1), w_out,
                                          norm_ffn[l], wr, br, tm)
            hs[gi] = _moe(h, xn, gate, w1, w3, w2, norm_final, last, tm)

    names = ("pool", "dsa_k", "dsa_v", "idx_k", "sb_k", "sb_v", "diff_k", "diff_v")
    res = [hs[0].reshape(b, t, d), hs[1].reshape(bd, td, d)]
    for o in outs:
        res += [jnp.stack(o[nm]) for nm in names]
    return tuple(res)
```

```python
import functools
import math

import jax
import jax.numpy as jnp
import numpy as np
from jax import lax
from jax.experimental import pallas as pl
from jax.experimental.pallas import tpu as pltpu

F32 = jnp.float32
BF16 = jnp.bfloat16

LANES = 128
HEAD_DIM = 64
CHUNK = 64
ROPE_THETA = 10000.0
NORM_EPS = 1e-6
NEG_INF = -1e30
PAD_SCORE = -3e38
BIG_POS = 3e38
POOL_WINDOWS = (2, 4, 8, 16)
POOL_HIST = 15
POOL_HIST_PAD = 16
DSA_TOPK = 256
IDX_HEADS = 4
MOE_GROUPS = 4
MOE_EPG = 4
MOE_EXPERTS = 16
GATE_COL0 = MOE_GROUPS
GROUP_LANE = 0
BF16_ROWS = 16
SORT_ROWS = 256
VMEM_LIMIT = 56 * 1024 * 1024
BISECT_STEPS = 8
BISECT_ROUNDS = 48
BISECT_UNCHECKED = 2
ATTN_TQ = 256
DSA_TQ = 128
MOE_TM = 1024
MOE_SUBTILES = 2
MOE_VMEM_LIMIT = 62 * 1024 * 1024
DIFF_LANES = 2 * LANES


def _cparams(sem):
    return pltpu.CompilerParams(dimension_semantics=sem, vmem_limit_bytes=VMEM_LIMIT)


def _dot(a, b):
    return jnp.dot(a, b, preferred_element_type=F32)


def _dot_nt(a, b):
    return lax.dot_general(a, b, (((1,), (1,)), ((), ())), preferred_element_type=F32)


def _dot_tn(a, b):
    return lax.dot_general(a, b, (((0,), (0,)), ((), ())), preferred_element_type=F32)


def _rms(x, g):
    ms = jnp.mean(x * x, axis=-1, keepdims=True)
    return x * lax.rsqrt(ms + NORM_EPS) * g


def _proj_kernel(x_ref, g_ref, w_ref, tab_ref, *out_refs, segs):
    xn = _rms(x_ref[...], g_ref[...]).astype(BF16)
    copies = iter(out_refs[len(segs):])
    for o_ref, (c0, width, mode, twin) in zip(out_refs, segs):
        t_ref = next(copies) if twin else None
        y = _dot(xn, w_ref[:, c0:c0 + width])
        if mode is None:
            o_ref[...] = y
            if twin:
                t_ref[...] = y.astype(BF16)
            continue
        t0 = 0 if mode == "full" else 3 * LANES
        cos = tab_ref[:, t0:t0 + LANES]
        sin_a = tab_ref[:, t0 + LANES:t0 + 2 * LANES]
        sin_b = tab_ref[:, t0 + 2 * LANES:t0 + 3 * LANES]
        for c in range(0, width, LANES):
            yc = y[:, c:c + LANES]
            yr = (yc * cos + pltpu.roll(yc, LANES - HEAD_DIM // 2, 1) * sin_a
                  + pltpu.roll(yc, HEAD_DIM // 2, 1) * sin_b)
            o_ref[:, c:c + LANES] = yr
            if twin:
                t_ref[:, c:c + LANES] = yr.astype(BF16)


def _rope_tables(pos):
    half = HEAD_DIM // 2
    inv = ROPE_THETA ** (-jnp.arange(half, dtype=F32) / half)
    ang = pos.astype(F32)[:, None] * inv[None, :]
    cos, sin = jnp.cos(ang), jnp.sin(ang)
    zero, one = jnp.zeros_like(sin), jnp.ones_like(cos)
    cos_h = jnp.concatenate([cos, cos], axis=1)
    sa_h = jnp.concatenate([-sin, zero], axis=1)
    sb_h = jnp.concatenate([zero, sin], axis=1)
    one_h = jnp.concatenate([one, one], axis=1)
    zero_h = jnp.concatenate([zero, zero], axis=1)
    return jnp.concatenate([cos_h, cos_h, sa_h, sa_h, sb_h, sb_h,
                            cos_h, one_h, sa_h, zero_h, sb_h, zero_h], axis=1)


def _project(x, g, w, tab, segs, tm):
    n, d = x.shape
    tt = tab.shape[0]
    nt = tt // tm
    kern = functools.partial(_proj_kernel, segs=segs)
    return pl.pallas_call(
        kern,
        grid=(n // tm,),
        in_specs=[pl.BlockSpec((tm, d), lambda i: (i, 0)),
                  pl.BlockSpec((1, d), lambda i: (0, 0)),
                  pl.BlockSpec(w.shape, lambda i: (0, 0)),
                  pl.BlockSpec((tm, tab.shape[1]), lambda i: (i % nt, 0))],
        out_specs=[pl.BlockSpec((tm, wd), lambda i: (i, 0)) for _, wd, _, _ in segs]
        + [pl.BlockSpec((tm, wd), lambda i: (i, 0)) for _, wd, _, twin in segs if twin],
        out_shape=[jax.ShapeDtypeStruct((n, wd), F32) for _, wd, _, _ in segs]
        + [jax.ShapeDtypeStruct((n, wd), BF16) for _, wd, _, twin in segs if twin],
        compiler_params=_cparams(("parallel",)),
        name="proj",
    )(x, g.reshape(1, d), w, tab)


def _pool_kernel(u_ref, h_ref, w_ref, s_ref, o_ref, ext_ref, *, t, pos0, rc):
    ext_ref[0:POOL_HIST_PAD, :] = h_ref[0]
    ext_ref[POOL_HIST_PAD:POOL_HIST_PAD + t, :] = u_ref[0]
    for r0 in range(0, t, rc):
        pos = pos0 + r0 + lax.broadcasted_iota(jnp.int32, (rc, 1), 0)
        for g, win in enumerate(POOL_WINDOWS):
            c0 = g * LANES
            u_new = ext_ref[POOL_HIST_PAD + r0:POOL_HIST_PAD + r0 + rc, c0:c0 + LANES]
            s = u_new
            for k in range(1, win):
                s = s + ext_ref[POOL_HIST_PAD + r0 - k:POOL_HIST_PAD + r0 - k + rc, c0:c0 + LANES]
            cnt = jnp.minimum(pos + 1, win).astype(F32)
            dlt = (s / cnt - u_new).astype(BF16)
            o_ref[0, r0:r0 + rc, c0:c0 + LANES] = _dot(dlt, w_ref[g]) * s_ref[:, c0:c0 + LANES]


def _pool_mix(u, hist, w_pool, pool_scale, pos0):
    b, t, c = u.shape
    rc = min(t, 256)
    hist16 = jnp.pad(hist, ((0, 0), (POOL_HIST_PAD - POOL_HIST, 0), (0, 0)))
    kern = functools.partial(_pool_kernel, t=t, pos0=pos0, rc=rc)
    return pl.pallas_call(
        kern,
        grid=(b,),
        in_specs=[pl.BlockSpec((1, t, c), lambda i: (i, 0, 0)),
                  pl.BlockSpec((1, POOL_HIST_PAD, c), lambda i: (i, 0, 0)),
                  pl.BlockSpec(w_pool.shape, lambda i: (0, 0, 0)),
                  pl.BlockSpec((1, c), lambda i: (0, 0))],
        out_specs=pl.BlockSpec((1, t, c), lambda i: (i, 0, 0)),
        out_shape=jax.ShapeDtypeStruct((b, t, c), F32),
        scratch_shapes=[pltpu.VMEM((POOL_HIST_PAD + t, c), F32)],
        compiler_params=_cparams(("parallel",)),
        name="pool_mix",
    )(u, hist16, w_pool, pool_scale.reshape(1, c))


def _dsa_kernel(q_ref, qi_ref, kw_ref, k_ref, v_ref, ki_ref, o_ref, lo_ref, hi_ref, bias_ref,
                *, tq, extents, n_keys, q_pos0, n_sel):
    i = pl.program_id(1)
    qpos = q_pos0 + i * tq + lax.broadcasted_iota(jnp.int32, (tq, 1), 0)
    lim = jnp.minimum((qpos // CHUNK + 1) * CHUNK, n_keys)
    kf = float(n_sel)
    low = lax.broadcasted_iota(jnp.int32, (tq, LANES), 1) < HEAD_DIM
    q = q_ref[0] * HEAD_DIM ** -0.5
    qi = qi_ref[0].astype(BF16)
    wi = kw_ref[0][:, HEAD_DIM:HEAD_DIM + IDX_HEADS] * (IDX_HEADS * HEAD_DIM) ** -0.5

    def body(ext):
        kpos = lax.broadcasted_iota(jnp.int32, (tq, ext), 1)
        adm = kpos < lim
        padded = ext > n_keys
        virt = float(max(n_keys - ext, 0))

        sidx = jnp.zeros((tq, ext), F32)
        ki = ki_ref[0, :ext, :]
        for h in range(IDX_HEADS):
            sh = _dot_nt(qi[:, h * HEAD_DIM:(h + 1) * HEAD_DIM], ki)
            sidx = sidx + jnp.maximum(sh, 0.0) * wi[:, h:h + 1]
        sm = jnp.where(adm, sidx, NEG_INF)
        if padded:
            real = kpos < n_keys
            sm = jnp.where(real, sm, PAD_SCORE)

        def count_gt_wide(x):
            part = jnp.where(sm[:, :LANES] > x, 1.0, 0.0)
            for j in range(1, ext // LANES):
                part = part + jnp.where(sm[:, j * LANES:(j + 1) * LANES] > x, 1.0, 0.0)
            cnt = jnp.broadcast_to(jnp.sum(part, axis=1, keepdims=True), (tq, LANES))
            return cnt + jnp.where(x < NEG_INF, virt, 0.0) if virt else cnt

        def count_gt(x):
            return count_gt_wide(jnp.broadcast_to(x, (tq, LANES)))[:, :1]

        def bracket(lo, hi):
            above = jnp.min(jnp.where(sm > lo, sm, BIG_POS), axis=1, keepdims=True)
            below = jnp.max(jnp.where(sm <= hi, sm, PAD_SCORE), axis=1, keepdims=True)
            if virt:
                above = jnp.minimum(above, jnp.where(lo < NEG_INF, NEG_INF, BIG_POS))
                below = jnp.maximum(below, jnp.where(hi >= NEG_INF, NEG_INF, PAD_SCORE))
            return above, below

        row_max = jnp.max(sm, axis=1, keepdims=True)
        row_min = jnp.min(jnp.where(real, sm, BIG_POS) if padded else sm, axis=1, keepdims=True)
        if virt:
            row_min = jnp.minimum(row_min, NEG_INF)
        adm_min = jnp.min(jnp.where(adm, sm, BIG_POS), axis=1, keepdims=True)
        few = count_gt(row_min) < kf
        tight = count_gt(adm_min) >= kf
        lo_ref[...] = jnp.where(few, PAD_SCORE, jnp.where(tight, adm_min, row_min))
        hi_ref[...] = jnp.where(few, row_min, jnp.where(tight, row_max, adm_min))

        def unresolved(lo, hi):
            above, below = bracket(lo, hi)
            return jnp.sum(jnp.where(above < below, 1, 0))

        def cond(carry):
            rounds, open_rows = carry
            return jnp.logical_and(open_rows > 0, rounds < BISECT_ROUNDS)

        def step(carry):
            rounds, _ = carry
            lo = jnp.broadcast_to(lo_ref[...], (tq, LANES))
            hi = jnp.broadcast_to(hi_ref[...], (tq, LANES))
            for _ in range(BISECT_STEPS):
                mid = 0.5 * lo + 0.5 * hi
                under = count_gt_wide(mid) < kf
                hi = jnp.where(under, mid, hi)
                lo = jnp.where(under, lo, mid)
            lo_ref[...] = lo[:, :1]
            hi_ref[...] = hi[:, :1]
            open_rows = lax.cond(rounds + 1 >= BISECT_UNCHECKED,
                                 lambda: unresolved(lo_ref[...], hi_ref[...]), lambda: jnp.int32(1))
            return rounds + 1, open_rows

        lax.while_loop(cond, step, (jnp.int32(0), jnp.int32(1)))
        _, thr = bracket(lo_ref[...], hi_ref[...])

        gt = sm > thr
        eq = sm == thr
        need = kf - count_gt(thr)
        n_eq = jnp.sum(jnp.where(eq, 1.0, 0.0), axis=1, keepdims=True)
        crowded = jnp.sum(jnp.where(n_eq > need, 1, 0))

        @pl.when(crowded == 0)
        def _():
            keep = jnp.logical_and(jnp.logical_or(gt, eq), adm)
            bias_ref[:, :ext] = jnp.where(keep, 0.0, NEG_INF)

        @pl.when(crowded > 0)
        def _():
            rr = lax.broadcasted_iota(jnp.int32, (LANES, LANES), 0)
            cc = lax.broadcasted_iota(jnp.int32, (LANES, LANES), 1)
            prefix_ones = jnp.where(rr <= cc, 1.0, 0.0).astype(BF16)
            carry = jnp.zeros((tq, 1), F32)
            for j in range(ext // LANES):
                sl = slice(j * LANES, (j + 1) * LANES)
                eq_j = eq[:, sl]
                rank = _dot(jnp.where(eq_j, 1.0, 0.0).astype(BF16), prefix_ones) + carry
                keep = jnp.logical_or(gt[:, sl], jnp.logical_and(eq_j, rank <= need))
                bias_ref[:, sl] = jnp.where(jnp.logical_and(keep, adm[:, sl]), 0.0, NEG_INF)
                carry = rank[:, LANES - 1:LANES]

        bias = bias_ref[:, :ext]
        bias2 = jnp.concatenate([bias, bias], axis=0)

        kk = k_ref[0, :ext, :]
        vv = v_ref[0, :ext, :]
        low_k = lax.broadcasted_iota(jnp.int32, (ext, LANES), 1) < HEAD_DIM
        vsw = pltpu.roll(vv.astype(F32), HEAD_DIM, 1).astype(BF16)
        one = jnp.ones_like(vv)
        n_kv = LANES // HEAD_DIM
        stacks = [(g, odd) for g in range(n_kv) for odd in range(2)]
        scores = []
        for g, odd in stacks:
            rows = []
            for m in range(2):
                c = 2 * g + m
                qc = q[:, c * LANES:(c + 1) * LANES]
                if (odd == 1) != (g == 1):
                    qc = pltpu.roll(qc, HEAD_DIM, 1)
                rows.append(jnp.where(low, qc, 0.0) if g == 0 else jnp.where(low, 0.0, qc))
            qs = jnp.concatenate(rows, axis=0).astype(BF16)
            scores.append(_dot_nt(qs, kk) + bias2)
        probs = [jnp.exp(sc - jnp.max(sc, axis=1, keepdims=True)).astype(BF16) for sc in scores]
        outs = []
        for (g, odd), p in zip(stacks, probs):
            if odd:
                v_aug = jnp.where(low_k, one, vsw if g == 0 else vv)
            else:
                v_aug = jnp.where(low_k, vv if g == 0 else vsw, one)
            og = _dot(p, v_aug)
            outs.append(og / pltpu.roll(og, HEAD_DIM, 1))
        for g in range(n_kv):
            for m in range(2):
                c = 2 * g + m
                o_ref[0, :, c * LANES:(c + 1) * LANES] = jnp.where(
                    low, outs[2 * g][m * tq:(m + 1) * tq], outs[2 * g + 1][m * tq:(m + 1) * tq])

    _for_tile_extent(i, tq, q_pos0, n_keys, extents, body)


def _dsa(q, qi, kiwi, k_all, v_all, ki_all, n_keys, q_pos0):
    b, t, dq = q.shape
    s_len = k_all.shape[1]
    tq = min(t, DSA_TQ)
    n_sel = min(DSA_TOPK, n_keys // 4)
    extents = _key_extents(t // tq, tq, q_pos0, n_keys, s_len, 4 * LANES)
    kern = functools.partial(_dsa_kernel, tq=tq, extents=extents, n_keys=n_keys, q_pos0=q_pos0, n_sel=n_sel)
    qspec = lambda w: pl.BlockSpec((1, tq, w), lambda bi, i: (bi, i, 0))
    kspec = lambda w: pl.BlockSpec((1, s_len, w), lambda bi, i: (bi, 0, 0))
    return pl.pallas_call(
        kern,
        grid=(b, t // tq),
        in_specs=[qspec(dq), qspec(qi.shape[2]), qspec(kiwi.shape[2]),
                  kspec(k_all.shape[2]), kspec(v_all.shape[2]), kspec(ki_all.shape[2])],
        out_specs=qspec(dq),
        out_shape=jax.ShapeDtypeStruct((b, t, dq), F32),
        scratch_shapes=[pltpu.VMEM((tq, 1), F32), pltpu.VMEM((tq, 1), F32), pltpu.VMEM((tq, s_len), F32)],
        compiler_params=_cparams(("parallel", "parallel")),
        name="dsa",
    )(q, qi, kiwi, k_all, v_all, ki_all)


def _sb_kernel(q_ref, k_ref, v_ref, o_ref, acc_ref, run_ref, *, tq, n_keys, q_pos0):
    i = pl.program_id(1)
    pairs = q_ref.shape[2] // LANES
    first_q = q_pos0 + i * tq
    qpos = first_q + lax.broadcasted_iota(jnp.int32, (tq, 1), 0)
    n_blocks = (jnp.minimum(first_q + tq - 1, n_keys) + LANES - 1) // LANES
    n_full = jnp.minimum(first_q, n_keys) // LANES
    low_q = lax.broadcasted_iota(jnp.int32, (tq, LANES), 1) < HEAD_DIM
    low_k = lax.broadcasted_iota(jnp.int32, (LANES, LANES), 1) < HEAD_DIM
    q = q_ref[0] * HEAD_DIM ** -0.5
    qm = []
    for p in range(pairs):
        qp = q[:, p * LANES:(p + 1) * LANES]
        qm.append((jnp.where(low_q, qp, 0.0).astype(BF16), jnp.where(low_q, 0.0, qp).astype(BF16)))
    rr = lax.broadcasted_iota(jnp.int32, (2 * LANES, 2 * LANES), 0)
    cc = lax.broadcasted_iota(jnp.int32, (2 * LANES, 2 * LANES), 1)
    rk = jnp.where(rr >= LANES, rr - LANES, rr)
    cs_rhs = jnp.where(jnp.logical_or(cc >= LANES, rk >= cc), 1.0, 0.0).astype(BF16)
    acc_ref[...] = jnp.zeros_like(acc_ref)
    run_ref[...] = jnp.zeros_like(run_ref)

    def block(j, masked):
        ks = pl.multiple_of(j * LANES, LANES)
        if masked:
            kpos = ks + lax.broadcasted_iota(jnp.int32, (1, LANES), 1)
            causal = jnp.logical_and(kpos < qpos, kpos < n_keys)
        heads = [(p, c) for p in range(pairs) for c in range(2)]
        lanes = lambda n: slice(n * LANES, (n + 1) * LANES)
        zs = [_dot_nt(qm[p][c], k_ref[0, pl.ds(ks, LANES), lanes(p)]) for p, c in heads]
        css = []
        for z in zs:
            sp = jnp.maximum(z, 0.0) + jnp.log(1.0 + jnp.exp(-jnp.abs(z)))
            if masked:
                sp = jnp.where(causal, sp, 0.0)
            hi = sp.astype(BF16)
            lo = (sp - hi.astype(F32)).astype(BF16)
            css.append(_dot(jnp.concatenate([hi, lo], axis=1), cs_rhs))
        probs = []
        for n, (z, cs) in enumerate(zip(zs, css)):
            run = run_ref[:, lanes(n)]
            a = jnp.exp(z - cs[:, :LANES] - run)
            if masked:
                a = jnp.where(causal, a, 0.0)
            run_ref[:, lanes(n)] = run + cs[:, LANES:]
            probs.append(a.astype(BF16))
        for p in range(pairs):
            vb = v_ref[0, pl.ds(ks, LANES), lanes(p)]
            zero = jnp.zeros_like(vb)
            v_cat = jnp.concatenate([jnp.where(low_k, vb, zero), jnp.where(low_k, zero, vb)], axis=0)
            acc_ref[:, lanes(p)] += _dot(jnp.concatenate(probs[2 * p:2 * p + 2], axis=1), v_cat)

    def masked_step(jj, carry):
        block(n_blocks - 1 - jj, True)
        return carry

    def full_pair(jj, carry):
        block(n_full - 1 - 2 * jj, False)
        block(n_full - 2 - 2 * jj, False)
        return carry

    lax.fori_loop(0, n_blocks - n_full, masked_step, 0)
    lax.fori_loop(0, n_full // 2, full_pair, 0)

    @pl.when(n_full % 2 == 1)
    def _():
        block(jnp.int32(0), False)

    o_ref[0] = acc_ref[...]


def _sb_attend(q, k_all, v_all, n_keys, q_pos0):
    b, t, d = q.shape
    s_len = k_all.shape[1]
    tq = min(t, ATTN_TQ)
    kern = functools.partial(_sb_kernel, tq=tq, n_keys=n_keys, q_pos0=q_pos0)
    return pl.pallas_call(
        kern,
        grid=(b, t // tq),
        in_specs=[pl.BlockSpec((1, tq, d), lambda bi, i: (bi, i, 0)),
                  pl.BlockSpec((1, s_len, d), lambda bi, i: (bi, 0, 0)),
                  pl.BlockSpec((1, s_len, d), lambda bi, i: (bi, 0, 0))],
        out_specs=pl.BlockSpec((1, tq, d), lambda bi, i: (bi, i, 0)),
        out_shape=jax.ShapeDtypeStruct((b, t, d), F32),
        scratch_shapes=[pltpu.VMEM((tq, d), F32), pltpu.VMEM((tq, 2 * d), F32)],
        compiler_params=_cparams(("parallel", "parallel")),
        name="sb_attend",
    )(q, k_all, v_all)


def _key_extents(nq, tq, q_pos0, n_keys, s_len, step):
    need = [min(n_keys, ((q_pos0 + (i + 1) * tq - 1) // CHUNK + 1) * CHUNK) for i in range(nq)]
    return tuple(sorted({min(s_len, -(-n // step) * step) for n in need}))


def _for_tile_extent(i, tq, q_pos0, n_keys, extents, body):
    need = jnp.minimum(n_keys, ((q_pos0 + (i + 1) * tq - 1) // CHUNK + 1) * CHUNK)
    prev = 0
    for ext in extents:
        pl.when(jnp.logical_and(need > prev, need <= ext))(functools.partial(body, ext))
        prev = ext


def _diff_kernel(q_ref, k_ref, v_ref, lam_ref, gain_ref, o_ref, *, tq, extents, n_keys, q_pos0, lam_init):
    i = pl.program_id(2)
    lp = lam_ref[...]
    lam = (jnp.exp(jnp.sum(lp[0:1] * lp[1:2], axis=1, keepdims=True))
           - jnp.exp(jnp.sum(lp[2:3] * lp[3:4], axis=1, keepdims=True)) + lam_init)
    qpos = q_pos0 + i * tq + lax.broadcasted_iota(jnp.int32, (tq, 1), 0)
    lim = jnp.minimum((qpos // CHUNK + 1) * CHUNK, n_keys)
    low = lax.broadcasted_iota(jnp.int32, (tq, LANES), 1) < HEAD_DIM

    def body(ext):
        mask = lax.broadcasted_iota(jnp.int32, (tq, ext), 1) < lim
        heads = range(q_ref.shape[2] // LANES)
        lanes = lambda h: slice(h * LANES, (h + 1) * LANES)
        scores = []
        for h in heads:
            q = q_ref[0, :, lanes(h)] * HEAD_DIM ** -0.5
            kk = k_ref[0, :ext, lanes(h)]
            for c in range(2):
                qc = (jnp.where(low, q, 0.0) if c == 0 else jnp.where(low, 0.0, q)).astype(BF16)
                scores.append(jnp.where(mask, _dot_nt(qc, kk), NEG_INF))
        probs = []
        for sc in scores:
            p = jnp.exp(sc - jnp.max(sc, axis=1, keepdims=True))
            probs.append(p * (1.0 / jnp.sum(p, axis=1, keepdims=True)))
        for h in heads:
            a = probs[2 * h] - lam * probs[2 * h + 1]
            o = _dot(a.astype(BF16), v_ref[0, :ext, lanes(h)])
            o = o * lax.rsqrt(jnp.mean(o * o, axis=-1, keepdims=True) + NORM_EPS)
            o_ref[0, :, lanes(h)] = o * gain_ref[...] * (1.0 - lam_init)

    _for_tile_extent(i, tq, q_pos0, n_keys, extents, body)


def _diff_attend(q, k_all, v_all, diff_lambda, gain, n_keys, q_pos0, lam_init):
    b, t, d = q.shape
    s_len = k_all.shape[1]
    tq = min(t, ATTN_TQ)
    extents = _key_extents(t // tq, tq, q_pos0, n_keys, s_len, 2 * LANES)
    kern = functools.partial(_diff_kernel, tq=tq, extents=extents, n_keys=n_keys, q_pos0=q_pos0, lam_init=lam_init)
    return pl.pallas_call(
        kern,
        grid=(b, d // DIFF_LANES, t // tq),
        in_specs=[pl.BlockSpec((1, tq, DIFF_LANES), lambda bi, h, i: (bi, i, h)),
                  pl.BlockSpec((1, s_len, DIFF_LANES), lambda bi, h, i: (bi, 0, h)),
                  pl.BlockSpec((1, s_len, DIFF_LANES), lambda bi, h, i: (bi, 0, h)),
                  pl.BlockSpec(diff_lambda.shape, lambda bi, h, i: (0, 0)),
                  pl.BlockSpec((1, LANES), lambda bi, h, i: (0, 0))],
        out_specs=pl.BlockSpec((1, tq, DIFF_LANES), lambda bi, h, i: (bi, i, h)),
        out_shape=jax.ShapeDtypeStruct((b, t, d), F32),
        compiler_params=_cparams(("parallel", "parallel", "parallel")),
        name="diff_attend",
    )(q, k_all, v_all, diff_lambda, gain.reshape(1, LANES))


def _route(logits):
    lane = lax.broadcasted_iota(jnp.int32, logits.shape, 1)
    is_group = lane < MOE_GROUPS
    gl = jnp.where(is_group, logits, NEG_INF)
    g_max = jnp.max(gl, axis=1, keepdims=True)
    g_sel = jnp.min(jnp.where(gl == g_max, lane, LANES), axis=1, keepdims=True)
    g_gate = 1.0 / jnp.sum(jnp.where(is_group, jnp.exp(gl - g_max), 0.0), axis=1, keepdims=True)
    in_group = jnp.logical_and(lane >= GATE_COL0, (lane - GATE_COL0) // MOE_EPG == g_sel)
    in_group = jnp.logical_and(in_group, lane < GATE_COL0 + MOE_EXPERTS)
    el = jnp.where(in_group, logits, NEG_INF)
    top1 = jnp.max(el, axis=1, keepdims=True)
    i1 = jnp.min(jnp.where(jnp.logical_and(in_group, el == top1), lane, LANES), axis=1, keepdims=True)
    rest = jnp.logical_and(in_group, lane != i1)
    el2 = jnp.where(rest, logits, NEG_INF)
    top2 = jnp.max(el2, axis=1, keepdims=True)
    i2 = jnp.min(jnp.where(jnp.logical_and(rest, el2 == top2), lane, LANES), axis=1, keepdims=True)
    e2 = jnp.exp(top2 - top1)
    w1 = g_gate / (1.0 + e2)
    gates = jnp.where(lane == i1, w1, jnp.where(lane == i2, w1 * e2, 0.0))
    return jnp.where(lane == GROUP_LANE, g_sel.astype(F32), gates)


def _out_kernel(h_ref, a_ref, b_ref, wa_ref, wb_ref, g_ref, wr_ref, br_ref, o_ref, xn_ref, gate_ref):
    h = (h_ref[...] + _dot(a_ref[...].astype(BF16), wa_ref[...])
         + _dot(b_ref[...].astype(BF16), wb_ref[...]))
    o_ref[...] = h
    xn = _rms(h, g_ref[...]).astype(BF16)
    xn_ref[...] = xn
    gate_ref[...] = _route(_dot(xn, wr_ref[...]) + br_ref[...])


def _out_proj_route(h, a, bmix, w_out, g_ffn, wr, br, tm):
    n, d = h.shape
    ca = a.shape[1]
    wa, wb = w_out[:ca], w_out[ca:]
    row = lambda w: pl.BlockSpec((tm, w), lambda i: (i, 0))
    whole = lambda x: pl.BlockSpec(x.shape, lambda i: (0, 0))
    return pl.pallas_call(
        _out_kernel,
        grid=(n // tm,),
        in_specs=[row(d), row(ca), row(bmix.shape[1]), whole(wa), whole(wb),
                  pl.BlockSpec((1, d), lambda i: (0, 0)), whole(wr), whole(br)],
        out_specs=[row(d), row(d), row(LANES)],
        out_shape=[jax.ShapeDtypeStruct((n, d), F32), jax.ShapeDtypeStruct((n, d), BF16),
                   jax.ShapeDtypeStruct((n, LANES), F32)],
        compiler_params=_cparams(("parallel",)),
        name="out_proj_route",
    )(h, a, bmix, wa, wb, g_ffn.reshape(1, d), wr, br)


def _split3(x):
    hi = x.astype(BF16)
    r1 = x - hi.astype(F32)
    mid = r1.astype(BF16)
    lo = (r1 - mid.astype(F32)).astype(BF16)
    return hi, mid, lo


def _expert_kernel(xn_ref, gate_ref, w1_ref, w3_ref, w2_ref, h_ref, gf_ref, o_ref,
                   acc_ref, xs_ref, gs_ref, pt_ref, tri_ref, seg_ref, *, final_norm, tm, win):
    i = pl.program_id(0)
    e = pl.program_id(1)
    n_sub = xn_ref.shape[0] // tm
    sr = min(SORT_ROWS, tm)
    tile_rows = lambda s: pl.ds(pl.multiple_of(s * tm, tm), tm)

    @pl.when(jnp.logical_and(i == 0, e == 0))
    def _():
        rr = lax.broadcasted_iota(jnp.int32, (tm, tm), 0)
        cc = lax.broadcasted_iota(jnp.int32, (tm, tm), 1)
        tri_ref[...] = jnp.where(cc < rr, 1.0, 0.0).astype(BF16)

    def sort_tile(s, carry):
        gate = gate_ref[tile_rows(s), :]
        lane = lax.broadcasted_iota(jnp.int32, gate.shape, 1)
        lane_row = lax.broadcasted_iota(jnp.int32, (1, LANES), 1)
        in_grp = jnp.logical_and(lane < MOE_GROUPS, lane.astype(F32) == gate[:, GROUP_LANE:GROUP_LANE + 1])
        onehot = jnp.where(in_grp, 1.0, 0.0)
        before = _dot(tri_ref[...], onehot.astype(BF16))
        count = jnp.sum(onehot, axis=0, keepdims=True)
        first = jnp.zeros((1, LANES), F32)
        start = jnp.float32(0.0)
        for g in range(MOE_GROUPS):
            n_g = jnp.sum(jnp.where(lane_row == g, count, 0.0))
            seg_ref[s, g] = start.astype(jnp.int32)
            seg_ref[s, MOE_GROUPS + g] = n_g.astype(jnp.int32)
            first = jnp.where(lane_row == g, start, first)
            start = start + n_g
        dest = jnp.sum(onehot * (before + first), axis=1, keepdims=True).astype(jnp.int32)
        g_hi, g_mid, g_lo = _split3(gate)
        xn = xn_ref[tile_rows(s), :]
        for r in range(0, tm, sr):
            rs = slice(r, r + sr)
            pt_ref[s, rs, :] = jnp.where(lax.broadcasted_iota(jnp.int32, (sr, tm), 1) == dest[rs],
                                         1.0, 0.0).astype(BF16)
        for r in range(0, tm, sr):
            rs = slice(r, r + sr)
            pt_cols = pt_ref[s, :, rs]
            xs_ref[s, rs, :] = _dot_tn(pt_cols, xn).astype(BF16)
            gs_ref[s, rs, :] = _dot_tn(pt_cols, g_hi) + _dot_tn(pt_cols, g_mid) + _dot_tn(pt_cols, g_lo)
        xs_ref[s, tm:, :] = jnp.zeros((win, xs_ref.shape[2]), BF16)
        gs_ref[s, tm:, :] = jnp.zeros((win, LANES), F32)
        acc_ref[s] = jnp.zeros(acc_ref.shape[1:], F32)
        return carry

    @pl.when(e == 0)
    def _():
        lax.fori_loop(0, n_sub, sort_tile, 0)

    grp = e // MOE_EPG
    for s in range(n_sub):
        seg_first = seg_ref[s, grp]
        seg_rows = seg_ref[s, MOE_GROUPS + grp]
        row0 = (seg_first // BF16_ROWS) * BF16_ROWS
        n_win = (seg_first + seg_rows - row0 + win - 1) // win

        def window(w, carry, s=s, row0=row0):
            rows = pl.ds(pl.multiple_of(row0 + w * win, BF16_ROWS), win)
            x = xs_ref[s, rows, :]
            gsw = gs_ref[s, rows, :]
            lane = lax.broadcasted_iota(jnp.int32, gsw.shape, 1)
            ge = jnp.sum(jnp.where(lane == GATE_COL0 + e, gsw, 0.0), axis=1, keepdims=True)
            a = _dot(x, w1_ref[0])
            b = _dot(x, w3_ref[0])
            act = (a * (1.0 / (1.0 + jnp.exp(-a))) * b).astype(BF16)
            acc_ref[s, rows, :] += ge * _dot(act, w2_ref[0])
            return carry

        lax.fori_loop(0, n_win, window, 0)

    def unsort_tile(s, carry):
        a_hi, a_mid, _ = _split3(acc_ref[s, 0:tm, :])
        for r in range(0, tm, sr):
            rows = pl.ds(pl.multiple_of(s * tm + r, sr), sr)
            pt = pt_ref[s, r:r + sr, :]
            y = h_ref[rows, :] + (_dot(pt, a_hi) + _dot(pt, a_mid))
            o_ref[rows, :] = _rms(y, gf_ref[...]) if final_norm else y
        return carry

    @pl.when(e == pl.num_programs(1) - 1)
    def _():
        lax.fori_loop(0, n_sub, unsort_tile, 0)


def _moe(h, xn, gate, w1, w3, w2, g_final, final_norm, tm):
    n, d = h.shape
    n_e = w1.shape[0]
    tme = MOE_TM if n % MOE_TM == 0 else tm
    n_sub = MOE_SUBTILES if n % (MOE_SUBTILES * tme) == 0 else 1
    blk = n_sub * tme
    win = -(-(tme * 5 // 16) // BF16_ROWS) * BF16_ROWS
    row2 = lambda w: pl.BlockSpec((blk, w), lambda i, e: (i, 0), pipeline_mode=pl.Buffered(1))
    wspec = lambda w: pl.BlockSpec((1,) + w.shape[1:], lambda i, e: (e, 0, 0))
    kern = functools.partial(_expert_kernel, final_norm=final_norm, tm=tme, win=win)
    return pl.pallas_call(
        kern,
        grid=(n // blk, n_e),
        in_specs=[row2(d), row2(LANES),
                  wspec(w1), wspec(w3), wspec(w2),
                  row2(d), pl.BlockSpec((1, d), lambda i, e: (0, 0))],
        out_specs=row2(d),
        out_shape=jax.ShapeDtypeStruct((n, d), F32),
        scratch_shapes=[pltpu.VMEM((n_sub, tme + win, d), F32), pltpu.VMEM((n_sub, tme + win, d), BF16),
                        pltpu.VMEM((n_sub, tme + win, LANES), F32), pltpu.VMEM((n_sub, tme, tme), BF16),
                        pltpu.VMEM((tme, tme), BF16), pltpu.SMEM((n_sub, 2 * MOE_GROUPS), jnp.int32)],
        compiler_params=pltpu.CompilerParams(dimension_semantics=("arbitrary", "arbitrary"),
                                             vmem_limit_bytes=MOE_VMEM_LIMIT),
        name="moe_experts",
    )(xn, gate, w1, w3, w2, h, g_final.reshape(1, d))


EVEN_SEGS = ((0, 512, None, False), (512, 512, "full", False), (1024, 128, "full", True), (1152, 128, None, True),
             (1280, 256, "full", False), (1536, 128, "half", False))
ODD_SEGS = ((0, 512, None, False), (512, 512, None, True), (1024, 512, None, True),
            (1536, 512, "full", False), (2048, 512, "full", True), (2560, 512, None, True))


def _cat_keys(hist, new):
    allk = jnp.concatenate([hist, new], axis=1) if hist is not None else new
    n_keys = allk.shape[1]
    pad = -n_keys % LANES
    if pad:
        allk = jnp.pad(allk, ((0, 0), (0, pad), (0, 0)))
    return allk.astype(BF16), n_keys


def kernel(x_prompt, x_sample, cache_pool, cache_dsa_k, cache_dsa_v, cache_idx_k, cache_sb_k, cache_sb_v,
           cache_diff_k, cache_diff_v, norm_mix, norm_ffn, norm_final, w_in_even, w_pool, pool_scale,
           w_out_even, w_in_odd, diff_lambda, diff_subln, w_out_odd, moe_w_group, moe_b_group,
           moe_w_expert, moe_b_expert, moe_w1, moe_w3, moe_w2):
    b, t, d = x_prompt.shape
    bd, td, _ = x_sample.shape
    past = cache_dsa_k.shape[2]
    depth = norm_mix.shape[0]
    groups = ((b, t, 0, min(512, b * t)), (bd, td, past, bd * td))

    tabs = []
    for (gb, gt, p0, tm) in groups:
        tab = _rope_tables(p0 + jnp.arange(gt, dtype=jnp.int32))
        if tm > gt:
            tab = jnp.tile(tab, (tm // gt, 1))
        tabs.append(tab)

    hs = [x_prompt.reshape(b * t, d), x_sample.reshape(bd * td, d)]
    outs = [dict(), dict()]
    for l in range(depth):
        li = l // 2
        last = l == depth - 1
        if l % 2 == 0:
            n_in = w_in_even.shape[2]
            w_in = jnp.pad(w_in_even[li], ((0, 0), (0, -n_in % LANES))).astype(BF16)
            w_out = w_out_even[li].astype(BF16)
            wp = w_pool[li].astype(BF16)
        else:
            w_in = w_in_odd[li].astype(BF16)
            w_out = w_out_odd[li].astype(BF16)
            lam_init = 0.8 - 0.6 * math.exp(-0.3 * l)
        wr = jnp.concatenate([moe_w_group[l]] + [moe_w_expert[l, g] for g in range(MOE_GROUPS)], axis=1)
        wr = jnp.pad(wr, ((0, 0), (0, LANES - wr.shape[1]))).astype(BF16)
        br = jnp.concatenate([moe_b_group[l], moe_b_expert[l].reshape(-1)])
        br = jnp.pad(br, (0, LANES - br.shape[0])).reshape(1, LANES).astype(F32)
        w1, w3, w2 = moe_w1[l].astype(BF16), moe_w3[l].astype(BF16), moe_w2[l].astype(BF16)

        for gi, (gb, gt, p0, tm) in enumerate(groups):
            h = hs[gi]
            o = outs[gi]
            sample = gi == 1
            r3 = lambda x: x.reshape(gb, gt, x.shape[-1])
            if l % 2 == 0:
                u, q, k, v, qi, kiwi, k16, v16 = [
                    r3(x) for x in _project(h, norm_mix[l], w_in, tabs[gi], EVEN_SEGS, tm)]
                ki = kiwi[..., :HEAD_DIM]
                hist = cache_pool[li] if sample else jnp.zeros((gb, POOL_HIST, u.shape[2]), F32)
                a_out = _pool_mix(u, hist, wp, pool_scale[li], p0)
                if sample:
                    k_all, n_keys = _cat_keys(cache_dsa_k[li].reshape(gb, past, -1), k)
                    v_all, _ = _cat_keys(cache_dsa_v[li].reshape(gb, past, -1), v)
                    ki_all, _ = _cat_keys(cache_idx_k[li], ki)
                else:
                    (k_all, n_keys), (v_all, _), (ki_all, _) = _cat_keys(None, k16), _cat_keys(None, v16), _cat_keys(None, ki)
                b_out = _dsa(q, qi, kiwi, k_all, v_all, ki_all, n_keys, p0)
                o.setdefault("pool", []).append(jnp.concatenate([hist, u], axis=1)[:, -POOL_HIST:])
                o.setdefault("dsa_k", []).append(k.reshape(gb, gt, -1, HEAD_DIM))
                o.setdefault("dsa_v", []).append(v.reshape(gb, gt, -1, HEAD_DIM))
                o.setdefault("idx_k", []).append(ki)
                mix_a, mix_b = a_out, b_out
            else:
                sq, sk, sv, dq, dk, dv, sk16, sv16, dk16, dv16 = [
                    r3(x) for x in _project(h, norm_mix[l], w_in, tabs[gi], ODD_SEGS, tm)]
                if sample:
                    sk_all, n_keys = _cat_keys(cache_sb_k[li].reshape(gb, past, -1), sk)
                    sv_all, _ = _cat_keys(cache_sb_v[li].reshape(gb, past, -1), sv)
                    dk_all, _ = _cat_keys(cache_diff_k[li].reshape(gb, past, -1), dk)
                    dv_all, _ = _cat_keys(cache_diff_v[li].reshape(gb, past, -1), dv)
                else:
                    (sk_all, n_keys), (sv_all, _) = _cat_keys(None, sk16), _cat_keys(None, sv16)
                    (dk_all, _), (dv_all, _) = _cat_keys(None, dk16), _cat_keys(None, dv16)
                c_out = _sb_attend(sq, sk_all, sv_all, n_keys, p0)
                d_out = _diff_attend(dq, dk_all, dv_all, diff_lambda[li], diff_subln[li], n_keys, p0, lam_init)
                n_sb = sk.shape[2] // HEAD_DIM
                n_df = dk.shape[2] // (2 * HEAD_DIM)
                o.setdefault("sb_k", []).append(sk.reshape(gb, gt, n_sb, HEAD_DIM))
                o.setdefault("sb_v", []).append(sv.reshape(gb, gt, n_sb, HEAD_DIM))
                o.setdefault("diff_k", []).append(dk.reshape(gb, gt, n_df, 2, HEAD_DIM))
                o.setdefault("diff_v", []).append(dv.reshape(gb, gt, n_df, 2 * HEAD_DIM))
                mix_a, mix_b = c_out, d_out
            h, xn, gate = _out_proj_route(h, mix_a.reshape(gb * gt, -1), mix_b.reshape(gb * gt, -1), w_out,
                                          norm_ffn[l], wr, br, tm)
            hs[gi] = _moe(h, xn, gate, w1, w3, w2, norm_final, last, tm)

    names = ("pool", "dsa_k", "dsa_v", "idx_k", "sb_k", "sb_v", "diff_k", "diff_v")
    res = [hs[0].reshape(b, t, d), hs[1].reshape(bd, td, d)]
    for o in outs:
        res += [jnp.stack(o[nm]) for nm in names]
    return tuple(res)
```

```python
import functools
import math

import jax
import jax.numpy as jnp
import numpy as np
from jax import lax
from jax.experimental import pallas as pl
from jax.experimental.pallas import tpu as pltpu

F32 = jnp.float32
BF16 = jnp.bfloat16

LANES = 128
HEAD_DIM = 64
CHUNK = 64
ROPE_THETA = 10000.0
NORM_EPS = 1e-6
NEG_INF = -1e30
PAD_SCORE = -3e38
BIG_POS = 3e38
POOL_WINDOWS = (2, 4, 8, 16)
POOL_HIST = 15
POOL_HIST_PAD = 16
DSA_TOPK = 256
IDX_HEADS = 4
MOE_GROUPS = 4
MOE_EPG = 4
MOE_EXPERTS = 16
GATE_COL0 = MOE_GROUPS
GROUP_LANE = 0
BF16_ROWS = 16
SORT_ROWS = 256
VMEM_LIMIT = 56 * 1024 * 1024
BISECT_STEPS = 8
BISECT_ROUNDS = 48
BISECT_UNCHECKED = 2
ATTN_TQ = 256
DSA_TQ = 128
MOE_TM = 1024
MOE_SUBTILES = 2
MOE_VMEM_LIMIT = 62 * 1024 * 1024
DIFF_LANES = 2 * LANES


def _cparams(sem):
    return pltpu.CompilerParams(dimension_semantics=sem, vmem_limit_bytes=VMEM_LIMIT)


def _dot(a, b):
    return jnp.dot(a, b, preferred_element_type=F32)


def _dot_nt(a, b):
    return lax.dot_general(a, b, (((1,), (1,)), ((), ())), preferred_element_type=F32)


def _dot_tn(a, b):
    return lax.dot_general(a, b, (((0,), (0,)), ((), ())), preferred_element_type=F32)


def _rms(x, g):
    ms = jnp.mean(x * x, axis=-1, keepdims=True)
    return x * lax.rsqrt(ms + NORM_EPS) * g


def _proj_kernel(x_ref, g_ref, w_ref, tab_ref, *out_refs, segs):
    xn = _rms(x_ref[...], g_ref[...]).astype(BF16)
    copies = iter(out_refs[len(segs):])
    for o_ref, (c0, width, mode, twin) in zip(out_refs, segs):
        t_ref = next(copies) if twin else None
        y = _dot(xn, w_ref[:, c0:c0 + width])
        if mode is None:
            o_ref[...] = y
            if twin:
                t_ref[...] = y.astype(BF16)
            continue
        t0 = 0 if mode == "full" else 3 * LANES
        cos = tab_ref[:, t0:t0 + LANES]
        sin_a = tab_ref[:, t0 + LANES:t0 + 2 * LANES]
        sin_b = tab_ref[:, t0 + 2 * LANES:t0 + 3 * LANES]
        for c in range(0, width, LANES):
            yc = y[:, c:c + LANES]
            yr = (yc * cos + pltpu.roll(yc, LANES - HEAD_DIM // 2, 1) * sin_a
                  + pltpu.roll(yc, HEAD_DIM // 2, 1) * sin_b)
            o_ref[:, c:c + LANES] = yr
            if twin:
                t_ref[:, c:c + LANES] = yr.astype(BF16)


def _rope_tables(pos):
    half = HEAD_DIM // 2
    inv = ROPE_THETA ** (-jnp.arange(half, dtype=F32) / half)
    ang = pos.astype(F32)[:, None] * inv[None, :]
    cos, sin = jnp.cos(ang), jnp.sin(ang)
    zero, one = jnp.zeros_like(sin), jnp.ones_like(cos)
    cos_h = jnp.concatenate([cos, cos], axis=1)
    sa_h = jnp.concatenate([-sin, zero], axis=1)
    sb_h = jnp.concatenate([zero, sin], axis=1)
    one_h = jnp.concatenate([one, one], axis=1)
    zero_h = jnp.concatenate([zero, zero], axis=1)
    return jnp.concatenate([cos_h, cos_h, sa_h, sa_h, sb_h, sb_h,
                            cos_h, one_h, sa_h, zero_h, sb_h, zero_h], axis=1)


def _project(x, g, w, tab, segs, tm):
    n, d = x.shape
    tt = tab.shape[0]
    nt = tt // tm
    kern = functools.partial(_proj_kernel, segs=segs)
    return pl.pallas_call(
        kern,
        grid=(n // tm,),
        in_specs=[pl.BlockSpec((tm, d), lambda i: (i, 0)),
                  pl.BlockSpec((1, d), lambda i: (0, 0)),
                  pl.BlockSpec(w.shape, lambda i: (0, 0)),
                  pl.BlockSpec((tm, tab.shape[1]), lambda i: (i % nt, 0))],
        out_specs=[pl.BlockSpec((tm, wd), lambda i: (i, 0)) for _, wd, _, _ in segs]
        + [pl.BlockSpec((tm, wd), lambda i: (i, 0)) for _, wd, _, twin in segs if twin],
        out_shape=[jax.ShapeDtypeStruct((n, wd), F32) for _, wd, _, _ in segs]
        + [jax.ShapeDtypeStruct((n, wd), BF16) for _, wd, _, twin in segs if twin],
        compiler_params=_cparams(("parallel",)),
        name="proj",
    )(x, g.reshape(1, d), w, tab)


def _pool_kernel(u_ref, h_ref, w_ref, s_ref, o_ref, ext_ref, *, t, pos0, rc):
    ext_ref[0:POOL_HIST_PAD, :] = h_ref[0]
    ext_ref[POOL_HIST_PAD:POOL_HIST_PAD + t, :] = u_ref[0]
    for r0 in range(0, t, rc):
        pos = pos0 + r0 + lax.broadcasted_iota(jnp.int32, (rc, 1), 0)
        for g, win in enumerate(POOL_WINDOWS):
            c0 = g * LANES
            u_new = ext_ref[POOL_HIST_PAD + r0:POOL_HIST_PAD + r0 + rc, c0:c0 + LANES]
            s = u_new
            for k in range(1, win):
                s = s + ext_ref[POOL_HIST_PAD + r0 - k:POOL_HIST_PAD + r0 - k + rc, c0:c0 + LANES]
            cnt = jnp.minimum(pos + 1, win).astype(F32)
            dlt = (s / cnt - u_new).astype(BF16)
            o_ref[0, r0:r0 + rc, c0:c0 + LANES] = _dot(dlt, w_ref[g]) * s_ref[:, c0:c0 + LANES]


def _pool_mix(u, hist, w_pool, pool_scale, pos0):
    b, t, c = u.shape
    rc = min(t, 256)
    hist16 = jnp.pad(hist, ((0, 0), (POOL_HIST_PAD - POOL_HIST, 0), (0, 0)))
    kern = functools.partial(_pool_kernel, t=t, pos0=pos0, rc=rc)
    return pl.pallas_call(
        kern,
        grid=(b,),
        in_specs=[pl.BlockSpec((1, t, c), lambda i: (i, 0, 0)),
                  pl.BlockSpec((1, POOL_HIST_PAD, c), lambda i: (i, 0, 0)),
                  pl.BlockSpec(w_pool.shape, lambda i: (0, 0, 0)),
                  pl.BlockSpec((1, c), lambda i: (0, 0))],
        out_specs=pl.BlockSpec((1, t, c), lambda i: (i, 0, 0)),
        out_shape=jax.ShapeDtypeStruct((b, t, c), F32),
        scratch_shapes=[pltpu.VMEM((POOL_HIST_PAD + t, c), F32)],
        compiler_params=_cparams(("parallel",)),
        name="pool_mix",
    )(u, hist16, w_pool, pool_scale.reshape(1, c))


def _dsa_kernel(q_ref, qi_ref, kw_ref, k_ref, v_ref, ki_ref, o_ref, lo_ref, hi_ref, bias_ref,
                *, tq, extents, n_keys, q_pos0, n_sel):
    i = pl.program_id(1)
    qpos = q_pos0 + i * tq + lax.broadcasted_iota(jnp.int32, (tq, 1), 0)
    lim = jnp.minimum((qpos // CHUNK + 1) * CHUNK, n_keys)
    kf = float(n_sel)
    low = lax.broadcasted_iota(jnp.int32, (tq, LANES), 1) < HEAD_DIM
    q = q_ref[0] * HEAD_DIM ** -0.5
    qi = qi_ref[0].astype(BF16)
    wi = kw_ref[0][:, HEAD_DIM:HEAD_DIM + IDX_HEADS] * (IDX_HEADS * HEAD_DIM) ** -0.5

    def body(ext):
        kpos = lax.broadcasted_iota(jnp.int32, (tq, ext), 1)
        adm = kpos < lim
        padded = ext > n_keys
        virt = float(max(n_keys - ext, 0))

        sidx = jnp.zeros((tq, ext), F32)
        ki = ki_ref[0, :ext, :]
        for h in range(IDX_HEADS):
            sh = _dot_nt(qi[:, h * HEAD_DIM:(h + 1) * HEAD_DIM], ki)
            sidx = sidx + jnp.maximum(sh, 0.0) * wi[:, h:h + 1]
        sm = jnp.where(adm, sidx, NEG_INF)
        if padded:
            real = kpos < n_keys
            sm = jnp.where(real, sm, PAD_SCORE)

        def count_gt_wide(x):
            part = jnp.where(sm[:, :LANES] > x, 1.0, 0.0)
            for j in range(1, ext // LANES):
                part = part + jnp.where(sm[:, j * LANES:(j + 1) * LANES] > x, 1.0, 0.0)
            cnt = jnp.broadcast_to(jnp.sum(part, axis=1, keepdims=True), (tq, LANES))
            return cnt + jnp.where(x < NEG_INF, virt, 0.0) if virt else cnt

        def count_gt(x):
            return count_gt_wide(jnp.broadcast_to(x, (tq, LANES)))[:, :1]

        def bracket(lo, hi):
            above = jnp.min(jnp.where(sm > lo, sm, BIG_POS), axis=1, keepdims=True)
            below = jnp.max(jnp.where(sm <= hi, sm, PAD_SCORE), axis=1, keepdims=True)
            if virt:
                above = jnp.minimum(above, jnp.where(lo < NEG_INF, NEG_INF, BIG_POS))
                below = jnp.maximum(below, jnp.where(hi >= NEG_INF, NEG_INF, PAD_SCORE))
            return above, below

        row_max = jnp.max(sm, axis=1, keepdims=True)
        row_min = jnp.min(jnp.where(real, sm, BIG_POS) if padded else sm, axis=1, keepdims=True)
        if virt:
            row_min = jnp.minimum(row_min, NEG_INF)
        adm_min = jnp.min(jnp.where(adm, sm, BIG_POS), axis=1, keepdims=True)
        few = count_gt(row_min) < kf
        tight = count_gt(adm_min) >= kf
        lo_ref[...] = jnp.where(few, PAD_SCORE, jnp.where(tight, adm_min, row_min))
        hi_ref[...] = jnp.where(few, row_min, jnp.where(tight, row_max, adm_min))

        def unresolved(lo, hi):
            above, below = bracket(lo, hi)
            return jnp.sum(jnp.where(above < below, 1, 0))

        def cond(carry):
            rounds, open_rows = carry
            return jnp.logical_and(open_rows > 0, rounds < BISECT_ROUNDS)

        def step(carry):
            rounds, _ = carry
            lo = jnp.broadcast_to(lo_ref[...], (tq, LANES))
            hi = jnp.broadcast_to(hi_ref[...], (tq, LANES))
            for _ in range(BISECT_STEPS):
                mid = 0.5 * lo + 0.5 * hi
                under = count_gt_wide(mid) < kf
                hi = jnp.where(under, mid, hi)
                lo = jnp.where(under, lo, mid)
            lo_ref[...] = lo[:, :1]
            hi_ref[...] = hi[:, :1]
            open_rows = lax.cond(rounds + 1 >= BISECT_UNCHECKED,
                                 lambda: unresolved(lo_ref[...], hi_ref[...]), lambda: jnp.int32(1))
            return rounds + 1, open_rows

        lax.while_loop(cond, step, (jnp.int32(0), jnp.int32(1)))
        _, thr = bracket(lo_ref[...], hi_ref[...])

        gt = sm > thr
        eq = sm == thr
        need = kf - count_gt(thr)
        n_eq = jnp.sum(jnp.where(eq, 1.0, 0.0), axis=1, keepdims=True)
        crowded = jnp.sum(jnp.where(n_eq > need, 1, 0))

        @pl.when(crowded == 0)
        def _():
            keep = jnp.logical_and(jnp.logical_or(gt, eq), adm)
            bias_ref[:, :ext] = jnp.where(keep, 0.0, NEG_INF)

        @pl.when(crowded > 0)
        def _():
            rr = lax.broadcasted_iota(jnp.int32, (LANES, LANES), 0)
            cc = lax.broadcasted_iota(jnp.int32, (LANES, LANES), 1)
            prefix_ones = jnp.where(rr <= cc, 1.0, 0.0).astype(BF16)
            carry = jnp.zeros((tq, 1), F32)
            for j in range(ext // LANES):
                sl = slice(j * LANES, (j + 1) * LANES)
                eq_j = eq[:, sl]
                rank = _dot(jnp.where(eq_j, 1.0, 0.0).astype(BF16), prefix_ones) + carry
                keep = jnp.logical_or(gt[:, sl], jnp.logical_and(eq_j, rank <= need))
                bias_ref[:, sl] = jnp.where(jnp.logical_and(keep, adm[:, sl]), 0.0, NEG_INF)
                carry = rank[:, LANES - 1:LANES]

        bias = bias_ref[:, :ext]
        bias2 = jnp.concatenate([bias, bias], axis=0)

        kk = k_ref[0, :ext, :]
        vv = v_ref[0, :ext, :]
        low_k = lax.broadcasted_iota(jnp.int32, (ext, LANES), 1) < HEAD_DIM
        vsw = pltpu.roll(vv.astype(F32), HEAD_DIM, 1).astype(BF16)
        one = jnp.ones_like(vv)
        n_kv = LANES // HEAD_DIM
        stacks = [(g, odd) for g in range(n_kv) for odd in range(2)]
        scores = []
        for g, odd in stacks:
            rows = []
            for m in range(2):
                c = 2 * g + m
                qc = q[:, c * LANES:(c + 1) * LANES]
                if (odd == 1) != (g == 1):
                    qc = pltpu.roll(qc, HEAD_DIM, 1)
                rows.append(jnp.where(low, qc, 0.0) if g == 0 else jnp.where(low, 0.0, qc))
            qs = jnp.concatenate(rows, axis=0).astype(BF16)
            scores.append(_dot_nt(qs, kk) + bias2)
        probs = [jnp.exp(sc - jnp.max(sc, axis=1, keepdims=True)).astype(BF16) for sc in scores]
        outs = []
        for (g, odd), p in zip(stacks, probs):
            if odd:
                v_aug = jnp.where(low_k, one, vsw if g == 0 else vv)
            else:
                v_aug = jnp.where(low_k, vv if g == 0 else vsw, one)
            og = _dot(p, v_aug)
            outs.append(og / pltpu.roll(og, HEAD_DIM, 1))
        for g in range(n_kv):
            for m in range(2):
                c = 2 * g + m
                o_ref[0, :, c * LANES:(c + 1) * LANES] = jnp.where(
                    low, outs[2 * g][m * tq:(m + 1) * tq], outs[2 * g + 1][m * tq:(m + 1) * tq])

    _for_tile_extent(i, tq, q_pos0, n_keys, extents, body)


def _dsa(q, qi, kiwi, k_all, v_all, ki_all, n_keys, q_pos0):
    b, t, dq = q.shape
    s_len = k_all.shape[1]
    tq = min(t, DSA_TQ)
    n_sel = min(DSA_TOPK, n_keys // 4)
    extents = _key_extents(t // tq, tq, q_pos0, n_keys, s_len, 4 * LANES)
    kern = functools.partial(_dsa_kernel, tq=tq, extents=extents, n_keys=n_keys, q_pos0=q_pos0, n_sel=n_sel)
    qspec = lambda w: pl.BlockSpec((1, tq, w), lambda bi, i: (bi, i, 0))
    kspec = lambda w: pl.BlockSpec((1, s_len, w), lambda bi, i: (bi, 0, 0))
    return pl.pallas_call(
        kern,
        grid=(b, t // tq),
        in_specs=[qspec(dq), qspec(qi.shape[2]), qspec(kiwi.shape[2]),
                  kspec(k_all.shape[2]), kspec(v_all.shape[2]), kspec(ki_all.shape[2])],
        out_specs=qspec(dq),
        out_shape=jax.ShapeDtypeStruct((b, t, dq), F32),
        scratch_shapes=[pltpu.VMEM((tq, 1), F32), pltpu.VMEM((tq, 1), F32), pltpu.VMEM((tq, s_len), F32)],
        compiler_params=_cparams(("parallel", "parallel")),
        name="dsa",
    )(q, qi, kiwi, k_all, v_all, ki_all)


def _sb_kernel(q_ref, k_ref, v_ref, o_ref, acc_ref, run_ref, *, tq, n_keys, q_pos0):
    i = pl.program_id(1)
    pairs = q_ref.shape[2] // LANES
    first_q = q_pos0 + i * tq
    qpos = first_q + lax.broadcasted_iota(jnp.int32, (tq, 1), 0)
    n_blocks = (jnp.minimum(first_q + tq - 1, n_keys) + LANES - 1) // LANES
    n_full = jnp.minimum(first_q, n_keys) // LANES
    low_q = lax.broadcasted_iota(jnp.int32, (tq, LANES), 1) < HEAD_DIM
    low_k = lax.broadcasted_iota(jnp.int32, (LANES, LANES), 1) < HEAD_DIM
    q = q_ref[0] * HEAD_DIM ** -0.5
    qm = []
    for p in range(pairs):
        qp = q[:, p * LANES:(p + 1) * LANES]
        qm.append((jnp.where(low_q, qp, 0.0).astype(BF16), jnp.where(low_q, 0.0, qp).astype(BF16)))
    rr = lax.broadcasted_iota(jnp.int32, (2 * LANES, 2 * LANES), 0)
    cc = lax.broadcasted_iota(jnp.int32, (2 * LANES, 2 * LANES), 1)
    rk = jnp.where(rr >= LANES, rr - LANES, rr)
    cs_rhs = jnp.where(jnp.logical_or(cc >= LANES, rk >= cc), 1.0, 0.0).astype(BF16)
    acc_ref[...] = jnp.zeros_like(acc_ref)
    run_ref[...] = jnp.zeros_like(run_ref)

    def block(j, masked):
        ks = pl.multiple_of(j * LANES, LANES)
        if masked:
            kpos = ks + lax.broadcasted_iota(jnp.int32, (1, LANES), 1)
            causal = jnp.logical_and(kpos < qpos, kpos < n_keys)
        heads = [(p, c) for p in range(pairs) for c in range(2)]
        lanes = lambda n: slice(n * LANES, (n + 1) * LANES)
        zs = [_dot_nt(qm[p][c], k_ref[0, pl.ds(ks, LANES), lanes(p)]) for p, c in heads]
        css = []
        for z in zs:
            sp = jnp.maximum(z, 0.0) + jnp.log(1.0 + jnp.exp(-jnp.abs(z)))
            if masked:
                sp = jnp.where(causal, sp, 0.0)
            hi = sp.astype(BF16)
            lo = (sp - hi.astype(F32)).astype(BF16)
            css.append(_dot(jnp.concatenate([hi, lo], axis=1), cs_rhs))
        probs = []
        for n, (z, cs) in enumerate(zip(zs, css)):
            run = run_ref[:, lanes(n)]
            a = jnp.exp(z - cs[:, :LANES] - run)
            if masked:
                a = jnp.where(causal, a, 0.0)
            run_ref[:, lanes(n)] = run + cs[:, LANES:]
            probs.append(a.astype(BF16))
        for p in range(pairs):
            vb = v_ref[0, pl.ds(ks, LANES), lanes(p)]
            zero = jnp.zeros_like(vb)
            v_cat = jnp.concatenate([jnp.where(low_k, vb, zero), jnp.where(low_k, zero, vb)], axis=0)
            acc_ref[:, lanes(p)] += _dot(jnp.concatenate(probs[2 * p:2 * p + 2], axis=1), v_cat)

    def masked_step(jj, carry):
        block(n_blocks - 1 - jj, True)
        return carry

    def full_pair(jj, carry):
        block(n_full - 1 - 2 * jj, False)
        block(n_full - 2 - 2 * jj, False)
        return carry

    lax.fori_loop(0, n_blocks - n_full, masked_step, 0)
    lax.fori_loop(0, n_full // 2, full_pair, 0)

    @pl.when(n_full % 2 == 1)
    def _():
        block(jnp.int32(0), False)

    o_ref[0] = acc_ref[...]


def _sb_attend(q, k_all, v_all, n_keys, q_pos0):
    b, t, d = q.shape
    s_len = k_all.shape[1]
    tq = min(t, ATTN_TQ)
    kern = functools.partial(_sb_kernel, tq=tq, n_keys=n_keys, q_pos0=q_pos0)
    return pl.pallas_call(
        kern,
        grid=(b, t // tq),
        in_specs=[pl.BlockSpec((1, tq, d), lambda bi, i: (bi, i, 0)),
                  pl.BlockSpec((1, s_len, d), lambda bi, i: (bi, 0, 0)),
                  pl.BlockSpec((1, s_len, d), lambda bi, i: (bi, 0, 0))],
        out_specs=pl.BlockSpec((1, tq, d), lambda bi, i: (bi, i, 0)),
        out_shape=jax.ShapeDtypeStruct((b, t, d), F32),
        scratch_shapes=[pltpu.VMEM((tq, d), F32), pltpu.VMEM((tq, 2 * d), F32)],
        compiler_params=_cparams(("parallel", "parallel")),
        name="sb_attend",
    )(q, k_all, v_all)


def _key_extents(nq, tq, q_pos0, n_keys, s_len, step):
    need = [min(n_keys, ((q_pos0 + (i + 1) * tq - 1) // CHUNK + 1) * CHUNK) for i in range(nq)]
    return tuple(sorted({min(s_len, -(-n // step) * step) for n in need}))


def _for_tile_extent(i, tq, q_pos0, n_keys, extents, body):
    need = jnp.minimum(n_keys, ((q_pos0 + (i + 1) * tq - 1) // CHUNK + 1) * CHUNK)
    prev = 0
    for ext in extents:
        pl.when(jnp.logical_and(need > prev, need <= ext))(functools.partial(body, ext))
        prev = ext


def _diff_kernel(q_ref, k_ref, v_ref, lam_ref, gain_ref, o_ref, *, tq, extents, n_keys, q_pos0, lam_init):
    i = pl.program_id(2)
    lp = lam_ref[...]
    lam = (jnp.exp(jnp.sum(lp[0:1] * lp[1:2], axis=1, keepdims=True))
           - jnp.exp(jnp.sum(lp[2:3] * lp[3:4], axis=1, keepdims=True)) + lam_init)
    qpos = q_pos0 + i * tq + lax.broadcasted_iota(jnp.int32, (tq, 1), 0)
    lim = jnp.minimum((qpos // CHUNK + 1) * CHUNK, n_keys)
    low = lax.broadcasted_iota(jnp.int32, (tq, LANES), 1) < HEAD_DIM

    def body(ext):
        mask = lax.broadcasted_iota(jnp.int32, (tq, ext), 1) < lim
        heads = range(q_ref.shape[2] // LANES)
        lanes = lambda h: slice(h * LANES, (h + 1) * LANES)
        scores = []
        for h in heads:
            q = q_ref[0, :, lanes(h)] * HEAD_DIM ** -0.5
            kk = k_ref[0, :ext, lanes(h)]
            for c in range(2):
                qc = (jnp.where(low, q, 0.0) if c == 0 else jnp.where(low, 0.0, q)).astype(BF16)
                scores.append(jnp.where(mask, _dot_nt(qc, kk), NEG_INF))
        probs = []
        for sc in scores:
            p = jnp.exp(sc - jnp.max(sc, axis=1, keepdims=True))
            probs.append(p * (1.0 / jnp.sum(p, axis=1, keepdims=True)))
        for h in heads:
            a = probs[2 * h] - lam * probs[2 * h + 1]
            o = _dot(a.astype(BF16), v_ref[0, :ext, lanes(h)])
            o = o * lax.rsqrt(jnp.mean(o * o, axis=-1, keepdims=True) + NORM_EPS)
            o_ref[0, :, lanes(h)] = o * gain_ref[...] * (1.0 - lam_init)

    _for_tile_extent(i, tq, q_pos0, n_keys, extents, body)


def _diff_attend(q, k_all, v_all, diff_lambda, gain, n_keys, q_pos0, lam_init):
    b, t, d = q.shape
    s_len = k_all.shape[1]
    tq = min(t, ATTN_TQ)
    extents = _key_extents(t // tq, tq, q_pos0, n_keys, s_len, 2 * LANES)
    kern = functools.partial(_diff_kernel, tq=tq, extents=extents, n_keys=n_keys, q_pos0=q_pos0, lam_init=lam_init)
    return pl.pallas_call(
        kern,
        grid=(b, d // DIFF_LANES, t // tq),
        in_specs=[pl.BlockSpec((1, tq, DIFF_LANES), lambda bi, h, i: (bi, i, h)),
                  pl.BlockSpec((1, s_len, DIFF_LANES), lambda bi, h, i: (bi, 0, h)),
                  pl.BlockSpec((1, s_len, DIFF_LANES), lambda bi, h, i: (bi, 0, h)),
                  pl.BlockSpec(diff_lambda.shape, lambda bi, h, i: (0, 0)),
                  pl.BlockSpec((1, LANES), lambda bi, h, i: (0, 0))],
        out_specs=pl.BlockSpec((1, tq, DIFF_LANES), lambda bi, h, i: (bi, i, h)),
        out_shape=jax.ShapeDtypeStruct((b, t, d), F32),
        compiler_params=_cparams(("parallel", "parallel", "parallel")),
        name="diff_attend",
    )(q, k_all, v_all, diff_lambda, gain.reshape(1, LANES))


def _route(logits):
    lane = lax.broadcasted_iota(jnp.int32, logits.shape, 1)
    is_group = lane < MOE_GROUPS
    gl = jnp.where(is_group, logits, NEG_INF)
    g_max = jnp.max(gl, axis=1, keepdims=True)
    g_sel = jnp.min(jnp.where(gl == g_max, lane, LANES), axis=1, keepdims=True)
    g_gate = 1.0 / jnp.sum(jnp.where(is_group, jnp.exp(gl - g_max), 0.0), axis=1, keepdims=True)
    in_group = jnp.logical_and(lane >= GATE_COL0, (lane - GATE_COL0) // MOE_EPG == g_sel)
    in_group = jnp.logical_and(in_group, lane < GATE_COL0 + MOE_EXPERTS)
    el = jnp.where(in_group, logits, NEG_INF)
    top1 = jnp.max(el, axis=1, keepdims=True)
    i1 = jnp.min(jnp.where(jnp.logical_and(in_group, el == top1), lane, LANES), axis=1, keepdims=True)
    rest = jnp.logical_and(in_group, lane != i1)
    el2 = jnp.where(rest, logits, NEG_INF)
    top2 = jnp.max(el2, axis=1, keepdims=True)
    i2 = jnp.min(jnp.where(jnp.logical_and(rest, el2 == top2), lane, LANES), axis=1, keepdims=True)
    e2 = jnp.exp(top2 - top1)
    w1 = g_gate / (1.0 + e2)
    gates = jnp.where(lane == i1, w1, jnp.where(lane == i2, w1 * e2, 0.0))
    return jnp.where(lane == GROUP_LANE, g_sel.astype(F32), gates)


def _out_kernel(h_ref, a_ref, b_ref, wa_ref, wb_ref, g_ref, wr_ref, br_ref, o_ref, xn_ref, gate_ref):
    h = (h_ref[...] + _dot(a_ref[...].astype(BF16), wa_ref[...])
         + _dot(b_ref[...].astype(BF16), wb_ref[...]))
    o_ref[...] = h
    xn = _rms(h, g_ref[...]).astype(BF16)
    xn_ref[...] = xn
    gate_ref[...] = _route(_dot(xn, wr_ref[...]) + br_ref[...])


def _out_proj_route(h, a, bmix, w_out, g_ffn, wr, br, tm):
    n, d = h.shape
    ca = a.shape[1]
    wa, wb = w_out[:ca], w_out[ca:]
    row = lambda w: pl.BlockSpec((tm, w), lambda i: (i, 0))
    whole = lambda x: pl.BlockSpec(x.shape, lambda i: (0, 0))
    return pl.pallas_call(
        _out_kernel,
        grid=(n // tm,),
        in_specs=[row(d), row(ca), row(bmix.shape[1]), whole(wa), whole(wb),
                  pl.BlockSpec((1, d), lambda i: (0, 0)), whole(wr), whole(br)],
        out_specs=[row(d), row(d), row(LANES)],
        out_shape=[jax.ShapeDtypeStruct((n, d), F32), jax.ShapeDtypeStruct((n, d), BF16),
                   jax.ShapeDtypeStruct((n, LANES), F32)],
        compiler_params=_cparams(("parallel",)),
        name="out_proj_route",
    )(h, a, bmix, wa, wb, g_ffn.reshape(1, d), wr, br)


def _split3(x):
    hi = x.astype(BF16)
    r1 = x - hi.astype(F32)
    mid = r1.astype(BF16)
    lo = (r1 - mid.astype(F32)).astype(BF16)
    return hi, mid, lo


def _expert_kernel(xn_ref, gate_ref, w1_ref, w3_ref, w2_ref, h_ref, gf_ref, o_ref,
                   acc_ref, xs_ref, gs_ref, pt_ref, tri_ref, seg_ref, *, final_norm, tm, win):
    i = pl.program_id(0)
    e = pl.program_id(1)
    n_sub = xn_ref.shape[0] // tm
    sr = min(SORT_ROWS, tm)
    tile_rows = lambda s: pl.ds(pl.multiple_of(s * tm, tm), tm)

    @pl.when(jnp.logical_and(i == 0, e == 0))
    def _():
        rr = lax.broadcasted_iota(jnp.int32, (tm, tm), 0)
        cc = lax.broadcasted_iota(jnp.int32, (tm, tm), 1)
        tri_ref[...] = jnp.where(cc < rr, 1.0, 0.0).astype(BF16)

    def sort_tile(s, carry):
        gate = gate_ref[tile_rows(s), :]
        lane = lax.broadcasted_iota(jnp.int32, gate.shape, 1)
        lane_row = lax.broadcasted_iota(jnp.int32, (1, LANES), 1)
        in_grp = jnp.logical_and(lane < MOE_GROUPS, lane.astype(F32) == gate[:, GROUP_LANE:GROUP_LANE + 1])
        onehot = jnp.where(in_grp, 1.0, 0.0)
        before = _dot(tri_ref[...], onehot.astype(BF16))
        count = jnp.sum(onehot, axis=0, keepdims=True)
        first = jnp.zeros((1, LANES), F32)
        start = jnp.float32(0.0)
        for g in range(MOE_GROUPS):
            n_g = jnp.sum(jnp.where(lane_row == g, count, 0.0))
            seg_ref[s, g] = start.astype(jnp.int32)
            seg_ref[s, MOE_GROUPS + g] = n_g.astype(jnp.int32)
            first = jnp.where(lane_row == g, start, first)
            start = start + n_g
        dest = jnp.sum(onehot * (before + first), axis=1, keepdims=True).astype(jnp.int32)
        g_hi, g_mid, g_lo = _split3(gate)
        xn = xn_ref[tile_rows(s), :]
        for r in range(0, tm, sr):
            rs = slice(r, r + sr)
            pt_ref[s, rs, :] = jnp.where(lax.broadcasted_iota(jnp.int32, (sr, tm), 1) == dest[rs],
                                         1.0, 0.0).astype(BF16)
        for r in range(0, tm, sr):
            rs = slice(r, r + sr)
            pt_cols = pt_ref[s, :, rs]
            xs_ref[s, rs, :] = _dot_tn(pt_cols, xn).astype(BF16)
            gs_ref[s, rs, :] = _dot_tn(pt_cols, g_hi) + _dot_tn(pt_cols, g_mid) + _dot_tn(pt_cols, g_lo)
        xs_ref[s, tm:, :] = jnp.zeros((win, xs_ref.shape[2]), BF16)
        gs_ref[s, tm:, :] = jnp.zeros((win, LANES), F32)
        acc_ref[s] = jnp.zeros(acc_ref.shape[1:], F32)
        return carry

    @pl.when(e == 0)
    def _():
        lax.fori_loop(0, n_sub, sort_tile, 0)

    grp = e // MOE_EPG
    for s in range(n_sub):
        seg_first = seg_ref[s, grp]
        seg_rows = seg_ref[s, MOE_GROUPS + grp]
        row0 = (seg_first // BF16_ROWS) * BF16_ROWS
        n_win = (seg_first + seg_rows - row0 + win - 1) // win

        def window(w, carry, s=s, row0=row0):
            rows = pl.ds(pl.multiple_of(row0 + w * win, BF16_ROWS), win)
            x = xs_ref[s, rows, :]
            gsw = gs_ref[s, rows, :]
            lane = lax.broadcasted_iota(jnp.int32, gsw.shape, 1)
            ge = jnp.sum(jnp.where(lane == GATE_COL0 + e, gsw, 0.0), axis=1, keepdims=True)
            a = _dot(x, w1_ref[0])
            b = _dot(x, w3_ref[0])
            act = (a * (1.0 / (1.0 + jnp.exp(-a))) * b).astype(BF16)
            acc_ref[s, rows, :] += ge * _dot(act, w2_ref[0])
            return carry

        lax.fori_loop(0, n_win, window, 0)

    def unsort_tile(s, carry):
        a_hi, a_mid, _ = _split3(acc_ref[s, 0:tm, :])
        for r in range(0, tm, sr):
            rows = pl.ds(pl.multiple_of(s * tm + r, sr), sr)
            pt = pt_ref[s, r:r + sr, :]
            y = h_ref[rows, :] + (_dot(pt, a_hi) + _dot(pt, a_mid))
            o_ref[rows, :] = _rms(y, gf_ref[...]) if final_norm else y
        return carry

    @pl.when(e == pl.num_programs(1) - 1)
    def _():
        lax.fori_loop(0, n_sub, unsort_tile, 0)


def _moe(h, xn, gate, w1, w3, w2, g_final, final_norm, tm):
    n, d = h.shape
    n_e = w1.shape[0]
    tme = MOE_TM if n % MOE_TM == 0 else tm
    n_sub = MOE_SUBTILES if n % (MOE_SUBTILES * tme) == 0 else 1
    blk = n_sub * tme
    win = -(-(tme * 19 // 64) // BF16_ROWS) * BF16_ROWS
    row2 = lambda w: pl.BlockSpec((blk, w), lambda i, e: (i, 0), pipeline_mode=pl.Buffered(1))
    wspec = lambda w: pl.BlockSpec((1,) + w.shape[1:], lambda i, e: (e, 0, 0))
    kern = functools.partial(_expert_kernel, final_norm=final_norm, tm=tme, win=win)
    return pl.pallas_call(
        kern,
        grid=(n // blk, n_e),
        in_specs=[row2(d), row2(LANES),
                  wspec(w1), wspec(w3), wspec(w2),
                  row2(d), pl.BlockSpec((1, d), lambda i, e: (0, 0))],
        out_specs=row2(d),
        out_shape=jax.ShapeDtypeStruct((n, d), F32),
        scratch_shapes=[pltpu.VMEM((n_sub, tme + win, d), F32), pltpu.VMEM((n_sub, tme + win, d), BF16),
                        pltpu.VMEM((n_sub, tme + win, LANES), F32), pltpu.VMEM((n_sub, tme, tme), BF16),
                        pltpu.VMEM((tme, tme), BF16), pltpu.SMEM((n_sub, 2 * MOE_GROUPS), jnp.int32)],
        compiler_params=pltpu.CompilerParams(dimension_semantics=("arbitrary", "arbitrary"),
                                             vmem_limit_bytes=MOE_VMEM_LIMIT),
        name="moe_experts",
    )(xn, gate, w1, w3, w2, h, g_final.reshape(1, d))


EVEN_SEGS = ((0, 512, None, False), (512, 512, "full", False), (1024, 128, "full", True), (1152, 128, None, True),
             (1280, 256, "full", False), (1536, 128, "half", False))
ODD_SEGS = ((0, 512, None, False), (512, 512, None, True), (1024, 512, None, True),
            (1536, 512, "full", False), (2048, 512, "full", True), (2560, 512, None, True))


def _cat_keys(hist, new):
    allk = jnp.concatenate([hist, new], axis=1) if hist is not None else new
    n_keys = allk.shape[1]
    pad = -n_keys % LANES
    if pad:
        allk = jnp.pad(allk, ((0, 0), (0, pad), (0, 0)))
    return allk.astype(BF16), n_keys


def kernel(x_prompt, x_sample, cache_pool, cache_dsa_k, cache_dsa_v, cache_idx_k, cache_sb_k, cache_sb_v,
           cache_diff_k, cache_diff_v, norm_mix, norm_ffn, norm_final, w_in_even, w_pool, pool_scale,
           w_out_even, w_in_odd, diff_lambda, diff_subln, w_out_odd, moe_w_group, moe_b_group,
           moe_w_expert, moe_b_expert, moe_w1, moe_w3, moe_w2):
    b, t, d = x_prompt.shape
    bd, td, _ = x_sample.shape
    past = cache_dsa_k.shape[2]
    depth = norm_mix.shape[0]
    groups = ((b, t, 0, min(512, b * t)), (bd, td, past, bd * td))

    tabs = []
    for (gb, gt, p0, tm) in groups:
        tab = _rope_tables(p0 + jnp.arange(gt, dtype=jnp.int32))
        if tm > gt:
            tab = jnp.tile(tab, (tm // gt, 1))
        tabs.append(tab)

    hs = [x_prompt.reshape(b * t, d), x_sample.reshape(bd * td, d)]
    outs = [dict(), dict()]
    for l in range(depth):
        li = l // 2
        last = l == depth - 1
        if l % 2 == 0:
            n_in = w_in_even.shape[2]
            w_in = jnp.pad(w_in_even[li], ((0, 0), (0, -n_in % LANES))).astype(BF16)
            w_out = w_out_even[li].astype(BF16)
            wp = w_pool[li].astype(BF16)
        else:
            w_in = w_in_odd[li].astype(BF16)
            w_out = w_out_odd[li].astype(BF16)
            lam_init = 0.8 - 0.6 * math.exp(-0.3 * l)
        wr = jnp.concatenate([moe_w_group[l]] + [moe_w_expert[l, g] for g in range(MOE_GROUPS)], axis=1)
        wr = jnp.pad(wr, ((0, 0), (0, LANES - wr.shape[1]))).astype(BF16)
        br = jnp.concatenate([moe_b_group[l], moe_b_expert[l].reshape(-1)])
        br = jnp.pad(br, (0, LANES - br.shape[0])).reshape(1, LANES).astype(F32)
        w1, w3, w2 = moe_w1[l].astype(BF16), moe_w3[l].astype(BF16), moe_w2[l].astype(BF16)

        for gi, (gb, gt, p0, tm) in enumerate(groups):
            h = hs[gi]
            o = outs[gi]
            sample = gi == 1
            r3 = lambda x: x.reshape(gb, gt, x.shape[-1])
            if l % 2 == 0:
                u, q, k, v, qi, kiwi, k16, v16 = [
                    r3(x) for x in _project(h, norm_mix[l], w_in, tabs[gi], EVEN_SEGS, tm)]
                ki = kiwi[..., :HEAD_DIM]
                hist = cache_pool[li] if sample else jnp.zeros((gb, POOL_HIST, u.shape[2]), F32)
                a_out = _pool_mix(u, hist, wp, pool_scale[li], p0)
                if sample:
                    k_all, n_keys = _cat_keys(cache_dsa_k[li].reshape(gb, past, -1), k)
                    v_all, _ = _cat_keys(cache_dsa_v[li].reshape(gb, past, -1), v)
                    ki_all, _ = _cat_keys(cache_idx_k[li], ki)
                else:
                    (k_all, n_keys), (v_all, _), (ki_all, _) = _cat_keys(None, k16), _cat_keys(None, v16), _cat_keys(None, ki)
                b_out = _dsa(q, qi, kiwi, k_all, v_all, ki_all, n_keys, p0)
                o.setdefault("pool", []).append(jnp.concatenate([hist, u], axis=1)[:, -POOL_HIST:])
                o.setdefault("dsa_k", []).append(k.reshape(gb, gt, -1, HEAD_DIM))
                o.setdefault("dsa_v", []).append(v.reshape(gb, gt, -1, HEAD_DIM))
                o.setdefault("idx_k", []).append(ki)
                mix_a, mix_b = a_out, b_out
            else:
                sq, sk, sv, dq, dk, dv, sk16, sv16, dk16, dv16 = [
                    r3(x) for x in _project(h, norm_mix[l], w_in, tabs[gi], ODD_SEGS, tm)]
                if sample:
                    sk_all, n_keys = _cat_keys(cache_sb_k[li].reshape(gb, past, -1), sk)
                    sv_all, _ = _cat_keys(cache_sb_v[li].reshape(gb, past, -1), sv)
                    dk_all, _ = _cat_keys(cache_diff_k[li].reshape(gb, past, -1), dk)
                    dv_all, _ = _cat_keys(cache_diff_v[li].reshape(gb, past, -1), dv)
                else:
                    (sk_all, n_keys), (sv_all, _) = _cat_keys(None, sk16), _cat_keys(None, sv16)
                    (dk_all, _), (dv_all, _) = _cat_keys(None, dk16), _cat_keys(None, dv16)
                c_out = _sb_attend(sq, sk_all, sv_all, n_keys, p0)
                d_out = _diff_attend(dq, dk_all, dv_all, diff_lambda[li], diff_subln[li], n_keys, p0, lam_init)
                n_sb = sk.shape[2] // HEAD_DIM
                n_df = dk.shape[2] // (2 * HEAD_DIM)
                o.setdefault("sb_k", []).append(sk.reshape(gb, gt, n_sb, HEAD_DIM))
                o.setdefault("sb_v", []).append(sv.reshape(gb, gt, n_sb, HEAD_DIM))
                o.setdefault("diff_k", []).append(dk.reshape(gb, gt, n_df, 2, HEAD_DIM))
                o.setdefault("diff_v", []).append(dv.reshape(gb, gt, n_df, 2 * HEAD_DIM))
                mix_a, mix_b = c_out, d_out
            h, xn, gate = _out_proj_route(h, mix_a.reshape(gb * gt, -1), mix_b.reshape(gb * gt, -1), w_out,
                                          norm_ffn[l], wr, br, tm)
            hs[gi] = _moe(h, xn, gate, w1, w3, w2, norm_final, last, tm)

    names = ("pool", "dsa_k", "dsa_v", "idx_k", "sb_k", "sb_v", "diff_k", "diff_v")
    res = [hs[0].reshape(b, t, d), hs[1].reshape(bd, td, d)]
    for o in outs:
        res += [jnp.stack(o[nm]) for nm in names]
    return tuple(res)
```

```python
import functools
import math

import jax
import jax.numpy as jnp
import numpy as np
from jax import lax
from jax.experimental import pallas as pl
from jax.experimental.pallas import tpu as pltpu

F32 = jnp.float32
BF16 = jnp.bfloat16

LANES = 128
HEAD_DIM = 64
CHUNK = 64
ROPE_THETA = 10000.0
NORM_EPS = 1e-6
NEG_INF = -1e30
PAD_SCORE = -3e38
BIG_POS = 3e38
POOL_WINDOWS = (2, 4, 8, 16)
POOL_HIST = 15
POOL_HIST_PAD = 16
DSA_TOPK = 256
IDX_HEADS = 4
MOE_GROUPS = 4
MOE_EPG = 4
MOE_EXPERTS = 16
GATE_COL0 = MOE_GROUPS
GROUP_LANE = 0
BF16_ROWS = 16
SORT_ROWS = 256
VMEM_LIMIT = 56 * 1024 * 1024
BISECT_STEPS = 8
BISECT_ROUNDS = 48
BISECT_UNCHECKED = 2
ATTN_TQ = 256
DSA_TQ = 128
MOE_TM = 1024
MOE_SUBTILES = 2
MOE_VMEM_LIMIT = 62 * 1024 * 1024
DIFF_LANES = 2 * LANES


def _cparams(sem):
    return pltpu.CompilerParams(dimension_semantics=sem, vmem_limit_bytes=VMEM_LIMIT)


def _dot(a, b):
    return jnp.dot(a, b, preferred_element_type=F32)


def _dot_nt(a, b):
    return lax.dot_general(a, b, (((1,), (1,)), ((), ())), preferred_element_type=F32)


def _dot_tn(a, b):
    return lax.dot_general(a, b, (((0,), (0,)), ((), ())), preferred_element_type=F32)


def _rms(x, g):
    ms = jnp.mean(x * x, axis=-1, keepdims=True)
    return x * lax.rsqrt(ms + NORM_EPS) * g


def _proj_kernel(x_ref, g_ref, w_ref, tab_ref, *out_refs, segs):
    xn = _rms(x_ref[...], g_ref[...]).astype(BF16)
    copies = iter(out_refs[len(segs):])
    for o_ref, (c0, width, mode, twin) in zip(out_refs, segs):
        t_ref = next(copies) if twin else None
        y = _dot(xn, w_ref[:, c0:c0 + width])
        if mode is None:
            o_ref[...] = y
            if twin:
                t_ref[...] = y.astype(BF16)
            continue
        t0 = 0 if mode == "full" else 3 * LANES
        cos = tab_ref[:, t0:t0 + LANES]
        sin_a = tab_ref[:, t0 + LANES:t0 + 2 * LANES]
        sin_b = tab_ref[:, t0 + 2 * LANES:t0 + 3 * LANES]
        for c in range(0, width, LANES):
            yc = y[:, c:c + LANES]
            yr = (yc * cos + pltpu.roll(yc, LANES - HEAD_DIM // 2, 1) * sin_a
                  + pltpu.roll(yc, HEAD_DIM // 2, 1) * sin_b)
            o_ref[:, c:c + LANES] = yr
            if twin:
                t_ref[:, c:c + LANES] = yr.astype(BF16)


def _rope_tables(pos):
    half = HEAD_DIM // 2
    inv = ROPE_THETA ** (-jnp.arange(half, dtype=F32) / half)
    ang = pos.astype(F32)[:, None] * inv[None, :]
    cos, sin = jnp.cos(ang), jnp.sin(ang)
    zero, one = jnp.zeros_like(sin), jnp.ones_like(cos)
    cos_h = jnp.concatenate([cos, cos], axis=1)
    sa_h = jnp.concatenate([-sin, zero], axis=1)
    sb_h = jnp.concatenate([zero, sin], axis=1)
    one_h = jnp.concatenate([one, one], axis=1)
    zero_h = jnp.concatenate([zero, zero], axis=1)
    return jnp.concatenate([cos_h, cos_h, sa_h, sa_h, sb_h, sb_h,
                            cos_h, one_h, sa_h, zero_h, sb_h, zero_h], axis=1)


def _project(x, g, w, tab, segs, tm):
    n, d = x.shape
    tt = tab.shape[0]
    nt = tt // tm
    kern = functools.partial(_proj_kernel, segs=segs)
    return pl.pallas_call(
        kern,
        grid=(n // tm,),
        in_specs=[pl.BlockSpec((tm, d), lambda i: (i, 0)),
                  pl.BlockSpec((1, d), lambda i: (0, 0)),
                  pl.BlockSpec(w.shape, lambda i: (0, 0)),
                  pl.BlockSpec((tm, tab.shape[1]), lambda i: (i % nt, 0))],
        out_specs=[pl.BlockSpec((tm, wd), lambda i: (i, 0)) for _, wd, _, _ in segs]
        + [pl.BlockSpec((tm, wd), lambda i: (i, 0)) for _, wd, _, twin in segs if twin],
        out_shape=[jax.ShapeDtypeStruct((n, wd), F32) for _, wd, _, _ in segs]
        + [jax.ShapeDtypeStruct((n, wd), BF16) for _, wd, _, twin in segs if twin],
        compiler_params=_cparams(("parallel",)),
        name="proj",
    )(x, g.reshape(1, d), w, tab)


def _pool_kernel(u_ref, h_ref, w_ref, s_ref, o_ref, ext_ref, *, t, pos0, rc):
    ext_ref[0:POOL_HIST_PAD, :] = h_ref[0]
    ext_ref[POOL_HIST_PAD:POOL_HIST_PAD + t, :] = u_ref[0]
    for r0 in range(0, t, rc):
        pos = pos0 + r0 + lax.broadcasted_iota(jnp.int32, (rc, 1), 0)
        for g, win in enumerate(POOL_WINDOWS):
            c0 = g * LANES
            u_new = ext_ref[POOL_HIST_PAD + r0:POOL_HIST_PAD + r0 + rc, c0:c0 + LANES]
            s = u_new
            for k in range(1, win):
                s = s + ext_ref[POOL_HIST_PAD + r0 - k:POOL_HIST_PAD + r0 - k + rc, c0:c0 + LANES]
            cnt = jnp.minimum(pos + 1, win).astype(F32)
            dlt = (s / cnt - u_new).astype(BF16)
            o_ref[0, r0:r0 + rc, c0:c0 + LANES] = _dot(dlt, w_ref[g]) * s_ref[:, c0:c0 + LANES]


def _pool_mix(u, hist, w_pool, pool_scale, pos0):
    b, t, c = u.shape
    rc = min(t, 256)
    hist16 = jnp.pad(hist, ((0, 0), (POOL_HIST_PAD - POOL_HIST, 0), (0, 0)))
    kern = functools.partial(_pool_kernel, t=t, pos0=pos0, rc=rc)
    return pl.pallas_call(
        kern,
        grid=(b,),
        in_specs=[pl.BlockSpec((1, t, c), lambda i: (i, 0, 0)),
                  pl.BlockSpec((1, POOL_HIST_PAD, c), lambda i: (i, 0, 0)),
                  pl.BlockSpec(w_pool.shape, lambda i: (0, 0, 0)),
                  pl.BlockSpec((1, c), lambda i: (0, 0))],
        out_specs=pl.BlockSpec((1, t, c), lambda i: (i, 0, 0)),
        out_shape=jax.ShapeDtypeStruct((b, t, c), F32),
        scratch_shapes=[pltpu.VMEM((POOL_HIST_PAD + t, c), F32)],
        compiler_params=_cparams(("parallel",)),
        name="pool_mix",
    )(u, hist16, w_pool, pool_scale.reshape(1, c))


def _dsa_kernel(q_ref, qi_ref, kw_ref, k_ref, v_ref, ki_ref, o_ref, lo_ref, hi_ref, bias_ref,
                *, tq, extents, n_keys, q_pos0, n_sel):
    i = pl.program_id(1)
    qpos = q_pos0 + i * tq + lax.broadcasted_iota(jnp.int32, (tq, 1), 0)
    lim = jnp.minimum((qpos // CHUNK + 1) * CHUNK, n_keys)
    kf = float(n_sel)
    low = lax.broadcasted_iota(jnp.int32, (tq, LANES), 1) < HEAD_DIM
    q = q_ref[0] * HEAD_DIM ** -0.5
    qi = qi_ref[0].astype(BF16)
    wi = kw_ref[0][:, HEAD_DIM:HEAD_DIM + IDX_HEADS] * (IDX_HEADS * HEAD_DIM) ** -0.5

    def body(ext):
        kpos = lax.broadcasted_iota(jnp.int32, (tq, ext), 1)
        adm = kpos < lim
        padded = ext > n_keys
        virt = float(max(n_keys - ext, 0))

        sidx = jnp.zeros((tq, ext), F32)
        ki = ki_ref[0, :ext, :]
        for h in range(IDX_HEADS):
            sh = _dot_nt(qi[:, h * HEAD_DIM:(h + 1) * HEAD_DIM], ki)
            sidx = sidx + jnp.maximum(sh, 0.0) * wi[:, h:h + 1]
        sm = jnp.where(adm, sidx, NEG_INF)
        if padded:
            real = kpos < n_keys
            sm = jnp.where(real, sm, PAD_SCORE)

        def count_gt_wide(x):
            part = jnp.where(sm[:, :LANES] > x, 1.0, 0.0)
            for j in range(1, ext // LANES):
                part = part + jnp.where(sm[:, j * LANES:(j + 1) * LANES] > x, 1.0, 0.0)
            cnt = jnp.broadcast_to(jnp.sum(part, axis=1, keepdims=True), (tq, LANES))
            return cnt + jnp.where(x < NEG_INF, virt, 0.0) if virt else cnt

        def count_gt(x):
            return count_gt_wide(jnp.broadcast_to(x, (tq, LANES)))[:, :1]

        def bracket(lo, hi):
            above = jnp.min(jnp.where(sm > lo, sm, BIG_POS), axis=1, keepdims=True)
            below = jnp.max(jnp.where(sm <= hi, sm, PAD_SCORE), axis=1, keepdims=True)
            if virt:
                above = jnp.minimum(above, jnp.where(lo < NEG_INF, NEG_INF, BIG_POS))
                below = jnp.maximum(below, jnp.where(hi >= NEG_INF, NEG_INF, PAD_SCORE))
            return above, below

        row_max = jnp.max(sm, axis=1, keepdims=True)
        row_min = jnp.min(jnp.where(real, sm, BIG_POS) if padded else sm, axis=1, keepdims=True)
        if virt:
            row_min = jnp.minimum(row_min, NEG_INF)
        adm_min = jnp.min(jnp.where(adm, sm, BIG_POS), axis=1, keepdims=True)
        few = count_gt(row_min) < kf
        tight = count_gt(adm_min) >= kf
        lo_ref[...] = jnp.where(few, PAD_SCORE, jnp.where(tight, adm_min, row_min))
        hi_ref[...] = jnp.where(few, row_min, jnp.where(tight, row_max, adm_min))

        def unresolved(lo, hi):
            above, below = bracket(lo, hi)
            return jnp.sum(jnp.where(above < below, 1, 0))

        def cond(carry):
            rounds, open_rows = carry
            return jnp.logical_and(open_rows > 0, rounds < BISECT_ROUNDS)

        def step(carry):
            rounds, _ = carry
            lo = jnp.broadcast_to(lo_ref[...], (tq, LANES))
            hi = jnp.broadcast_to(hi_ref[...], (tq, LANES))
            for _ in range(BISECT_STEPS):
                mid = 0.5 * lo + 0.5 * hi
                under = count_gt_wide(mid) < kf
                hi = jnp.where(under, mid, hi)
                lo = jnp.where(under, lo, mid)
            lo_ref[...] = lo[:, :1]
            hi_ref[...] = hi[:, :1]
            open_rows = lax.cond(rounds + 1 >= BISECT_UNCHECKED,
                                 lambda: unresolved(lo_ref[...], hi_ref[...]), lambda: jnp.int32(1))
            return rounds + 1, open_rows

        lax.while_loop(cond, step, (jnp.int32(0), jnp.sum(jnp.where(few, 0, 1))))
        _, thr = bracket(lo_ref[...], hi_ref[...])

        gt = sm > thr
        eq = sm == thr
        need = kf - count_gt(thr)
        n_eq = jnp.sum(jnp.where(eq, 1.0, 0.0), axis=1, keepdims=True)
        crowded = jnp.sum(jnp.where(n_eq > need, 1, 0))

        @pl.when(crowded == 0)
        def _():
            keep = jnp.logical_and(jnp.logical_or(gt, eq), adm)
            bias_ref[:, :ext] = jnp.where(keep, 0.0, NEG_INF)

        @pl.when(crowded > 0)
        def _():
            rr = lax.broadcasted_iota(jnp.int32, (LANES, LANES), 0)
            cc = lax.broadcasted_iota(jnp.int32, (LANES, LANES), 1)
            prefix_ones = jnp.where(rr <= cc, 1.0, 0.0).astype(BF16)
            carry = jnp.zeros((tq, 1), F32)
            for j in range(ext // LANES):
                sl = slice(j * LANES, (j + 1) * LANES)
                eq_j = eq[:, sl]
                rank = _dot(jnp.where(eq_j, 1.0, 0.0).astype(BF16), prefix_ones) + carry
                keep = jnp.logical_or(gt[:, sl], jnp.logical_and(eq_j, rank <= need))
                bias_ref[:, sl] = jnp.where(jnp.logical_and(keep, adm[:, sl]), 0.0, NEG_INF)
                carry = rank[:, LANES - 1:LANES]

        bias = bias_ref[:, :ext]
        bias2 = jnp.concatenate([bias, bias], axis=0)

        kk = k_ref[0, :ext, :]
        vv = v_ref[0, :ext, :]
        low_k = lax.broadcasted_iota(jnp.int32, (ext, LANES), 1) < HEAD_DIM
        vsw = pltpu.roll(vv.astype(F32), HEAD_DIM, 1).astype(BF16)
        one = jnp.ones_like(vv)
        n_kv = LANES // HEAD_DIM
        stacks = [(g, odd) for g in range(n_kv) for odd in range(2)]
        scores = []
        for g, odd in stacks:
            rows = []
            for m in range(2):
                c = 2 * g + m
                qc = q[:, c * LANES:(c + 1) * LANES]
                if (odd == 1) != (g == 1):
                    qc = pltpu.roll(qc, HEAD_DIM, 1)
                rows.append(jnp.where(low, qc, 0.0) if g == 0 else jnp.where(low, 0.0, qc))
            qs = jnp.concatenate(rows, axis=0).astype(BF16)
            scores.append(_dot_nt(qs, kk) + bias2)
        probs = [jnp.exp(sc - jnp.max(sc, axis=1, keepdims=True)).astype(BF16) for sc in scores]
        outs = []
        for (g, odd), p in zip(stacks, probs):
            if odd:
                v_aug = jnp.where(low_k, one, vsw if g == 0 else vv)
            else:
                v_aug = jnp.where(low_k, vv if g == 0 else vsw, one)
            og = _dot(p, v_aug)
            outs.append(og / pltpu.roll(og, HEAD_DIM, 1))
        for g in range(n_kv):
            for m in range(2):
                c = 2 * g + m
                o_ref[0, :, c * LANES:(c + 1) * LANES] = jnp.where(
                    low, outs[2 * g][m * tq:(m + 1) * tq], outs[2 * g + 1][m * tq:(m + 1) * tq])

    _for_tile_extent(i, tq, q_pos0, n_keys, extents, body)


def _dsa(q, qi, kiwi, k_all, v_all, ki_all, n_keys, q_pos0):
    b, t, dq = q.shape
    s_len = k_all.shape[1]
    tq = min(t, DSA_TQ)
    n_sel = min(DSA_TOPK, n_keys // 4)
    extents = _key_extents(t // tq, tq, q_pos0, n_keys, s_len, 4 * LANES)
    kern = functools.partial(_dsa_kernel, tq=tq, extents=extents, n_keys=n_keys, q_pos0=q_pos0, n_sel=n_sel)
    qspec = lambda w: pl.BlockSpec((1, tq, w), lambda bi, i: (bi, i, 0))
    kspec = lambda w: pl.BlockSpec((1, s_len, w), lambda bi, i: (bi, 0, 0))
    return pl.pallas_call(
        kern,
        grid=(b, t // tq),
        in_specs=[qspec(dq), qspec(qi.shape[2]), qspec(kiwi.shape[2]),
                  kspec(k_all.shape[2]), kspec(v_all.shape[2]), kspec(ki_all.shape[2])],
        out_specs=qspec(dq),
        out_shape=jax.ShapeDtypeStruct((b, t, dq), F32),
        scratch_shapes=[pltpu.VMEM((tq, 1), F32), pltpu.VMEM((tq, 1), F32), pltpu.VMEM((tq, s_len), F32)],
        compiler_params=_cparams(("parallel", "parallel")),
        name="dsa",
    )(q, qi, kiwi, k_all, v_all, ki_all)


def _sb_kernel(q_ref, k_ref, v_ref, o_ref, acc_ref, run_ref, *, tq, n_keys, q_pos0):
    i = pl.program_id(1)
    pairs = q_ref.shape[2] // LANES
    first_q = q_pos0 + i * tq
    qpos = first_q + lax.broadcasted_iota(jnp.int32, (tq, 1), 0)
    n_blocks = (jnp.minimum(first_q + tq - 1, n_keys) + LANES - 1) // LANES
    n_full = jnp.minimum(first_q, n_keys) // LANES
    low_q = lax.broadcasted_iota(jnp.int32, (tq, LANES), 1) < HEAD_DIM
    low_k = lax.broadcasted_iota(jnp.int32, (LANES, LANES), 1) < HEAD_DIM
    q = q_ref[0] * HEAD_DIM ** -0.5
    qm = []
    for p in range(pairs):
        qp = q[:, p * LANES:(p + 1) * LANES]
        qm.append((jnp.where(low_q, qp, 0.0).astype(BF16), jnp.where(low_q, 0.0, qp).astype(BF16)))
    rr = lax.broadcasted_iota(jnp.int32, (2 * LANES, 2 * LANES), 0)
    cc = lax.broadcasted_iota(jnp.int32, (2 * LANES, 2 * LANES), 1)
    rk = jnp.where(rr >= LANES, rr - LANES, rr)
    cs_rhs = jnp.where(jnp.logical_or(cc >= LANES, rk >= cc), 1.0, 0.0).astype(BF16)
    acc_ref[...] = jnp.zeros_like(acc_ref)
    run_ref[...] = jnp.zeros_like(run_ref)

    def block(j, masked):
        ks = pl.multiple_of(j * LANES, LANES)
        if masked:
            kpos = ks + lax.broadcasted_iota(jnp.int32, (1, LANES), 1)
            causal = jnp.logical_and(kpos < qpos, kpos < n_keys)
        heads = [(p, c) for p in range(pairs) for c in range(2)]
        lanes = lambda n: slice(n * LANES, (n + 1) * LANES)
        zs = [_dot_nt(qm[p][c], k_ref[0, pl.ds(ks, LANES), lanes(p)]) for p, c in heads]
        css = []
        for z in zs:
            sp = jnp.maximum(z, 0.0) + jnp.log(1.0 + jnp.exp(-jnp.abs(z)))
            if masked:
                sp = jnp.where(causal, sp, 0.0)
            hi = sp.astype(BF16)
            lo = (sp - hi.astype(F32)).astype(BF16)
            css.append(_dot(jnp.concatenate([hi, lo], axis=1), cs_rhs))
        probs = []
        for n, (z, cs) in enumerate(zip(zs, css)):
            run = run_ref[:, lanes(n)]
            a = jnp.exp(z - cs[:, :LANES] - run)
            if masked:
                a = jnp.where(causal, a, 0.0)
            run_ref[:, lanes(n)] = run + cs[:, LANES:]
            probs.append(a.astype(BF16))
        for p in range(pairs):
            vb = v_ref[0, pl.ds(ks, LANES), lanes(p)]
            zero = jnp.zeros_like(vb)
            v_cat = jnp.concatenate([jnp.where(low_k, vb, zero), jnp.where(low_k, zero, vb)], axis=0)
            acc_ref[:, lanes(p)] += _dot(jnp.concatenate(probs[2 * p:2 * p + 2], axis=1), v_cat)

    def masked_step(jj, carry):
        block(n_blocks - 1 - jj, True)
        return carry

    def full_pair(jj, carry):
        block(n_full - 1 - 2 * jj, False)
        block(n_full - 2 - 2 * jj, False)
        return carry

    lax.fori_loop(0, n_blocks - n_full, masked_step, 0)
    lax.fori_loop(0, n_full // 2, full_pair, 0)

    @pl.when(n_full % 2 == 1)
    def _():
        block(jnp.int32(0), False)

    o_ref[0] = acc_ref[...]


def _sb_attend(q, k_all, v_all, n_keys, q_pos0):
    b, t, d = q.shape
    s_len = k_all.shape[1]
    tq = min(t, ATTN_TQ)
    kern = functools.partial(_sb_kernel, tq=tq, n_keys=n_keys, q_pos0=q_pos0)
    return pl.pallas_call(
        kern,
        grid=(b, t // tq),
        in_specs=[pl.BlockSpec((1, tq, d), lambda bi, i: (bi, i, 0)),
                  pl.BlockSpec((1, s_len, d), lambda bi, i: (bi, 0, 0)),
                  pl.BlockSpec((1, s_len, d), lambda bi, i: (bi, 0, 0))],
        out_specs=pl.BlockSpec((1, tq, d), lambda bi, i: (bi, i, 0)),
        out_shape=jax.ShapeDtypeStruct((b, t, d), F32),
        scratch_shapes=[pltpu.VMEM((tq, d), F32), pltpu.VMEM((tq, 2 * d), F32)],
        compiler_params=_cparams(("parallel", "parallel")),
        name="sb_attend",
    )(q, k_all, v_all)


def _key_extents(nq, tq, q_pos0, n_keys, s_len, step):
    need = [min(n_keys, ((q_pos0 + (i + 1) * tq - 1) // CHUNK + 1) * CHUNK) for i in range(nq)]
    return tuple(sorted({min(s_len, -(-n // step) * step) for n in need}))


def _for_tile_extent(i, tq, q_pos0, n_keys, extents, body):
    need = jnp.minimum(n_keys, ((q_pos0 + (i + 1) * tq - 1) // CHUNK + 1) * CHUNK)
    prev = 0
    for ext in extents:
        pl.when(jnp.logical_and(need > prev, need <= ext))(functools.partial(body, ext))
        prev = ext


def _diff_kernel(q_ref, k_ref, v_ref, lam_ref, gain_ref, o_ref, *, tq, extents, n_keys, q_pos0, lam_init):
    i = pl.program_id(2)
    lp = lam_ref[...]
    lam = (jnp.exp(jnp.sum(lp[0:1] * lp[1:2], axis=1, keepdims=True))
           - jnp.exp(jnp.sum(lp[2:3] * lp[3:4], axis=1, keepdims=True)) + lam_init)
    qpos = q_pos0 + i * tq + lax.broadcasted_iota(jnp.int32, (tq, 1), 0)
    lim = jnp.minimum((qpos // CHUNK + 1) * CHUNK, n_keys)
    low = lax.broadcasted_iota(jnp.int32, (tq, LANES), 1) < HEAD_DIM

    def body(ext):
        mask = lax.broadcasted_iota(jnp.int32, (tq, ext), 1) < lim
        heads = range(q_ref.shape[2] // LANES)
        lanes = lambda h: slice(h * LANES, (h + 1) * LANES)
        scores = []
        for h in heads:
            q = q_ref[0, :, lanes(h)] * HEAD_DIM ** -0.5
            kk = k_ref[0, :ext, lanes(h)]
            for c in range(2):
                qc = (jnp.where(low, q, 0.0) if c == 0 else jnp.where(low, 0.0, q)).astype(BF16)
                scores.append(jnp.where(mask, _dot_nt(qc, kk), NEG_INF))
        probs = []
        for sc in scores:
            p = jnp.exp(sc - jnp.max(sc, axis=1, keepdims=True))
            probs.append(p * (1.0 / jnp.sum(p, axis=1, keepdims=True)))
        for h in heads:
            a = probs[2 * h] - lam * probs[2 * h + 1]
            o = _dot(a.astype(BF16), v_ref[0, :ext, lanes(h)])
            o = o * lax.rsqrt(jnp.mean(o * o, axis=-1, keepdims=True) + NORM_EPS)
            o_ref[0, :, lanes(h)] = o * gain_ref[...] * (1.0 - lam_init)

    _for_tile_extent(i, tq, q_pos0, n_keys, extents, body)


def _diff_attend(q, k_all, v_all, diff_lambda, gain, n_keys, q_pos0, lam_init):
    b, t, d = q.shape
    s_len = k_all.shape[1]
    tq = min(t, ATTN_TQ)
    extents = _key_extents(t // tq, tq, q_pos0, n_keys, s_len, 2 * LANES)
    kern = functools.partial(_diff_kernel, tq=tq, extents=extents, n_keys=n_keys, q_pos0=q_pos0, lam_init=lam_init)
    return pl.pallas_call(
        kern,
        grid=(b, d // DIFF_LANES, t // tq),
        in_specs=[pl.BlockSpec((1, tq, DIFF_LANES), lambda bi, h, i: (bi, i, h)),
                  pl.BlockSpec((1, s_len, DIFF_LANES), lambda bi, h, i: (bi, 0, h)),
                  pl.BlockSpec((1, s_len, DIFF_LANES), lambda bi, h, i: (bi, 0, h)),
                  pl.BlockSpec(diff_lambda.shape, lambda bi, h, i: (0, 0)),
                  pl.BlockSpec((1, LANES), lambda bi, h, i: (0, 0))],
        out_specs=pl.BlockSpec((1, tq, DIFF_LANES), lambda bi, h, i: (bi, i, h)),
        out_shape=jax.ShapeDtypeStruct((b, t, d), F32),
        compiler_params=_cparams(("parallel", "parallel", "parallel")),
        name="diff_attend",
    )(q, k_all, v_all, diff_lambda, gain.reshape(1, LANES))


def _route(logits):
    lane = lax.broadcasted_iota(jnp.int32, logits.shape, 1)
    is_group = lane < MOE_GROUPS
    gl = jnp.where(is_group, logits, NEG_INF)
    g_max = jnp.max(gl, axis=1, keepdims=True)
    g_sel = jnp.min(jnp.where(gl == g_max, lane, LANES), axis=1, keepdims=True)
    g_gate = 1.0 / jnp.sum(jnp.where(is_group, jnp.exp(gl - g_max), 0.0), axis=1, keepdims=True)
    in_group = jnp.logical_and(lane >= GATE_COL0, (lane - GATE_COL0) // MOE_EPG == g_sel)
    in_group = jnp.logical_and(in_group, lane < GATE_COL0 + MOE_EXPERTS)
    el = jnp.where(in_group, logits, NEG_INF)
    top1 = jnp.max(el, axis=1, keepdims=True)
    i1 = jnp.min(jnp.where(jnp.logical_and(in_group, el == top1), lane, LANES), axis=1, keepdims=True)
    rest = jnp.logical_and(in_group, lane != i1)
    el2 = jnp.where(rest, logits, NEG_INF)
    top2 = jnp.max(el2, axis=1, keepdims=True)
    i2 = jnp.min(jnp.where(jnp.logical_and(rest, el2 == top2), lane, LANES), axis=1, keepdims=True)
    e2 = jnp.exp(top2 - top1)
    w1 = g_gate / (1.0 + e2)
    gates = jnp.where(lane == i1, w1, jnp.where(lane == i2, w1 * e2, 0.0))
    return jnp.where(lane == GROUP_LANE, g_sel.astype(F32), gates)


def _out_kernel(h_ref, a_ref, b_ref, wa_ref, wb_ref, g_ref, wr_ref, br_ref, o_ref, xn_ref, gate_ref):
    h = (h_ref[...] + _dot(a_ref[...].astype(BF16), wa_ref[...])
         + _dot(b_ref[...].astype(BF16), wb_ref[...]))
    o_ref[...] = h
    xn = _rms(h, g_ref[...]).astype(BF16)
    xn_ref[...] = xn
    gate_ref[...] = _route(_dot(xn, wr_ref[...]) + br_ref[...])


def _out_proj_route(h, a, bmix, w_out, g_ffn, wr, br, tm):
    n, d = h.shape
    ca = a.shape[1]
    wa, wb = w_out[:ca], w_out[ca:]
    row = lambda w: pl.BlockSpec((tm, w), lambda i: (i, 0))
    whole = lambda x: pl.BlockSpec(x.shape, lambda i: (0, 0))
    return pl.pallas_call(
        _out_kernel,
        grid=(n // tm,),
        in_specs=[row(d), row(ca), row(bmix.shape[1]), whole(wa), whole(wb),
                  pl.BlockSpec((1, d), lambda i: (0, 0)), whole(wr), whole(br)],
        out_specs=[row(d), row(d), row(LANES)],
        out_shape=[jax.ShapeDtypeStruct((n, d), F32), jax.ShapeDtypeStruct((n, d), BF16),
                   jax.ShapeDtypeStruct((n, LANES), F32)],
        compiler_params=_cparams(("parallel",)),
        name="out_proj_route",
    )(h, a, bmix, wa, wb, g_ffn.reshape(1, d), wr, br)


def _split3(x):
    hi = x.astype(BF16)
    r1 = x - hi.astype(F32)
    mid = r1.astype(BF16)
    lo = (r1 - mid.astype(F32)).astype(BF16)
    return hi, mid, lo


def _expert_kernel(xn_ref, gate_ref, w1_ref, w3_ref, w2_ref, h_ref, gf_ref, o_ref,
                   acc_ref, xs_ref, gs_ref, pt_ref, tri_ref, seg_ref, *, final_norm, tm, win):
    i = pl.program_id(0)
    e = pl.program_id(1)
    n_sub = xn_ref.shape[0] // tm
    sr = min(SORT_ROWS, tm)
    tile_rows = lambda s: pl.ds(pl.multiple_of(s * tm, tm), tm)

    @pl.when(jnp.logical_and(i == 0, e == 0))
    def _():
        rr = lax.broadcasted_iota(jnp.int32, (tm, tm), 0)
        cc = lax.broadcasted_iota(jnp.int32, (tm, tm), 1)
        tri_ref[...] = jnp.where(cc < rr, 1.0, 0.0).astype(BF16)

    def sort_tile(s, carry):
        gate = gate_ref[tile_rows(s), :]
        lane = lax.broadcasted_iota(jnp.int32, gate.shape, 1)
        lane_row = lax.broadcasted_iota(jnp.int32, (1, LANES), 1)
        in_grp = jnp.logical_and(lane < MOE_GROUPS, lane.astype(F32) == gate[:, GROUP_LANE:GROUP_LANE + 1])
        onehot = jnp.where(in_grp, 1.0, 0.0)
        before = _dot(tri_ref[...], onehot.astype(BF16))
        count = jnp.sum(onehot, axis=0, keepdims=True)
        first = jnp.zeros((1, LANES), F32)
        start = jnp.float32(0.0)
        for g in range(MOE_GROUPS):
            n_g = jnp.sum(jnp.where(lane_row == g, count, 0.0))
            seg_ref[s, g] = start.astype(jnp.int32)
            seg_ref[s, MOE_GROUPS + g] = n_g.astype(jnp.int32)
            first = jnp.where(lane_row == g, start, first)
            start = start + n_g
        dest = jnp.sum(onehot * (before + first), axis=1, keepdims=True).astype(jnp.int32)
        g_hi, g_mid, g_lo = _split3(gate)
        xn = xn_ref[tile_rows(s), :]
        for r in range(0, tm, sr):
            rs = slice(r, r + sr)
            pt_ref[s, rs, :] = jnp.where(lax.broadcasted_iota(jnp.int32, (sr, tm), 1) == dest[rs],
                                         1.0, 0.0).astype(BF16)
        for r in range(0, tm, sr):
            rs = slice(r, r + sr)
            pt_cols = pt_ref[s, :, rs]
            xs_ref[s, rs, :] = _dot_tn(pt_cols, xn).astype(BF16)
            gs_ref[s, rs, :] = _dot_tn(pt_cols, g_hi) + _dot_tn(pt_cols, g_mid) + _dot_tn(pt_cols, g_lo)
        xs_ref[s, tm:, :] = jnp.zeros((win, xs_ref.shape[2]), BF16)
        gs_ref[s, tm:, :] = jnp.zeros((win, LANES), F32)
        acc_ref[s] = jnp.zeros(acc_ref.shape[1:], F32)
        return carry

    @pl.when(e == 0)
    def _():
        lax.fori_loop(0, n_sub, sort_tile, 0)

    grp = e // MOE_EPG
    for s in range(n_sub):
        seg_first = seg_ref[s, grp]
        seg_rows = seg_ref[s, MOE_GROUPS + grp]
        row0 = (seg_first // BF16_ROWS) * BF16_ROWS
        n_win = (seg_first + seg_rows - row0 + win - 1) // win

        def window(w, carry, s=s, row0=row0):
            rows = pl.ds(pl.multiple_of(row0 + w * win, BF16_ROWS), win)
            x = xs_ref[s, rows, :]
            gsw = gs_ref[s, rows, :]
            lane = lax.broadcasted_iota(jnp.int32, gsw.shape, 1)
            ge = jnp.sum(jnp.where(lane == GATE_COL0 + e, gsw, 0.0), axis=1, keepdims=True)
            a = _dot(x, w1_ref[0])
            b = _dot(x, w3_ref[0])
            act = (a * (1.0 / (1.0 + jnp.exp(-a))) * b).astype(BF16)
            acc_ref[s, rows, :] += ge * _dot(act, w2_ref[0])
            return carry

        lax.fori_loop(0, n_win, window, 0)

    def unsort_tile(s, carry):
        a_hi, a_mid, _ = _split3(acc_ref[s, 0:tm, :])
        for r in range(0, tm, sr):
            rows = pl.ds(pl.multiple_of(s * tm + r, sr), sr)
            pt = pt_ref[s, r:r + sr, :]
            y = h_ref[rows, :] + (_dot(pt, a_hi) + _dot(pt, a_mid))
            o_ref[rows, :] = _rms(y, gf_ref[...]) if final_norm else y
        return carry

    @pl.when(e == pl.num_programs(1) - 1)
    def _():
        lax.fori_loop(0, n_sub, unsort_tile, 0)


def _moe(h, xn, gate, w1, w3, w2, g_final, final_norm, tm):
    n, d = h.shape
    n_e = w1.shape[0]
    tme = MOE_TM if n % MOE_TM == 0 else tm
    n_sub = MOE_SUBTILES if n % (MOE_SUBTILES * tme) == 0 else 1
    blk = n_sub * tme
    win = -(-(tme * 5 // 16) // BF16_ROWS) * BF16_ROWS
    row2 = lambda w: pl.BlockSpec((blk, w), lambda i, e: (i, 0), pipeline_mode=pl.Buffered(1))
    wspec = lambda w: pl.BlockSpec((1,) + w.shape[1:], lambda i, e: (e, 0, 0))
    kern = functools.partial(_expert_kernel, final_norm=final_norm, tm=tme, win=win)
    return pl.pallas_call(
        kern,
        grid=(n // blk, n_e),
        in_specs=[row2(d), row2(LANES),
                  wspec(w1), wspec(w3), wspec(w2),
                  row2(d), pl.BlockSpec((1, d), lambda i, e: (0, 0))],
        out_specs=row2(d),
        out_shape=jax.ShapeDtypeStruct((n, d), F32),
        scratch_shapes=[pltpu.VMEM((n_sub, tme + win, d), F32), pltpu.VMEM((n_sub, tme + win, d), BF16),
                        pltpu.VMEM((n_sub, tme + win, LANES), F32), pltpu.VMEM((n_sub, tme, tme), BF16),
                        pltpu.VMEM((tme, tme), BF16), pltpu.SMEM((n_sub, 2 * MOE_GROUPS), jnp.int32)],
        compiler_params=pltpu.CompilerParams(dimension_semantics=("arbitrary", "arbitrary"),
                                             vmem_limit_bytes=MOE_VMEM_LIMIT),
        name="moe_experts",
    )(xn, gate, w1, w3, w2, h, g_final.reshape(1, d))


EVEN_SEGS = ((0, 512, None, False), (512, 512, "full", False), (1024, 128, "full", True), (1152, 128, None, True),
             (1280, 256, "full", False), (1536, 128, "half", False))
ODD_SEGS = ((0, 512, None, False), (512, 512, None, True), (1024, 512, None, True),
            (1536, 512, "full", False), (2048, 512, "full", True), (2560, 512, None, True))


def _cat_keys(hist, new):
    allk = jnp.concatenate([hist, new], axis=1) if hist is not None else new
    n_keys = allk.shape[1]
    pad = -n_keys % LANES
    if pad:
        allk = jnp.pad(allk, ((0, 0), (0, pad), (0, 0)))
    return allk.astype(BF16), n_keys


def kernel(x_prompt, x_sample, cache_pool, cache_dsa_k, cache_dsa_v, cache_idx_k, cache_sb_k, cache_sb_v,
           cache_diff_k, cache_diff_v, norm_mix, norm_ffn, norm_final, w_in_even, w_pool, pool_scale,
           w_out_even, w_in_odd, diff_lambda, diff_subln, w_out_odd, moe_w_group, moe_b_group,
           moe_w_expert, moe_b_expert, moe_w1, moe_w3, moe_w2):
    b, t, d = x_prompt.shape
    bd, td, _ = x_sample.shape
    past = cache_dsa_k.shape[2]
    depth = norm_mix.shape[0]
    groups = ((b, t, 0, min(512, b * t)), (bd, td, past, bd * td))

    tabs = []
    for (gb, gt, p0, tm) in groups:
        tab = _rope_tables(p0 + jnp.arange(gt, dtype=jnp.int32))
        if tm > gt:
            tab = jnp.tile(tab, (tm // gt, 1))
        tabs.append(tab)

    hs = [x_prompt.reshape(b * t, d), x_sample.reshape(bd * td, d)]
    outs = [dict(), dict()]
    for l in range(depth):
        li = l // 2
        last = l == depth - 1
        if l % 2 == 0:
            n_in = w_in_even.shape[2]
            w_in = jnp.pad(w_in_even[li], ((0, 0), (0, -n_in % LANES))).astype(BF16)
            w_out = w_out_even[li].astype(BF16)
            wp = w_pool[li].astype(BF16)
        else:
            w_in = w_in_odd[li].astype(BF16)
            w_out = w_out_odd[li].astype(BF16)
            lam_init = 0.8 - 0.6 * math.exp(-0.3 * l)
        wr = jnp.concatenate([moe_w_group[l]] + [moe_w_expert[l, g] for g in range(MOE_GROUPS)], axis=1)
        wr = jnp.pad(wr, ((0, 0), (0, LANES - wr.shape[1]))).astype(BF16)
        br = jnp.concatenate([moe_b_group[l], moe_b_expert[l].reshape(-1)])
        br = jnp.pad(br, (0, LANES - br.shape[0])).reshape(1, LANES).astype(F32)
        w1, w3, w2 = moe_w1[l].astype(BF16), moe_w3[l].astype(BF16), moe_w2[l].astype(BF16)

        for gi, (gb, gt, p0, tm) in enumerate(groups):
            h = hs[gi]
            o = outs[gi]
            sample = gi == 1
            r3 = lambda x: x.reshape(gb, gt, x.shape[-1])
            if l % 2 == 0:
                u, q, k, v, qi, kiwi, k16, v16 = [
                    r3(x) for x in _project(h, norm_mix[l], w_in, tabs[gi], EVEN_SEGS, tm)]
                ki = kiwi[..., :HEAD_DIM]
                hist = cache_pool[li] if sample else jnp.zeros((gb, POOL_HIST, u.shape[2]), F32)
                a_out = _pool_mix(u, hist, wp, pool_scale[li], p0)
                if sample:
                    k_all, n_keys = _cat_keys(cache_dsa_k[li].reshape(gb, past, -1), k)
                    v_all, _ = _cat_keys(cache_dsa_v[li].reshape(gb, past, -1), v)
                    ki_all, _ = _cat_keys(cache_idx_k[li], ki)
                else:
                    (k_all, n_keys), (v_all, _), (ki_all, _) = _cat_keys(None, k16), _cat_keys(None, v16), _cat_keys(None, ki)
                b_out = _dsa(q, qi, kiwi, k_all, v_all, ki_all, n_keys, p0)
                o.setdefault("pool", []).append(jnp.concatenate([hist, u], axis=1)[:, -POOL_HIST:])
                o.setdefault("dsa_k", []).append(k.reshape(gb, gt, -1, HEAD_DIM))
                o.setdefault("dsa_v", []).append(v.reshape(gb, gt, -1, HEAD_DIM))
                o.setdefault("idx_k", []).append(ki)
                mix_a, mix_b = a_out, b_out
            else:
                sq, sk, sv, dq, dk, dv, sk16, sv16, dk16, dv16 = [
                    r3(x) for x in _project(h, norm_mix[l], w_in, tabs[gi], ODD_SEGS, tm)]
                if sample:
                    sk_all, n_keys = _cat_keys(cache_sb_k[li].reshape(gb, past, -1), sk)
                    sv_all, _ = _cat_keys(cache_sb_v[li].reshape(gb, past, -1), sv)
                    dk_all, _ = _cat_keys(cache_diff_k[li].reshape(gb, past, -1), dk)
                    dv_all, _ = _cat_keys(cache_diff_v[li].reshape(gb, past, -1), dv)
                else:
                    (sk_all, n_keys), (sv_all, _) = _cat_keys(None, sk16), _cat_keys(None, sv16)
                    (dk_all, _), (dv_all, _) = _cat_keys(None, dk16), _cat_keys(None, dv16)
                c_out = _sb_attend(sq, sk_all, sv_all, n_keys, p0)
                d_out = _diff_attend(dq, dk_all, dv_all, diff_lambda[li], diff_subln[li], n_keys, p0, lam_init)
                n_sb = sk.shape[2] // HEAD_DIM
                n_df = dk.shape[2] // (2 * HEAD_DIM)
                o.setdefault("sb_k", []).append(sk.reshape(gb, gt, n_sb, HEAD_DIM))
                o.setdefault("sb_v", []).append(sv.reshape(gb, gt, n_sb, HEAD_DIM))
                o.setdefault("diff_k", []).append(dk.reshape(gb, gt, n_df, 2, HEAD_DIM))
                o.setdefault("diff_v", []).append(dv.reshape(gb, gt, n_df, 2 * HEAD_DIM))
                mix_a, mix_b = c_out, d_out
            h, xn, gate = _out_proj_route(h, mix_a.reshape(gb * gt, -1), mix_b.reshape(gb * gt, -1), w_out,
                                          norm_ffn[l], wr, br, tm)
            hs[gi] = _moe(h, xn, gate, w1, w3, w2, norm_final, last, tm)

    names = ("pool", "dsa_k", "dsa_v", "idx_k", "sb_k", "sb_v", "diff_k", "diff_v")
    res = [hs[0].reshape(b, t, d), hs[1].reshape(bd, td, d)]
    for o in outs:
        res += [jnp.stack(o[nm]) for nm in names]
    return tuple(res)
```

```python
import functools
import math

import jax
import jax.numpy as jnp
import numpy as np
from jax import lax
from jax.experimental import pallas as pl
from jax.experimental.pallas import tpu as pltpu

F32 = jnp.float32
BF16 = jnp.bfloat16

LANES = 128
HEAD_DIM = 64
CHUNK = 64
ROPE_THETA = 10000.0
NORM_EPS = 1e-6
NEG_INF = -1e30
PAD_SCORE = -3e38
BIG_POS = 3e38
POOL_WINDOWS = (2, 4, 8, 16)
POOL_HIST = 15
POOL_HIST_PAD = 16
DSA_TOPK = 256
IDX_HEADS = 4
MOE_GROUPS = 4
MOE_EPG = 4
MOE_EXPERTS = 16
GATE_COL0 = MOE_GROUPS
GROUP_LANE = 0
BF16_ROWS = 16
SORT_ROWS = 256
VMEM_LIMIT = 56 * 1024 * 1024
BISECT_STEPS = 8
BISECT_ROUNDS = 48
BISECT_UNCHECKED = 2
ATTN_TQ = 256
DSA_TQ = 128
OUT_TM = 1024
MOE_TM = 1024
MOE_SUBTILES = 2
MOE_VMEM_LIMIT = 62 * 1024 * 1024
DIFF_LANES = 2 * LANES


def _cparams(sem):
    return pltpu.CompilerParams(dimension_semantics=sem, vmem_limit_bytes=VMEM_LIMIT)


def _dot(a, b):
    return jnp.dot(a, b, preferred_element_type=F32)


def _dot_nt(a, b):
    return lax.dot_general(a, b, (((1,), (1,)), ((), ())), preferred_element_type=F32)


def _dot_tn(a, b):
    return lax.dot_general(a, b, (((0,), (0,)), ((), ())), preferred_element_type=F32)


def _rms(x, g):
    ms = jnp.mean(x * x, axis=-1, keepdims=True)
    return x * lax.rsqrt(ms + NORM_EPS) * g


def _proj_kernel(x_ref, g_ref, w_ref, tab_ref, *out_refs, segs):
    xn = _rms(x_ref[...], g_ref[...]).astype(BF16)
    copies = iter(out_refs[len(segs):])
    for o_ref, (c0, width, mode, twin) in zip(out_refs, segs):
        t_ref = next(copies) if twin else None
        y = _dot(xn, w_ref[:, c0:c0 + width])
        if mode is None:
            o_ref[...] = y
            if twin:
                t_ref[...] = y.astype(BF16)
            continue
        t0 = 0 if mode == "full" else 3 * LANES
        cos = tab_ref[:, t0:t0 + LANES]
        sin_a = tab_ref[:, t0 + LANES:t0 + 2 * LANES]
        sin_b = tab_ref[:, t0 + 2 * LANES:t0 + 3 * LANES]
        for c in range(0, width, LANES):
            yc = y[:, c:c + LANES]
            yr = (yc * cos + pltpu.roll(yc, LANES - HEAD_DIM // 2, 1) * sin_a
                  + pltpu.roll(yc, HEAD_DIM // 2, 1) * sin_b)
            o_ref[:, c:c + LANES] = yr
            if twin:
                t_ref[:, c:c + LANES] = yr.astype(BF16)


def _rope_tables(pos):
    half = HEAD_DIM // 2
    inv = ROPE_THETA ** (-jnp.arange(half, dtype=F32) / half)
    ang = pos.astype(F32)[:, None] * inv[None, :]
    cos, sin = jnp.cos(ang), jnp.sin(ang)
    zero, one = jnp.zeros_like(sin), jnp.ones_like(cos)
    cos_h = jnp.concatenate([cos, cos], axis=1)
    sa_h = jnp.concatenate([-sin, zero], axis=1)
    sb_h = jnp.concatenate([zero, sin], axis=1)
    one_h = jnp.concatenate([one, one], axis=1)
    zero_h = jnp.concatenate([zero, zero], axis=1)
    return jnp.concatenate([cos_h, cos_h, sa_h, sa_h, sb_h, sb_h,
                            cos_h, one_h, sa_h, zero_h, sb_h, zero_h], axis=1)


def _project(x, g, w, tab, segs, tm):
    n, d = x.shape
    tt = tab.shape[0]
    nt = tt // tm
    kern = functools.partial(_proj_kernel, segs=segs)
    return pl.pallas_call(
        kern,
        grid=(n // tm,),
        in_specs=[pl.BlockSpec((tm, d), lambda i: (i, 0)),
                  pl.BlockSpec((1, d), lambda i: (0, 0)),
                  pl.BlockSpec(w.shape, lambda i: (0, 0)),
                  pl.BlockSpec((tm, tab.shape[1]), lambda i: (i % nt, 0))],
        out_specs=[pl.BlockSpec((tm, wd), lambda i: (i, 0)) for _, wd, _, _ in segs]
        + [pl.BlockSpec((tm, wd), lambda i: (i, 0)) for _, wd, _, twin in segs if twin],
        out_shape=[jax.ShapeDtypeStruct((n, wd), F32) for _, wd, _, _ in segs]
        + [jax.ShapeDtypeStruct((n, wd), BF16) for _, wd, _, twin in segs if twin],
        compiler_params=_cparams(("parallel",)),
        name="proj",
    )(x, g.reshape(1, d), w, tab)


def _pool_kernel(u_ref, h_ref, w_ref, s_ref, o_ref, ext_ref, *, t, pos0, rc):
    ext_ref[0:POOL_HIST_PAD, :] = h_ref[0]
    ext_ref[POOL_HIST_PAD:POOL_HIST_PAD + t, :] = u_ref[0]
    for r0 in range(0, t, rc):
        pos = pos0 + r0 + lax.broadcasted_iota(jnp.int32, (rc, 1), 0)
        for g, win in enumerate(POOL_WINDOWS):
            c0 = g * LANES
            u_new = ext_ref[POOL_HIST_PAD + r0:POOL_HIST_PAD + r0 + rc, c0:c0 + LANES]
            s = u_new
            for k in range(1, win):
                s = s + ext_ref[POOL_HIST_PAD + r0 - k:POOL_HIST_PAD + r0 - k + rc, c0:c0 + LANES]
            cnt = jnp.minimum(pos + 1, win).astype(F32)
            dlt = (s / cnt - u_new).astype(BF16)
            o_ref[0, r0:r0 + rc, c0:c0 + LANES] = _dot(dlt, w_ref[g]) * s_ref[:, c0:c0 + LANES]


def _pool_mix(u, hist, w_pool, pool_scale, pos0):
    b, t, c = u.shape
    rc = min(t, 256)
    hist16 = jnp.pad(hist, ((0, 0), (POOL_HIST_PAD - POOL_HIST, 0), (0, 0)))
    kern = functools.partial(_pool_kernel, t=t, pos0=pos0, rc=rc)
    return pl.pallas_call(
        kern,
        grid=(b,),
        in_specs=[pl.BlockSpec((1, t, c), lambda i: (i, 0, 0)),
                  pl.BlockSpec((1, POOL_HIST_PAD, c), lambda i: (i, 0, 0)),
                  pl.BlockSpec(w_pool.shape, lambda i: (0, 0, 0)),
                  pl.BlockSpec((1, c), lambda i: (0, 0))],
        out_specs=pl.BlockSpec((1, t, c), lambda i: (i, 0, 0)),
        out_shape=jax.ShapeDtypeStruct((b, t, c), F32),
        scratch_shapes=[pltpu.VMEM((POOL_HIST_PAD + t, c), F32)],
        compiler_params=_cparams(("parallel",)),
        name="pool_mix",
    )(u, hist16, w_pool, pool_scale.reshape(1, c))


def _dsa_kernel(q_ref, qi_ref, kw_ref, k_ref, v_ref, ki_ref, o_ref, lo_ref, hi_ref, bias_ref,
                *, tq, extents, n_keys, q_pos0, n_sel):
    i = pl.program_id(1)
    qpos = q_pos0 + i * tq + lax.broadcasted_iota(jnp.int32, (tq, 1), 0)
    lim = jnp.minimum((qpos // CHUNK + 1) * CHUNK, n_keys)
    kf = float(n_sel)
    low = lax.broadcasted_iota(jnp.int32, (tq, LANES), 1) < HEAD_DIM
    q = q_ref[0] * HEAD_DIM ** -0.5
    qi = qi_ref[0].astype(BF16)
    wi = kw_ref[0][:, HEAD_DIM:HEAD_DIM + IDX_HEADS] * (IDX_HEADS * HEAD_DIM) ** -0.5

    def body(ext):
        kpos = lax.broadcasted_iota(jnp.int32, (tq, ext), 1)
        adm = kpos < lim
        padded = ext > n_keys
        virt = float(max(n_keys - ext, 0))

        sidx = jnp.zeros((tq, ext), F32)
        ki = ki_ref[0, :ext, :]
        for h in range(IDX_HEADS):
            sh = _dot_nt(qi[:, h * HEAD_DIM:(h + 1) * HEAD_DIM], ki)
            sidx = sidx + jnp.maximum(sh, 0.0) * wi[:, h:h + 1]
        sm = jnp.where(adm, sidx, NEG_INF)
        if padded:
            real = kpos < n_keys
            sm = jnp.where(real, sm, PAD_SCORE)

        def count_gt_wide(x):
            part = jnp.where(sm[:, :LANES] > x, 1.0, 0.0)
            for j in range(1, ext // LANES):
                part = part + jnp.where(sm[:, j * LANES:(j + 1) * LANES] > x, 1.0, 0.0)
            cnt = jnp.broadcast_to(jnp.sum(part, axis=1, keepdims=True), (tq, LANES))
            return cnt + jnp.where(x < NEG_INF, virt, 0.0) if virt else cnt

        def count_gt(x):
            return count_gt_wide(jnp.broadcast_to(x, (tq, LANES)))[:, :1]

        def bracket(lo, hi):
            above = jnp.min(jnp.where(sm > lo, sm, BIG_POS), axis=1, keepdims=True)
            below = jnp.max(jnp.where(sm <= hi, sm, PAD_SCORE), axis=1, keepdims=True)
            if virt:
                above = jnp.minimum(above, jnp.where(lo < NEG_INF, NEG_INF, BIG_POS))
                below = jnp.maximum(below, jnp.where(hi >= NEG_INF, NEG_INF, PAD_SCORE))
            return above, below

        row_max = jnp.max(sm, axis=1, keepdims=True)
        row_min = jnp.min(jnp.where(real, sm, BIG_POS) if padded else sm, axis=1, keepdims=True)
        if virt:
            row_min = jnp.minimum(row_min, NEG_INF)
        adm_min = jnp.min(jnp.where(adm, sm, BIG_POS), axis=1, keepdims=True)
        few = count_gt(row_min) < kf
        tight = count_gt(adm_min) >= kf
        lo_ref[...] = jnp.where(few, PAD_SCORE, jnp.where(tight, adm_min, row_min))
        hi_ref[...] = jnp.where(few, row_min, jnp.where(tight, row_max, adm_min))

        def unresolved(lo, hi):
            above, below = bracket(lo, hi)
            return jnp.sum(jnp.where(above < below, 1, 0))

        def cond(carry):
            rounds, open_rows = carry
            return jnp.logical_and(open_rows > 0, rounds < BISECT_ROUNDS)

        def step(carry):
            rounds, _ = carry
            lo = jnp.broadcast_to(lo_ref[...], (tq, LANES))
            hi = jnp.broadcast_to(hi_ref[...], (tq, LANES))
            for _ in range(BISECT_STEPS):
                mid = 0.5 * lo + 0.5 * hi
                under = count_gt_wide(mid) < kf
                hi = jnp.where(under, mid, hi)
                lo = jnp.where(under, lo, mid)
            lo_ref[...] = lo[:, :1]
            hi_ref[...] = hi[:, :1]
            open_rows = lax.cond(rounds + 1 >= BISECT_UNCHECKED,
                                 lambda: unresolved(lo_ref[...], hi_ref[...]), lambda: jnp.int32(1))
            return rounds + 1, open_rows

        lax.while_loop(cond, step, (jnp.int32(0), jnp.int32(1)))
        _, thr = bracket(lo_ref[...], hi_ref[...])

        gt = sm > thr
        eq = sm == thr
        need = kf - count_gt(thr)
        n_eq = jnp.sum(jnp.where(eq, 1.0, 0.0), axis=1, keepdims=True)
        crowded = jnp.sum(jnp.where(n_eq > need, 1, 0))

        @pl.when(crowded == 0)
        def _():
            keep = jnp.logical_and(jnp.logical_or(gt, eq), adm)
            bias_ref[:, :ext] = jnp.where(keep, 0.0, NEG_INF)

        @pl.when(crowded > 0)
        def _():
            rr = lax.broadcasted_iota(jnp.int32, (LANES, LANES), 0)
            cc = lax.broadcasted_iota(jnp.int32, (LANES, LANES), 1)
            prefix_ones = jnp.where(rr <= cc, 1.0, 0.0).astype(BF16)
            carry = jnp.zeros((tq, 1), F32)
            for j in range(ext // LANES):
                sl = slice(j * LANES, (j + 1) * LANES)
                eq_j = eq[:, sl]
                rank = _dot(jnp.where(eq_j, 1.0, 0.0).astype(BF16), prefix_ones) + carry
                keep = jnp.logical_or(gt[:, sl], jnp.logical_and(eq_j, rank <= need))
                bias_ref[:, sl] = jnp.where(jnp.logical_and(keep, adm[:, sl]), 0.0, NEG_INF)
                carry = rank[:, LANES - 1:LANES]

        bias = bias_ref[:, :ext]
        bias2 = jnp.concatenate([bias, bias], axis=0)

        kk = k_ref[0, :ext, :]
        vv = v_ref[0, :ext, :]
        low_k = lax.broadcasted_iota(jnp.int32, (ext, LANES), 1) < HEAD_DIM
        vsw = pltpu.roll(vv.astype(F32), HEAD_DIM, 1).astype(BF16)
        one = jnp.ones_like(vv)
        n_kv = LANES // HEAD_DIM
        stacks = [(g, odd) for g in range(n_kv) for odd in range(2)]
        scores = []
        for g, odd in stacks:
            rows = []
            for m in range(2):
                c = 2 * g + m
                qc = q[:, c * LANES:(c + 1) * LANES]
                if (odd == 1) != (g == 1):
                    qc = pltpu.roll(qc, HEAD_DIM, 1)
                rows.append(jnp.where(low, qc, 0.0) if g == 0 else jnp.where(low, 0.0, qc))
            qs = jnp.concatenate(rows, axis=0).astype(BF16)
            scores.append(_dot_nt(qs, kk) + bias2)
        probs = [jnp.exp(sc - jnp.max(sc, axis=1, keepdims=True)).astype(BF16) for sc in scores]
        outs = []
        for (g, odd), p in zip(stacks, probs):
            if odd:
                v_aug = jnp.where(low_k, one, vsw if g == 0 else vv)
            else:
                v_aug = jnp.where(low_k, vv if g == 0 else vsw, one)
            og = _dot(p, v_aug)
            outs.append(og / pltpu.roll(og, HEAD_DIM, 1))
        for g in range(n_kv):
            for m in range(2):
                c = 2 * g + m
                o_ref[0, :, c * LANES:(c + 1) * LANES] = jnp.where(
                    low, outs[2 * g][m * tq:(m + 1) * tq], outs[2 * g + 1][m * tq:(m + 1) * tq])

    _for_tile_extent(i, tq, q_pos0, n_keys, extents, body)


def _dsa(q, qi, kiwi, k_all, v_all, ki_all, n_keys, q_pos0):
    b, t, dq = q.shape
    s_len = k_all.shape[1]
    tq = min(t, DSA_TQ)
    n_sel = min(DSA_TOPK, n_keys // 4)
    extents = _key_extents(t // tq, tq, q_pos0, n_keys, s_len, 4 * LANES)
    kern = functools.partial(_dsa_kernel, tq=tq, extents=extents, n_keys=n_keys, q_pos0=q_pos0, n_sel=n_sel)
    qspec = lambda w: pl.BlockSpec((1, tq, w), lambda bi, i: (bi, i, 0))
    kspec = lambda w: pl.BlockSpec((1, s_len, w), lambda bi, i: (bi, 0, 0))
    return pl.pallas_call(
        kern,
        grid=(b, t // tq),
        in_specs=[qspec(dq), qspec(qi.shape[2]), qspec(kiwi.shape[2]),
                  kspec(k_all.shape[2]), kspec(v_all.shape[2]), kspec(ki_all.shape[2])],
        out_specs=qspec(dq),
        out_shape=jax.ShapeDtypeStruct((b, t, dq), F32),
        scratch_shapes=[pltpu.VMEM((tq, 1), F32), pltpu.VMEM((tq, 1), F32), pltpu.VMEM((tq, s_len), F32)],
        compiler_params=_cparams(("parallel", "parallel")),
        name="dsa",
    )(q, qi, kiwi, k_all, v_all, ki_all)


def _sb_kernel(q_ref, k_ref, v_ref, o_ref, acc_ref, run_ref, *, tq, n_keys, q_pos0):
    i = pl.program_id(1)
    pairs = q_ref.shape[2] // LANES
    first_q = q_pos0 + i * tq
    qpos = first_q + lax.broadcasted_iota(jnp.int32, (tq, 1), 0)
    n_blocks = (jnp.minimum(first_q + tq - 1, n_keys) + LANES - 1) // LANES
    n_full = jnp.minimum(first_q, n_keys) // LANES
    low_q = lax.broadcasted_iota(jnp.int32, (tq, LANES), 1) < HEAD_DIM
    low_k = lax.broadcasted_iota(jnp.int32, (LANES, LANES), 1) < HEAD_DIM
    q = q_ref[0] * HEAD_DIM ** -0.5
    qm = []
    for p in range(pairs):
        qp = q[:, p * LANES:(p + 1) * LANES]
        qm.append((jnp.where(low_q, qp, 0.0).astype(BF16), jnp.where(low_q, 0.0, qp).astype(BF16)))
    rr = lax.broadcasted_iota(jnp.int32, (2 * LANES, 2 * LANES), 0)
    cc = lax.broadcasted_iota(jnp.int32, (2 * LANES, 2 * LANES), 1)
    rk = jnp.where(rr >= LANES, rr - LANES, rr)
    cs_rhs = jnp.where(jnp.logical_or(cc >= LANES, rk >= cc), 1.0, 0.0).astype(BF16)
    acc_ref[...] = jnp.zeros_like(acc_ref)
    run_ref[...] = jnp.zeros_like(run_ref)

    def block(j, masked):
        ks = pl.multiple_of(j * LANES, LANES)
        if masked:
            kpos = ks + lax.broadcasted_iota(jnp.int32, (1, LANES), 1)
            causal = jnp.logical_and(kpos < qpos, kpos < n_keys)
        heads = [(p, c) for p in range(pairs) for c in range(2)]
        lanes = lambda n: slice(n * LANES, (n + 1) * LANES)
        zs = [_dot_nt(qm[p][c], k_ref[0, pl.ds(ks, LANES), lanes(p)]) for p, c in heads]
        css = []
        for z in zs:
            sp = jnp.maximum(z, 0.0) + jnp.log(1.0 + jnp.exp(-jnp.abs(z)))
            if masked:
                sp = jnp.where(causal, sp, 0.0)
            hi = sp.astype(BF16)
            lo = (sp - hi.astype(F32)).astype(BF16)
            css.append(_dot(jnp.concatenate([hi, lo], axis=1), cs_rhs))
        probs = []
        for n, (z, cs) in enumerate(zip(zs, css)):
            run = run_ref[:, lanes(n)]
            a = jnp.exp(z - cs[:, :LANES] - run)
            if masked:
                a = jnp.where(causal, a, 0.0)
            run_ref[:, lanes(n)] = run + cs[:, LANES:]
            probs.append(a.astype(BF16))
        for p in range(pairs):
            vb = v_ref[0, pl.ds(ks, LANES), lanes(p)]
            zero = jnp.zeros_like(vb)
            v_cat = jnp.concatenate([jnp.where(low_k, vb, zero), jnp.where(low_k, zero, vb)], axis=0)
            acc_ref[:, lanes(p)] += _dot(jnp.concatenate(probs[2 * p:2 * p + 2], axis=1), v_cat)

    def masked_step(jj, carry):
        block(n_blocks - 1 - jj, True)
        return carry

    def full_pair(jj, carry):
        block(n_full - 1 - 2 * jj, False)
        block(n_full - 2 - 2 * jj, False)
        return carry

    lax.fori_loop(0, n_blocks - n_full, masked_step, 0)
    lax.fori_loop(0, n_full // 2, full_pair, 0)

    @pl.when(n_full % 2 == 1)
    def _():
        block(jnp.int32(0), False)

    o_ref[0] = acc_ref[...]


def _sb_attend(q, k_all, v_all, n_keys, q_pos0):
    b, t, d = q.shape
    s_len = k_all.shape[1]
    tq = min(t, ATTN_TQ)
    kern = functools.partial(_sb_kernel, tq=tq, n_keys=n_keys, q_pos0=q_pos0)
    return pl.pallas_call(
        kern,
        grid=(b, t // tq),
        in_specs=[pl.BlockSpec((1, tq, d), lambda bi, i: (bi, i, 0)),
                  pl.BlockSpec((1, s_len, d), lambda bi, i: (bi, 0, 0)),
                  pl.BlockSpec((1, s_len, d), lambda bi, i: (bi, 0, 0))],
        out_specs=pl.BlockSpec((1, tq, d), lambda bi, i: (bi, i, 0)),
        out_shape=jax.ShapeDtypeStruct((b, t, d), F32),
        scratch_shapes=[pltpu.VMEM((tq, d), F32), pltpu.VMEM((tq, 2 * d), F32)],
        compiler_params=_cparams(("parallel", "parallel")),
        name="sb_attend",
    )(q, k_all, v_all)


def _key_extents(nq, tq, q_pos0, n_keys, s_len, step):
    need = [min(n_keys, ((q_pos0 + (i + 1) * tq - 1) // CHUNK + 1) * CHUNK) for i in range(nq)]
    return tuple(sorted({min(s_len, -(-n // step) * step) for n in need}))


def _for_tile_extent(i, tq, q_pos0, n_keys, extents, body):
    need = jnp.minimum(n_keys, ((q_pos0 + (i + 1) * tq - 1) // CHUNK + 1) * CHUNK)
    prev = 0
    for ext in extents:
        pl.when(jnp.logical_and(need > prev, need <= ext))(functools.partial(body, ext))
        prev = ext


def _diff_kernel(q_ref, k_ref, v_ref, lam_ref, gain_ref, o_ref, *, tq, extents, n_keys, q_pos0, lam_init):
    i = pl.program_id(2)
    lp = lam_ref[...]
    lam = (jnp.exp(jnp.sum(lp[0:1] * lp[1:2], axis=1, keepdims=True))
           - jnp.exp(jnp.sum(lp[2:3] * lp[3:4], axis=1, keepdims=True)) + lam_init)
    qpos = q_pos0 + i * tq + lax.broadcasted_iota(jnp.int32, (tq, 1), 0)
    lim = jnp.minimum((qpos // CHUNK + 1) * CHUNK, n_keys)
    low = lax.broadcasted_iota(jnp.int32, (tq, LANES), 1) < HEAD_DIM

    def body(ext):
        mask = lax.broadcasted_iota(jnp.int32, (tq, ext), 1) < lim
        heads = range(q_ref.shape[2] // LANES)
        lanes = lambda h: slice(h * LANES, (h + 1) * LANES)
        scores = []
        for h in heads:
            q = q_ref[0, :, lanes(h)] * HEAD_DIM ** -0.5
            kk = k_ref[0, :ext, lanes(h)]
            for c in range(2):
                qc = (jnp.where(low, q, 0.0) if c == 0 else jnp.where(low, 0.0, q)).astype(BF16)
                scores.append(jnp.where(mask, _dot_nt(qc, kk), NEG_INF))
        probs = []
        for sc in scores:
            p = jnp.exp(sc - jnp.max(sc, axis=1, keepdims=True))
            probs.append(p * (1.0 / jnp.sum(p, axis=1, keepdims=True)))
        for h in heads:
            a = probs[2 * h] - lam * probs[2 * h + 1]
            o = _dot(a.astype(BF16), v_ref[0, :ext, lanes(h)])
            o = o * lax.rsqrt(jnp.mean(o * o, axis=-1, keepdims=True) + NORM_EPS)
            o_ref[0, :, lanes(h)] = o * gain_ref[...] * (1.0 - lam_init)

    _for_tile_extent(i, tq, q_pos0, n_keys, extents, body)


def _diff_attend(q, k_all, v_all, diff_lambda, gain, n_keys, q_pos0, lam_init):
    b, t, d = q.shape
    s_len = k_all.shape[1]
    tq = min(t, ATTN_TQ)
    extents = _key_extents(t // tq, tq, q_pos0, n_keys, s_len, 2 * LANES)
    kern = functools.partial(_diff_kernel, tq=tq, extents=extents, n_keys=n_keys, q_pos0=q_pos0, lam_init=lam_init)
    return pl.pallas_call(
        kern,
        grid=(b, d // DIFF_LANES, t // tq),
        in_specs=[pl.BlockSpec((1, tq, DIFF_LANES), lambda bi, h, i: (bi, i, h)),
                  pl.BlockSpec((1, s_len, DIFF_LANES), lambda bi, h, i: (bi, 0, h)),
                  pl.BlockSpec((1, s_len, DIFF_LANES), lambda bi, h, i: (bi, 0, h)),
                  pl.BlockSpec(diff_lambda.shape, lambda bi, h, i: (0, 0)),
                  pl.BlockSpec((1, LANES), lambda bi, h, i: (0, 0))],
        out_specs=pl.BlockSpec((1, tq, DIFF_LANES), lambda bi, h, i: (bi, i, h)),
        out_shape=jax.ShapeDtypeStruct((b, t, d), F32),
        compiler_params=_cparams(("parallel", "parallel", "parallel")),
        name="diff_attend",
    )(q, k_all, v_all, diff_lambda, gain.reshape(1, LANES))


def _route(logits):
    lane = lax.broadcasted_iota(jnp.int32, logits.shape, 1)
    is_group = lane < MOE_GROUPS
    gl = jnp.where(is_group, logits, NEG_INF)
    g_max = jnp.max(gl, axis=1, keepdims=True)
    g_sel = jnp.min(jnp.where(gl == g_max, lane, LANES), axis=1, keepdims=True)
    g_gate = 1.0 / jnp.sum(jnp.where(is_group, jnp.exp(gl - g_max), 0.0), axis=1, keepdims=True)
    in_group = jnp.logical_and(lane >= GATE_COL0, (lane - GATE_COL0) // MOE_EPG == g_sel)
    in_group = jnp.logical_and(in_group, lane < GATE_COL0 + MOE_EXPERTS)
    el = jnp.where(in_group, logits, NEG_INF)
    top1 = jnp.max(el, axis=1, keepdims=True)
    i1 = jnp.min(jnp.where(jnp.logical_and(in_group, el == top1), lane, LANES), axis=1, keepdims=True)
    rest = jnp.logical_and(in_group, lane != i1)
    el2 = jnp.where(rest, logits, NEG_INF)
    top2 = jnp.max(el2, axis=1, keepdims=True)
    i2 = jnp.min(jnp.where(jnp.logical_and(rest, el2 == top2), lane, LANES), axis=1, keepdims=True)
    e2 = jnp.exp(top2 - top1)
    w1 = g_gate / (1.0 + e2)
    gates = jnp.where(lane == i1, w1, jnp.where(lane == i2, w1 * e2, 0.0))
    return jnp.where(lane == GROUP_LANE, g_sel.astype(F32), gates)


def _out_kernel(h_ref, a_ref, b_ref, wa_ref, wb_ref, g_ref, wr_ref, br_ref, o_ref, xn_ref, gate_ref):
    h = (h_ref[...] + _dot(a_ref[...].astype(BF16), wa_ref[...])
         + _dot(b_ref[...].astype(BF16), wb_ref[...]))
    o_ref[...] = h
    xn = _rms(h, g_ref[...]).astype(BF16)
    xn_ref[...] = xn
    gate_ref[...] = _route(_dot(xn, wr_ref[...]) + br_ref[...])


def _out_proj_route(h, a, bmix, w_out, g_ffn, wr, br, tm):
    n, d = h.shape
    ca = a.shape[1]
    wa, wb = w_out[:ca], w_out[ca:]
    tm = OUT_TM if n % OUT_TM == 0 else tm
    row = lambda w: pl.BlockSpec((tm, w), lambda i: (i, 0))
    whole = lambda x: pl.BlockSpec(x.shape, lambda i: (0, 0))
    return pl.pallas_call(
        _out_kernel,
        grid=(n // tm,),
        in_specs=[row(d), row(ca), row(bmix.shape[1]), whole(wa), whole(wb),
                  pl.BlockSpec((1, d), lambda i: (0, 0)), whole(wr), whole(br)],
        out_specs=[row(d), row(d), row(LANES)],
        out_shape=[jax.ShapeDtypeStruct((n, d), F32), jax.ShapeDtypeStruct((n, d), BF16),
                   jax.ShapeDtypeStruct((n, LANES), F32)],
        compiler_params=_cparams(("parallel",)),
        name="out_proj_route",
    )(h, a, bmix, wa, wb, g_ffn.reshape(1, d), wr, br)


def _split3(x):
    hi = x.astype(BF16)
    r1 = x - hi.astype(F32)
    mid = r1.astype(BF16)
    lo = (r1 - mid.astype(F32)).astype(BF16)
    return hi, mid, lo


def _expert_kernel(xn_ref, gate_ref, w1_ref, w3_ref, w2_ref, h_ref, gf_ref, o_ref,
                   acc_ref, xs_ref, gs_ref, pt_ref, tri_ref, seg_ref, *, final_norm, tm, win):
    i = pl.program_id(0)
    e = pl.program_id(1)
    n_sub = xn_ref.shape[0] // tm
    sr = min(SORT_ROWS, tm)
    tile_rows = lambda s: pl.ds(pl.multiple_of(s * tm, tm), tm)

    @pl.when(jnp.logical_and(i == 0, e == 0))
    def _():
        rr = lax.broadcasted_iota(jnp.int32, (tm, tm), 0)
        cc = lax.broadcasted_iota(jnp.int32, (tm, tm), 1)
        tri_ref[...] = jnp.where(cc < rr, 1.0, 0.0).astype(BF16)

    def sort_tile(s, carry):
        gate = gate_ref[tile_rows(s), :]
        lane = lax.broadcasted_iota(jnp.int32, gate.shape, 1)
        lane_row = lax.broadcasted_iota(jnp.int32, (1, LANES), 1)
        in_grp = jnp.logical_and(lane < MOE_GROUPS, lane.astype(F32) == gate[:, GROUP_LANE:GROUP_LANE + 1])
        onehot = jnp.where(in_grp, 1.0, 0.0)
        before = _dot(tri_ref[...], onehot.astype(BF16))
        count = jnp.sum(onehot, axis=0, keepdims=True)
        first = jnp.zeros((1, LANES), F32)
        start = jnp.float32(0.0)
        for g in range(MOE_GROUPS):
            n_g = jnp.sum(jnp.where(lane_row == g, count, 0.0))
            seg_ref[s, g] = start.astype(jnp.int32)
            seg_ref[s, MOE_GROUPS + g] = n_g.astype(jnp.int32)
            first = jnp.where(lane_row == g, start, first)
            start = start + n_g
        dest = jnp.sum(onehot * (before + first), axis=1, keepdims=True).astype(jnp.int32)
        g_hi, g_mid, g_lo = _split3(gate)
        xn = xn_ref[tile_rows(s), :]
        d_x = xn.shape[1]
        payload = jnp.concatenate([xn, g_hi, g_mid, g_lo], axis=1)
        for r in range(0, tm, sr):
            rs = slice(r, r + sr)
            pt_ref[s, rs, :] = jnp.where(lax.broadcasted_iota(jnp.int32, (sr, tm), 1) == dest[rs],
                                         1.0, 0.0).astype(BF16)
        for r in range(0, tm, sr):
            rs = slice(r, r + sr)
            pt_cols = pt_ref[s, :, rs]
            moved = _dot_tn(pt_cols, payload)
            xs_ref[s, rs, :] = moved[:, :d_x].astype(BF16)
            gs_ref[s, rs, :] = (moved[:, d_x:d_x + LANES] + moved[:, d_x + LANES:d_x + 2 * LANES]
                                + moved[:, d_x + 2 * LANES:])
        xs_ref[s, tm:, :] = jnp.zeros((win, xs_ref.shape[2]), BF16)
        gs_ref[s, tm:, :] = jnp.zeros((win, LANES), F32)
        acc_ref[s] = jnp.zeros(acc_ref.shape[1:], F32)
        return carry

    @pl.when(e == 0)
    def _():
        lax.fori_loop(0, n_sub, sort_tile, 0)

    grp = e // MOE_EPG
    for s in range(n_sub):
        seg_first = seg_ref[s, grp]
        seg_rows = seg_ref[s, MOE_GROUPS + grp]
        row0 = (seg_first // BF16_ROWS) * BF16_ROWS
        n_win = (seg_first + seg_rows - row0 + win - 1) // win

        def window(w, carry, s=s, row0=row0):
            rows = pl.ds(pl.multiple_of(row0 + w * win, BF16_ROWS), win)
            x = xs_ref[s, rows, :]
            gsw = gs_ref[s, rows, :]
            lane = lax.broadcasted_iota(jnp.int32, gsw.shape, 1)
            ge = jnp.sum(jnp.where(lane == GATE_COL0 + e, gsw, 0.0), axis=1, keepdims=True)
            a = _dot(x, w1_ref[0])
            b = _dot(x, w3_ref[0])
            act = (a * (1.0 / (1.0 + jnp.exp(-a))) * b).astype(BF16)
            acc_ref[s, rows, :] += ge * _dot(act, w2_ref[0])
            return carry

        lax.fori_loop(0, n_win, window, 0)

    def unsort_tile(s, carry):
        a_hi, a_mid, _ = _split3(acc_ref[s, 0:tm, :])
        for r in range(0, tm, sr):
            rows = pl.ds(pl.multiple_of(s * tm + r, sr), sr)
            pt = pt_ref[s, r:r + sr, :]
            y = h_ref[rows, :] + (_dot(pt, a_hi) + _dot(pt, a_mid))
            o_ref[rows, :] = _rms(y, gf_ref[...]) if final_norm else y
        return carry

    @pl.when(e == pl.num_programs(1) - 1)
    def _():
        lax.fori_loop(0, n_sub, unsort_tile, 0)


def _moe(h, xn, gate, w1, w3, w2, g_final, final_norm, tm):
    n, d = h.shape
    n_e = w1.shape[0]
    tme = MOE_TM if n % MOE_TM == 0 else tm
    n_sub = MOE_SUBTILES if n % (MOE_SUBTILES * tme) == 0 else 1
    blk = n_sub * tme
    win = -(-(tme * 5 // 16) // BF16_ROWS) * BF16_ROWS
    row2 = lambda w: pl.BlockSpec((blk, w), lambda i, e: (i, 0), pipeline_mode=pl.Buffered(1))
    wspec = lambda w: pl.BlockSpec((1,) + w.shape[1:], lambda i, e: (e, 0, 0))
    kern = functools.partial(_expert_kernel, final_norm=final_norm, tm=tme, win=win)
    return pl.pallas_call(
        kern,
        grid=(n // blk, n_e),
        in_specs=[row2(d), row2(LANES),
                  wspec(w1), wspec(w3), wspec(w2),
                  row2(d), pl.BlockSpec((1, d), lambda i, e: (0, 0))],
        out_specs=row2(d),
        out_shape=jax.ShapeDtypeStruct((n, d), F32),
        scratch_shapes=[pltpu.VMEM((n_sub, tme + win, d), F32), pltpu.VMEM((n_sub, tme + win, d), BF16),
                        pltpu.VMEM((n_sub, tme + win, LANES), F32), pltpu.VMEM((n_sub, tme, tme), BF16),
                        pltpu.VMEM((tme, tme), BF16), pltpu.SMEM((n_sub, 2 * MOE_GROUPS), jnp.int32)],
        compiler_params=pltpu.CompilerParams(dimension_semantics=("arbitrary", "arbitrary"),
                                             vmem_limit_bytes=MOE_VMEM_LIMIT),
        name="moe_experts",
    )(xn, gate, w1, w3, w2, h, g_final.reshape(1, d))


EVEN_SEGS = ((0, 512, None, False), (512, 512, "full", False), (1024, 128, "full", True), (1152, 128, None, True),
             (1280, 256, "full", False), (1536, 128, "half", False))
ODD_SEGS = ((0, 512, None, False), (512, 512, None, True), (1024, 512, None, True),
            (1536, 512, "full", False), (2048, 512, "full", True), (2560, 512, None, True))


def _cat_keys(hist, new):
    allk = jnp.concatenate([hist, new], axis=1) if hist is not None else new
    n_keys = allk.shape[1]
    pad = -n_keys % LANES
    if pad:
        allk = jnp.pad(allk, ((0, 0), (0, pad), (0, 0)))
    return allk.astype(BF16), n_keys


def kernel(x_prompt, x_sample, cache_pool, cache_dsa_k, cache_dsa_v, cache_idx_k, cache_sb_k, cache_sb_v,
           cache_diff_k, cache_diff_v, norm_mix, norm_ffn, norm_final, w_in_even, w_pool, pool_scale,
           w_out_even, w_in_odd, diff_lambda, diff_subln, w_out_odd, moe_w_group, moe_b_group,
           moe_w_expert, moe_b_expert, moe_w1, moe_w3, moe_w2):
    b, t, d = x_prompt.shape
    bd, td, _ = x_sample.shape
    past = cache_dsa_k.shape[2]
    depth = norm_mix.shape[0]
    groups = ((b, t, 0, min(512, b * t)), (bd, td, past, bd * td))

    tabs = []
    for (gb, gt, p0, tm) in groups:
        tab = _rope_tables(p0 + jnp.arange(gt, dtype=jnp.int32))
        if tm > gt:
            tab = jnp.tile(tab, (tm // gt, 1))
        tabs.append(tab)

    hs = [x_prompt.reshape(b * t, d), x_sample.reshape(bd * td, d)]
    outs = [dict(), dict()]
    for l in range(depth):
        li = l // 2
        last = l == depth - 1
        if l % 2 == 0:
            n_in = w_in_even.shape[2]
            w_in = jnp.pad(w_in_even[li], ((0, 0), (0, -n_in % LANES))).astype(BF16)
            w_out = w_out_even[li].astype(BF16)
            wp = w_pool[li].astype(BF16)
        else:
            w_in = w_in_odd[li].astype(BF16)
            w_out = w_out_odd[li].astype(BF16)
            lam_init = 0.8 - 0.6 * math.exp(-0.3 * l)
        wr = jnp.concatenate([moe_w_group[l]] + [moe_w_expert[l, g] for g in range(MOE_GROUPS)], axis=1)
        wr = jnp.pad(wr, ((0, 0), (0, LANES - wr.shape[1]))).astype(BF16)
        br = jnp.concatenate([moe_b_group[l], moe_b_expert[l].reshape(-1)])
        br = jnp.pad(br, (0, LANES - br.shape[0])).reshape(1, LANES).astype(F32)
        w1, w3, w2 = moe_w1[l].astype(BF16), moe_w3[l].astype(BF16), moe_w2[l].astype(BF16)

        for gi, (gb, gt, p0, tm) in enumerate(groups):
            h = hs[gi]
            o = outs[gi]
            sample = gi == 1
            r3 = lambda x: x.reshape(gb, gt, x.shape[-1])
            if l % 2 == 0:
                u, q, k, v, qi, kiwi, k16, v16 = [
                    r3(x) for x in _project(h, norm_mix[l], w_in, tabs[gi], EVEN_SEGS, tm)]
                ki = kiwi[..., :HEAD_DIM]
                hist = cache_pool[li] if sample else jnp.zeros((gb, POOL_HIST, u.shape[2]), F32)
                a_out = _pool_mix(u, hist, wp, pool_scale[li], p0)
                if sample:
                    k_all, n_keys = _cat_keys(cache_dsa_k[li].reshape(gb, past, -1), k)
                    v_all, _ = _cat_keys(cache_dsa_v[li].reshape(gb, past, -1), v)
                    ki_all, _ = _cat_keys(cache_idx_k[li], ki)
                else:
                    (k_all, n_keys), (v_all, _), (ki_all, _) = _cat_keys(None, k16), _cat_keys(None, v16), _cat_keys(None, ki)
                b_out = _dsa(q, qi, kiwi, k_all, v_all, ki_all, n_keys, p0)
                o.setdefault("pool", []).append(jnp.concatenate([hist, u], axis=1)[:, -POOL_HIST:])
                o.setdefault("dsa_k", []).append(k.reshape(gb, gt, -1, HEAD_DIM))
                o.setdefault("dsa_v", []).append(v.reshape(gb, gt, -1, HEAD_DIM))
                o.setdefault("idx_k", []).append(ki)
                mix_a, mix_b = a_out, b_out
            else:
                sq, sk, sv, dq, dk, dv, sk16, sv16, dk16, dv16 = [
                    r3(x) for x in _project(h, norm_mix[l], w_in, tabs[gi], ODD_SEGS, tm)]
                if sample:
                    sk_all, n_keys = _cat_keys(cache_sb_k[li].reshape(gb, past, -1), sk)
                    sv_all, _ = _cat_keys(cache_sb_v[li].reshape(gb, past, -1), sv)
                    dk_all, _ = _cat_keys(cache_diff_k[li].reshape(gb, past, -1), dk)
                    dv_all, _ = _cat_keys(cache_diff_v[li].reshape(gb, past, -1), dv)
                else:
                    (sk_all, n_keys), (sv_all, _) = _cat_keys(None, sk16), _cat_keys(None, sv16)
                    (dk_all, _), (dv_all, _) = _cat_keys(None, dk16), _cat_keys(None, dv16)
                c_out = _sb_attend(sq, sk_all, sv_all, n_keys, p0)
                d_out = _diff_attend(dq, dk_all, dv_all, diff_lambda[li], diff_subln[li], n_keys, p0, lam_init)
                n_sb = sk.shape[2] // HEAD_DIM
                n_df = dk.shape[2] // (2 * HEAD_DIM)
                o.setdefault("sb_k", []).append(sk.reshape(gb, gt, n_sb, HEAD_DIM))
                o.setdefault("sb_v", []).append(sv.reshape(gb, gt, n_sb, HEAD_DIM))
                o.setdefault("diff_k", []).append(dk.reshape(gb, gt, n_df, 2, HEAD_DIM))
                o.setdefault("diff_v", []).append(dv.reshape(gb, gt, n_df, 2 * HEAD_DIM))
                mix_a, mix_b = c_out, d_out
            h, xn, gate = _out_proj_route(h, mix_a.reshape(gb * gt, -1), mix_b.reshape(gb * gt, -1), w_out,
                                          norm_ffn[l], wr, br, tm)
            hs[gi] = _moe(h, xn, gate, w1, w3, w2, norm_final, last, tm)

    names = ("pool", "dsa_k", "dsa_v", "idx_k", "sb_k", "sb_v", "diff_k", "diff_v")
    res = [hs[0].reshape(b, t, d), hs[1].reshape(bd, td, d)]
    for o in outs:
        res += [jnp.stack(o[nm]) for nm in names]
    return tuple(res)
```
